```python
import functools
import numpy as np
import jax
import jax.numpy as jnp
from jax import lax

D_MODEL = 2048
BATCH = 4
SEQ = 2048
DEPTH = 1
DEC_BATCH = 128
DEC_SEQ = 1
PAST_LEN = 2048
PAGE_SIZE = 128

N_META = 16
R_HEADS = 16
R_HEAD = 64
R_DIM = R_HEADS * R_HEAD
D_DECAY_LORA = 64
D_AAA_LORA = 64
D_GATE_LORA = 160
SHIFT_DIM = 3 * R_DIM + D_DECAY_LORA + D_AAA_LORA + D_GATE_LORA
GN_EPS = 64e-5
A_HEADS = 8
A_KV_HEADS = 2
A_HEAD = 128
A_DIM = A_HEADS * A_HEAD
A_KV_DIM = A_KV_HEADS * A_HEAD
IDX_HEADS = 16
IDX_DIM = 64
TOPK_KEYS = 256
Q_BLOCK = 128
ROPE_THETA = 10000.0
ATT_COLS = A_DIM + 2 * A_KV_DIM + IDX_HEADS * IDX_DIM + IDX_DIM + IDX_HEADS
IN_COLS = SHIFT_DIM + ATT_COLS
MIX_DIM = R_DIM + A_DIM
N_EXPERTS = 64
N_EXPERT_GROUPS = 8
TOPK_GROUPS = 4
TOP_K = 8
D_EXPERT = 512
ROUTED_SCALE = 2.5
EXPERT_BLOCK = 128
LN_EPS = 1e-5
DN_ALPHA = (2 * DEPTH) ** 0.25
DN_BETA = (8 * DEPTH) ** -0.25

kernel_name = 'hymba_rwkv7_dsa_moe_step'


def layer_norm(x, g, b):
    xf = x.astype(jnp.float32)
    mu = jnp.mean(xf, axis=-1, keepdims=True)
    var = jnp.mean(jnp.square(xf - mu), axis=-1, keepdims=True)
    return ((xf - mu) * lax.rsqrt(var + LN_EPS) * g + b).astype(x.dtype)


def rope(x, pos):
    half = x.shape[-1] // 2
    inv = ROPE_THETA ** (-jnp.arange(half, dtype=jnp.float32) / half)
    ang = pos.astype(jnp.float32)[:, None] * inv[None, :]
    cos, sin = jnp.cos(ang)[:, None, :], jnp.sin(ang)[:, None, :]
    xf = x.astype(jnp.float32)
    x1, x2 = xf[..., :half], xf[..., half:]
    return jnp.concatenate([x1 * cos - x2 * sin, x2 * cos + x1 * sin], axis=-1).astype(x.dtype)


def gather_rows(t, ix):
    return jax.vmap(lambda tb, ib: tb[ib])(t, ix)


def split_proj(p):
    b, t = p.shape[:2]
    xs = p[..., :SHIFT_DIM]
    sizes = [A_DIM, A_KV_DIM, A_KV_DIM, IDX_HEADS * IDX_DIM, IDX_DIM, IDX_HEADS]
    q, k, v, iq, ik, iw = jnp.split(p[..., SHIFT_DIM:], np.cumsum(sizes)[:-1].tolist(), axis=-1)
    return (xs, q.reshape(b, t, A_HEADS, A_HEAD), k.reshape(b, t, A_KV_HEADS, A_HEAD),
            v.reshape(b, t, A_KV_HEADS, A_HEAD), iq.reshape(b, t, IDX_HEADS, IDX_DIM), ik, iw)


def wkv_step(s, inp):
    r, dec, k, v, kk, b = inp
    sa = jnp.einsum('bhvk,bhk->bhv', s, -kk)
    s = s * dec[:, :, None, :] + sa[..., None] * b[:, :, None, :] + v[..., None] * k[:, :, None, :]
    return s, jnp.einsum('bhvk,bhk->bhv', s, r)


def rwkv7_mix(xs, prev, s0, mu, w0, w_b, a0, a_b, g_b, k_k, k_a, r_k, gn_g, gn_b):
    b, t, _ = xs.shape
    xf = xs.astype(jnp.float32)
    shifted = jnp.concatenate([prev.astype(jnp.float32)[:, None, :], xf[:, :-1]], axis=1)
    xx = xf + (shifted - xf) * mu
    cuts = np.cumsum([R_DIM, R_DIM, R_DIM, D_DECAY_LORA, D_AAA_LORA]).tolist()
    r, k, v, xw, xa, xg = jnp.split(xx, cuts, axis=-1)
    w = -jax.nn.softplus(-(w0 + jnp.tanh(xw) @ w_b)) - 0.5
    decay = jnp.exp(-jnp.exp(w))
    a = jax.nn.sigmoid(a0 + xa @ a_b)
    g = jax.nn.sigmoid(xg) @ g_b
    heads = lambda z: z.reshape(b, t, R_HEADS, R_HEAD)
    kk = heads(k * k_k)
    kk = kk / jnp.maximum(jnp.linalg.norm(kk, axis=-1, keepdims=True), 1e-12)
    k = k * (1.0 + (a - 1.0) * k_a)
    r, k, v, a, decay = heads(r), heads(k), heads(v), heads(a), heads(decay)
    seq = tuple(jnp.swapaxes(z, 0, 1) for z in (r, decay, k, v, kk, kk * a))
    s_fin, y = lax.scan(wkv_step, s0.astype(jnp.float32), seq)
    y = jnp.swapaxes(y, 0, 1)
    mu_y = jnp.mean(y, axis=-1, keepdims=True)
    var_y = jnp.mean(jnp.square(y - mu_y), axis=-1, keepdims=True)
    y = ((y - mu_y) * lax.rsqrt(var_y + GN_EPS)).reshape(b, t, R_DIM) * gn_g + gn_b
    bonus = jnp.sum(r * k * r_k, axis=-1, keepdims=True) * v
    out = (y + bonus.reshape(b, t, R_DIM)) * g
    return out.astype(xs.dtype), s_fin.astype(s0.dtype), xs[:, -1]


def index_scores(iq, iw, ik, qpos, kpos):
    dots = jnp.einsum('bqhd,bsd->bqhs', iq.astype(jnp.float32), ik.astype(jnp.float32)) * IDX_DIM ** -0.5
    sc = jnp.einsum('bqh,bqhs->bqs', iw.astype(jnp.float32) * IDX_HEADS ** -0.5, jax.nn.relu(dots))
    return jnp.where(kpos[None, None, :] <= qpos[None, :, None], sc, -jnp.inf)


def sparse_attend(q, k_sel, v_sel, valid):
    b, nq = q.shape[:2]
    qg = q.reshape(b, nq, A_KV_HEADS, A_HEADS // A_KV_HEADS, A_HEAD).astype(jnp.float32)
    lg = jnp.einsum('bqgrd,bqkgd->bqgrk', qg, k_sel.astype(jnp.float32)) * A_HEAD ** -0.5
    lg = jnp.where(valid[:, :, None, None, :], lg, -jnp.inf)
    p = jax.nn.softmax(lg, axis=-1)
    o = jnp.einsum('bqgrk,bqkgd->bqgrd', p, v_sel.astype(jnp.float32))
    return o.reshape(b, nq, A_DIM).astype(q.dtype)


def dsa_prompt(q, k, v, iq, ik, iw):
    b, t = q.shape[:2]
    n_sel = min(TOPK_KEYS, t // 4)
    nb = -(-t // Q_BLOCK)
    tp = nb * Q_BLOCK
    pad = lambda z: jnp.pad(z, [(0, 0), (0, tp - t)] + [(0, 0)] * (z.ndim - 2))
    qp, iqp, iwp = pad(q), pad(iq), pad(iw)
    kpos = jnp.arange(t)

    def block(i):
        s = i * Q_BLOCK
        qb = lax.dynamic_slice_in_dim(qp, s, Q_BLOCK, axis=1)
        iqb = lax.dynamic_slice_in_dim(iqp, s, Q_BLOCK, axis=1)
        iwb = lax.dynamic_slice_in_dim(iwp, s, Q_BLOCK, axis=1)
        qpos = s + jnp.arange(Q_BLOCK)
        _, sel = lax.top_k(index_scores(iqb, iwb, ik, qpos, kpos), n_sel)
        valid = sel <= qpos[None, :, None]
        return sparse_attend(qb, gather_rows(k, sel), gather_rows(v, sel), valid)

    out = lax.map(block, jnp.arange(nb))
    return jnp.swapaxes(out, 0, 1).reshape(b, tp, A_DIM)[:, :t]


def dsa_sample(q, k, v, iq, ik, iw, ck, cv, cik, page_table):
    dec_b, s_new = q.shape[:2]
    past = page_table.shape[1] * PAGE_SIZE
    n_keys = past + s_new
    ik_past = cik[page_table].reshape(dec_b, past, IDX_DIM).astype(ik.dtype)
    ik_all = jnp.concatenate([ik_past, ik], axis=1)
    qpos = past + jnp.arange(s_new)
    _, sel = lax.top_k(index_scores(iq, iw, ik_all, qpos, jnp.arange(n_keys)), min(TOPK_KEYS, n_keys // 4))
    valid = sel <= qpos[None, :, None]
    in_past = (sel < past)[..., None, None]
    ps = jnp.minimum(sel, past - 1)
    phys = page_table[jnp.arange(dec_b)[:, None, None], ps // PAGE_SIZE] * PAGE_SIZE + ps % PAGE_SIZE
    ns = jnp.clip(sel - past, 0, s_new - 1)

    def pick(pool, new):
        rows = pool.reshape((-1,) + pool.shape[2:])[phys].astype(new.dtype)
        return jnp.where(in_past, rows, gather_rows(new, ns))

    return sparse_attend(q, pick(ck, k), pick(cv, v), valid)


def swiglu(x, wg, wu, wd):
    return (jax.nn.silu(x @ wg) * (x @ wu)) @ wd


def route(h2d, w_router, e_bias):
    n = h2d.shape[0]
    per_group = N_EXPERTS // N_EXPERT_GROUPS
    scores = jax.nn.sigmoid(h2d.astype(jnp.float32) @ w_router.astype(jnp.float32))
    biased = scores + e_bias.astype(jnp.float32)
    grp = jnp.sum(lax.top_k(biased.reshape(n, N_EXPERT_GROUPS, per_group), 2)[0], axis=-1)
    _, gsel = lax.top_k(grp, TOPK_GROUPS)
    gmask = jnp.sum(jax.nn.one_hot(gsel, N_EXPERT_GROUPS), axis=1) > 0
    emask = jnp.repeat(gmask, per_group, axis=1)
    _, eidx = lax.top_k(jnp.where(emask, biased, -jnp.inf), TOP_K)
    gate = jnp.take_along_axis(scores, eidx, axis=1)
    gate = gate / jnp.sum(gate, axis=-1, keepdims=True) * ROUTED_SCALE
    return eidx, gate


def routed_experts(h2d, eidx, gate, w_gate, w_up, w_down):
    n, d = h2d.shape
    n_assign = n * TOP_K
    nb = (n_assign + N_EXPERTS * (EXPERT_BLOCK - 1)) // EXPERT_BLOCK + 1
    flat_e = eidx.reshape(n_assign)
    order = jnp.argsort(flat_e)
    e_s = flat_e[order]
    t_s = (order // TOP_K).astype(jnp.int32)
    g_s = gate.reshape(n_assign)[order]
    counts = jnp.bincount(flat_e, length=N_EXPERTS)
    padded = (counts + EXPERT_BLOCK - 1) // EXPERT_BLOCK * EXPERT_BLOCK
    pad_end = jnp.cumsum(padded)
    start = jnp.cumsum(counts) - counts
    dest = (pad_end - padded)[e_s] + jnp.arange(n_assign) - start[e_s]
    rows = nb * EXPERT_BLOCK
    tok = jnp.zeros((rows,), jnp.int32).at[dest].set(t_s)
    gw = jnp.zeros((rows,), h2d.dtype).at[dest].set(g_s.astype(h2d.dtype))
    blk_e = jnp.minimum(jnp.searchsorted(pad_end, jnp.arange(nb) * EXPERT_BLOCK, side='right'), N_EXPERTS - 1)

    def block(args):
        e, tk, g = args
        return swiglu(h2d[tk], w_gate[e], w_up[e], w_down[e]) * g[:, None]

    out = lax.map(block, (blk_e, tok.reshape(nb, EXPERT_BLOCK), gw.reshape(nb, EXPERT_BLOCK)))
    return jnp.zeros_like(h2d).at[tok].add(out.reshape(rows, d).astype(h2d.dtype))


def moe_ffn(h, w_router, e_bias, w_gate, w_up, w_down, ws_gate, ws_up, ws_down):
    h2d = h.reshape(-1, h.shape[-1])
    eidx, gate = route(h2d, w_router, e_bias)
    y = routed_experts(h2d, eidx, gate, w_gate, w_up, w_down) + swiglu(h2d, ws_gate, ws_up, ws_down)
    return y.reshape(h.shape)


def trunk_layer(h, pos, prev_shift, s0, attend,
                w_in, mu_shift, w0, w_b, a0, a_b, g_b, k_k, k_a, r_k, gn_g, gn_b,
                w_out, ln1_g, ln1_b, w_router, e_bias, w_gate, w_up, w_down,
                ws_gate, ws_up, ws_down, ln2_g, ln2_b):
    xs, q, k, v, iq, ik, iw = split_proj(h @ w_in)
    q, k, iq = rope(q, pos), rope(k, pos), rope(iq, pos)
    ik = rope(ik[:, :, None, :], pos)[:, :, 0, :]
    r_out, s_new, shift_new = rwkv7_mix(xs, prev_shift, s0, mu_shift, w0, w_b, a0, a_b, g_b,
                                        k_k, k_a, r_k, gn_g, gn_b)
    a_out = attend(q, k, v, iq, ik, iw)
    mix = jnp.concatenate([r_out, a_out], axis=-1) @ w_out
    h = layer_norm(DN_ALPHA * h + mix, ln1_g, ln1_b)
    f = moe_ffn(h, w_router, e_bias, w_gate, w_up, w_down, ws_gate, ws_up, ws_down)
    h = layer_norm(DN_ALPHA * h + f, ln2_g, ln2_b)
    return h, k, v, ik, s_new, shift_new


def setup_inputs(seed: int = 0) -> dict:
    key = jax.random.key(seed)
    ks = iter(jax.random.split(key, 48))

    def nrm(shape, scale=1.0):
        return jax.random.normal(next(ks), shape, jnp.float32) * scale

    n_pages = PAST_LEN // PAGE_SIZE
    n_used = DEC_BATCH * n_pages
    n_pool = n_used + (n_used + 3) // 4
    page_table = jax.random.permutation(next(ks), n_pool)[:n_used].reshape(DEC_BATCH, n_pages).astype(jnp.int32)
    L = DEPTH
    return {
        'x_prompt': nrm((BATCH, SEQ, D_MODEL)),
        'x_sample': nrm((DEC_BATCH, DEC_SEQ, D_MODEL)),
        'cache_k': nrm((L, n_pool, PAGE_SIZE, A_KV_HEADS, A_HEAD)),
        'cache_v': nrm((L, n_pool, PAGE_SIZE, A_KV_HEADS, A_HEAD)),
        'cache_idx_k': nrm((L, n_pool, PAGE_SIZE, IDX_DIM)),
        'state_wkv': nrm((L, DEC_BATCH, R_HEADS, R_HEAD, R_HEAD), 0.5),
        'state_shift': nrm((L, DEC_BATCH, SHIFT_DIM)),
        'page_table': page_table,
        'meta': nrm((N_META, D_MODEL)),
        'ln0_g': 1.0 + nrm((D_MODEL,), 0.02),
        'ln0_b': nrm((D_MODEL,), 0.02),
        'w_in': nrm((L, D_MODEL, IN_COLS), D_MODEL ** -0.5),
        'mu_shift': jax.random.uniform(next(ks), (L, SHIFT_DIM), jnp.float32),
        'w0': nrm((L, R_DIM), 0.5),
        'w_b': nrm((L, D_DECAY_LORA, R_DIM), 0.1),
        'a0': nrm((L, R_DIM), 0.5),
        'a_b': nrm((L, D_AAA_LORA, R_DIM), D_AAA_LORA ** -0.5),
        'g_b': nrm((L, D_GATE_LORA, R_DIM), D_GATE_LORA ** -0.5),
        'k_k': 0.85 + nrm((L, R_DIM), 0.02),
        'k_a': 1.0 + nrm((L, R_DIM), 0.02),
        'r_k': nrm((L, R_HEADS, R_HEAD), 0.1),
        'gn_g': 1.0 + nrm((L, R_DIM), 0.02),
        'gn_b': nrm((L, R_DIM), 0.02),
        'w_out': nrm((L, MIX_DIM, D_MODEL), DN_BETA * MIX_DIM ** -0.5),
        'ln1_g': 1.0 + nrm((L, D_MODEL), 0.02),
        'ln1_b': nrm((L, D_MODEL), 0.02),
        'w_router': nrm((L, D_MODEL, N_EXPERTS), D_MODEL ** -0.5),
        'e_bias': nrm((L, N_EXPERTS), 0.01),
        'w_gate': nrm((L, N_EXPERTS, D_MODEL, D_EXPERT), D_MODEL ** -0.5),
        'w_up': nrm((L, N_EXPERTS, D_MODEL, D_EXPERT), D_MODEL ** -0.5),
        'w_down': nrm((L, N_EXPERTS, D_EXPERT, D_MODEL), DN_BETA * D_EXPERT ** -0.5),
        'ws_gate': nrm((L, D_MODEL, D_EXPERT), D_MODEL ** -0.5),
        'ws_up': nrm((L, D_MODEL, D_EXPERT), D_MODEL ** -0.5),
        'ws_down': nrm((L, D_EXPERT, D_MODEL), DN_BETA * D_EXPERT ** -0.5),
        'ln2_g': 1.0 + nrm((L, D_MODEL), 0.02),
        'ln2_b': nrm((L, D_MODEL), 0.02),
    }


def reference(x_prompt, x_sample, cache_k, cache_v, cache_idx_k, state_wkv, state_shift, page_table,
              meta, ln0_g, ln0_b, w_in, mu_shift, w0, w_b, a0, a_b, g_b, k_k, k_a, r_k, gn_g, gn_b,
              w_out, ln1_g, ln1_b, w_router, e_bias, w_gate, w_up, w_down, ws_gate, ws_up, ws_down,
              ln2_g, ln2_b):
    b, s_p, d = x_prompt.shape
    t_p = N_META + s_p
    s_s = x_sample.shape[1]
    past = page_table.shape[1] * PAGE_SIZE
    meta_rows = jnp.broadcast_to(meta.astype(x_prompt.dtype)[None], (b, N_META, d))
    hp = layer_norm(jnp.concatenate([meta_rows, x_prompt], axis=1), ln0_g, ln0_b)
    hs = layer_norm(x_sample, ln0_g, ln0_b)
    pos_p = jnp.arange(t_p)
    pos_s = past + jnp.arange(s_s)
    st_p, st_s = [], []
    for l in range(DEPTH):
        lp = (w_in[l], mu_shift[l], w0[l], w_b[l], a0[l], a_b[l], g_b[l], k_k[l], k_a[l], r_k[l],
              gn_g[l], gn_b[l], w_out[l], ln1_g[l], ln1_b[l], w_router[l], e_bias[l], w_gate[l],
              w_up[l], w_down[l], ws_gate[l], ws_up[l], ws_down[l], ln2_g[l], ln2_b[l])
        hp, *new_p = trunk_layer(hp, pos_p, jnp.zeros((b, SHIFT_DIM), hp.dtype),
                                 jnp.zeros((b, R_HEADS, R_HEAD, R_HEAD), state_wkv.dtype), dsa_prompt, *lp)
        attend_s = functools.partial(dsa_sample, ck=cache_k[l], cv=cache_v[l], cik=cache_idx_k[l],
                                     page_table=page_table)
        hs, *new_s = trunk_layer(hs, pos_s, state_shift[l], state_wkv[l], attend_s, *lp)
        st_p.append(new_p)
        st_s.append(new_s)

    def stacked(st, i):
        return jnp.stack([s[i] for s in st])

    return (hp[:, N_META:], hs,
            stacked(st_p, 0), stacked(st_p, 1), stacked(st_p, 2), stacked(st_p, 3), stacked(st_p, 4),
            stacked(st_s, 0), stacked(st_s, 1), stacked(st_s, 2), stacked(st_s, 3), stacked(st_s, 4))
```

```python
import functools

import numpy as np
import jax
import jax.numpy as jnp
from jax import lax
from jax.experimental import pallas as pl
from jax.experimental.pallas import tpu as pltpu

F32 = jnp.float32
BF16 = jnp.bfloat16
I32 = jnp.int32
HIGHEST = lax.Precision.HIGHEST

N_META = 16
R_HEADS, R_HEAD = 16, 64
R_DIM = R_HEADS * R_HEAD
D_DECAY_LORA, D_AAA_LORA, D_GATE_LORA = 64, 64, 160
SHIFT_DIM = 3 * R_DIM + D_DECAY_LORA + D_AAA_LORA + D_GATE_LORA
GN_EPS = 64e-5
A_HEADS, A_KV_HEADS, A_HEAD = 8, 2, 128
A_DIM = A_HEADS * A_HEAD
A_KV_DIM = A_KV_HEADS * A_HEAD
IDX_HEADS, IDX_DIM = 16, 64
TOPK_KEYS = 256
ROPE_THETA = 10000.0
N_EXPERTS, N_EXPERT_GROUPS, TOPK_GROUPS, TOP_K = 64, 8, 4, 8
ROUTED_SCALE = 2.5
LN_EPS = 1e-5

LANES = 128
SUBLANES = 8
ROW_TILE = 256
Q_TILE = 128
CHUNK = 64
EXPERT_TILE = 256
VMEM_LIMIT = 56 * 1024 * 1024
NEG_BIG = -1e30
INT_MIN = -2 ** 31

C_R, C_K, C_V = 0, R_DIM, 2 * R_DIM
C_Q = 3 * R_DIM
C_IQ = C_Q + A_DIM
C_KA = C_IQ + IDX_HEADS * IDX_DIM
C_VA = C_KA + A_KV_DIM
C_IK = C_VA + A_KV_DIM
C_LORA = C_IK + LANES
LORA_W = 384
P_COLS = C_LORA + LORA_W


def _cparams(sem):
    return pltpu.CompilerParams(dimension_semantics=sem, vmem_limit_bytes=VMEM_LIMIT)


def _dot(a, b, precision=None):
    return jnp.dot(a, b, preferred_element_type=F32, precision=precision)


def _dot_nt(a, b, precision=None):
    return lax.dot_general(a, b, (((1,), (1,)), ((), ())), preferred_element_type=F32,
                           precision=precision)


def _dot_tn(a, b, precision=None):
    return lax.dot_general(a, b, (((0,), (0,)), ((), ())), preferred_element_type=F32,
                           precision=precision)


def _layer_norm(x, g, b):
    mu = jnp.mean(x, axis=-1, keepdims=True)
    xc = x - mu
    var = jnp.mean(xc * xc, axis=-1, keepdims=True)
    return xc * lax.rsqrt(var + LN_EPS) * g + b


def _sigmoid(z):
    return 1.0 / (1.0 + jnp.exp(-z))


def _ln_proj_kernel(x_ref, g_ref, b_ref, w_ref, o_ref):
    h = _layer_norm(x_ref[...], g_ref[...], b_ref[...])
    o_ref[...] = _dot(h.astype(BF16), w_ref[...])


def ln_proj(x, g, b, w_bf16, tn):
    n, d = x.shape
    cols = w_bf16.shape[1]
    return pl.pallas_call(
        _ln_proj_kernel,
        grid=(cols // tn, n // ROW_TILE),
        in_specs=[
            pl.BlockSpec((ROW_TILE, d), lambda j, i: (i, 0)),
            pl.BlockSpec((1, d), lambda j, i: (0, 0)),
            pl.BlockSpec((1, d), lambda j, i: (0, 0)),
            pl.BlockSpec((d, tn), lambda j, i: (0, j)),
        ],
        out_specs=pl.BlockSpec((ROW_TILE, tn), lambda j, i: (i, j)),
        out_shape=jax.ShapeDtypeStruct((n, cols), F32),
        compiler_params=_cparams(("parallel", "parallel")),
        name="ln_proj",
    )(x, g, b, w_bf16)


def _rot_half(x, head):
    w = x.shape[-1]
    half = head // 2
    lane = lax.broadcasted_iota(I32, x.shape, 1)
    left = pltpu.roll(x, w - half, axis=1)
    right = pltpu.roll(x, half, axis=1)
    return jnp.where((lane % head) < half, left, right)


def _rope_kernel(q_ref, iq_ref, ka_ref, ikw_ref, c128_ref, s128_ref, c64_ref, s64_ref,
                 qo_ref, iqo_ref, ko_ref, iko_ref, iwo_ref):
    c128, s128 = c128_ref[...], s128_ref[...]
    c64, s64 = c64_ref[...], s64_ref[...]

    def rope(x, head, c, s):
        rep = x.shape[-1] // LANES
        if rep > 1:
            c = jnp.concatenate([c] * rep, axis=1)
            s = jnp.concatenate([s] * rep, axis=1)
        return x * c + _rot_half(x, head) * s

    q = rope(q_ref[...], A_HEAD, c128, s128)
    qo_ref[...] = (q * (A_HEAD ** -0.5)).astype(BF16)
    iqo_ref[...] = rope(iq_ref[...], IDX_DIM, c64, s64).astype(BF16)
    ko_ref[...] = rope(ka_ref[...], A_HEAD, c128, s128)
    ikw = ikw_ref[...]
    ik = rope(ikw, IDX_DIM, c64, s64)
    iko_ref[...] = ik[:, :IDX_DIM]
    iwo_ref[...] = ikw[:, IDX_DIM:IDX_DIM + IDX_HEADS]


def rope_all(p, c128, s128, c64, s64):
    n = p.shape[0]
    tm = ROW_TILE
    row = lambda w, blk: pl.BlockSpec((tm, w), lambda i: (i, blk))
    return pl.pallas_call(
        _rope_kernel,
        grid=(n // tm,),
        in_specs=[row(A_DIM, C_Q // A_DIM), row(A_DIM, C_IQ // A_DIM),
                  row(A_KV_DIM, C_KA // A_KV_DIM), row(LANES, C_IK // LANES),
                  row(LANES, 0), row(LANES, 0), row(LANES, 0), row(LANES, 0)],
        out_specs=[row(A_DIM, 0), row(A_DIM, 0), row(A_KV_DIM, 0),
                   row(IDX_DIM, 0), row(IDX_HEADS, 0)],
        out_shape=[jax.ShapeDtypeStruct((n, A_DIM), BF16),
                   jax.ShapeDtypeStruct((n, IDX_HEADS * IDX_DIM), BF16),
                   jax.ShapeDtypeStruct((n, A_KV_DIM), F32),
                   jax.ShapeDtypeStruct((n, IDX_DIM), F32),
                   jax.ShapeDtypeStruct((n, IDX_HEADS), F32)],
        compiler_params=_cparams(("parallel",)),
        name="rope",
    )(p, p, p, p, c128, s128, c64, s64)


def _rwkv_pre_kernel(t_real, tp, from_rows, *refs):
    (x_ref, lo_ref, px_ref, plo_ref, mu_ref, mulo_ref, w0_ref, wb_ref, a0_ref, ab_ref,
     gb_ref, kk_ref, ka_ref, rk_ref, e_ref, et_ref,
     r_o, ld_o, k_o, v_o, kk_o, b_o, g_o, bon_o) = refs
    x = x_ref[...]
    lo = lo_ref[...]
    tm = x.shape[0]
    if from_rows:
        i = pl.program_id(0)
        row = lax.broadcasted_iota(I32, (tm, 1), 0)
        t = (i * tm + row) % tp
        first = row == 0
        sx = jnp.where(first, px_ref[SUBLANES - 1:SUBLANES, :], pltpu.roll(x, 1, axis=0))
        slo = jnp.where(first, plo_ref[SUBLANES - 1:SUBLANES, :], pltpu.roll(lo, 1, axis=0))
        sx = jnp.where(t == 0, 0.0, sx)
        slo = jnp.where(t == 0, 0.0, slo)
        live = t < t_real
    else:
        sx = px_ref[...]
        slo = plo_ref[...]
        live = None
    xx = x + (sx - x) * mu_ref[...]
    xlo = lo + (slo - lo) * mulo_ref[...]
    r = xx[:, C_R:C_R + R_DIM]
    k = xx[:, C_K:C_K + R_DIM]
    v = xx[:, C_V:C_V + R_DIM]
    wa = xlo[:, :LANES]
    xg = xlo[:, LANES:]
    z = w0_ref[...] + _dot(jnp.tanh(wa).astype(BF16), wb_ref[...])
    nz = -z
    softplus = jnp.maximum(nz, 0.0) + jnp.log(1.0 + jnp.exp(-jnp.abs(nz)))
    logd = -jnp.exp(-softplus - 0.5)
    a = _sigmoid(a0_ref[...] + _dot(wa.astype(BF16), ab_ref[...]))
    g = _dot(_sigmoid(xg).astype(BF16), gb_ref[...])
    e, et = e_ref[...], et_ref[...]
    kkr = k * kk_ref[...]
    ss = _dot(_dot(kkr * kkr, e, HIGHEST), et, HIGHEST)
    kk = kkr / jnp.maximum(jnp.sqrt(ss), 1e-12)
    k2 = k * (1.0 + (a - 1.0) * ka_ref[...])
    bonus = _dot(_dot(r * k2 * rk_ref[...], e, HIGHEST), et, HIGHEST) * v
    b = kk * a
    if live is not None:
        zero = lambda y: jnp.where(live, y, 0.0)
        logd, k2s, vs, kk, b = zero(logd), zero(k2), zero(v), zero(kk), zero(b)
    else:
        k2s, vs = k2, v
    r_o[...] = r
    ld_o[...] = logd
    k_o[...] = k2s
    v_o[...] = vs
    kk_o[...] = kk
    b_o[...] = b
    g_o[...] = g
    bon_o[...] = bonus


def rwkv_pre(p, row0, nrows, prev, pw, t_real, tp):
    tm = min(ROW_TILE, nrows)
    blk0 = row0 // tm
    from_rows = prev is None
    xw = 3 * R_DIM
    cur_x = pl.BlockSpec((tm, xw), lambda i: (blk0 + i, 0))
    cur_lo = pl.BlockSpec((tm, LORA_W), lambda i: (blk0 + i, C_LORA // LORA_W))
    if from_rows:
        r8 = tm // SUBLANES
        prev_x = pl.BlockSpec((SUBLANES, xw), lambda i: (jnp.maximum((blk0 + i) * r8 - 1, 0), 0))
        prev_lo = pl.BlockSpec((SUBLANES, LORA_W),
                               lambda i: (jnp.maximum((blk0 + i) * r8 - 1, 0), C_LORA // LORA_W))
        prev_args = (p, p)
    else:
        prev_x = pl.BlockSpec((tm, xw), lambda i: (i, 0))
        prev_lo = pl.BlockSpec((tm, LORA_W), lambda i: (i, 0))
        prev_args = prev
    full = lambda a: pl.BlockSpec(a.shape, lambda i: (0,) * a.ndim)
    params = (pw["mu_x"], pw["mu_lo"], pw["w0"], pw["w_b"], pw["a0"], pw["a_b"], pw["g_b"],
              pw["k_k"], pw["k_a"], pw["r_k"], pw["e"], pw["et"])
    out = pl.BlockSpec((tm, R_DIM), lambda i: (i, 0))
    return pl.pallas_call(
        functools.partial(_rwkv_pre_kernel, t_real, tp, from_rows),
        grid=(nrows // tm,),
        in_specs=[cur_x, cur_lo, prev_x, prev_lo] + [full(a) for a in params],
        out_specs=[out] * 8,
        out_shape=[jax.ShapeDtypeStruct((nrows, R_DIM), F32)] * 8,
        compiler_params=_cparams(("parallel",)),
        name="rwkv_pre_rows" if from_rows else "rwkv_pre_step",
    )(p, p, *prev_args, *params)


def _rwkv_scan_kernel(r_ref, ld_ref, k_ref, v_ref, kk_ref, b_ref, y_ref, s_ref, ss_scr):
    c = pl.program_id(2)

    @pl.when(c == 0)
    def _():
        ss_scr[...] = jnp.zeros_like(ss_scr)

    n = CHUNK
    ld = ld_ref[...]
    ri = lax.broadcasted_iota(I32, (n, n), 0)
    ci = lax.broadcasted_iota(I32, (n, n), 1)
    incl = ci <= ri
    cum = _dot(incl.astype(F32), ld, HIGHEST)
    e_pos = jnp.exp(cum)
    e_neg = jnp.exp(-cum)
    at = -kk_ref[...] * jnp.exp(cum - ld)
    rt = r_ref[...] * e_pos
    bt = b_ref[...] * e_neg
    kt = k_ref[...] * e_neg
    v = v_ref[...]
    lane = lax.broadcasted_iota(I32, (n, LANES), 1)
    head0 = lane < R_HEAD
    bk = jnp.concatenate([bt, kt], axis=0).astype(BF16)
    r2 = lax.broadcasted_iota(I32, (2 * n, 2 * n), 0)
    c2 = lax.broadcasted_iota(I32, (2 * n, 2 * n), 1)
    tri2 = (c2 % n) < jnp.where(r2 < n, r2, r2 - n + 1)
    eye = (ri == ci).astype(F32)
    ss = ss_scr[...]
    ssb = ss.astype(BF16)
    x0 = _dot_nt(at.astype(BF16), ssb)
    y0 = _dot_nt(rt.astype(BF16), ssb)
    vb = v.astype(BF16)
    ws, tinv, mr = [], [], []
    for h in range(2):
        mh = head0 if h == 0 else jnp.logical_not(head0)
        ar = jnp.concatenate([jnp.where(mh, at, 0.0), jnp.where(mh, rt, 0.0)], axis=0)
        sc = jnp.where(tri2, _dot_nt(ar.astype(BF16), bk), 0.0)
        lab = sc[:n, :n]
        lak = sc[:n, n:]
        mr.append(sc[n:, :])
        t = eye + lab
        lp = lab
        m = 1
        while 2 * m < n:
            lpb = lp.astype(BF16)
            lp = _dot(lpb, lpb)
            t = t + _dot(t.astype(BF16), lp.astype(BF16))
            m *= 2
        tinv.append(t)
        ws.append(_dot(lak.astype(BF16), vb))
    w = x0 + jnp.where(head0, ws[0], ws[1])
    wb = w.astype(BF16)
    u = jnp.where(head0, _dot(tinv[0].astype(BF16), wb), _dot(tinv[1].astype(BF16), wb))
    uv = jnp.concatenate([u, v], axis=0).astype(BF16)
    y = y0 + jnp.where(head0, _dot(mr[0].astype(BF16), uv), _dot(mr[1].astype(BF16), uv))
    y_ref[...] = y
    r128 = lax.broadcasted_iota(I32, (LANES, LANES), 0)
    c128 = lax.broadcasted_iota(I32, (LANES, LANES), 1)
    diag = (r128 // R_HEAD) == (c128 // R_HEAD)
    upd = _dot_tn(uv, bk)
    ss_new = (ss + jnp.where(diag, upd, 0.0)) * e_pos[n - 1:n, :]
    ss_scr[...] = ss_new

    @pl.when(c == pl.num_programs(2) - 1)
    def _():
        s_ref[0, 0] = ss_new[:R_HEAD, :R_HEAD]
        s_ref[0, 1] = ss_new[R_HEAD:, R_HEAD:]


def rwkv_scan(r, ld, k, v, kk, b, batch, tp):
    nchunk = tp // CHUNK
    hp = R_HEADS // 2
    blk = pl.BlockSpec((CHUNK, LANES), lambda bi, h, c: (bi * nchunk + c, h))
    return pl.pallas_call(
        _rwkv_scan_kernel,
        grid=(batch, hp, nchunk),
        in_specs=[blk] * 6,
        out_specs=[blk, pl.BlockSpec((1, 2, R_HEAD, R_HEAD), lambda bi, h, c: (bi, h, 0, 0))],
        out_shape=[jax.ShapeDtypeStruct((batch * tp, R_DIM), F32),
                   jax.ShapeDtypeStruct((batch, R_HEADS, R_HEAD, R_HEAD), F32)],
        scratch_shapes=[pltpu.VMEM((LANES, LANES), F32)],
        compiler_params=_cparams(("parallel", "parallel", "arbitrary")),
        name="rwkv_scan",
    )(r, ld, k, v, kk, b)


def _rwkv_step_kernel(r_ref, ld_ref, k_ref, v_ref, kk_ref, b_ref, s_ref, y_ref, so_ref):
    r, k, v = r_ref[0], k_ref[0], v_ref[0]
    dec = jnp.exp(ld_ref[0])
    na = -kk_ref[0]
    b = b_ref[0]
    hrow = lax.broadcasted_iota(I32, (R_HEADS, R_HEAD), 0)
    y = jnp.zeros((R_HEADS, R_HEAD), F32)
    for h in range(R_HEADS):
        s = s_ref[0, h]
        mine = hrow == h
        sa = jnp.where(mine, _dot_nt(na, s, HIGHEST), 0.0)
        uv = jnp.concatenate([sa, jnp.where(mine, v, 0.0)], axis=0)
        bk = jnp.concatenate([b, k], axis=0)
        s_new = s * dec[h:h + 1, :] + _dot_tn(uv, bk, HIGHEST)
        so_ref[0, h] = s_new
        y = y + jnp.where(mine, _dot_nt(r, s_new, HIGHEST), 0.0)
    y_ref[0] = y


def rwkv_step(r, ld, k, v, kk, b, state):
    s = state.shape[0]
    vec = pl.BlockSpec((1, R_HEADS, R_HEAD), lambda i: (i, 0, 0))
    st = pl.BlockSpec((1, R_HEADS, R_HEAD, R_HEAD), lambda i: (i, 0, 0, 0))
    return pl.pallas_call(
        _rwkv_step_kernel,
        grid=(s,),
        in_specs=[vec] * 6 + [st],
        out_specs=[vec, st],
        out_shape=[jax.ShapeDtypeStruct((s, R_HEADS, R_HEAD), F32),
                   jax.ShapeDtypeStruct(state.shape, F32)],
        compiler_params=_cparams(("parallel",)),
        name="rwkv_step",
    )(r, ld, k, v, kk, b, state)


def _select_topk(score, allowed, n_sel):
    bits = lax.bitcast_convert_type(score, I32)
    key = jnp.where(bits < 0, bits ^ jnp.int32(0x7FFFFFFF), bits)
    key = jnp.where(allowed, key, jnp.int32(INT_MIN))
    m = score.shape[0]

    def body(i, tau):
        cand = tau + lax.shift_left(jnp.int32(1), jnp.int32(31) - i)
        cnt = jnp.sum((key >= cand).astype(I32), axis=1, keepdims=True)
        return jnp.where(cnt >= n_sel, cand, tau)

    tau = lax.fori_loop(0, 32, body, jnp.full((m, 1), INT_MIN, I32))
    return jnp.logical_and(key >= tau, allowed)


def _dsa_prompt_kernel(n_sel, q_ref, iq_ref, iw_ref, ik_ref, k_ref, v_ref, o_ref, sc_ref):
    i = pl.program_id(1)
    tq = q_ref.shape[0]
    tk = k_ref.shape[0]
    ikb = ik_ref[...].astype(BF16)
    iw = iw_ref[...] * ((IDX_HEADS * IDX_DIM) ** -0.5)
    iq = iq_ref[...]
    for h in range(IDX_HEADS):
        d = _dot_nt(iq[:, h * IDX_DIM:(h + 1) * IDX_DIM], ikb)
        term = jnp.maximum(d, 0.0) * iw[:, h:h + 1]
        if h == 0:
            sc_ref[...] = term
        else:
            sc_ref[...] += term
    qpos = i * tq + lax.broadcasted_iota(I32, (tq, 1), 0)
    kpos = lax.broadcasted_iota(I32, (1, tk), 1)
    causal = kpos <= qpos
    sel = _select_topk(sc_ref[...], causal, n_sel)
    bias = jnp.where(sel, 0.0, NEG_BIG)
    q = q_ref[...]
    rep = A_HEADS // A_KV_HEADS
    for g in range(A_KV_HEADS):
        kg = k_ref[:, g * A_HEAD:(g + 1) * A_HEAD].astype(BF16)
        vg = v_ref[:, g * A_HEAD:(g + 1) * A_HEAD].astype(BF16)
        for rr in range(rep):
            h = g * rep + rr
            s = _dot_nt(q[:, h * A_HEAD:(h + 1) * A_HEAD], kg) + bias
            m = jnp.max(s, axis=1, keepdims=True)
            p = jnp.exp(s - m)
            l = jnp.sum(p, axis=1, keepdims=True)
            o_ref[:, h * A_HEAD:(h + 1) * A_HEAD] = _dot(p.astype(BF16), vg) / l


def dsa_prompt(q, iq, iw, ik, k, p, batch, tp, n_sel):
    nq = tp // Q_TILE
    qrow = lambda w: pl.BlockSpec((Q_TILE, w), lambda b, i: (b * nq + i, 0))
    keys = lambda w, blk: pl.BlockSpec((tp, w), lambda b, i: (b, blk))
    return pl.pallas_call(
        functools.partial(_dsa_prompt_kernel, n_sel),
        grid=(batch, nq),
        in_specs=[qrow(A_DIM), qrow(IDX_HEADS * IDX_DIM), qrow(IDX_HEADS),
                  keys(IDX_DIM, 0), keys(A_KV_DIM, 0), keys(A_KV_DIM, C_VA // A_KV_DIM)],
        out_specs=qrow(A_DIM),
        out_shape=jax.ShapeDtypeStruct((batch * tp, A_DIM), F32),
        scratch_shapes=[pltpu.VMEM((Q_TILE, tp), F32)],
        compiler_params=_cparams(("parallel", "parallel")),
        name="dsa_prompt",
    )(q, iq, iw, ik, k, p)


def _dsa_step_score_kernel(n_pages, page, pt_ref, iq_ref, iw_ref, ikn_ref, *refs):
    pages = refs[:n_pages]
    o_ref = refs[n_pages]
    iq = iq_ref[0]
    iw = iw_ref[0] * ((IDX_HEADS * IDX_DIM) ** -0.5)
    for j in range(n_pages):
        d = _dot_nt(iq, pages[j][...].astype(BF16))
        o_ref[0, :, j * page:(j + 1) * page] = jnp.sum(jnp.maximum(d, 0.0) * iw, axis=0,
                                                       keepdims=True)
    dn = jnp.sum(iq.astype(F32) * ikn_ref[0], axis=1, keepdims=True)
    sn = jnp.sum(jnp.maximum(dn, 0.0) * iw, axis=0, keepdims=True)
    lane = lax.broadcasted_iota(I32, (1, LANES), 1)
    o_ref[0, :, n_pages * page:] = jnp.where(lane == 0, sn, 0.0)


def dsa_step_scores(pt_flat, iq, iw, ik_new, cik2d, n_pages, page):
    s = iq.shape[0]
    kw = n_pages * page + LANES
    page_spec = lambda j: pl.BlockSpec((page, IDX_DIM), lambda i, pt: (pt[i * n_pages + j], 0))
    grid_spec = pltpu.PrefetchScalarGridSpec(
        num_scalar_prefetch=1,
        grid=(s,),
        in_specs=[pl.BlockSpec((1, IDX_HEADS, IDX_DIM), lambda i, pt: (i, 0, 0)),
                  pl.BlockSpec((1, IDX_HEADS, 1), lambda i, pt: (i, 0, 0)),
                  pl.BlockSpec((1, 1, IDX_DIM), lambda i, pt: (i, 0, 0))]
                 + [page_spec(j) for j in range(n_pages)],
        out_specs=pl.BlockSpec((1, 1, kw), lambda i, pt: (i, 0, 0)),
    )
    return pl.pallas_call(
        functools.partial(_dsa_step_score_kernel, n_pages, page),
        grid_spec=grid_spec,
        out_shape=jax.ShapeDtypeStruct((s, 1, kw), F32),
        compiler_params=_cparams(("arbitrary",)),
        name="dsa_step_scores",
    )(pt_flat, iq, iw, ik_new, *([cik2d] * n_pages))


def _dsa_step_select_kernel(n_sel, past, sc_ref, o_ref):
    sc = sc_ref[...]
    kpos = lax.broadcasted_iota(I32, sc.shape, 1)
    sel = _select_topk(sc, kpos <= past, n_sel)
    o_ref[...] = sel.astype(F32)


def dsa_step_select(sc, n_sel, past):
    return pl.pallas_call(
        functools.partial(_dsa_step_select_kernel, n_sel, past),
        out_shape=jax.ShapeDtypeStruct(sc.shape, F32),
        compiler_params=pltpu.CompilerParams(vmem_limit_bytes=VMEM_LIMIT),
        name="dsa_step_select",
    )(sc)


def _dsa_step_attn_kernel(n_pages, page, pt_ref, q_ref, kn_ref, vn_ref, sel_ref, ex_ref, *refs):
    kp = refs[:n_pages]
    vp = refs[n_pages:2 * n_pages]
    o_ref = refs[2 * n_pages]
    q = q_ref[0]
    rep = A_HEADS // A_KV_HEADS
    w2 = page * A_KV_HEADS
    hrow = lax.broadcasted_iota(I32, (A_HEADS, w2), 0)
    col = lax.broadcasted_iota(I32, (A_HEADS, w2), 1)
    own = (col % A_KV_HEADS) == (hrow // rep)
    ex = ex_ref[...]
    logits = []
    for j in range(n_pages):
        s = _dot_nt(q, kp[j][...].astype(BF16))
        selj = _dot(sel_ref[0, :, j * page:(j + 1) * page].astype(BF16), ex)
        logits.append(jnp.where(jnp.logical_and(selj > 0.5, own), s, NEG_BIG))
    h8 = lax.broadcasted_iota(I32, (A_HEADS, A_HEAD), 0)
    kn = jnp.where(h8 < rep, kn_ref[0, 0:1, :], kn_ref[0, 1:2, :])
    vn = jnp.where(h8 < rep, vn_ref[0, 0:1, :], vn_ref[0, 1:2, :])
    sn = jnp.sum(q.astype(F32) * kn, axis=1, keepdims=True)
    seln = sel_ref[0, :, n_pages * page:n_pages * page + 1]
    sn = jnp.where(seln > 0.5, sn, NEG_BIG)
    m = sn
    for s in logits:
        m = jnp.maximum(m, jnp.max(s, axis=1, keepdims=True))
    pn = jnp.exp(sn - m)
    l = pn
    acc = pn * vn
    for j in range(n_pages):
        p = jnp.exp(logits[j] - m)
        l = l + jnp.sum(p, axis=1, keepdims=True)
        acc = acc + _dot(p.astype(BF16), vp[j][...].astype(BF16))
    o_ref[0] = acc / l


def dsa_step_attn(pt_flat, q, k_new, v_new, sel, expand, ck2d, cv2d, n_pages, page):
    s = q.shape[0]
    kw = sel.shape[-1]
    w2 = page * A_KV_HEADS
    page_spec = lambda j: pl.BlockSpec((w2, A_HEAD), lambda i, pt: (pt[i * n_pages + j], 0))
    grid_spec = pltpu.PrefetchScalarGridSpec(
        num_scalar_prefetch=1,
        grid=(s,),
        in_specs=[pl.BlockSpec((1, A_HEADS, A_HEAD), lambda i, pt: (i, 0, 0)),
                  pl.BlockSpec((1, A_KV_HEADS, A_HEAD), lambda i, pt: (i, 0, 0)),
                  pl.BlockSpec((1, A_KV_HEADS, A_HEAD), lambda i, pt: (i, 0, 0)),
                  pl.BlockSpec((1, 1, kw), lambda i, pt: (i, 0, 0)),
                  pl.BlockSpec((page, w2), lambda i, pt: (0, 0))]
                 + [page_spec(j) for j in range(n_pages)] * 2,
        out_specs=pl.BlockSpec((1, A_HEADS, A_HEAD), lambda i, pt: (i, 0, 0)),
    )
    return pl.pallas_call(
        functools.partial(_dsa_step_attn_kernel, n_pages, page),
        grid_spec=grid_spec,
        out_shape=jax.ShapeDtypeStruct((s, A_HEADS, A_HEAD), F32),
        compiler_params=_cparams(("arbitrary",)),
        name="dsa_step_attn",
    )(pt_flat, q, k_new, v_new, sel, expand, *([ck2d] * n_pages), *([cv2d] * n_pages))


def _pack_bf16_pairs(x):
    w = x.shape[1] // 2
    hi = lax.bitcast_convert_type(x[:, :w].astype(BF16).astype(F32), I32)
    lo = lax.bitcast_convert_type(x[:, w:].astype(BF16).astype(F32), I32)
    return hi | lax.shift_right_logical(lo, 16)


def _unpack_bf16_pairs(p):
    hi = lax.bitcast_convert_type(p & jnp.int32(-65536), F32)
    lo = lax.bitcast_convert_type(lax.shift_left(p, 16), F32)
    return hi, lo


def _mix_kernel(alpha, x_ref, y_ref, bon_ref, g_ref, a_ref, e_ref, et_ref, gng_ref, gnb_ref,
                l0g_ref, l0b_ref, wo_ref, l1g_ref, l1b_ref, wr_ref,
                h_ref, sc_ref, pk_ref):
    e, et = e_ref[...], et_ref[...]
    y = y_ref[...]
    inv = 1.0 / R_HEAD
    mu = _dot(_dot(y, e, HIGHEST), et, HIGHEST) * inv
    d = y - mu
    var = _dot(_dot(d * d, e, HIGHEST), et, HIGHEST) * inv
    yn = d * lax.rsqrt(var + GN_EPS) * gng_ref[...] + gnb_ref[...]
    r_out = (yn + bon_ref[...]) * g_ref[...]
    mix = _dot(r_out.astype(BF16), wo_ref[:R_DIM, :]) + _dot(a_ref[...].astype(BF16), wo_ref[R_DIM:, :])
    h0 = _layer_norm(x_ref[...], l0g_ref[...], l0b_ref[...])
    h1 = _layer_norm(alpha * h0 + mix, l1g_ref[...], l1b_ref[...])
    h_ref[...] = h1
    sc_ref[...] = _sigmoid(_dot_nt(wr_ref[...], h1, HIGHEST))
    pk_ref[...] = _pack_bf16_pairs(h1)


def mix_ln1_router(x, y, bonus, g, a_out, pw, alpha):
    n, d = x.shape
    tm = ROW_TILE
    row = lambda w: pl.BlockSpec((tm, w), lambda i: (i, 0))
    full = lambda a: pl.BlockSpec(a.shape, lambda i: (0,) * a.ndim)
    params = (pw["e"], pw["et"], pw["gn_g"], pw["gn_b"], pw["ln0_g"], pw["ln0_b"], pw["w_out"],
              pw["ln1_g"], pw["ln1_b"], pw["w_router_t"])
    return pl.pallas_call(
        functools.partial(_mix_kernel, alpha),
        grid=(n // tm,),
        in_specs=[row(d), row(R_DIM), row(R_DIM), row(R_DIM), row(A_DIM)] + [full(a) for a in params],
        out_specs=[row(d), pl.BlockSpec((N_EXPERTS, tm), lambda i: (0, i)), row(d // 2)],
        out_shape=[jax.ShapeDtypeStruct((n, d), F32),
                   jax.ShapeDtypeStruct((N_EXPERTS, n), F32),
                   jax.ShapeDtypeStruct((n, d // 2), I32)],
        compiler_params=_cparams(("parallel",)),
        name="mix_ln1_router",
    )(x, y, bonus, g, a_out, *params)


def _route_kernel(sc_ref, bias_ref, idx_ref, gate_ref):
    scores = sc_ref[...]
    biased = scores + bias_ref[...]
    tn = scores.shape[1]
    per = N_EXPERTS // N_EXPERT_GROUPS
    sub = lax.broadcasted_iota(I32, (per, tn), 0)
    grp_rows = []
    for g in range(N_EXPERT_GROUPS):
        xg = biased[g * per:(g + 1) * per, :]
        m1 = jnp.max(xg, axis=0, keepdims=True)
        first = jnp.min(jnp.where(xg == m1, sub, per), axis=0, keepdims=True)
        m2 = jnp.max(jnp.where(sub == first, -jnp.inf, xg), axis=0, keepdims=True)
        grp_rows.append(m1 + m2)
    grp = jnp.concatenate(grp_rows, axis=0)
    gi = lax.broadcasted_iota(I32, (N_EXPERT_GROUPS, tn), 0)
    gsel = jnp.zeros((N_EXPERT_GROUPS, tn), jnp.bool_)
    for _ in range(TOPK_GROUPS):
        m = jnp.max(grp, axis=0, keepdims=True)
        first = jnp.min(jnp.where(grp == m, gi, N_EXPERT_GROUPS), axis=0, keepdims=True)
        hit = gi == first
        gsel = jnp.logical_or(gsel, hit)
        grp = jnp.where(hit, -jnp.inf, grp)
    ei = lax.broadcasted_iota(I32, (N_EXPERTS, tn), 0)
    emask = jnp.concatenate(
        [jnp.broadcast_to(gsel[g:g + 1, :], (per, tn)) for g in range(N_EXPERT_GROUPS)], axis=0)
    cand = jnp.where(emask, biased, -jnp.inf)
    idxs, gates = [], []
    for _ in range(TOP_K):
        m = jnp.max(cand, axis=0, keepdims=True)
        first = jnp.min(jnp.where(cand == m, ei, N_EXPERTS), axis=0, keepdims=True)
        hit = ei == first
        idxs.append(first)
        gates.append(jnp.sum(jnp.where(hit, scores, 0.0), axis=0, keepdims=True))
        cand = jnp.where(hit, -jnp.inf, cand)
    gate = jnp.concatenate(gates, axis=0)
    gate = gate / jnp.sum(gate, axis=0, keepdims=True) * ROUTED_SCALE
    idx_ref[...] = jnp.concatenate(idxs, axis=0)
    gate_ref[...] = gate


def route(scores_t, e_bias):
    n = scores_t.shape[1]
    tn = ROW_TILE
    return pl.pallas_call(
        _route_kernel,
        grid=(n // tn,),
        in_specs=[pl.BlockSpec((N_EXPERTS, tn), lambda i: (0, i)),
                  pl.BlockSpec((N_EXPERTS, 1), lambda i: (0, 0))],
        out_specs=[pl.BlockSpec((TOP_K, tn), lambda i: (0, i))] * 2,
        out_shape=[jax.ShapeDtypeStruct((TOP_K, n), I32), jax.ShapeDtypeStruct((TOP_K, n), F32)],
        compiler_params=_cparams(("parallel",)),
        name="route",
    )(scores_t, e_bias)


def _dispatch_kernel(dest_ref, x_ref, init_ref, o_ref, sem):
    del init_ref
    tm = x_ref.shape[0]

    def copy(i, j):
        return pltpu.make_async_copy(x_ref.at[pl.ds(i, 1)],
                                     o_ref.at[pl.ds(dest_ref[i * TOP_K + j], 1)], sem)

    def start(i, carry):
        for j in range(TOP_K):
            @pl.when(dest_ref[i * TOP_K + j] >= 0)
            def _():
                copy(i, j).start()
        return carry

    def wait(i, carry):
        for j in range(TOP_K):
            @pl.when(dest_ref[i * TOP_K + j] >= 0)
            def _():
                copy(i, j).wait()
        return carry

    lax.fori_loop(0, tm, start, 0)
    lax.fori_loop(0, tm, wait, 0)


def moe_dispatch(dest_flat, xpk, rows_total):
    n, w = xpk.shape
    tm = ROW_TILE
    init = jnp.zeros((rows_total, w), I32)
    return pl.pallas_call(
        _dispatch_kernel,
        grid=(n // tm,),
        in_specs=[pl.BlockSpec((tm * TOP_K,), lambda i: (i,), memory_space=pltpu.SMEM),
                  pl.BlockSpec((tm, w), lambda i: (i, 0)),
                  pl.BlockSpec(memory_space=pl.ANY)],
        out_specs=pl.BlockSpec(memory_space=pl.ANY),
        out_shape=jax.ShapeDtypeStruct((rows_total, w), I32),
        scratch_shapes=[pltpu.SemaphoreType.DMA(())],
        input_output_aliases={2: 0},
        compiler_params=_cparams(("arbitrary",)),
        name="moe_dispatch",
    )(dest_flat, xpk, init)


def _experts_kernel(be_ref, nu_ref, x_ref, wg_ref, wu_ref, wd_ref, o_ref, wg_s, wu_s, wd_s):
    i = pl.program_id(0)
    used = i < nu_ref[0]
    prev = be_ref[jnp.maximum(i - 1, 0)]
    fresh = jnp.logical_or(i == 0, be_ref[i] != prev)

    @pl.when(jnp.logical_and(used, fresh))
    def _():
        wg_s[...] = wg_ref[0].astype(BF16)
        wu_s[...] = wu_ref[0].astype(BF16)
        wd_s[...] = wd_ref[0].astype(BF16)

    @pl.when(used)
    def _():
        hi, lo = _unpack_bf16_pairs(x_ref[...])
        hi, lo = hi.astype(BF16), lo.astype(BF16)
        half = hi.shape[1]
        gp = _dot(hi, wg_s[:half, :]) + _dot(lo, wg_s[half:, :])
        up = _dot(hi, wu_s[:half, :]) + _dot(lo, wu_s[half:, :])
        act = gp * _sigmoid(gp) * up
        o_ref[...] = _pack_bf16_pairs(_dot(act.astype(BF16), wd_s[...]))

    @pl.when(jnp.logical_not(used))
    def _():
        o_ref[...] = jnp.zeros_like(o_ref)


def moe_experts(blk_e, n_used, xs, w_gate, w_up, w_down):
    rows, w = xs.shape
    nb = rows // EXPERT_TILE
    _, d, de = w_gate.shape
    last = lambda i, nu: jnp.minimum(i, nu[0] - 1)
    grid_spec = pltpu.PrefetchScalarGridSpec(
        num_scalar_prefetch=2,
        grid=(nb,),
        in_specs=[pl.BlockSpec((EXPERT_TILE, w), lambda i, be, nu: (last(i, nu), 0)),
                  pl.BlockSpec((1, d, de), lambda i, be, nu: (be[last(i, nu)], 0, 0)),
                  pl.BlockSpec((1, d, de), lambda i, be, nu: (be[last(i, nu)], 0, 0)),
                  pl.BlockSpec((1, de, d), lambda i, be, nu: (be[last(i, nu)], 0, 0))],
        out_specs=pl.BlockSpec((EXPERT_TILE, w), lambda i, be, nu: (i, 0)),
        scratch_shapes=[pltpu.VMEM((d, de), BF16), pltpu.VMEM((d, de), BF16),
                        pltpu.VMEM((de, d), BF16)],
    )
    return pl.pallas_call(
        _experts_kernel,
        grid_spec=grid_spec,
        out_shape=jax.ShapeDtypeStruct((rows, w), I32),
        compiler_params=_cparams(("arbitrary",)),
        name="moe_experts",
    )(blk_e, n_used, xs, w_gate, w_up, w_down)


def _combine_kernel(alpha, dest_ref, h_ref, gate_ref, wsg_ref, wsu_ref, wsd_ref, l2g_ref, l2b_ref,
                    ys_ref, o_ref, buf, sem):
    tm = h_ref.shape[0]

    def copy(i, j):
        return pltpu.make_async_copy(ys_ref.at[pl.ds(dest_ref[i * TOP_K + j], 1)],
                                     buf.at[j, pl.ds(i, 1)], sem)

    def start(i, carry):
        for j in range(TOP_K):
            copy(i, j).start()
        return carry

    def wait(i, carry):
        for j in range(TOP_K):
            copy(i, j).wait()
        return carry

    lax.fori_loop(0, tm, start, 0)
    h = h_ref[...]
    hb = h.astype(BF16)
    gp = _dot(hb, wsg_ref[...])
    up = _dot(hb, wsu_ref[...])
    shared = _dot((gp * _sigmoid(gp) * up).astype(BF16), wsd_ref[...])
    lax.fori_loop(0, tm, wait, 0)
    gate = gate_ref[...]
    half = buf.shape[2]
    acc_hi = jnp.zeros((tm, half), F32)
    acc_lo = jnp.zeros((tm, half), F32)
    for j in range(TOP_K):
        hi, lo = _unpack_bf16_pairs(buf[j])
        gj = gate[:, j:j + 1]
        acc_hi = acc_hi + gj * hi
        acc_lo = acc_lo + gj * lo
    f = jnp.concatenate([acc_hi, acc_lo], axis=1) + shared
    o_ref[...] = _layer_norm(alpha * h + f, l2g_ref[...], l2b_ref[...])


def moe_combine(dest_flat, h1, gate, ys, pw, alpha):
    n, d = h1.shape
    tm = Q_TILE
    full = lambda a: pl.BlockSpec(a.shape, lambda i: (0,) * a.ndim)
    params = (pw["ws_gate"], pw["ws_up"], pw["ws_down"], pw["ln2_g"], pw["ln2_b"])
    return pl.pallas_call(
        functools.partial(_combine_kernel, alpha),
        grid=(n // tm,),
        in_specs=[pl.BlockSpec((tm * TOP_K,), lambda i: (i,), memory_space=pltpu.SMEM),
                  pl.BlockSpec((tm, d), lambda i: (i, 0)),
                  pl.BlockSpec((tm, TOP_K), lambda i: (i, 0))]
                 + [full(a) for a in params]
                 + [pl.BlockSpec(memory_space=pl.ANY)],
        out_specs=pl.BlockSpec((tm, d), lambda i: (i, 0)),
        out_shape=jax.ShapeDtypeStruct((n, d), F32),
        scratch_shapes=[pltpu.VMEM((TOP_K, tm, d // 2), I32), pltpu.SemaphoreType.DMA(())],
        compiler_params=_cparams(("arbitrary",)),
        name="moe_combine",
    )(dest_flat, h1, gate, *params, ys)


def _round_up(x, m):
    return (x + m - 1) // m * m


def _rope_tables(pos, head):
    half = head // 2
    inv = ROPE_THETA ** (-jnp.arange(half, dtype=F32) / half)
    ang = pos.astype(F32)[:, None] * inv[None, :]
    cos, sin = jnp.cos(ang), jnp.sin(ang)
    rep = LANES // head
    c = jnp.tile(jnp.concatenate([cos, cos], axis=1), (1, rep))
    s = jnp.tile(jnp.concatenate([-sin, sin], axis=1), (1, rep))
    return c, s


def _permute_cols(m):
    a0 = SHIFT_DIM
    pieces = [
        m[..., 0:3 * R_DIM],
        m[..., a0:a0 + A_DIM],
        m[..., a0 + A_DIM + 2 * A_KV_DIM:a0 + A_DIM + 2 * A_KV_DIM + IDX_HEADS * IDX_DIM],
        m[..., a0 + A_DIM:a0 + A_DIM + 2 * A_KV_DIM],
    ]
    i0 = a0 + A_DIM + 2 * A_KV_DIM + IDX_HEADS * IDX_DIM
    pieces.append(m[..., i0:i0 + IDX_DIM + IDX_HEADS])
    pad = lambda w: jnp.zeros(m.shape[:-1] + (w,), m.dtype)
    pieces.append(pad(LANES - IDX_DIM - IDX_HEADS))
    pieces.append(m[..., 3 * R_DIM:SHIFT_DIM])
    pieces.append(pad(LORA_W - (SHIFT_DIM - 3 * R_DIM)))
    return jnp.concatenate(pieces, axis=-1)


def kernel(x_prompt, x_sample, cache_k, cache_v, cache_idx_k, state_wkv, state_shift, page_table,
           meta, ln0_g, ln0_b, w_in, mu_shift, w0, w_b, a0, a_b, g_b, k_k, k_a, r_k, gn_g, gn_b,
           w_out, ln1_g, ln1_b, w_router, e_bias, w_gate, w_up, w_down, ws_gate, ws_up, ws_down,
           ln2_g, ln2_b):
    depth = w_in.shape[0]
    assert depth == 1, "single trunk layer"
    bsz, s_p, d = x_prompt.shape
    s_dec, s_s, _ = x_sample.shape
    assert s_s == 1, "one decode token per sequence"
    t_real = N_META + s_p
    tp = _round_up(t_real, LANES)
    assert (bsz * tp) % ROW_TILE == 0
    sp = _round_up(s_dec, ROW_TILE)
    n_prompt = bsz * tp
    n = n_prompt + sp
    n_pool, page = cache_k.shape[1], cache_k.shape[2]
    n_pages = page_table.shape[1]
    past = n_pages * page
    alpha = float((2 * depth) ** 0.25)
    row2 = lambda a: a.reshape(1, -1)

    meta_rows = jnp.broadcast_to(meta[None], (bsz, N_META, d))
    xp = jnp.concatenate([meta_rows, x_prompt, jnp.zeros((bsz, tp - t_real, d), F32)], axis=1)
    x_all = jnp.concatenate([xp.reshape(n_prompt, d), x_sample.reshape(s_dec, d),
                             jnp.zeros((sp - s_dec, d), F32)], axis=0)
    pos = jnp.concatenate([jnp.tile(jnp.arange(tp), bsz), jnp.full((sp,), past)])
    c128, s128 = _rope_tables(pos, A_HEAD)
    c64, s64 = _rope_tables(pos, IDX_DIM)

    w_in_k = _permute_cols(w_in[0]).astype(BF16)
    mu_k = _permute_cols(
        jnp.concatenate([mu_shift[0], jnp.zeros((w_in.shape[2] - SHIFT_DIM,), F32)])[None, :])
    head_of = jnp.arange(R_DIM) // R_HEAD
    e_mat = (head_of[:, None] == jnp.arange(R_HEADS)[None, :]).astype(F32)
    zpad = lambda a, rows_before, rows_total: jnp.concatenate(
        [jnp.zeros((rows_before, a.shape[1]), a.dtype), a,
         jnp.zeros((rows_total - rows_before - a.shape[0], a.shape[1]), a.dtype)], axis=0)
    pw = {
        "mu_x": mu_k[:, :3 * R_DIM], "mu_lo": mu_k[:, C_LORA:],
        "w0": row2(w0[0]), "a0": row2(a0[0]), "k_k": row2(k_k[0]), "k_a": row2(k_a[0]),
        "r_k": row2(r_k[0]), "gn_g": row2(gn_g[0]), "gn_b": row2(gn_b[0]),
        "w_b": zpad(w_b[0], 0, LANES).astype(BF16),
        "a_b": zpad(a_b[0], D_DECAY_LORA, LANES).astype(BF16),
        "g_b": zpad(g_b[0], 0, LORA_W - LANES).astype(BF16),
        "e": e_mat, "et": e_mat.T,
        "ln0_g": row2(ln0_g), "ln0_b": row2(ln0_b),
        "ln1_g": row2(ln1_g[0]), "ln1_b": row2(ln1_b[0]),
        "ln2_g": row2(ln2_g[0]), "ln2_b": row2(ln2_b[0]),
        "w_out": w_out[0].astype(BF16), "w_router_t": w_router[0].T,
        "ws_gate": ws_gate[0].astype(BF16), "ws_up": ws_up[0].astype(BF16),
        "ws_down": ws_down[0].astype(BF16),
    }

    p = ln_proj(x_all, pw["ln0_g"], pw["ln0_b"], w_in_k, tn=P_COLS // 3)
    q_r, iq_r, k_r, ik_r, iw = rope_all(p, c128, s128, c64, s64)

    pre_p = rwkv_pre(p, 0, n_prompt, None, pw, t_real, tp)
    shift_k = _permute_cols(jnp.concatenate(
        [state_shift[0], jnp.zeros((s_dec, w_in.shape[2] - SHIFT_DIM), F32)], axis=1))
    shift_k = jnp.concatenate([shift_k, jnp.zeros((sp - s_dec, P_COLS), F32)], axis=0)
    pre_s = rwkv_pre(p, n_prompt, sp, (shift_k[:, :3 * R_DIM], shift_k[:, C_LORA:]), pw, t_real, tp)
    r_p, ld_p, k_p, v_p, kk_p, b_p, g_p, bon_p = pre_p
    r_s, ld_s, k_s, v_s, kk_s, b_s, g_s, bon_s = pre_s
    y_p, wkv_p = rwkv_scan(r_p, ld_p, k_p, v_p, kk_p, b_p, bsz, tp)
    heads = lambda a: a[:s_dec].reshape(s_dec, R_HEADS, R_HEAD)
    y_s, wkv_s = rwkv_step(heads(r_s), heads(ld_s), heads(k_s), heads(v_s), heads(kk_s), heads(b_s),
                           state_wkv[0])
    y_s = jnp.concatenate([y_s.reshape(s_dec, R_DIM), jnp.zeros((sp - s_dec, R_DIM), F32)], axis=0)

    n_sel_p = min(TOPK_KEYS, t_real // 4)
    a_p = dsa_prompt(q_r, iq_r, iw, ik_r, k_r, p, bsz, tp, n_sel_p)
    n_sel_s = min(TOPK_KEYS, (past + 1) // 4)
    pt_flat = page_table.reshape(-1).astype(I32)
    srow = slice(n_prompt, n_prompt + s_dec)
    sc_s = dsa_step_scores(pt_flat, iq_r[srow].reshape(s_dec, IDX_HEADS, IDX_DIM),
                           iw[srow].reshape(s_dec, IDX_HEADS, 1), ik_r[srow].reshape(s_dec, 1, IDX_DIM),
                           cache_idx_k[0].reshape(n_pool * page, IDX_DIM), n_pages, page)
    sel_s = dsa_step_select(sc_s.reshape(s_dec, -1), n_sel_s, past).reshape(sc_s.shape)
    slot = jnp.arange(page)[:, None]
    expand = (jnp.arange(page * A_KV_HEADS)[None, :] // A_KV_HEADS == slot).astype(BF16)
    a_s = dsa_step_attn(pt_flat, q_r[srow].reshape(s_dec, A_HEADS, A_HEAD),
                        k_r[srow].reshape(s_dec, A_KV_HEADS, A_HEAD),
                        p[srow, C_VA:C_VA + A_KV_DIM].reshape(s_dec, A_KV_HEADS, A_HEAD),
                        sel_s, expand,
                        cache_k[0].reshape(n_pool * page * A_KV_HEADS, A_HEAD),
                        cache_v[0].reshape(n_pool * page * A_KV_HEADS, A_HEAD), n_pages, page)
    a_all = jnp.concatenate([a_p, a_s.reshape(s_dec, A_DIM), jnp.zeros((sp - s_dec, A_DIM), F32)], axis=0)

    cat = lambda u, w: jnp.concatenate([u, w], axis=0)
    h1, scores_t, xpk = mix_ln1_router(x_all, cat(y_p, y_s), cat(bon_p, bon_s), cat(g_p, g_s), a_all,
                                       pw, alpha)
    eidx_t, gate_t = route(scores_t, e_bias[0].reshape(N_EXPERTS, 1))

    row_id = jnp.arange(n)
    tok_ok = jnp.where(row_id < n_prompt, (row_id % tp) < t_real, row_id < n_prompt + s_dec)
    eidx = eidx_t.T
    gate = jnp.where(tok_ok[:, None], gate_t.T, 0.0)
    onehot = jnp.logical_and(eidx[:, :, None] == jnp.arange(N_EXPERTS)[None, None, :],
                             tok_ok[:, None, None]).any(axis=1).astype(I32)
    counts = jnp.sum(onehot, axis=0)
    rank = jnp.cumsum(onehot, axis=0) - onehot
    padded = (counts + EXPERT_TILE - 1) // EXPERT_TILE * EXPERT_TILE
    seg_end = jnp.cumsum(padded)
    seg_start = seg_end - padded
    dest = seg_start[eidx] + jnp.take_along_axis(rank, eidx, axis=1)
    nb = (n * TOP_K + N_EXPERTS * (EXPERT_TILE - 1)) // EXPERT_TILE + 1
    rows_total = nb * EXPERT_TILE
    dest_w = jnp.where(tok_ok[:, None], dest, -1).astype(I32).reshape(-1)
    dest_r = jnp.where(tok_ok[:, None], dest, 0).astype(I32).reshape(-1)
    blk_e = jnp.minimum(jnp.searchsorted(seg_end, jnp.arange(nb) * EXPERT_TILE, side="right"),
                        N_EXPERTS - 1).astype(I32)
    n_used = (seg_end[-1] // EXPERT_TILE).astype(I32).reshape(1)

    xs = moe_dispatch(dest_w, xpk, rows_total)
    ys = moe_experts(blk_e, n_used, xs, w_gate[0], w_up[0], w_down[0])
    h2 = moe_combine(dest_r, h1, gate, ys, pw, alpha)

    def prompt_rows(a):
        return a[:n_prompt].reshape(bsz, tp, -1)[:, :t_real]

    y_prompt = prompt_rows(h2)[:, N_META:]
    y_sample = h2[srow].reshape(s_dec, 1, d)
    k_prompt = prompt_rows(k_r).reshape(1, bsz, t_real, A_KV_HEADS, A_HEAD)
    v_prompt = prompt_rows(p[:, C_VA:C_VA + A_KV_DIM]).reshape(1, bsz, t_real, A_KV_HEADS, A_HEAD)
    ik_prompt = prompt_rows(ik_r)[None]
    last = jnp.arange(bsz) * tp + t_real - 1
    unperm = lambda rows: jnp.concatenate([rows[:, :3 * R_DIM],
                                           rows[:, C_LORA:C_LORA + SHIFT_DIM - 3 * R_DIM]], axis=1)
    shift_prompt = unperm(p[last])[None]
    k_sample = k_r[srow].reshape(1, s_dec, 1, A_KV_HEADS, A_HEAD)
    v_sample = p[srow, C_VA:C_VA + A_KV_DIM].reshape(1, s_dec, 1, A_KV_HEADS, A_HEAD)
    ik_sample = ik_r[srow].reshape(1, s_dec, 1, IDX_DIM)
    shift_sample = unperm(p[srow])[None]
    return (y_prompt, y_sample, k_prompt, v_prompt, ik_prompt, wkv_p[None], shift_prompt,
            k_sample, v_sample, ik_sample, wkv_s[None], shift_sample)
```

```python
import functools

import numpy as np
import jax
import jax.numpy as jnp
from jax import lax
from jax.experimental import pallas as pl
from jax.experimental.pallas import tpu as pltpu

F32 = jnp.float32
BF16 = jnp.bfloat16
I32 = jnp.int32
HIGHEST = lax.Precision.HIGHEST

N_META = 16
R_HEADS, R_HEAD = 16, 64
R_DIM = R_HEADS * R_HEAD
D_DECAY_LORA, D_AAA_LORA, D_GATE_LORA = 64, 64, 160
SHIFT_DIM = 3 * R_DIM + D_DECAY_LORA + D_AAA_LORA + D_GATE_LORA
GN_EPS = 64e-5
A_HEADS, A_KV_HEADS, A_HEAD = 8, 2, 128
A_DIM = A_HEADS * A_HEAD
A_KV_DIM = A_KV_HEADS * A_HEAD
IDX_HEADS, IDX_DIM = 16, 64
TOPK_KEYS = 256
ROPE_THETA = 10000.0
N_EXPERTS, N_EXPERT_GROUPS, TOPK_GROUPS, TOP_K = 64, 8, 4, 8
ROUTED_SCALE = 2.5
LN_EPS = 1e-5

LANES = 128
SUBLANES = 8
ROW_TILE = 256
Q_TILE = 128
CHUNK = 64
EXPERT_TILE = 256
VMEM_LIMIT = 56 * 1024 * 1024
NEG_BIG = -1e30
INT_MIN = -2 ** 31

C_R, C_K, C_V = 0, R_DIM, 2 * R_DIM
C_Q = 3 * R_DIM
C_IQ = C_Q + A_DIM
C_KA = C_IQ + IDX_HEADS * IDX_DIM
C_VA = C_KA + A_KV_DIM
C_IK = C_VA + A_KV_DIM
C_LORA = C_IK + LANES
LORA_W = 384
P_COLS = C_LORA + LORA_W


def _cparams(sem):
    return pltpu.CompilerParams(dimension_semantics=sem, vmem_limit_bytes=VMEM_LIMIT)


def _dot(a, b, precision=None):
    return jnp.dot(a, b, preferred_element_type=F32, precision=precision)


def _dot_nt(a, b, precision=None):
    return lax.dot_general(a, b, (((1,), (1,)), ((), ())), preferred_element_type=F32,
                           precision=precision)


def _dot_tn(a, b, precision=None):
    return lax.dot_general(a, b, (((0,), (0,)), ((), ())), preferred_element_type=F32,
                           precision=precision)


def _layer_norm(x, g, b):
    mu = jnp.mean(x, axis=-1, keepdims=True)
    xc = x - mu
    var = jnp.mean(xc * xc, axis=-1, keepdims=True)
    return xc * lax.rsqrt(var + LN_EPS) * g + b


def _sigmoid(z):
    return 1.0 / (1.0 + jnp.exp(-z))


def _ln_proj_kernel(x_ref, g_ref, b_ref, w_ref, o_ref):
    h = _layer_norm(x_ref[...], g_ref[...], b_ref[...])
    o_ref[...] = _dot(h.astype(BF16), w_ref[...])


def ln_proj(x, g, b, w_bf16, tn):
    n, d = x.shape
    cols = w_bf16.shape[1]
    return pl.pallas_call(
        _ln_proj_kernel,
        grid=(cols // tn, n // ROW_TILE),
        in_specs=[
            pl.BlockSpec((ROW_TILE, d), lambda j, i: (i, 0)),
            pl.BlockSpec((1, d), lambda j, i: (0, 0)),
            pl.BlockSpec((1, d), lambda j, i: (0, 0)),
            pl.BlockSpec((d, tn), lambda j, i: (0, j)),
        ],
        out_specs=pl.BlockSpec((ROW_TILE, tn), lambda j, i: (i, j)),
        out_shape=jax.ShapeDtypeStruct((n, cols), F32),
        compiler_params=_cparams(("parallel", "parallel")),
        name="ln_proj",
    )(x, g, b, w_bf16)


def _rot_half(x, head):
    w = x.shape[-1]
    half = head // 2
    lane = lax.broadcasted_iota(I32, x.shape, 1)
    left = pltpu.roll(x, w - half, axis=1)
    right = pltpu.roll(x, half, axis=1)
    return jnp.where((lane % head) < half, left, right)


def _rope_kernel(q_ref, iq_ref, ka_ref, ikw_ref, c128_ref, s128_ref, c64_ref, s64_ref,
                 qo_ref, iqo_ref, ko_ref, iko_ref, iwo_ref):
    c128, s128 = c128_ref[...], s128_ref[...]
    c64, s64 = c64_ref[...], s64_ref[...]

    def rope(x, head, c, s):
        rep = x.shape[-1] // LANES
        if rep > 1:
            c = jnp.concatenate([c] * rep, axis=1)
            s = jnp.concatenate([s] * rep, axis=1)
        return x * c + _rot_half(x, head) * s

    q = rope(q_ref[...], A_HEAD, c128, s128)
    qo_ref[...] = (q * (A_HEAD ** -0.5)).astype(BF16)
    iqo_ref[...] = rope(iq_ref[...], IDX_DIM, c64, s64).astype(BF16)
    ko_ref[...] = rope(ka_ref[...], A_HEAD, c128, s128)
    ikw = ikw_ref[...]
    ik = rope(ikw, IDX_DIM, c64, s64)
    iko_ref[...] = ik[:, :IDX_DIM]
    iwo_ref[...] = ikw[:, IDX_DIM:IDX_DIM + IDX_HEADS]


def rope_all(p, c128, s128, c64, s64):
    n = p.shape[0]
    tm = ROW_TILE
    row = lambda w, blk: pl.BlockSpec((tm, w), lambda i: (i, blk))
    return pl.pallas_call(
        _rope_kernel,
        grid=(n // tm,),
        in_specs=[row(A_DIM, C_Q // A_DIM), row(A_DIM, C_IQ // A_DIM),
                  row(A_KV_DIM, C_KA // A_KV_DIM), row(LANES, C_IK // LANES),
                  row(LANES, 0), row(LANES, 0), row(LANES, 0), row(LANES, 0)],
        out_specs=[row(A_DIM, 0), row(A_DIM, 0), row(A_KV_DIM, 0),
                   row(IDX_DIM, 0), row(IDX_HEADS, 0)],
        out_shape=[jax.ShapeDtypeStruct((n, A_DIM), BF16),
                   jax.ShapeDtypeStruct((n, IDX_HEADS * IDX_DIM), BF16),
                   jax.ShapeDtypeStruct((n, A_KV_DIM), F32),
                   jax.ShapeDtypeStruct((n, IDX_DIM), F32),
                   jax.ShapeDtypeStruct((n, IDX_HEADS), F32)],
        compiler_params=_cparams(("parallel",)),
        name="rope",
    )(p, p, p, p, c128, s128, c64, s64)


def _rwkv_pre_kernel(t_real, tp, from_rows, *refs):
    (x_ref, lo_ref, px_ref, plo_ref, mu_ref, mulo_ref, w0_ref, wb_ref, a0_ref, ab_ref,
     gb_ref, kk_ref, ka_ref, rk_ref, e_ref, et_ref,
     r_o, ld_o, k_o, v_o, kk_o, b_o, g_o, bon_o) = refs
    x = x_ref[...]
    lo = lo_ref[...]
    tm = x.shape[0]
    if from_rows:
        i = pl.program_id(0)
        row = lax.broadcasted_iota(I32, (tm, 1), 0)
        t = (i * tm + row) % tp
        first = row == 0
        sx = jnp.where(first, px_ref[SUBLANES - 1:SUBLANES, :], pltpu.roll(x, 1, axis=0))
        slo = jnp.where(first, plo_ref[SUBLANES - 1:SUBLANES, :], pltpu.roll(lo, 1, axis=0))
        sx = jnp.where(t == 0, 0.0, sx)
        slo = jnp.where(t == 0, 0.0, slo)
        live = t < t_real
    else:
        sx = px_ref[...]
        slo = plo_ref[...]
        live = None
    xx = x + (sx - x) * mu_ref[...]
    xlo = lo + (slo - lo) * mulo_ref[...]
    r = xx[:, C_R:C_R + R_DIM]
    k = xx[:, C_K:C_K + R_DIM]
    v = xx[:, C_V:C_V + R_DIM]
    wa = xlo[:, :LANES]
    xg = xlo[:, LANES:]
    z = w0_ref[...] + _dot(jnp.tanh(wa).astype(BF16), wb_ref[...])
    nz = -z
    softplus = jnp.maximum(nz, 0.0) + jnp.log(1.0 + jnp.exp(-jnp.abs(nz)))
    logd = -jnp.exp(-softplus - 0.5)
    a = _sigmoid(a0_ref[...] + _dot(wa.astype(BF16), ab_ref[...]))
    g = _dot(_sigmoid(xg).astype(BF16), gb_ref[...])
    e, et = e_ref[...], et_ref[...]
    kkr = k * kk_ref[...]
    ss = _dot(_dot(kkr * kkr, e, HIGHEST), et, HIGHEST)
    kk = kkr / jnp.maximum(jnp.sqrt(ss), 1e-12)
    k2 = k * (1.0 + (a - 1.0) * ka_ref[...])
    bonus = _dot(_dot(r * k2 * rk_ref[...], e, HIGHEST), et, HIGHEST) * v
    b = kk * a
    if live is not None:
        zero = lambda y: jnp.where(live, y, 0.0)
        logd, k2s, vs, kk, b = zero(logd), zero(k2), zero(v), zero(kk), zero(b)
    else:
        k2s, vs = k2, v
    r_o[...] = r
    ld_o[...] = logd
    k_o[...] = k2s
    v_o[...] = vs
    kk_o[...] = kk
    b_o[...] = b
    g_o[...] = g
    bon_o[...] = bonus


def rwkv_pre(p, row0, nrows, prev, pw, t_real, tp):
    tm = min(ROW_TILE, nrows)
    blk0 = row0 // tm
    from_rows = prev is None
    xw = 3 * R_DIM
    cur_x = pl.BlockSpec((tm, xw), lambda i: (blk0 + i, 0))
    cur_lo = pl.BlockSpec((tm, LORA_W), lambda i: (blk0 + i, C_LORA // LORA_W))
    if from_rows:
        r8 = tm // SUBLANES
        prev_x = pl.BlockSpec((SUBLANES, xw), lambda i: (jnp.maximum((blk0 + i) * r8 - 1, 0), 0))
        prev_lo = pl.BlockSpec((SUBLANES, LORA_W),
                               lambda i: (jnp.maximum((blk0 + i) * r8 - 1, 0), C_LORA // LORA_W))
        prev_args = (p, p)
    else:
        prev_x = pl.BlockSpec((tm, xw), lambda i: (i, 0))
        prev_lo = pl.BlockSpec((tm, LORA_W), lambda i: (i, 0))
        prev_args = prev
    full = lambda a: pl.BlockSpec(a.shape, lambda i: (0,) * a.ndim)
    params = (pw["mu_x"], pw["mu_lo"], pw["w0"], pw["w_b"], pw["a0"], pw["a_b"], pw["g_b"],
              pw["k_k"], pw["k_a"], pw["r_k"], pw["e"], pw["et"])
    out = pl.BlockSpec((tm, R_DIM), lambda i: (i, 0))
    return pl.pallas_call(
        functools.partial(_rwkv_pre_kernel, t_real, tp, from_rows),
        grid=(nrows // tm,),
        in_specs=[cur_x, cur_lo, prev_x, prev_lo] + [full(a) for a in params],
        out_specs=[out] * 8,
        out_shape=[jax.ShapeDtypeStruct((nrows, R_DIM), F32)] * 8,
        compiler_params=_cparams(("parallel",)),
        name="rwkv_pre_rows" if from_rows else "rwkv_pre_step",
    )(p, p, *prev_args, *params)


def _rwkv_scan_kernel(r_ref, ld_ref, k_ref, v_ref, kk_ref, b_ref, y_ref, s_ref, ss_scr):
    c = pl.program_id(1)

    @pl.when(c == 0)
    def _():
        ss_scr[...] = jnp.zeros_like(ss_scr)

    n = CHUNK
    n2 = 2 * n
    pairs = R_HEADS // 2
    ld_all = ld_ref[...]
    ri = lax.broadcasted_iota(I32, (n, n), 0)
    ci = lax.broadcasted_iota(I32, (n, n), 1)
    cum_all = _dot((ci <= ri).astype(F32), ld_all, HIGHEST)
    head0 = lax.broadcasted_iota(I32, (n, LANES), 1) < R_HEAD
    r4 = lax.broadcasted_iota(I32, (2 * n2, 2 * n2), 0)
    c4 = lax.broadcasted_iota(I32, (2 * n2, 2 * n2), 1)
    tri = (c4 % n) < (r4 % n) + jnp.where(r4 < n2, 0, 1)
    re = lax.broadcasted_iota(I32, (n2, n2), 0)
    ce = lax.broadcasted_iota(I32, (n2, n2), 1)
    eye = (re == ce).astype(F32)

    def stack(x):
        return jnp.concatenate([jnp.where(head0, x, 0.0), jnp.where(head0, 0.0, x)], axis=0)

    ar, bk, v2, ss, e_last = [], [], [], [], []
    for p in range(pairs):
        sl = slice(p * LANES, (p + 1) * LANES)
        cum, ld = cum_all[:, sl], ld_all[:, sl]
        e_pos = jnp.exp(cum)
        e_neg = jnp.exp(-cum)
        at = -kk_ref[:, sl] * jnp.exp(cum - ld)
        ar.append(jnp.concatenate([stack(at), stack(r_ref[:, sl] * e_pos)], axis=0).astype(BF16))
        bk.append(jnp.concatenate([stack(b_ref[:, sl] * e_neg), stack(k_ref[:, sl] * e_neg)],
                                  axis=0).astype(BF16))
        v2.append(stack(v_ref[:, sl]).astype(BF16))
        ss.append(ss_scr[p])
        e_last.append(e_pos[n - 1:n, :])
    xy0 = [_dot_nt(ar[p], ss[p].astype(BF16)) for p in range(pairs)]
    sc = [jnp.where(tri, _dot_nt(ar[p], bk[p]), 0.0) for p in range(pairs)]
    lp = [s[:n2, :n2] for s in sc]
    t = [eye + l for l in lp]
    m = 1
    while 2 * m < n:
        lpb = [l.astype(BF16) for l in lp]
        lp = [_dot(l, l) for l in lpb]
        t = [t[p] + _dot(t[p].astype(BF16), lp[p].astype(BF16)) for p in range(pairs)]
        m *= 2
    w = [xy0[p][:n2] + _dot(sc[p][:n2, n2:].astype(BF16), v2[p]) for p in range(pairs)]
    u = [_dot(t[p].astype(BF16), w[p].astype(BF16)) for p in range(pairs)]
    uv = [jnp.concatenate([u[p].astype(BF16), v2[p]], axis=0) for p in range(pairs)]
    y = [xy0[p][n2:] + _dot(sc[p][n2:].astype(BF16), uv[p]) for p in range(pairs)]
    upd = [_dot_tn(uv[p], bk[p]) for p in range(pairs)]
    for p in range(pairs):
        y_ref[:, p * LANES:(p + 1) * LANES] = y[p][:n] + y[p][n:]
        ss_scr[p] = (ss[p] + upd[p]) * e_last[p]

    @pl.when(c == pl.num_programs(1) - 1)
    def _():
        for p in range(R_HEADS // 2):
            ss = ss_scr[p]
            s_ref[0, 2 * p] = ss[:R_HEAD, :R_HEAD]
            s_ref[0, 2 * p + 1] = ss[R_HEAD:, R_HEAD:]


def rwkv_scan(r, ld, k, v, kk, b, batch, tp):
    nchunk = tp // CHUNK
    blk = pl.BlockSpec((CHUNK, R_DIM), lambda bi, c: (bi * nchunk + c, 0))
    return pl.pallas_call(
        _rwkv_scan_kernel,
        grid=(batch, nchunk),
        in_specs=[blk] * 6,
        out_specs=[blk, pl.BlockSpec((1, R_HEADS, R_HEAD, R_HEAD), lambda bi, c: (bi, 0, 0, 0))],
        out_shape=[jax.ShapeDtypeStruct((batch * tp, R_DIM), F32),
                   jax.ShapeDtypeStruct((batch, R_HEADS, R_HEAD, R_HEAD), F32)],
        scratch_shapes=[pltpu.VMEM((R_HEADS // 2, LANES, LANES), F32)],
        compiler_params=_cparams(("parallel", "arbitrary")),
        name="rwkv_scan",
    )(r, ld, k, v, kk, b)


STEP_SEQS = 4


def _rwkv_step_kernel(r_ref, ld_ref, k_ref, vt_ref, kk_ref, b_ref, s_ref, yt_ref, so_ref):
    for i in range(r_ref.shape[0]):
        r, k = r_ref[i], k_ref[i]
        dec = jnp.exp(ld_ref[i])
        na = -kk_ref[i]
        b = b_ref[i]
        vt = vt_ref[i]
        row = lambda a, h: a[h:h + 1, :]
        s = [s_ref[i, h] for h in range(R_HEADS)]
        sa = [jnp.sum(s[h] * row(na, h), axis=1, keepdims=True) for h in range(R_HEADS)]
        s_new = [s[h] * row(dec, h) + sa[h] * row(b, h) + vt[:, h:h + 1] * row(k, h)
                 for h in range(R_HEADS)]
        cols = [jnp.sum(s_new[h] * row(r, h), axis=1, keepdims=True) for h in range(R_HEADS)]
        for h in range(R_HEADS):
            so_ref[i, h] = s_new[h]
        yt_ref[i] = jnp.concatenate(cols, axis=1)


def rwkv_step(r, ld, k, vt, kk, b, state):
    s = state.shape[0]
    sb = STEP_SEQS if s % STEP_SEQS == 0 else 1
    vec = pl.BlockSpec((sb, R_HEADS, R_HEAD), lambda i: (i, 0, 0))
    col = pl.BlockSpec((sb, R_HEAD, R_HEADS), lambda i: (i, 0, 0))
    st = pl.BlockSpec((sb, R_HEADS, R_HEAD, R_HEAD), lambda i: (i, 0, 0, 0))
    return pl.pallas_call(
        _rwkv_step_kernel,
        grid=(s // sb,),
        in_specs=[vec, vec, vec, col, vec, vec, st],
        out_specs=[col, st],
        out_shape=[jax.ShapeDtypeStruct((s, R_HEAD, R_HEADS), F32),
                   jax.ShapeDtypeStruct(state.shape, F32)],
        compiler_params=_cparams(("parallel",)),
        name="rwkv_step",
    )(r, ld, k, vt, kk, b, state)


def _select_topk(score, allowed, n_sel):
    bits = lax.bitcast_convert_type(score, I32)
    key = jnp.where(bits < 0, bits ^ jnp.int32(0x7FFFFFFF), bits)
    key = jnp.where(allowed, key, jnp.int32(INT_MIN))
    m = score.shape[0]

    def body(i, tau):
        cand = tau + lax.shift_left(jnp.int32(1), jnp.int32(31) - i)
        cnt = jnp.sum((key >= cand).astype(I32), axis=1, keepdims=True)
        return jnp.where(cnt >= n_sel, cand, tau)

    tau = lax.fori_loop(0, 32, body, jnp.full((m, 1), INT_MIN, I32))
    return jnp.logical_and(key >= tau, allowed)


def _dsa_prompt_kernel(n_sel, q_ref, iq_ref, iw_ref, ik_ref, k_ref, v_ref, o_ref, sc_ref):
    i = pl.program_id(1)
    tq = q_ref.shape[0]
    tk = k_ref.shape[0]
    ikb = ik_ref[...].astype(BF16)
    iw = iw_ref[...] * ((IDX_HEADS * IDX_DIM) ** -0.5)
    iq = iq_ref[...]
    for h in range(IDX_HEADS):
        d = _dot_nt(iq[:, h * IDX_DIM:(h + 1) * IDX_DIM], ikb)
        term = jnp.maximum(d, 0.0) * iw[:, h:h + 1]
        if h == 0:
            sc_ref[...] = term
        else:
            sc_ref[...] += term
    qpos = i * tq + lax.broadcasted_iota(I32, (tq, 1), 0)
    kpos = lax.broadcasted_iota(I32, (1, tk), 1)
    causal = kpos <= qpos
    sel = _select_topk(sc_ref[...], causal, n_sel)
    bias = jnp.where(sel, 0.0, NEG_BIG)
    q = q_ref[...]
    rep = A_HEADS // A_KV_HEADS
    for g in range(A_KV_HEADS):
        kg = k_ref[:, g * A_HEAD:(g + 1) * A_HEAD].astype(BF16)
        vg = v_ref[:, g * A_HEAD:(g + 1) * A_HEAD].astype(BF16)
        for rr in range(rep):
            h = g * rep + rr
            s = _dot_nt(q[:, h * A_HEAD:(h + 1) * A_HEAD], kg) + bias
            m = jnp.max(s, axis=1, keepdims=True)
            p = jnp.exp(s - m)
            l = jnp.sum(p, axis=1, keepdims=True)
            o_ref[:, h * A_HEAD:(h + 1) * A_HEAD] = _dot(p.astype(BF16), vg) / l


def dsa_prompt(q, iq, iw, ik, k, p, batch, tp, n_sel):
    nq = tp // Q_TILE
    qrow = lambda w: pl.BlockSpec((Q_TILE, w), lambda b, i: (b * nq + i, 0))
    keys = lambda w, blk: pl.BlockSpec((tp, w), lambda b, i: (b, blk))
    return pl.pallas_call(
        functools.partial(_dsa_prompt_kernel, n_sel),
        grid=(batch, nq),
        in_specs=[qrow(A_DIM), qrow(IDX_HEADS * IDX_DIM), qrow(IDX_HEADS),
                  keys(IDX_DIM, 0), keys(A_KV_DIM, 0), keys(A_KV_DIM, C_VA // A_KV_DIM)],
        out_specs=qrow(A_DIM),
        out_shape=jax.ShapeDtypeStruct((batch * tp, A_DIM), F32),
        scratch_shapes=[pltpu.VMEM((Q_TILE, tp), F32)],
        compiler_params=_cparams(("parallel", "parallel")),
        name="dsa_prompt",
    )(q, iq, iw, ik, k, p)


def _dsa_step_score_kernel(n_pages, page, pt_ref, iq_ref, iw_ref, ikn_ref, *refs):
    pages = refs[:n_pages]
    o_ref = refs[n_pages]
    iq = iq_ref[0]
    iw = iw_ref[0] * ((IDX_HEADS * IDX_DIM) ** -0.5)
    for j in range(n_pages):
        d = _dot_nt(iq, pages[j][...].astype(BF16))
        o_ref[0, :, j * page:(j + 1) * page] = jnp.sum(jnp.maximum(d, 0.0) * iw, axis=0,
                                                       keepdims=True)
    dn = jnp.sum(iq.astype(F32) * ikn_ref[0], axis=1, keepdims=True)
    sn = jnp.sum(jnp.maximum(dn, 0.0) * iw, axis=0, keepdims=True)
    lane = lax.broadcasted_iota(I32, (1, LANES), 1)
    o_ref[0, :, n_pages * page:] = jnp.where(lane == 0, sn, 0.0)


def dsa_step_scores(pt_flat, iq, iw, ik_new, cik2d, n_pages, page):
    s = iq.shape[0]
    kw = n_pages * page + LANES
    page_spec = lambda j: pl.BlockSpec((page, IDX_DIM), lambda i, pt: (pt[i * n_pages + j], 0))
    grid_spec = pltpu.PrefetchScalarGridSpec(
        num_scalar_prefetch=1,
        grid=(s,),
        in_specs=[pl.BlockSpec((1, IDX_HEADS, IDX_DIM), lambda i, pt: (i, 0, 0)),
                  pl.BlockSpec((1, IDX_HEADS, 1), lambda i, pt: (i, 0, 0)),
                  pl.BlockSpec((1, 1, IDX_DIM), lambda i, pt: (i, 0, 0))]
                 + [page_spec(j) for j in range(n_pages)],
        out_specs=pl.BlockSpec((1, 1, kw), lambda i, pt: (i, 0, 0)),
    )
    return pl.pallas_call(
        functools.partial(_dsa_step_score_kernel, n_pages, page),
        grid_spec=grid_spec,
        out_shape=jax.ShapeDtypeStruct((s, 1, kw), F32),
        compiler_params=_cparams(("arbitrary",)),
        name="dsa_step_scores",
    )(pt_flat, iq, iw, ik_new, *([cik2d] * n_pages))


def _dsa_step_select_kernel(n_sel, past, sc_ref, o_ref):
    sc = sc_ref[...]
    kpos = lax.broadcasted_iota(I32, sc.shape, 1)
    sel = _select_topk(sc, kpos <= past, n_sel)
    o_ref[...] = sel.astype(F32)


def dsa_step_select(sc, n_sel, past):
    return pl.pallas_call(
        functools.partial(_dsa_step_select_kernel, n_sel, past),
        out_shape=jax.ShapeDtypeStruct(sc.shape, F32),
        compiler_params=pltpu.CompilerParams(vmem_limit_bytes=VMEM_LIMIT),
        name="dsa_step_select",
    )(sc)


def _dsa_step_attn_kernel(n_pages, page, pt_ref, q_ref, kn_ref, vn_ref, sel_ref, ex_ref, *refs):
    kp = refs[:n_pages]
    vp = refs[n_pages:2 * n_pages]
    o_ref = refs[2 * n_pages]
    q = q_ref[0]
    rep = A_HEADS // A_KV_HEADS
    w2 = page * A_KV_HEADS
    hrow = lax.broadcasted_iota(I32, (A_HEADS, w2), 0)
    col = lax.broadcasted_iota(I32, (A_HEADS, w2), 1)
    own = (col % A_KV_HEADS) == (hrow // rep)
    ex = ex_ref[...]
    logits = []
    for j in range(n_pages):
        s = _dot_nt(q, kp[j][...].astype(BF16))
        selj = _dot(sel_ref[0, :, j * page:(j + 1) * page].astype(BF16), ex)
        logits.append(jnp.where(jnp.logical_and(selj > 0.5, own), s, NEG_BIG))
    h8 = lax.broadcasted_iota(I32, (A_HEADS, A_HEAD), 0)
    kn = jnp.where(h8 < rep, kn_ref[0, 0:1, :], kn_ref[0, 1:2, :])
    vn = jnp.where(h8 < rep, vn_ref[0, 0:1, :], vn_ref[0, 1:2, :])
    sn = jnp.sum(q.astype(F32) * kn, axis=1, keepdims=True)
    seln = sel_ref[0, :, n_pages * page:n_pages * page + 1]
    sn = jnp.where(seln > 0.5, sn, NEG_BIG)
    m = sn
    for s in logits:
        m = jnp.maximum(m, jnp.max(s, axis=1, keepdims=True))
    pn = jnp.exp(sn - m)
    l = pn
    acc = pn * vn
    for j in range(n_pages):
        p = jnp.exp(logits[j] - m)
        l = l + jnp.sum(p, axis=1, keepdims=True)
        acc = acc + _dot(p.astype(BF16), vp[j][...].astype(BF16))
    o_ref[0] = acc / l


def dsa_step_attn(pt_flat, q, k_new, v_new, sel, expand, ck2d, cv2d, n_pages, page):
    s = q.shape[0]
    kw = sel.shape[-1]
    w2 = page * A_KV_HEADS
    page_spec = lambda j: pl.BlockSpec((w2, A_HEAD), lambda i, pt: (pt[i * n_pages + j], 0))
    grid_spec = pltpu.PrefetchScalarGridSpec(
        num_scalar_prefetch=1,
        grid=(s,),
        in_specs=[pl.BlockSpec((1, A_HEADS, A_HEAD), lambda i, pt: (i, 0, 0)),
                  pl.BlockSpec((1, A_KV_HEADS, A_HEAD), lambda i, pt: (i, 0, 0)),
                  pl.BlockSpec((1, A_KV_HEADS, A_HEAD), lambda i, pt: (i, 0, 0)),
                  pl.BlockSpec((1, 1, kw), lambda i, pt: (i, 0, 0)),
                  pl.BlockSpec((page, w2), lambda i, pt: (0, 0))]
                 + [page_spec(j) for j in range(n_pages)] * 2,
        out_specs=pl.BlockSpec((1, A_HEADS, A_HEAD), lambda i, pt: (i, 0, 0)),
    )
    return pl.pallas_call(
        functools.partial(_dsa_step_attn_kernel, n_pages, page),
        grid_spec=grid_spec,
        out_shape=jax.ShapeDtypeStruct((s, A_HEADS, A_HEAD), F32),
        compiler_params=_cparams(("arbitrary",)),
        name="dsa_step_attn",
    )(pt_flat, q, k_new, v_new, sel, expand, *([ck2d] * n_pages), *([cv2d] * n_pages))


def _pack_bf16_pairs(x):
    w = x.shape[1] // 2
    hi = lax.bitcast_convert_type(x[:, :w].astype(BF16).astype(F32), I32)
    lo = lax.bitcast_convert_type(x[:, w:].astype(BF16).astype(F32), I32)
    return hi | lax.shift_right_logical(lo, 16)


def _unpack_bf16_pairs(p):
    hi = lax.bitcast_convert_type(p & jnp.int32(-65536), F32)
    lo = lax.bitcast_convert_type(lax.shift_left(p, 16), F32)
    return hi, lo


def _mix_kernel(alpha, prompt_blocks, x_ref, yp_ref, bonp_ref, gp_ref, ap_ref, ys_ref, bons_ref, gs_ref,
                as_ref, e_ref, et_ref, gng_ref, gnb_ref, l0g_ref, l0b_ref, wo_ref, l1g_ref, l1b_ref,
                wr_ref, h_ref, sc_ref, pk_ref):
    is_prompt = pl.program_id(0) < prompt_blocks
    pick = lambda p_ref, s_ref: jnp.where(is_prompt, p_ref[...], s_ref[...])
    e, et = e_ref[...], et_ref[...]
    y = pick(yp_ref, ys_ref)
    inv = 1.0 / R_HEAD
    mu = _dot(_dot(y, e, HIGHEST), et, HIGHEST) * inv
    d = y - mu
    var = _dot(_dot(d * d, e, HIGHEST), et, HIGHEST) * inv
    yn = d * lax.rsqrt(var + GN_EPS) * gng_ref[...] + gnb_ref[...]
    r_out = (yn + pick(bonp_ref, bons_ref)) * pick(gp_ref, gs_ref)
    mix = (_dot(r_out.astype(BF16), wo_ref[:R_DIM, :])
           + _dot(pick(ap_ref, as_ref).astype(BF16), wo_ref[R_DIM:, :]))
    h0 = _layer_norm(x_ref[...], l0g_ref[...], l0b_ref[...])
    h1 = _layer_norm(alpha * h0 + mix, l1g_ref[...], l1b_ref[...])
    h_ref[...] = h1
    sc_ref[...] = _sigmoid(_dot_nt(wr_ref[...], h1, HIGHEST))
    pk_ref[...] = _pack_bf16_pairs(h1)


def mix_ln1_router(x, prompt_parts, step_parts, pw, alpha):
    n, d = x.shape
    tm = ROW_TILE
    pb = prompt_parts[0].shape[0] // tm
    row = lambda w: pl.BlockSpec((tm, w), lambda i: (i, 0))
    head = lambda w: pl.BlockSpec((tm, w), lambda i: (jnp.minimum(i, pb - 1), 0))
    tail = lambda w: pl.BlockSpec((tm, w), lambda i: (jnp.maximum(i - pb, 0), 0))
    full = lambda a: pl.BlockSpec(a.shape, lambda i: (0,) * a.ndim)
    params = (pw["e"], pw["et"], pw["gn_g"], pw["gn_b"], pw["ln0_g"], pw["ln0_b"], pw["w_out"],
              pw["ln1_g"], pw["ln1_b"], pw["w_router_t"])
    widths = (R_DIM, R_DIM, R_DIM, A_DIM)
    return pl.pallas_call(
        functools.partial(_mix_kernel, alpha, pb),
        grid=(n // tm,),
        in_specs=[row(d)] + [head(w) for w in widths] + [tail(w) for w in widths]
                 + [full(a) for a in params],
        out_specs=[row(d), pl.BlockSpec((N_EXPERTS, tm), lambda i: (0, i)), row(d // 2)],
        out_shape=[jax.ShapeDtypeStruct((n, d), F32),
                   jax.ShapeDtypeStruct((N_EXPERTS, n), F32),
                   jax.ShapeDtypeStruct((n, d // 2), I32)],
        compiler_params=_cparams(("parallel",)),
        name="mix_ln1_router",
    )(x, *prompt_parts, *step_parts, *params)


def _route_kernel(n_prompt, tp, t_real, s_dec, sc_ref, bias_ref, idx_ref, gate_ref, pos_ref, cnt_ref,
                  cnt_scr):
    scores = sc_ref[...]
    biased = scores + bias_ref[...]
    tn = scores.shape[1]
    per = N_EXPERTS // N_EXPERT_GROUPS
    sub = lax.broadcasted_iota(I32, (per, tn), 0)
    grp_rows = []
    for g in range(N_EXPERT_GROUPS):
        xg = biased[g * per:(g + 1) * per, :]
        m1 = jnp.max(xg, axis=0, keepdims=True)
        first = jnp.min(jnp.where(xg == m1, sub, per), axis=0, keepdims=True)
        m2 = jnp.max(jnp.where(sub == first, -jnp.inf, xg), axis=0, keepdims=True)
        grp_rows.append(m1 + m2)
    grp = jnp.concatenate(grp_rows, axis=0)
    gi = lax.broadcasted_iota(I32, (N_EXPERT_GROUPS, tn), 0)
    gsel = jnp.zeros((N_EXPERT_GROUPS, tn), jnp.bool_)
    for _ in range(TOPK_GROUPS):
        m = jnp.max(grp, axis=0, keepdims=True)
        first = jnp.min(jnp.where(grp == m, gi, N_EXPERT_GROUPS), axis=0, keepdims=True)
        hit = gi == first
        gsel = jnp.logical_or(gsel, hit)
        grp = jnp.where(hit, -jnp.inf, grp)
    ei = lax.broadcasted_iota(I32, (N_EXPERTS, tn), 0)
    emask = jnp.concatenate(
        [jnp.broadcast_to(gsel[g:g + 1, :], (per, tn)) for g in range(N_EXPERT_GROUPS)], axis=0)
    cand = jnp.where(emask, biased, -jnp.inf)
    idxs, gates, hits = [], [], []
    for _ in range(TOP_K):
        m = jnp.max(cand, axis=0, keepdims=True)
        first = jnp.min(jnp.where(cand == m, ei, N_EXPERTS), axis=0, keepdims=True)
        hit = ei == first
        idxs.append(first)
        hits.append(hit)
        gates.append(jnp.sum(jnp.where(hit, scores, 0.0), axis=0, keepdims=True))
        cand = jnp.where(hit, -jnp.inf, cand)
    gate = jnp.concatenate(gates, axis=0)
    gate = gate / jnp.sum(gate, axis=0, keepdims=True) * ROUTED_SCALE
    tok = pl.program_id(0) * tn + lax.broadcasted_iota(I32, (1, tn), 1)
    live = (jnp.where(tok < n_prompt, tok % tp, tok - n_prompt)
            < jnp.where(tok < n_prompt, t_real, s_dec))
    idx_ref[...] = jnp.concatenate(idxs, axis=0)
    gate_ref[...] = jnp.where(live, gate, 0.0)
    chosen = hits[0]
    for hit in hits[1:]:
        chosen = jnp.logical_or(chosen, hit)
    onehot = jnp.where(jnp.logical_and(chosen, live), 1.0, 0.0)
    ta = lax.broadcasted_iota(I32, (tn, tn), 0)
    tb = lax.broadcasted_iota(I32, (tn, tn), 1)
    prefix = _dot(onehot.astype(BF16), (ta < tb).astype(BF16))

    @pl.when(pl.program_id(0) == 0)
    def _():
        cnt_scr[...] = jnp.zeros_like(cnt_scr)

    rank = prefix + cnt_scr[:, 0:1]
    pos_ref[...] = jnp.concatenate(
        [jnp.sum(jnp.where(hit, rank, 0.0), axis=0, keepdims=True) for hit in hits], axis=0).astype(I32)
    cnt_scr[...] = cnt_scr[...] + jnp.sum(onehot, axis=1, keepdims=True)
    cnt_ref[...] = cnt_scr[...].astype(I32)


def route(scores_t, e_bias, n_prompt, tp, t_real, s_dec):
    n = scores_t.shape[1]
    tn = ROW_TILE
    tok = pl.BlockSpec((TOP_K, tn), lambda i: (0, i))
    return pl.pallas_call(
        functools.partial(_route_kernel, n_prompt, tp, t_real, s_dec),
        grid=(n // tn,),
        in_specs=[pl.BlockSpec((N_EXPERTS, tn), lambda i: (0, i)),
                  pl.BlockSpec((N_EXPERTS, 1), lambda i: (0, 0))],
        out_specs=[tok, tok, tok, pl.BlockSpec((N_EXPERTS, LANES), lambda i: (0, 0))],
        out_shape=[jax.ShapeDtypeStruct((TOP_K, n), I32), jax.ShapeDtypeStruct((TOP_K, n), F32),
                   jax.ShapeDtypeStruct((TOP_K, n), I32),
                   jax.ShapeDtypeStruct((N_EXPERTS, LANES), I32)],
        scratch_shapes=[pltpu.VMEM((N_EXPERTS, LANES), F32)],
        compiler_params=_cparams(("arbitrary",)),
        name="route",
    )(scores_t, e_bias)


def _dispatch_kernel(dest_ref, x_ref, init_ref, o_ref, sem):
    del init_ref
    tm = x_ref.shape[0]

    def start(i, carry):
        for j in range(TOP_K):
            pltpu.make_async_copy(x_ref.at[pl.ds(i, 1)],
                                  o_ref.at[pl.ds(dest_ref[i * TOP_K + j], 1)], sem).start()
        return carry

    lax.fori_loop(0, tm, start, 0)
    for j in range(TOP_K):
        pltpu.make_async_copy(x_ref, o_ref.at[pl.ds(0, tm)], sem).wait()


def moe_dispatch(dest_flat, xpk, rows_total):
    n, w = xpk.shape
    tm = ROW_TILE
    init = jnp.zeros((rows_total, w), I32)
    return pl.pallas_call(
        _dispatch_kernel,
        grid=(n // tm,),
        in_specs=[pl.BlockSpec((tm * TOP_K,), lambda i: (i,), memory_space=pltpu.SMEM),
                  pl.BlockSpec((tm, w), lambda i: (i, 0)),
                  pl.BlockSpec(memory_space=pl.ANY)],
        out_specs=pl.BlockSpec(memory_space=pl.ANY),
        out_shape=jax.ShapeDtypeStruct((rows_total, w), I32),
        scratch_shapes=[pltpu.SemaphoreType.DMA(())],
        input_output_aliases={2: 0},
        compiler_params=_cparams(("arbitrary",)),
        name="moe_dispatch",
    )(dest_flat, xpk, init)


def _experts_kernel(be_ref, nu_ref, x_ref, wg_ref, wu_ref, wd_ref, o_ref, wg_s, wu_s, wd_s):
    i = pl.program_id(0)
    used = i < nu_ref[0]
    prev = be_ref[jnp.maximum(i - 1, 0)]
    fresh = jnp.logical_or(i == 0, be_ref[i] != prev)

    @pl.when(jnp.logical_and(used, fresh))
    def _():
        wg_s[...] = wg_ref[0].astype(BF16)
        wu_s[...] = wu_ref[0].astype(BF16)
        wd_s[...] = wd_ref[0].astype(BF16)

    @pl.when(used)
    def _():
        hi, lo = _unpack_bf16_pairs(x_ref[...])
        hi, lo = hi.astype(BF16), lo.astype(BF16)
        half = hi.shape[1]
        gp = _dot(hi, wg_s[:half, :]) + _dot(lo, wg_s[half:, :])
        up = _dot(hi, wu_s[:half, :]) + _dot(lo, wu_s[half:, :])
        act = gp * _sigmoid(gp) * up
        o_ref[...] = _pack_bf16_pairs(_dot(act.astype(BF16), wd_s[...]))

    @pl.when(jnp.logical_not(used))
    def _():
        o_ref[...] = jnp.zeros_like(o_ref)


def moe_experts(blk_e, n_used, xs, nb, w_gate, w_up, w_down):
    w = xs.shape[1]
    rows = nb * EXPERT_TILE
    _, d, de = w_gate.shape
    last = lambda i, nu: jnp.minimum(i, nu[0] - 1)
    grid_spec = pltpu.PrefetchScalarGridSpec(
        num_scalar_prefetch=2,
        grid=(nb,),
        in_specs=[pl.BlockSpec((EXPERT_TILE, w), lambda i, be, nu: (last(i, nu), 0)),
                  pl.BlockSpec((1, d, de), lambda i, be, nu: (be[last(i, nu)], 0, 0)),
                  pl.BlockSpec((1, d, de), lambda i, be, nu: (be[last(i, nu)], 0, 0)),
                  pl.BlockSpec((1, de, d), lambda i, be, nu: (be[last(i, nu)], 0, 0))],
        out_specs=pl.BlockSpec((EXPERT_TILE, w), lambda i, be, nu: (i, 0)),
        scratch_shapes=[pltpu.VMEM((d, de), BF16), pltpu.VMEM((d, de), BF16),
                        pltpu.VMEM((de, d), BF16)],
    )
    return pl.pallas_call(
        _experts_kernel,
        grid_spec=grid_spec,
        out_shape=jax.ShapeDtypeStruct((rows, w), I32),
        compiler_params=_cparams(("arbitrary",)),
        name="moe_experts",
    )(blk_e, n_used, xs, w_gate, w_up, w_down)


def _combine_kernel(alpha, dest_ref, h_ref, gate_ref, wsg_ref, wsu_ref, wsd_ref, l2g_ref, l2b_ref,
                    ys_ref, o_ref, buf, sem):
    tm = h_ref.shape[0]

    def start(i, carry):
        for j in range(TOP_K):
            pltpu.make_async_copy(ys_ref.at[pl.ds(dest_ref[i * TOP_K + j], 1)],
                                  buf.at[j, pl.ds(i, 1)], sem).start()
        return carry

    lax.fori_loop(0, tm, start, 0)
    h = h_ref[...]
    hb = h.astype(BF16)
    gp = _dot(hb, wsg_ref[...])
    up = _dot(hb, wsu_ref[...])
    shared = _dot((gp * _sigmoid(gp) * up).astype(BF16), wsd_ref[...])
    for j in range(TOP_K):
        pltpu.make_async_copy(ys_ref.at[pl.ds(0, tm)], buf.at[j], sem).wait()
    gate = gate_ref[...]
    half = buf.shape[2]
    acc_hi = jnp.zeros((tm, half), F32)
    acc_lo = jnp.zeros((tm, half), F32)
    for j in range(TOP_K):
        hi, lo = _unpack_bf16_pairs(buf[j])
        gj = gate[:, j:j + 1]
        acc_hi = acc_hi + gj * hi
        acc_lo = acc_lo + gj * lo
    f = jnp.concatenate([acc_hi, acc_lo], axis=1) + shared
    o_ref[...] = _layer_norm(alpha * h + f, l2g_ref[...], l2b_ref[...])


def moe_combine(dest_flat, h1, gate, ys, pw, alpha):
    n, d = h1.shape
    tm = Q_TILE
    full = lambda a: pl.BlockSpec(a.shape, lambda i: (0,) * a.ndim)
    params = (pw["ws_gate"], pw["ws_up"], pw["ws_down"], pw["ln2_g"], pw["ln2_b"])
    return pl.pallas_call(
        functools.partial(_combine_kernel, alpha),
        grid=(n // tm,),
        in_specs=[pl.BlockSpec((tm * TOP_K,), lambda i: (i,), memory_space=pltpu.SMEM),
                  pl.BlockSpec((tm, d), lambda i: (i, 0)),
                  pl.BlockSpec((tm, TOP_K), lambda i: (i, 0))]
                 + [full(a) for a in params]
                 + [pl.BlockSpec(memory_space=pl.ANY)],
        out_specs=pl.BlockSpec((tm, d), lambda i: (i, 0)),
        out_shape=jax.ShapeDtypeStruct((n, d), F32),
        scratch_shapes=[pltpu.VMEM((TOP_K, tm, d // 2), I32), pltpu.SemaphoreType.DMA(())],
        compiler_params=_cparams(("arbitrary",)),
        name="moe_combine",
    )(dest_flat, h1, gate, *params, ys)


def _round_up(x, m):
    return (x + m - 1) // m * m


def _rope_tables(pos, head):
    half = head // 2
    inv = ROPE_THETA ** (-jnp.arange(half, dtype=F32) / half)
    ang = pos.astype(F32)[:, None] * inv[None, :]
    cos, sin = jnp.cos(ang), jnp.sin(ang)
    rep = LANES // head
    c = jnp.tile(jnp.concatenate([cos, cos], axis=1), (1, rep))
    s = jnp.tile(jnp.concatenate([-sin, sin], axis=1), (1, rep))
    return c, s


def _permute_cols(m):
    a0 = SHIFT_DIM
    pieces = [
        m[..., 0:3 * R_DIM],
        m[..., a0:a0 + A_DIM],
        m[..., a0 + A_DIM + 2 * A_KV_DIM:a0 + A_DIM + 2 * A_KV_DIM + IDX_HEADS * IDX_DIM],
        m[..., a0 + A_DIM:a0 + A_DIM + 2 * A_KV_DIM],
    ]
    i0 = a0 + A_DIM + 2 * A_KV_DIM + IDX_HEADS * IDX_DIM
    pieces.append(m[..., i0:i0 + IDX_DIM + IDX_HEADS])
    pad = lambda w: jnp.zeros(m.shape[:-1] + (w,), m.dtype)
    pieces.append(pad(LANES - IDX_DIM - IDX_HEADS))
    pieces.append(m[..., 3 * R_DIM:SHIFT_DIM])
    pieces.append(pad(LORA_W - (SHIFT_DIM - 3 * R_DIM)))
    return jnp.concatenate(pieces, axis=-1)


def kernel(x_prompt, x_sample, cache_k, cache_v, cache_idx_k, state_wkv, state_shift, page_table,
           meta, ln0_g, ln0_b, w_in, mu_shift, w0, w_b, a0, a_b, g_b, k_k, k_a, r_k, gn_g, gn_b,
           w_out, ln1_g, ln1_b, w_router, e_bias, w_gate, w_up, w_down, ws_gate, ws_up, ws_down,
           ln2_g, ln2_b):
    depth = w_in.shape[0]
    assert depth == 1, "single trunk layer"
    bsz, s_p, d = x_prompt.shape
    s_dec, s_s, _ = x_sample.shape
    assert s_s == 1, "one decode token per sequence"
    t_real = N_META + s_p
    tp = _round_up(t_real, LANES)
    assert (bsz * tp) % ROW_TILE == 0
    sp = _round_up(s_dec, ROW_TILE)
    n_prompt = bsz * tp
    n = n_prompt + sp
    n_pool, page = cache_k.shape[1], cache_k.shape[2]
    n_pages = page_table.shape[1]
    past = n_pages * page
    alpha = float((2 * depth) ** 0.25)
    row2 = lambda a: a.reshape(1, -1)

    meta_rows = jnp.broadcast_to(meta[None], (bsz, N_META, d))
    xp = jnp.concatenate([meta_rows, x_prompt, jnp.zeros((bsz, tp - t_real, d), F32)], axis=1)
    x_all = jnp.concatenate([xp.reshape(n_prompt, d), x_sample.reshape(s_dec, d),
                             jnp.zeros((sp - s_dec, d), F32)], axis=0)
    pos = jnp.concatenate([jnp.tile(jnp.arange(tp), bsz), jnp.full((sp,), past)])
    c128, s128 = _rope_tables(pos, A_HEAD)
    c64, s64 = _rope_tables(pos, IDX_DIM)

    w_in_k = _permute_cols(w_in[0]).astype(BF16)
    mu_k = _permute_cols(
        jnp.concatenate([mu_shift[0], jnp.zeros((w_in.shape[2] - SHIFT_DIM,), F32)])[None, :])
    head_of = jnp.arange(R_DIM) // R_HEAD
    e_mat = (head_of[:, None] == jnp.arange(R_HEADS)[None, :]).astype(F32)
    zpad = lambda a, rows_before, rows_total: jnp.concatenate(
        [jnp.zeros((rows_before, a.shape[1]), a.dtype), a,
         jnp.zeros((rows_total - rows_before - a.shape[0], a.shape[1]), a.dtype)], axis=0)
    pw = {
        "mu_x": mu_k[:, :3 * R_DIM], "mu_lo": mu_k[:, C_LORA:],
        "w0": row2(w0[0]), "a0": row2(a0[0]), "k_k": row2(k_k[0]), "k_a": row2(k_a[0]),
        "r_k": row2(r_k[0]), "gn_g": row2(gn_g[0]), "gn_b": row2(gn_b[0]),
        "w_b": zpad(w_b[0], 0, LANES).astype(BF16),
        "a_b": zpad(a_b[0], D_DECAY_LORA, LANES).astype(BF16),
        "g_b": zpad(g_b[0], 0, LORA_W - LANES).astype(BF16),
        "e": e_mat, "et": e_mat.T,
        "ln0_g": row2(ln0_g), "ln0_b": row2(ln0_b),
        "ln1_g": row2(ln1_g[0]), "ln1_b": row2(ln1_b[0]),
        "ln2_g": row2(ln2_g[0]), "ln2_b": row2(ln2_b[0]),
        "w_out": w_out[0].astype(BF16), "w_router_t": w_router[0].T,
        "ws_gate": ws_gate[0].astype(BF16), "ws_up": ws_up[0].astype(BF16),
        "ws_down": ws_down[0].astype(BF16),
    }

    p = ln_proj(x_all, pw["ln0_g"], pw["ln0_b"], w_in_k, tn=P_COLS // 3)
    q_r, iq_r, k_r, ik_r, iw = rope_all(p, c128, s128, c64, s64)

    pre_p = rwkv_pre(p, 0, n_prompt, None, pw, t_real, tp)
    shift_k = _permute_cols(jnp.concatenate(
        [state_shift[0], jnp.zeros((s_dec, w_in.shape[2] - SHIFT_DIM), F32)], axis=1))
    shift_k = jnp.concatenate([shift_k, jnp.zeros((sp - s_dec, P_COLS), F32)], axis=0)
    pre_s = rwkv_pre(p, n_prompt, sp, (shift_k[:, :3 * R_DIM], shift_k[:, C_LORA:]), pw, t_real, tp)
    r_p, ld_p, k_p, v_p, kk_p, b_p, g_p, bon_p = pre_p
    r_s, ld_s, k_s, v_s, kk_s, b_s, g_s, bon_s = pre_s
    y_p, wkv_p = rwkv_scan(r_p, ld_p, k_p, v_p, kk_p, b_p, bsz, tp)
    heads = lambda a: a[:s_dec].reshape(s_dec, R_HEADS, R_HEAD)
    yt_s, wkv_s = rwkv_step(heads(r_s), heads(ld_s), heads(k_s), heads(v_s).transpose(0, 2, 1),
                            heads(kk_s), heads(b_s), state_wkv[0])
    y_s = jnp.concatenate([yt_s.transpose(0, 2, 1).reshape(s_dec, R_DIM),
                           jnp.zeros((sp - s_dec, R_DIM), F32)], axis=0)

    n_sel_p = min(TOPK_KEYS, t_real // 4)
    a_p = dsa_prompt(q_r, iq_r, iw, ik_r, k_r, p, bsz, tp, n_sel_p)
    n_sel_s = min(TOPK_KEYS, (past + 1) // 4)
    pt_flat = page_table.reshape(-1).astype(I32)
    srow = slice(n_prompt, n_prompt + s_dec)
    sc_s = dsa_step_scores(pt_flat, iq_r[srow].reshape(s_dec, IDX_HEADS, IDX_DIM),
                           iw[srow].reshape(s_dec, IDX_HEADS, 1), ik_r[srow].reshape(s_dec, 1, IDX_DIM),
                           cache_idx_k[0].reshape(n_pool * page, IDX_DIM), n_pages, page)
    sel_s = dsa_step_select(sc_s.reshape(s_dec, -1), n_sel_s, past).reshape(sc_s.shape)
    slot = jnp.arange(page)[:, None]
    expand = (jnp.arange(page * A_KV_HEADS)[None, :] // A_KV_HEADS == slot).astype(BF16)
    a_s = dsa_step_attn(pt_flat, q_r[srow].reshape(s_dec, A_HEADS, A_HEAD),
                        k_r[srow].reshape(s_dec, A_KV_HEADS, A_HEAD),
                        p[srow, C_VA:C_VA + A_KV_DIM].reshape(s_dec, A_KV_HEADS, A_HEAD),
                        sel_s, expand,
                        cache_k[0].reshape(n_pool * page * A_KV_HEADS, A_HEAD),
                        cache_v[0].reshape(n_pool * page * A_KV_HEADS, A_HEAD), n_pages, page)
    a_s = jnp.concatenate([a_s.reshape(s_dec, A_DIM), jnp.zeros((sp - s_dec, A_DIM), F32)], axis=0)

    h1, scores_t, xpk = mix_ln1_router(x_all, (y_p, bon_p, g_p, a_p), (y_s, bon_s, g_s, a_s), pw, alpha)
    eidx_t, gate_t, pos_t, counts = route(scores_t, e_bias[0].reshape(N_EXPERTS, 1),
                                          n_prompt, tp, t_real, s_dec)

    row_id = np.arange(n)
    tok_ok = np.where(row_id < n_prompt, (row_id % tp) < t_real, row_id < n_prompt + s_dec)
    n_dead = int(n - tok_ok.sum())
    nb = (int(tok_ok.sum()) * TOP_K + N_EXPERTS * (EXPERT_TILE - 1)) // EXPERT_TILE + 1
    dead_rows = nb * EXPERT_TILE + (np.cumsum(~tok_ok) - 1)[:, None] * TOP_K + np.arange(TOP_K)[None, :]
    counts = counts[:, 0]
    padded = (counts + EXPERT_TILE - 1) // EXPERT_TILE * EXPERT_TILE
    seg_end = jnp.cumsum(padded)
    seg_start = seg_end - padded
    dest = (seg_start[eidx_t] + pos_t).T
    dest_w = jnp.where(tok_ok[:, None], dest, dead_rows).astype(I32).reshape(-1)
    dest_r = jnp.where(tok_ok[:, None], dest, 0).astype(I32).reshape(-1)
    blk_e = jnp.minimum(jnp.searchsorted(seg_end, jnp.arange(nb) * EXPERT_TILE, side="right"),
                        N_EXPERTS - 1).astype(I32)
    n_used = (seg_end[-1] // EXPERT_TILE).astype(I32).reshape(1)

    xs = moe_dispatch(dest_w, xpk, nb * EXPERT_TILE + n_dead * TOP_K)
    ys = moe_experts(blk_e, n_used, xs, nb, w_gate[0], w_up[0], w_down[0])
    h2 = moe_combine(dest_r, h1, gate_t.T, ys, pw, alpha)

    def prompt_rows(a):
        return a[:n_prompt].reshape(bsz, tp, -1)[:, :t_real]

    y_prompt = prompt_rows(h2)[:, N_META:]
    y_sample = h2[srow].reshape(s_dec, 1, d)
    k_prompt = prompt_rows(k_r).reshape(1, bsz, t_real, A_KV_HEADS, A_HEAD)
    v_prompt = prompt_rows(p[:, C_VA:C_VA + A_KV_DIM]).reshape(1, bsz, t_real, A_KV_HEADS, A_HEAD)
    ik_prompt = prompt_rows(ik_r)[None]
    last = jnp.arange(bsz) * tp + t_real - 1
    unperm = lambda rows: jnp.concatenate([rows[:, :3 * R_DIM],
                                           rows[:, C_LORA:C_LORA + SHIFT_DIM - 3 * R_DIM]], axis=1)
    shift_prompt = unperm(p[last])[None]
    k_sample = k_r[srow].reshape(1, s_dec, 1, A_KV_HEADS, A_HEAD)
    v_sample = p[srow, C_VA:C_VA + A_KV_DIM].reshape(1, s_dec, 1, A_KV_HEADS, A_HEAD)
    ik_sample = ik_r[srow].reshape(1, s_dec, 1, IDX_DIM)
    shift_sample = unperm(p[srow])[None]
    return (y_prompt, y_sample, k_prompt, v_prompt, ik_prompt, wkv_p[None], shift_prompt,
            k_sample, v_sample, ik_sample, wkv_s[None], shift_sample)
```

```python
import functools

import numpy as np
import jax
import jax.numpy as jnp
from jax import lax
from jax.experimental import pallas as pl
from jax.experimental.pallas import tpu as pltpu

F32 = jnp.float32
BF16 = jnp.bfloat16
I32 = jnp.int32
HIGHEST = lax.Precision.HIGHEST

N_META = 16
R_HEADS, R_HEAD = 16, 64
R_DIM = R_HEADS * R_HEAD
D_DECAY_LORA, D_AAA_LORA, D_GATE_LORA = 64, 64, 160
SHIFT_DIM = 3 * R_DIM + D_DECAY_LORA + D_AAA_LORA + D_GATE_LORA
GN_EPS = 64e-5
A_HEADS, A_KV_HEADS, A_HEAD = 8, 2, 128
A_DIM = A_HEADS * A_HEAD
A_KV_DIM = A_KV_HEADS * A_HEAD
IDX_HEADS, IDX_DIM = 16, 64
TOPK_KEYS = 256
ROPE_THETA = 10000.0
N_EXPERTS, N_EXPERT_GROUPS, TOPK_GROUPS, TOP_K = 64, 8, 4, 8
ROUTED_SCALE = 2.5
LN_EPS = 1e-5

LANES = 128
SUBLANES = 8
ROW_TILE = 256
Q_TILE = 128
CHUNK = 64
EXPERT_TILE = 256
VMEM_LIMIT = 56 * 1024 * 1024
NEG_BIG = -1e30
INT_MIN = -2 ** 31

C_R, C_K, C_V = 0, R_DIM, 2 * R_DIM
C_Q = 3 * R_DIM
C_IQ = C_Q + A_DIM
C_KA = C_IQ + IDX_HEADS * IDX_DIM
C_VA = C_KA + A_KV_DIM
C_IK = C_VA + A_KV_DIM
C_LORA = C_IK + LANES
LORA_W = 384
P_COLS = C_LORA + LORA_W


def _cparams(sem):
    return pltpu.CompilerParams(dimension_semantics=sem, vmem_limit_bytes=VMEM_LIMIT)


def _dot(a, b, precision=None):
    return jnp.dot(a, b, preferred_element_type=F32, precision=precision)


def _dot_nt(a, b, precision=None):
    return lax.dot_general(a, b, (((1,), (1,)), ((), ())), preferred_element_type=F32,
                           precision=precision)


def _dot_tn(a, b, precision=None):
    return lax.dot_general(a, b, (((0,), (0,)), ((), ())), preferred_element_type=F32,
                           precision=precision)


def _split_bf16(x):
    hi = x.astype(BF16)
    return hi, (x - hi.astype(F32)).astype(BF16)


def _dot_f32_by_bf16(a, b):
    hi, lo = _split_bf16(a)
    return _dot(hi, b) + _dot(lo, b)


def _head_sums(x, e, et):
    return _dot_f32_by_bf16(_dot_f32_by_bf16(x, e), et)


def _layer_norm(x, g, b):
    mu = jnp.mean(x, axis=-1, keepdims=True)
    xc = x - mu
    var = jnp.mean(xc * xc, axis=-1, keepdims=True)
    return xc * lax.rsqrt(var + LN_EPS) * g + b


def _sigmoid(z):
    return 1.0 / (1.0 + jnp.exp(-z))


def _ln_proj_kernel(x_ref, g_ref, b_ref, w_ref, o_ref):
    h = _layer_norm(x_ref[...], g_ref[...], b_ref[...])
    o_ref[...] = _dot(h.astype(BF16), w_ref[...])


def ln_proj(x, g, b, w_bf16, tn):
    n, d = x.shape
    cols = w_bf16.shape[1]
    return pl.pallas_call(
        _ln_proj_kernel,
        grid=(cols // tn, n // ROW_TILE),
        in_specs=[
            pl.BlockSpec((ROW_TILE, d), lambda j, i: (i, 0)),
            pl.BlockSpec((1, d), lambda j, i: (0, 0)),
            pl.BlockSpec((1, d), lambda j, i: (0, 0)),
            pl.BlockSpec((d, tn), lambda j, i: (0, j)),
        ],
        out_specs=pl.BlockSpec((ROW_TILE, tn), lambda j, i: (i, j)),
        out_shape=jax.ShapeDtypeStruct((n, cols), F32),
        compiler_params=_cparams(("parallel", "parallel")),
        name="ln_proj",
    )(x, g, b, w_bf16)


def _rot_half(x, head):
    w = x.shape[-1]
    half = head // 2
    lane = lax.broadcasted_iota(I32, x.shape, 1)
    left = pltpu.roll(x, w - half, axis=1)
    right = pltpu.roll(x, half, axis=1)
    return jnp.where((lane % head) < half, left, right)


def _rope_kernel(q_ref, iq_ref, ka_ref, ikw_ref, c128_ref, s128_ref, c64_ref, s64_ref,
                 qo_ref, iqo_ref, ko_ref, iko_ref, iwo_ref):
    c128, s128 = c128_ref[...], s128_ref[...]
    c64, s64 = c64_ref[...], s64_ref[...]

    def rope(x, head, c, s):
        rep = x.shape[-1] // LANES
        if rep > 1:
            c = jnp.concatenate([c] * rep, axis=1)
            s = jnp.concatenate([s] * rep, axis=1)
        return x * c + _rot_half(x, head) * s

    q = rope(q_ref[...], A_HEAD, c128, s128)
    qo_ref[...] = (q * (A_HEAD ** -0.5)).astype(BF16)
    iqo_ref[...] = rope(iq_ref[...], IDX_DIM, c64, s64).astype(BF16)
    ko_ref[...] = rope(ka_ref[...], A_HEAD, c128, s128)
    ikw = ikw_ref[...]
    ik = rope(ikw, IDX_DIM, c64, s64)
    iko_ref[...] = ik[:, :IDX_DIM]
    iwo_ref[...] = ikw[:, IDX_DIM:IDX_DIM + IDX_HEADS]


def rope_all(p, c128, s128, c64, s64):
    n = p.shape[0]
    tm = ROW_TILE
    row = lambda w, blk: pl.BlockSpec((tm, w), lambda i: (i, blk))
    return pl.pallas_call(
        _rope_kernel,
        grid=(n // tm,),
        in_specs=[row(A_DIM, C_Q // A_DIM), row(A_DIM, C_IQ // A_DIM),
                  row(A_KV_DIM, C_KA // A_KV_DIM), row(LANES, C_IK // LANES),
                  row(LANES, 0), row(LANES, 0), row(LANES, 0), row(LANES, 0)],
        out_specs=[row(A_DIM, 0), row(A_DIM, 0), row(A_KV_DIM, 0),
                   row(IDX_DIM, 0), row(IDX_HEADS, 0)],
        out_shape=[jax.ShapeDtypeStruct((n, A_DIM), BF16),
                   jax.ShapeDtypeStruct((n, IDX_HEADS * IDX_DIM), BF16),
                   jax.ShapeDtypeStruct((n, A_KV_DIM), F32),
                   jax.ShapeDtypeStruct((n, IDX_DIM), F32),
                   jax.ShapeDtypeStruct((n, IDX_HEADS), F32)],
        compiler_params=_cparams(("parallel",)),
        name="rope",
    )(p, p, p, p, c128, s128, c64, s64)


def _rwkv_pre_kernel(t_real, tp, from_rows, *refs):
    (x_ref, lo_ref, px_ref, plo_ref, mu_ref, mulo_ref, w0_ref, wb_ref, a0_ref, ab_ref,
     gb_ref, kk_ref, ka_ref, rk_ref, e_ref, et_ref,
     r_o, ld_o, k_o, v_o, kk_o, b_o, g_o, bon_o) = refs
    x = x_ref[...]
    lo = lo_ref[...]
    tm = x.shape[0]
    if from_rows:
        i = pl.program_id(0)
        row = lax.broadcasted_iota(I32, (tm, 1), 0)
        t = (i * tm + row) % tp
        first = row == 0
        sx = jnp.where(first, px_ref[SUBLANES - 1:SUBLANES, :], pltpu.roll(x, 1, axis=0))
        slo = jnp.where(first, plo_ref[SUBLANES - 1:SUBLANES, :], pltpu.roll(lo, 1, axis=0))
        sx = jnp.where(t == 0, 0.0, sx)
        slo = jnp.where(t == 0, 0.0, slo)
        live = t < t_real
    else:
        sx = px_ref[...]
        slo = plo_ref[...]
        live = None
    xx = x + (sx - x) * mu_ref[...]
    xlo = lo + (slo - lo) * mulo_ref[...]
    r = xx[:, C_R:C_R + R_DIM]
    k = xx[:, C_K:C_K + R_DIM]
    v = xx[:, C_V:C_V + R_DIM]
    wa = xlo[:, :LANES]
    xg = xlo[:, LANES:]
    z = w0_ref[...] + _dot(jnp.tanh(wa).astype(BF16), wb_ref[...])
    nz = -z
    softplus = jnp.maximum(nz, 0.0) + jnp.log(1.0 + jnp.exp(-jnp.abs(nz)))
    logd = -jnp.exp(-softplus - 0.5)
    a = _sigmoid(a0_ref[...] + _dot(wa.astype(BF16), ab_ref[...]))
    g = _dot(_sigmoid(xg).astype(BF16), gb_ref[...])
    e, et = e_ref[...], et_ref[...]
    kkr = k * kk_ref[...]
    ss = _head_sums(kkr * kkr, e, et)
    kk = kkr / jnp.maximum(jnp.sqrt(ss), 1e-12)
    k2 = k * (1.0 + (a - 1.0) * ka_ref[...])
    bonus = _head_sums(r * k2 * rk_ref[...], e, et) * v
    b = kk * a
    if live is not None:
        zero = lambda y: jnp.where(live, y, 0.0)
        logd, k2s, vs, kk, b = zero(logd), zero(k2), zero(v), zero(kk), zero(b)
    else:
        k2s, vs = k2, v
    r_o[...] = r
    ld_o[...] = logd
    k_o[...] = k2s
    v_o[...] = vs
    kk_o[...] = kk
    b_o[...] = b
    g_o[...] = g
    bon_o[...] = bonus


def rwkv_pre(p, row0, nrows, prev, pw, t_real, tp):
    tm = min(ROW_TILE, nrows)
    blk0 = row0 // tm
    from_rows = prev is None
    xw = 3 * R_DIM
    cur_x = pl.BlockSpec((tm, xw), lambda i: (blk0 + i, 0))
    cur_lo = pl.BlockSpec((tm, LORA_W), lambda i: (blk0 + i, C_LORA // LORA_W))
    if from_rows:
        r8 = tm // SUBLANES
        prev_x = pl.BlockSpec((SUBLANES, xw), lambda i: (jnp.maximum((blk0 + i) * r8 - 1, 0), 0))
        prev_lo = pl.BlockSpec((SUBLANES, LORA_W),
                               lambda i: (jnp.maximum((blk0 + i) * r8 - 1, 0), C_LORA // LORA_W))
        prev_args = (p, p)
    else:
        prev_x = pl.BlockSpec((tm, xw), lambda i: (i, 0))
        prev_lo = pl.BlockSpec((tm, LORA_W), lambda i: (i, 0))
        prev_args = prev
    full = lambda a: pl.BlockSpec(a.shape, lambda i: (0,) * a.ndim)
    params = (pw["mu_x"], pw["mu_lo"], pw["w0"], pw["w_b"], pw["a0"], pw["a_b"], pw["g_b"],
              pw["k_k"], pw["k_a"], pw["r_k"], pw["e"], pw["et"])
    out = pl.BlockSpec((tm, R_DIM), lambda i: (i, 0))
    return pl.pallas_call(
        functools.partial(_rwkv_pre_kernel, t_real, tp, from_rows),
        grid=(nrows // tm,),
        in_specs=[cur_x, cur_lo, prev_x, prev_lo] + [full(a) for a in params],
        out_specs=[out] * 8,
        out_shape=[jax.ShapeDtypeStruct((nrows, R_DIM), F32)] * 8,
        compiler_params=_cparams(("parallel",)),
        name="rwkv_pre_rows" if from_rows else "rwkv_pre_step",
    )(p, p, *prev_args, *params)


def _rwkv_scan_kernel(r_ref, ld_ref, k_ref, v_ref, kk_ref, b_ref, y_ref, s_ref, ss_scr):
    c = pl.program_id(1)

    @pl.when(c == 0)
    def _():
        ss_scr[...] = jnp.zeros_like(ss_scr)

    n = CHUNK
    n2 = 2 * n
    pairs = R_HEADS // 2
    ld_all = ld_ref[...]
    ri = lax.broadcasted_iota(I32, (n, n), 0)
    ci = lax.broadcasted_iota(I32, (n, n), 1)
    cum_all = _dot((ci <= ri).astype(F32), ld_all, HIGHEST)
    head0 = lax.broadcasted_iota(I32, (n, LANES), 1) < R_HEAD
    r4 = lax.broadcasted_iota(I32, (2 * n2, 2 * n2), 0)
    c4 = lax.broadcasted_iota(I32, (2 * n2, 2 * n2), 1)
    tri = (c4 % n) < (r4 % n) + jnp.where(r4 < n2, 0, 1)
    re = lax.broadcasted_iota(I32, (n2, n2), 0)
    ce = lax.broadcasted_iota(I32, (n2, n2), 1)
    eye = (re == ce).astype(F32)

    def stack(x):
        return jnp.concatenate([jnp.where(head0, x, 0.0), jnp.where(head0, 0.0, x)], axis=0)

    ar, bk, v2, ss, e_last = [], [], [], [], []
    for p in range(pairs):
        sl = slice(p * LANES, (p + 1) * LANES)
        cum, ld = cum_all[:, sl], ld_all[:, sl]
        e_pos = jnp.exp(cum)
        e_neg = jnp.exp(-cum)
        at = -kk_ref[:, sl] * jnp.exp(cum - ld)
        ar.append(jnp.concatenate([stack(at), stack(r_ref[:, sl] * e_pos)], axis=0).astype(BF16))
        bk.append(jnp.concatenate([stack(b_ref[:, sl] * e_neg), stack(k_ref[:, sl] * e_neg)],
                                  axis=0).astype(BF16))
        v2.append(stack(v_ref[:, sl]).astype(BF16))
        ss.append(ss_scr[p])
        e_last.append(e_pos[n - 1:n, :])
    xy0 = [_dot_nt(ar[p], ss[p].astype(BF16)) for p in range(pairs)]
    sc = [jnp.where(tri, _dot_nt(ar[p], bk[p]), 0.0) for p in range(pairs)]
    lp = [s[:n2, :n2] for s in sc]
    t = [eye + l for l in lp]
    m = 1
    while 2 * m < n:
        lpb = [l.astype(BF16) for l in lp]
        lp = [_dot(l, l) for l in lpb]
        t = [t[p] + _dot(t[p].astype(BF16), lp[p].astype(BF16)) for p in range(pairs)]
        m *= 2
    w = [xy0[p][:n2] + _dot(sc[p][:n2, n2:].astype(BF16), v2[p]) for p in range(pairs)]
    u = [_dot(t[p].astype(BF16), w[p].astype(BF16)) for p in range(pairs)]
    uv = [jnp.concatenate([u[p].astype(BF16), v2[p]], axis=0) for p in range(pairs)]
    y = [xy0[p][n2:] + _dot(sc[p][n2:].astype(BF16), uv[p]) for p in range(pairs)]
    upd = [_dot_tn(uv[p], bk[p]) for p in range(pairs)]
    for p in range(pairs):
        y_ref[:, p * LANES:(p + 1) * LANES] = y[p][:n] + y[p][n:]
        ss_scr[p] = (ss[p] + upd[p]) * e_last[p]

    @pl.when(c == pl.num_programs(1) - 1)
    def _():
        for p in range(R_HEADS // 2):
            ss = ss_scr[p]
            s_ref[0, 2 * p] = ss[:R_HEAD, :R_HEAD]
            s_ref[0, 2 * p + 1] = ss[R_HEAD:, R_HEAD:]


def rwkv_scan(r, ld, k, v, kk, b, batch, tp):
    nchunk = tp // CHUNK
    blk = pl.BlockSpec((CHUNK, R_DIM), lambda bi, c: (bi * nchunk + c, 0))
    return pl.pallas_call(
        _rwkv_scan_kernel,
        grid=(batch, nchunk),
        in_specs=[blk] * 6,
        out_specs=[blk, pl.BlockSpec((1, R_HEADS, R_HEAD, R_HEAD), lambda bi, c: (bi, 0, 0, 0))],
        out_shape=[jax.ShapeDtypeStruct((batch * tp, R_DIM), F32),
                   jax.ShapeDtypeStruct((batch, R_HEADS, R_HEAD, R_HEAD), F32)],
        scratch_shapes=[pltpu.VMEM((R_HEADS // 2, LANES, LANES), F32)],
        compiler_params=_cparams(("parallel", "arbitrary")),
        name="rwkv_scan",
    )(r, ld, k, v, kk, b)


STEP_ROWS = 16


def _rwkv_step_kernel(r_ref, ld_ref, k_ref, v_ref, kk_ref, b_ref, s_ref, y_ref, so_ref):
    r, k = r_ref[0], k_ref[0]
    dec = jnp.exp(ld_ref[0])
    na = -kk_ref[0]
    b = b_ref[0]
    v = v_ref[0]
    ys = []
    for v0 in range(0, R_HEAD, STEP_ROWS):
        rows = range(v0, v0 + STEP_ROWS)
        s = [s_ref[0, vi] for vi in rows]
        sa = [jnp.sum(x * na, axis=0, keepdims=True) for x in s]
        s_new = [x * dec + a * b + v[vi:vi + 1, :] * k for x, a, vi in zip(s, sa, rows)]
        ys += [jnp.sum(x * r, axis=0, keepdims=True) for x in s_new]
        for x, vi in zip(s_new, rows):
            so_ref[0, vi] = x
    y_ref[0] = jnp.concatenate(ys, axis=0)


def rwkv_step(r, ld, k, v, kk, b, state):
    s = state.shape[-1]
    vec = pl.BlockSpec((1, R_HEAD, s), lambda h: (h, 0, 0))
    st = pl.BlockSpec((1, R_HEAD, R_HEAD, s), lambda h: (h, 0, 0, 0))
    return pl.pallas_call(
        _rwkv_step_kernel,
        grid=(R_HEADS,),
        in_specs=[vec] * 6 + [st],
        out_specs=[vec, st],
        out_shape=[jax.ShapeDtypeStruct((R_HEADS, R_HEAD, s), F32),
                   jax.ShapeDtypeStruct(state.shape, F32)],
        compiler_params=_cparams(("parallel",)),
        name="rwkv_step",
    )(r, ld, k, v, kk, b, state)


def _select_topk(score, allowed, n_sel):
    bits = lax.bitcast_convert_type(score, I32)
    key = jnp.where(bits < 0, bits ^ jnp.int32(0x7FFFFFFF), bits)
    key = jnp.where(allowed, key, jnp.int32(INT_MIN))
    m = score.shape[0]

    def body(i, tau):
        cand = tau + lax.shift_left(jnp.int32(1), jnp.int32(31) - i)
        cnt = jnp.sum((key >= cand).astype(I32), axis=1, keepdims=True)
        return jnp.where(cnt >= n_sel, cand, tau)

    tau = lax.fori_loop(0, 32, body, jnp.full((m, 1), INT_MIN, I32))
    return jnp.logical_and(key >= tau, allowed)


KEY_TILE = 256
Q_TILES_PER_EXTENT = 3


def _dsa_prompt_block(n_sel, tk, i, q_ref, iq_ref, iw_ref, ik_ref, k_ref, v_ref, o_ref, sc_ref):
    tq = q_ref.shape[0]
    iw = iw_ref[...] * ((IDX_HEADS * IDX_DIM) ** -0.5)
    iq = iq_ref[...]
    iq_h = [iq[:, h * IDX_DIM:(h + 1) * IDX_DIM] for h in range(IDX_HEADS)]
    iw_h = [iw[:, h:h + 1] for h in range(IDX_HEADS)]
    for c0 in range(0, tk, KEY_TILE):
        c1 = min(c0 + KEY_TILE, tk)
        ikb = ik_ref[c0:c1, :].astype(BF16)
        acc = jnp.maximum(_dot_nt(iq_h[0], ikb), 0.0) * iw_h[0]
        for h in range(1, IDX_HEADS):
            acc = acc + jnp.maximum(_dot_nt(iq_h[h], ikb), 0.0) * iw_h[h]
        sc_ref[:, c0:c1] = acc
    qpos = i * tq + lax.broadcasted_iota(I32, (tq, 1), 0)
    kpos = lax.broadcasted_iota(I32, (1, tk), 1)
    sel = _select_topk(sc_ref[:, :tk], kpos <= qpos, n_sel)
    bias = jnp.where(sel, 0.0, NEG_BIG)
    q = q_ref[...]
    rep = A_HEADS // A_KV_HEADS
    for g in range(A_KV_HEADS):
        kg = k_ref[:tk, g * A_HEAD:(g + 1) * A_HEAD].astype(BF16)
        vg = v_ref[:tk, g * A_HEAD:(g + 1) * A_HEAD].astype(BF16)
        for rr in range(rep):
            h = g * rep + rr
            s = _dot_nt(q[:, h * A_HEAD:(h + 1) * A_HEAD], kg) + bias
            m = jnp.max(s, axis=1, keepdims=True)
            p = jnp.exp(s - m)
            l = jnp.sum(p, axis=1, keepdims=True)
            o_ref[:, h * A_HEAD:(h + 1) * A_HEAD] = _dot(p.astype(BF16), vg) / l


def _dsa_prompt_kernel(n_sel, *refs):
    i = pl.program_id(1)
    tq = refs[0].shape[0]
    tp = refs[4].shape[0]
    nq = tp // tq
    for lo in range(0, nq, Q_TILES_PER_EXTENT):
        hi = min(lo + Q_TILES_PER_EXTENT, nq)

        @pl.when(jnp.logical_and(i >= lo, i < hi))
        def _(hi=hi):
            _dsa_prompt_block(n_sel, hi * tq, i, *refs)


def dsa_prompt(q, iq, iw, ik, k, p, batch, tp, n_sel):
    nq = tp // Q_TILE
    qrow = lambda w: pl.BlockSpec((Q_TILE, w), lambda b, i: (b * nq + i, 0))
    keys = lambda w, blk: pl.BlockSpec((tp, w), lambda b, i: (b, blk))
    return pl.pallas_call(
        functools.partial(_dsa_prompt_kernel, n_sel),
        grid=(batch, nq),
        in_specs=[qrow(A_DIM), qrow(IDX_HEADS * IDX_DIM), qrow(IDX_HEADS),
                  keys(IDX_DIM, 0), keys(A_KV_DIM, 0), keys(A_KV_DIM, C_VA // A_KV_DIM)],
        out_specs=qrow(A_DIM),
        out_shape=jax.ShapeDtypeStruct((batch * tp, A_DIM), F32),
        scratch_shapes=[pltpu.VMEM((Q_TILE, tp), F32)],
        compiler_params=_cparams(("parallel", "parallel")),
        name="dsa_prompt",
    )(q, iq, iw, ik, k, p)


def _dsa_step_score_kernel(n_pages, page, pt_ref, iq_ref, iw_ref, ikn_ref, *refs):
    pages = refs[:n_pages]
    o_ref = refs[n_pages]
    iq = iq_ref[0]
    iw = iw_ref[0] * ((IDX_HEADS * IDX_DIM) ** -0.5)
    for j in range(n_pages):
        d = _dot(iq, pages[j][...].astype(BF16))
        o_ref[0, :, j * page:(j + 1) * page] = jnp.sum(jnp.maximum(d, 0.0) * iw, axis=0,
                                                       keepdims=True)
    dn = jnp.sum(iq.astype(F32) * ikn_ref[0], axis=1, keepdims=True)
    sn = jnp.sum(jnp.maximum(dn, 0.0) * iw, axis=0, keepdims=True)
    lane = lax.broadcasted_iota(I32, (1, LANES), 1)
    o_ref[0, :, n_pages * page:] = jnp.where(lane == 0, sn, 0.0)


def dsa_step_scores(pt_flat, iq, iw, ik_new, cik2d, n_pages, page):
    s = iq.shape[0]
    kw = n_pages * page + LANES
    page_spec = lambda j: pl.BlockSpec((IDX_DIM, page), lambda i, pt: (pt[i * n_pages + j], 0))
    grid_spec = pltpu.PrefetchScalarGridSpec(
        num_scalar_prefetch=1,
        grid=(s,),
        in_specs=[pl.BlockSpec((1, IDX_HEADS, IDX_DIM), lambda i, pt: (i, 0, 0)),
                  pl.BlockSpec((1, IDX_HEADS, 1), lambda i, pt: (i, 0, 0)),
                  pl.BlockSpec((1, 1, IDX_DIM), lambda i, pt: (i, 0, 0))]
                 + [page_spec(j) for j in range(n_pages)],
        out_specs=pl.BlockSpec((1, 1, kw), lambda i, pt: (i, 0, 0)),
    )
    return pl.pallas_call(
        functools.partial(_dsa_step_score_kernel, n_pages, page),
        grid_spec=grid_spec,
        out_shape=jax.ShapeDtypeStruct((s, 1, kw), F32),
        compiler_params=_cparams(("arbitrary",)),
        name="dsa_step_scores",
    )(pt_flat, iq, iw, ik_new, *([cik2d] * n_pages))


def _dsa_step_select_kernel(n_sel, past, sc_ref, o_ref):
    sc = sc_ref[...]
    kpos = lax.broadcasted_iota(I32, sc.shape, 1)
    sel = _select_topk(sc, kpos <= past, n_sel)
    o_ref[...] = sel.astype(F32)


def dsa_step_select(sc, n_sel, past):
    return pl.pallas_call(
        functools.partial(_dsa_step_select_kernel, n_sel, past),
        out_shape=jax.ShapeDtypeStruct(sc.shape, F32),
        compiler_params=pltpu.CompilerParams(vmem_limit_bytes=VMEM_LIMIT),
        name="dsa_step_select",
    )(sc)


def _dsa_step_attn_kernel(n_pages, page, pt_ref, q_ref, kn_ref, vn_ref, sel_ref, ex_ref, *refs):
    kp = refs[:n_pages]
    vp = refs[n_pages:2 * n_pages]
    o_ref = refs[2 * n_pages]
    q = q_ref[0]
    rep = A_HEADS // A_KV_HEADS
    w2 = page * A_KV_HEADS
    hrow = lax.broadcasted_iota(I32, (A_HEADS, w2), 0)
    col = lax.broadcasted_iota(I32, (A_HEADS, w2), 1)
    own = (col % A_KV_HEADS) == (hrow // rep)
    ex = ex_ref[...]
    logits = []
    for j in range(n_pages):
        s = _dot_nt(q, kp[j][...].astype(BF16))
        selj = _dot(sel_ref[0, :, j * page:(j + 1) * page].astype(BF16), ex)
        logits.append(jnp.where(jnp.logical_and(selj > 0.5, own), s, NEG_BIG))
    h8 = lax.broadcasted_iota(I32, (A_HEADS, A_HEAD), 0)
    kn = jnp.where(h8 < rep, kn_ref[0, 0:1, :], kn_ref[0, 1:2, :])
    vn = jnp.where(h8 < rep, vn_ref[0, 0:1, :], vn_ref[0, 1:2, :])
    sn = jnp.sum(q.astype(F32) * kn, axis=1, keepdims=True)
    seln = sel_ref[0, :, n_pages * page:n_pages * page + 1]
    sn = jnp.where(seln > 0.5, sn, NEG_BIG)
    m = sn
    for s in logits:
        m = jnp.maximum(m, jnp.max(s, axis=1, keepdims=True))
    pn = jnp.exp(sn - m)
    l = pn
    acc = pn * vn
    for j in range(n_pages):
        p = jnp.exp(logits[j] - m)
        l = l + jnp.sum(p, axis=1, keepdims=True)
        acc = acc + _dot(p.astype(BF16), vp[j][...].astype(BF16))
    o_ref[0] = acc / l


def dsa_step_attn(pt_flat, q, k_new, v_new, sel, expand, ck2d, cv2d, n_pages, page):
    s = q.shape[0]
    kw = sel.shape[-1]
    w2 = page * A_KV_HEADS
    page_spec = lambda j: pl.BlockSpec((w2, A_HEAD), lambda i, pt: (pt[i * n_pages + j], 0))
    grid_spec = pltpu.PrefetchScalarGridSpec(
        num_scalar_prefetch=1,
        grid=(s,),
        in_specs=[pl.BlockSpec((1, A_HEADS, A_HEAD), lambda i, pt: (i, 0, 0)),
                  pl.BlockSpec((1, A_KV_HEADS, A_HEAD), lambda i, pt: (i, 0, 0)),
                  pl.BlockSpec((1, A_KV_HEADS, A_HEAD), lambda i, pt: (i, 0, 0)),
                  pl.BlockSpec((1, 1, kw), lambda i, pt: (i, 0, 0)),
                  pl.BlockSpec((page, w2), lambda i, pt: (0, 0))]
                 + [page_spec(j) for j in range(n_pages)] * 2,
        out_specs=pl.BlockSpec((1, A_HEADS, A_HEAD), lambda i, pt: (i, 0, 0)),
    )
    return pl.pallas_call(
        functools.partial(_dsa_step_attn_kernel, n_pages, page),
        grid_spec=grid_spec,
        out_shape=jax.ShapeDtypeStruct((s, A_HEADS, A_HEAD), F32),
        compiler_params=_cparams(("arbitrary",)),
        name="dsa_step_attn",
    )(pt_flat, q, k_new, v_new, sel, expand, *([ck2d] * n_pages), *([cv2d] * n_pages))


def _pack_bf16_pairs(x):
    w = x.shape[1] // 2
    hi = lax.bitcast_convert_type(x[:, :w].astype(BF16).astype(F32), I32)
    lo = lax.bitcast_convert_type(x[:, w:].astype(BF16).astype(F32), I32)
    return hi | lax.shift_right_logical(lo, 16)


def _unpack_bf16_pairs(p):
    hi = lax.bitcast_convert_type(p & jnp.int32(-65536), F32)
    lo = lax.bitcast_convert_type(lax.shift_left(p, 16), F32)
    return hi, lo


def _mix_kernel(alpha, prompt_blocks, x_ref, yp_ref, bonp_ref, gp_ref, ap_ref, ys_ref, bons_ref, gs_ref,
                as_ref, e_ref, et_ref, gng_ref, gnb_ref, l0g_ref, l0b_ref, wo_ref, l1g_ref, l1b_ref,
                wr_ref, h_ref, sc_ref, pk_ref):
    is_prompt = pl.program_id(0) < prompt_blocks
    pick = lambda p_ref, s_ref: jnp.where(is_prompt, p_ref[...], s_ref[...])
    e, et = e_ref[...], et_ref[...]
    y = pick(yp_ref, ys_ref)
    inv = 1.0 / R_HEAD
    mu = _head_sums(y, e, et) * inv
    d = y - mu
    var = _head_sums(d * d, e, et) * inv
    yn = d * lax.rsqrt(var + GN_EPS) * gng_ref[...] + gnb_ref[...]
    r_out = (yn + pick(bonp_ref, bons_ref)) * pick(gp_ref, gs_ref)
    mix = (_dot(r_out.astype(BF16), wo_ref[:R_DIM, :])
           + _dot(pick(ap_ref, as_ref).astype(BF16), wo_ref[R_DIM:, :]))
    h0 = _layer_norm(x_ref[...], l0g_ref[...], l0b_ref[...])
    h1 = _layer_norm(alpha * h0 + mix, l1g_ref[...], l1b_ref[...])
    h_ref[...] = h1
    w_hi, w_lo = _split_bf16(wr_ref[...])
    h_hi, h_lo = _split_bf16(h1)
    sc_ref[...] = _sigmoid(_dot_nt(w_hi, h_hi) + _dot_nt(w_hi, h_lo) + _dot_nt(w_lo, h_hi))
    pk_ref[...] = _pack_bf16_pairs(h1)


def mix_ln1_router(x, prompt_parts, step_parts, pw, alpha):
    n, d = x.shape
    tm = ROW_TILE
    pb = prompt_parts[0].shape[0] // tm
    row = lambda w: pl.BlockSpec((tm, w), lambda i: (i, 0))
    head = lambda w: pl.BlockSpec((tm, w), lambda i: (jnp.minimum(i, pb - 1), 0))
    tail = lambda w: pl.BlockSpec((tm, w), lambda i: (jnp.maximum(i - pb, 0), 0))
    full = lambda a: pl.BlockSpec(a.shape, lambda i: (0,) * a.ndim)
    params = (pw["e"], pw["et"], pw["gn_g"], pw["gn_b"], pw["ln0_g"], pw["ln0_b"], pw["w_out"],
              pw["ln1_g"], pw["ln1_b"], pw["w_router_t"])
    widths = (R_DIM, R_DIM, R_DIM, A_DIM)
    return pl.pallas_call(
        functools.partial(_mix_kernel, alpha, pb),
        grid=(n // tm,),
        in_specs=[row(d)] + [head(w) for w in widths] + [tail(w) for w in widths]
                 + [full(a) for a in params],
        out_specs=[row(d), pl.BlockSpec((N_EXPERTS, tm), lambda i: (0, i)), row(d // 2)],
        out_shape=[jax.ShapeDtypeStruct((n, d), F32),
                   jax.ShapeDtypeStruct((N_EXPERTS, n), F32),
                   jax.ShapeDtypeStruct((n, d // 2), I32)],
        compiler_params=_cparams(("parallel",)),
        name="mix_ln1_router",
    )(x, *prompt_parts, *step_parts, *params)


def _route_kernel(n_prompt, tp, t_real, s_dec, sc_ref, bias_ref, idx_ref, gate_ref, pos_ref, cnt_ref,
                  cnt_scr):
    scores = sc_ref[...]
    biased = scores + bias_ref[...]
    tn = scores.shape[1]
    per = N_EXPERTS // N_EXPERT_GROUPS
    sub = lax.broadcasted_iota(I32, (per, tn), 0)
    grp_rows = []
    for g in range(N_EXPERT_GROUPS):
        xg = biased[g * per:(g + 1) * per, :]
        m1 = jnp.max(xg, axis=0, keepdims=True)
        first = jnp.min(jnp.where(xg == m1, sub, per), axis=0, keepdims=True)
        m2 = jnp.max(jnp.where(sub == first, -jnp.inf, xg), axis=0, keepdims=True)
        grp_rows.append(m1 + m2)
    grp = jnp.concatenate(grp_rows, axis=0)
    gi = lax.broadcasted_iota(I32, (N_EXPERT_GROUPS, tn), 0)
    gsel = jnp.zeros((N_EXPERT_GROUPS, tn), jnp.bool_)
    for _ in range(TOPK_GROUPS):
        m = jnp.max(grp, axis=0, keepdims=True)
        first = jnp.min(jnp.where(grp == m, gi, N_EXPERT_GROUPS), axis=0, keepdims=True)
        hit = gi == first
        gsel = jnp.logical_or(gsel, hit)
        grp = jnp.where(hit, -jnp.inf, grp)
    ei = lax.broadcasted_iota(I32, (N_EXPERTS, tn), 0)
    emask = jnp.concatenate(
        [jnp.broadcast_to(gsel[g:g + 1, :], (per, tn)) for g in range(N_EXPERT_GROUPS)], axis=0)
    cand = jnp.where(emask, biased, -jnp.inf)
    idxs, gates, hits = [], [], []
    for _ in range(TOP_K):
        m = jnp.max(cand, axis=0, keepdims=True)
        first = jnp.min(jnp.where(cand == m, ei, N_EXPERTS), axis=0, keepdims=True)
        hit = ei == first
        idxs.append(first)
        hits.append(hit)
        gates.append(jnp.sum(jnp.where(hit, scores, 0.0), axis=0, keepdims=True))
        cand = jnp.where(hit, -jnp.inf, cand)
    gate = jnp.concatenate(gates, axis=0)
    gate = gate / jnp.sum(gate, axis=0, keepdims=True) * ROUTED_SCALE
    tok = pl.program_id(0) * tn + lax.broadcasted_iota(I32, (1, tn), 1)
    live = (jnp.where(tok < n_prompt, tok % tp, tok - n_prompt)
            < jnp.where(tok < n_prompt, t_real, s_dec))
    idx_ref[...] = jnp.concatenate(idxs, axis=0)
    gate_ref[...] = jnp.where(live, gate, 0.0)
    chosen = hits[0]
    for hit in hits[1:]:
        chosen = jnp.logical_or(chosen, hit)
    onehot = jnp.where(jnp.logical_and(chosen, live), 1.0, 0.0)
    ta = lax.broadcasted_iota(I32, (tn, tn), 0)
    tb = lax.broadcasted_iota(I32, (tn, tn), 1)
    prefix = _dot(onehot.astype(BF16), (ta < tb).astype(BF16))

    @pl.when(pl.program_id(0) == 0)
    def _():
        cnt_scr[...] = jnp.zeros_like(cnt_scr)

    rank = prefix + cnt_scr[:, 0:1]
    pos_ref[...] = jnp.concatenate(
        [jnp.sum(jnp.where(hit, rank, 0.0), axis=0, keepdims=True) for hit in hits], axis=0).astype(I32)
    cnt_scr[...] = cnt_scr[...] + jnp.sum(onehot, axis=1, keepdims=True)
    cnt_ref[...] = cnt_scr[...].astype(I32)


def route(scores_t, e_bias, n_prompt, tp, t_real, s_dec):
    n = scores_t.shape[1]
    tn = ROW_TILE
    tok = pl.BlockSpec((TOP_K, tn), lambda i: (0, i))
    return pl.pallas_call(
        functools.partial(_route_kernel, n_prompt, tp, t_real, s_dec),
        grid=(n // tn,),
        in_specs=[pl.BlockSpec((N_EXPERTS, tn), lambda i: (0, i)),
                  pl.BlockSpec((N_EXPERTS, 1), lambda i: (0, 0))],
        out_specs=[tok, tok, tok, pl.BlockSpec((N_EXPERTS, LANES), lambda i: (0, 0))],
        out_shape=[jax.ShapeDtypeStruct((TOP_K, n), I32), jax.ShapeDtypeStruct((TOP_K, n), F32),
                   jax.ShapeDtypeStruct((TOP_K, n), I32),
                   jax.ShapeDtypeStruct((N_EXPERTS, LANES), I32)],
        scratch_shapes=[pltpu.VMEM((N_EXPERTS, LANES), F32)],
        compiler_params=_cparams(("arbitrary",)),
        name="route",
    )(scores_t, e_bias)


def _dispatch_kernel(dest_ref, x_ref, init_ref, o_ref, sem):
    del init_ref
    tm = x_ref.shape[0]

    def start(i, carry):
        for j in range(TOP_K):
            pltpu.make_async_copy(x_ref.at[pl.ds(i, 1)],
                                  o_ref.at[pl.ds(dest_ref[i * TOP_K + j], 1)], sem).start()
        return carry

    lax.fori_loop(0, tm, start, 0)
    for j in range(TOP_K):
        pltpu.make_async_copy(x_ref, o_ref.at[pl.ds(0, tm)], sem).wait()


def moe_dispatch(dest_flat, xpk, rows_total):
    n, w = xpk.shape
    tm = ROW_TILE
    init = jnp.zeros((rows_total, w), I32)
    return pl.pallas_call(
        _dispatch_kernel,
        grid=(n // tm,),
        in_specs=[pl.BlockSpec((tm * TOP_K,), lambda i: (i,), memory_space=pltpu.SMEM),
                  pl.BlockSpec((tm, w), lambda i: (i, 0)),
                  pl.BlockSpec(memory_space=pl.ANY)],
        out_specs=pl.BlockSpec(memory_space=pl.ANY),
        out_shape=jax.ShapeDtypeStruct((rows_total, w), I32),
        scratch_shapes=[pltpu.SemaphoreType.DMA(())],
        input_output_aliases={2: 0},
        compiler_params=_cparams(("arbitrary",)),
        name="moe_dispatch",
    )(dest_flat, xpk, init)


def _experts_kernel(be_ref, nu_ref, x_ref, wg_ref, wu_ref, wd_ref, o_ref, wg_s, wu_s, wd_s):
    i = pl.program_id(0)
    used = i < nu_ref[0]
    prev = be_ref[jnp.maximum(i - 1, 0)]
    fresh = jnp.logical_or(i == 0, be_ref[i] != prev)

    @pl.when(jnp.logical_and(used, fresh))
    def _():
        wg_s[...] = wg_ref[0].astype(BF16)
        wu_s[...] = wu_ref[0].astype(BF16)
        wd_s[...] = wd_ref[0].astype(BF16)

    @pl.when(used)
    def _():
        hi, lo = _unpack_bf16_pairs(x_ref[...])
        hi, lo = hi.astype(BF16), lo.astype(BF16)
        half = hi.shape[1]
        gp = _dot(hi, wg_s[:half, :]) + _dot(lo, wg_s[half:, :])
        up = _dot(hi, wu_s[:half, :]) + _dot(lo, wu_s[half:, :])
        act = gp * _sigmoid(gp) * up
        o_ref[...] = _pack_bf16_pairs(_dot(act.astype(BF16), wd_s[...]))

    @pl.when(jnp.logical_not(used))
    def _():
        o_ref[...] = jnp.zeros_like(o_ref)


def moe_experts(blk_e, n_used, xs, nb, w_gate, w_up, w_down):
    w = xs.shape[1]
    rows = nb * EXPERT_TILE
    _, d, de = w_gate.shape
    last = lambda i, nu: jnp.minimum(i, nu[0] - 1)
    grid_spec = pltpu.PrefetchScalarGridSpec(
        num_scalar_prefetch=2,
        grid=(nb,),
        in_specs=[pl.BlockSpec((EXPERT_TILE, w), lambda i, be, nu: (last(i, nu), 0)),
                  pl.BlockSpec((1, d, de), lambda i, be, nu: (be[last(i, nu)], 0, 0)),
                  pl.BlockSpec((1, d, de), lambda i, be, nu: (be[last(i, nu)], 0, 0)),
                  pl.BlockSpec((1, de, d), lambda i, be, nu: (be[last(i, nu)], 0, 0))],
        out_specs=pl.BlockSpec((EXPERT_TILE, w), lambda i, be, nu: (i, 0)),
        scratch_shapes=[pltpu.VMEM((d, de), BF16), pltpu.VMEM((d, de), BF16),
                        pltpu.VMEM((de, d), BF16)],
    )
    return pl.pallas_call(
        _experts_kernel,
        grid_spec=grid_spec,
        out_shape=jax.ShapeDtypeStruct((rows, w), I32),
        compiler_params=_cparams(("arbitrary",)),
        name="moe_experts",
    )(blk_e, n_used, xs, w_gate, w_up, w_down)


def _combine_kernel(alpha, dest_ref, h_ref, gate_ref, wsg_ref, wsu_ref, wsd_ref, l2g_ref, l2b_ref,
                    ys_ref, o_ref, buf, sem):
    tm = h_ref.shape[0]

    def start(i, carry):
        for j in range(TOP_K):
            pltpu.make_async_copy(ys_ref.at[pl.ds(dest_ref[i * TOP_K + j], 1)],
                                  buf.at[j, pl.ds(i, 1)], sem).start(priority=j % 2)
        return carry

    lax.fori_loop(0, tm, start, 0)
    h = h_ref[...]
    hb = h.astype(BF16)
    gp = _dot(hb, wsg_ref[...])
    up = _dot(hb, wsu_ref[...])
    shared = _dot((gp * _sigmoid(gp) * up).astype(BF16), wsd_ref[...])
    for j in range(TOP_K):
        pltpu.make_async_copy(ys_ref.at[pl.ds(0, tm)], buf.at[j], sem).wait()
    gate = gate_ref[...]
    half = buf.shape[2]
    acc_hi = jnp.zeros((tm, half), F32)
    acc_lo = jnp.zeros((tm, half), F32)
    for j in range(TOP_K):
        hi, lo = _unpack_bf16_pairs(buf[j])
        gj = gate[:, j:j + 1]
        acc_hi = acc_hi + gj * hi
        acc_lo = acc_lo + gj * lo
    f = jnp.concatenate([acc_hi, acc_lo], axis=1) + shared
    o_ref[...] = _layer_norm(alpha * h + f, l2g_ref[...], l2b_ref[...])


def moe_combine(dest_flat, h1, gate, ys, pw, alpha):
    n, d = h1.shape
    tm = Q_TILE
    full = lambda a: pl.BlockSpec(a.shape, lambda i: (0,) * a.ndim)
    params = (pw["ws_gate"], pw["ws_up"], pw["ws_down"], pw["ln2_g"], pw["ln2_b"])
    return pl.pallas_call(
        functools.partial(_combine_kernel, alpha),
        grid=(n // tm,),
        in_specs=[pl.BlockSpec((tm * TOP_K,), lambda i: (i,), memory_space=pltpu.SMEM),
                  pl.BlockSpec((tm, d), lambda i: (i, 0)),
                  pl.BlockSpec((tm, TOP_K), lambda i: (i, 0))]
                 + [full(a) for a in params]
                 + [pl.BlockSpec(memory_space=pl.ANY)],
        out_specs=pl.BlockSpec((tm, d), lambda i: (i, 0)),
        out_shape=jax.ShapeDtypeStruct((n, d), F32),
        scratch_shapes=[pltpu.VMEM((TOP_K, tm, d // 2), I32), pltpu.SemaphoreType.DMA(())],
        compiler_params=_cparams(("arbitrary",)),
        name="moe_combine",
    )(dest_flat, h1, gate, *params, ys)


def _round_up(x, m):
    return (x + m - 1) // m * m


def _rope_tables(pos, head):
    half = head // 2
    inv = ROPE_THETA ** (-jnp.arange(half, dtype=F32) / half)
    ang = pos.astype(F32)[:, None] * inv[None, :]
    cos, sin = jnp.cos(ang), jnp.sin(ang)
    rep = LANES // head
    c = jnp.tile(jnp.concatenate([cos, cos], axis=1), (1, rep))
    s = jnp.tile(jnp.concatenate([-sin, sin], axis=1), (1, rep))
    return c, s


def _permute_cols(m):
    a0 = SHIFT_DIM
    pieces = [
        m[..., 0:3 * R_DIM],
        m[..., a0:a0 + A_DIM],
        m[..., a0 + A_DIM + 2 * A_KV_DIM:a0 + A_DIM + 2 * A_KV_DIM + IDX_HEADS * IDX_DIM],
        m[..., a0 + A_DIM:a0 + A_DIM + 2 * A_KV_DIM],
    ]
    i0 = a0 + A_DIM + 2 * A_KV_DIM + IDX_HEADS * IDX_DIM
    pieces.append(m[..., i0:i0 + IDX_DIM + IDX_HEADS])
    pad = lambda w: jnp.zeros(m.shape[:-1] + (w,), m.dtype)
    pieces.append(pad(LANES - IDX_DIM - IDX_HEADS))
    pieces.append(m[..., 3 * R_DIM:SHIFT_DIM])
    pieces.append(pad(LORA_W - (SHIFT_DIM - 3 * R_DIM)))
    return jnp.concatenate(pieces, axis=-1)


def kernel(x_prompt, x_sample, cache_k, cache_v, cache_idx_k, state_wkv, state_shift, page_table,
           meta, ln0_g, ln0_b, w_in, mu_shift, w0, w_b, a0, a_b, g_b, k_k, k_a, r_k, gn_g, gn_b,
           w_out, ln1_g, ln1_b, w_router, e_bias, w_gate, w_up, w_down, ws_gate, ws_up, ws_down,
           ln2_g, ln2_b):
    depth = w_in.shape[0]
    assert depth == 1, "single trunk layer"
    bsz, s_p, d = x_prompt.shape
    s_dec, s_s, _ = x_sample.shape
    assert s_s == 1, "one decode token per sequence"
    t_real = N_META + s_p
    tp = _round_up(t_real, LANES)
    assert (bsz * tp) % ROW_TILE == 0
    sp = _round_up(s_dec, ROW_TILE)
    n_prompt = bsz * tp
    n = n_prompt + sp
    n_pool, page = cache_k.shape[1], cache_k.shape[2]
    n_pages = page_table.shape[1]
    past = n_pages * page
    alpha = float((2 * depth) ** 0.25)
    row2 = lambda a: a.reshape(1, -1)

    meta_rows = jnp.broadcast_to(meta[None], (bsz, N_META, d))
    xp = jnp.concatenate([meta_rows, x_prompt, jnp.zeros((bsz, tp - t_real, d), F32)], axis=1)
    x_all = jnp.concatenate([xp.reshape(n_prompt, d), x_sample.reshape(s_dec, d),
                             jnp.zeros((sp - s_dec, d), F32)], axis=0)
    pos = jnp.concatenate([jnp.tile(jnp.arange(tp), bsz), jnp.full((sp,), past)])
    c128, s128 = _rope_tables(pos, A_HEAD)
    c64, s64 = _rope_tables(pos, IDX_DIM)

    w_in_k = _permute_cols(w_in[0]).astype(BF16)
    mu_k = _permute_cols(
        jnp.concatenate([mu_shift[0], jnp.zeros((w_in.shape[2] - SHIFT_DIM,), F32)])[None, :])
    head_of = jnp.arange(R_DIM) // R_HEAD
    e_mat = (head_of[:, None] == jnp.arange(R_HEADS)[None, :]).astype(F32)
    zpad = lambda a, rows_before, rows_total: jnp.concatenate(
        [jnp.zeros((rows_before, a.shape[1]), a.dtype), a,
         jnp.zeros((rows_total - rows_before - a.shape[0], a.shape[1]), a.dtype)], axis=0)
    pw = {
        "mu_x": mu_k[:, :3 * R_DIM], "mu_lo": mu_k[:, C_LORA:],
        "w0": row2(w0[0]), "a0": row2(a0[0]), "k_k": row2(k_k[0]), "k_a": row2(k_a[0]),
        "r_k": row2(r_k[0]), "gn_g": row2(gn_g[0]), "gn_b": row2(gn_b[0]),
        "w_b": zpad(w_b[0], 0, LANES).astype(BF16),
        "a_b": zpad(a_b[0], D_DECAY_LORA, LANES).astype(BF16),
        "g_b": zpad(g_b[0], 0, LORA_W - LANES).astype(BF16),
        "e": e_mat.astype(BF16), "et": e_mat.T.astype(BF16),
        "ln0_g": row2(ln0_g), "ln0_b": row2(ln0_b),
        "ln1_g": row2(ln1_g[0]), "ln1_b": row2(ln1_b[0]),
        "ln2_g": row2(ln2_g[0]), "ln2_b": row2(ln2_b[0]),
        "w_out": w_out[0].astype(BF16), "w_router_t": w_router[0].T,
        "ws_gate": ws_gate[0].astype(BF16), "ws_up": ws_up[0].astype(BF16),
        "ws_down": ws_down[0].astype(BF16),
    }

    p = ln_proj(x_all, pw["ln0_g"], pw["ln0_b"], w_in_k, tn=P_COLS // 3)
    q_r, iq_r, k_r, ik_r, iw = rope_all(p, c128, s128, c64, s64)

    pre_p = rwkv_pre(p, 0, n_prompt, None, pw, t_real, tp)
    shift_k = _permute_cols(jnp.concatenate(
        [state_shift[0], jnp.zeros((s_dec, w_in.shape[2] - SHIFT_DIM), F32)], axis=1))
    shift_k = jnp.concatenate([shift_k, jnp.zeros((sp - s_dec, P_COLS), F32)], axis=0)
    pre_s = rwkv_pre(p, n_prompt, sp, (shift_k[:, :3 * R_DIM], shift_k[:, C_LORA:]), pw, t_real, tp)
    r_p, ld_p, k_p, v_p, kk_p, b_p, g_p, bon_p = pre_p
    r_s, ld_s, k_s, v_s, kk_s, b_s, g_s, bon_s = pre_s
    y_p, wkv_p = rwkv_scan(r_p, ld_p, k_p, v_p, kk_p, b_p, bsz, tp)
    heads = lambda a: a[:s_dec].reshape(s_dec, R_HEADS, R_HEAD).transpose(1, 2, 0)
    y_hs, wkv_hs = rwkv_step(heads(r_s), heads(ld_s), heads(k_s), heads(v_s), heads(kk_s), heads(b_s),
                             state_wkv[0].transpose(1, 2, 3, 0))
    wkv_s = wkv_hs.transpose(3, 0, 1, 2)
    y_s = jnp.concatenate([y_hs.transpose(2, 0, 1).reshape(s_dec, R_DIM),
                           jnp.zeros((sp - s_dec, R_DIM), F32)], axis=0)

    n_sel_p = min(TOPK_KEYS, t_real // 4)
    a_p = dsa_prompt(q_r, iq_r, iw, ik_r, k_r, p, bsz, tp, n_sel_p)
    n_sel_s = min(TOPK_KEYS, (past + 1) // 4)
    pt_flat = page_table.reshape(-1).astype(I32)
    srow = slice(n_prompt, n_prompt + s_dec)
    sc_s = dsa_step_scores(pt_flat, iq_r[srow].reshape(s_dec, IDX_HEADS, IDX_DIM),
                           iw[srow].reshape(s_dec, IDX_HEADS, 1), ik_r[srow].reshape(s_dec, 1, IDX_DIM),
                           cache_idx_k[0].transpose(0, 2, 1).reshape(n_pool * IDX_DIM, page),
                           n_pages, page)
    sel_s = dsa_step_select(sc_s.reshape(s_dec, -1), n_sel_s, past).reshape(sc_s.shape)
    slot = jnp.arange(page)[:, None]
    expand = (jnp.arange(page * A_KV_HEADS)[None, :] // A_KV_HEADS == slot).astype(BF16)
    a_s = dsa_step_attn(pt_flat, q_r[srow].reshape(s_dec, A_HEADS, A_HEAD),
                        k_r[srow].reshape(s_dec, A_KV_HEADS, A_HEAD),
                        p[srow, C_VA:C_VA + A_KV_DIM].reshape(s_dec, A_KV_HEADS, A_HEAD),
                        sel_s, expand,
                        cache_k[0].reshape(n_pool * page * A_KV_HEADS, A_HEAD),
                        cache_v[0].reshape(n_pool * page * A_KV_HEADS, A_HEAD), n_pages, page)
    a_s = jnp.concatenate([a_s.reshape(s_dec, A_DIM), jnp.zeros((sp - s_dec, A_DIM), F32)], axis=0)

    h1, scores_t, xpk = mix_ln1_router(x_all, (y_p, bon_p, g_p, a_p), (y_s, bon_s, g_s, a_s), pw, alpha)
    eidx_t, gate_t, pos_t, counts = route(scores_t, e_bias[0].reshape(N_EXPERTS, 1),
                                          n_prompt, tp, t_real, s_dec)

    row_id = np.arange(n)
    tok_ok = np.where(row_id < n_prompt, (row_id % tp) < t_real, row_id < n_prompt + s_dec)
    n_dead = int(n - tok_ok.sum())
    nb = (int(tok_ok.sum()) * TOP_K + N_EXPERTS * (EXPERT_TILE - 1)) // EXPERT_TILE + 1
    dead_rows = nb * EXPERT_TILE + (np.cumsum(~tok_ok) - 1)[:, None] * TOP_K + np.arange(TOP_K)[None, :]
    counts = counts[:, 0]
    padded = (counts + EXPERT_TILE - 1) // EXPERT_TILE * EXPERT_TILE
    seg_end = jnp.cumsum(padded)
    seg_start = seg_end - padded
    experts = jnp.arange(N_EXPERTS)
    start_of = jnp.sum(jnp.where(eidx_t[:, :, None] == experts, seg_start, 0), axis=-1)
    dest = (start_of + pos_t).T
    dest_w = jnp.where(tok_ok[:, None], dest, dead_rows).astype(I32).reshape(-1)
    dest_r = jnp.where(tok_ok[:, None], dest, 0).astype(I32).reshape(-1)
    blk_row = jnp.arange(nb) * EXPERT_TILE
    blk_e = jnp.minimum(jnp.sum(seg_end[None, :] <= blk_row[:, None], axis=1),
                        N_EXPERTS - 1).astype(I32)
    n_used = (seg_end[-1] // EXPERT_TILE).astype(I32).reshape(1)

    xs = moe_dispatch(dest_w, xpk, nb * EXPERT_TILE + n_dead * TOP_K)
    ys = moe_experts(blk_e, n_used, xs, nb, w_gate[0], w_up[0], w_down[0])
    h2 = moe_combine(dest_r, h1, gate_t.T, ys, pw, alpha)

    def prompt_rows(a):
        return a[:n_prompt].reshape(bsz, tp, -1)[:, :t_real]

    y_prompt = h2[:n_prompt].reshape(bsz, tp, d)[:, N_META:t_real]
    y_sample = h2[srow].reshape(s_dec, 1, d)
    k_prompt = prompt_rows(k_r).reshape(1, bsz, t_real, A_KV_HEADS, A_HEAD)
    v_prompt = prompt_rows(p[:, C_VA:C_VA + A_KV_DIM]).reshape(1, bsz, t_real, A_KV_HEADS, A_HEAD)
    ik_prompt = prompt_rows(ik_r)[None]
    last = jnp.arange(bsz) * tp + t_real - 1
    unperm = lambda rows: jnp.concatenate([rows[:, :3 * R_DIM],
                                           rows[:, C_LORA:C_LORA + SHIFT_DIM - 3 * R_DIM]], axis=1)
    shift_prompt = unperm(p[last])[None]
    k_sample = k_r[srow].reshape(1, s_dec, 1, A_KV_HEADS, A_HEAD)
    v_sample = p[srow, C_VA:C_VA + A_KV_DIM].reshape(1, s_dec, 1, A_KV_HEADS, A_HEAD)
    ik_sample = ik_r[srow].reshape(1, s_dec, 1, IDX_DIM)
    shift_sample = unperm(p[srow])[None]
    return (y_prompt, y_sample, k_prompt, v_prompt, ik_prompt, wkv_p[None], shift_prompt,
            k_sample, v_sample, ik_sample, wkv_s[None], shift_sample)
```

```python
import functools

import numpy as np
import jax
import jax.numpy as jnp
from jax import lax
from jax.experimental import pallas as pl
from jax.experimental.pallas import tpu as pltpu

F32 = jnp.float32
BF16 = jnp.bfloat16
I32 = jnp.int32
HIGHEST = lax.Precision.HIGHEST

N_META = 16
R_HEADS, R_HEAD = 16, 64
R_DIM = R_HEADS * R_HEAD
D_DECAY_LORA, D_AAA_LORA, D_GATE_LORA = 64, 64, 160
SHIFT_DIM = 3 * R_DIM + D_DECAY_LORA + D_AAA_LORA + D_GATE_LORA
GN_EPS = 64e-5
A_HEADS, A_KV_HEADS, A_HEAD = 8, 2, 128
A_DIM = A_HEADS * A_HEAD
A_KV_DIM = A_KV_HEADS * A_HEAD
IDX_HEADS, IDX_DIM = 16, 64
TOPK_KEYS = 256
ROPE_THETA = 10000.0
N_EXPERTS, N_EXPERT_GROUPS, TOPK_GROUPS, TOP_K = 64, 8, 4, 8
ROUTED_SCALE = 2.5
LN_EPS = 1e-5

LANES = 128
SUBLANES = 8
ROW_TILE = 256
Q_TILE = 128
CHUNK = 64
EXPERT_TILE = 256
VMEM_LIMIT = 56 * 1024 * 1024
SEARCH_GROUPS = 4
NEG_BIG = -1e30
INT_MIN = -2 ** 31

C_R, C_K, C_V = 0, R_DIM, 2 * R_DIM
C_Q = 3 * R_DIM
C_IQ = C_Q + A_DIM
C_KA = C_IQ + IDX_HEADS * IDX_DIM
C_VA = C_KA + A_KV_DIM
C_IK = C_VA + A_KV_DIM
C_LORA = C_IK + LANES
LORA_W = 384
P_COLS = C_LORA + LORA_W


def _cparams(sem):
    return pltpu.CompilerParams(dimension_semantics=sem, vmem_limit_bytes=VMEM_LIMIT)


def _dot(a, b, precision=None):
    return jnp.dot(a, b, preferred_element_type=F32, precision=precision)


def _dot_nt(a, b, precision=None):
    return lax.dot_general(a, b, (((1,), (1,)), ((), ())), preferred_element_type=F32,
                           precision=precision)


def _dot_tn(a, b, precision=None):
    return lax.dot_general(a, b, (((0,), (0,)), ((), ())), preferred_element_type=F32,
                           precision=precision)


def _split_bf16(x):
    hi = x.astype(BF16)
    return hi, (x - hi.astype(F32)).astype(BF16)


def _dot_f32_by_bf16(a, b):
    hi, lo = _split_bf16(a)
    return _dot(hi, b) + _dot(lo, b)


def _head_sums(x, e, et):
    return _dot_f32_by_bf16(_dot_f32_by_bf16(x, e), et)


def _layer_norm(x, g, b):
    mu = jnp.mean(x, axis=-1, keepdims=True)
    xc = x - mu
    var = jnp.mean(xc * xc, axis=-1, keepdims=True)
    return xc * lax.rsqrt(var + LN_EPS) * g + b


def _sigmoid(z):
    return 1.0 / (1.0 + jnp.exp(-z))


def _ln_proj_kernel(x_ref, g_ref, b_ref, w_ref, o_ref):
    h = _layer_norm(x_ref[...], g_ref[...], b_ref[...])
    o_ref[...] = _dot(h.astype(BF16), w_ref[...])


def ln_proj(x, g, b, w_bf16, tn):
    n, d = x.shape
    cols = w_bf16.shape[1]
    return pl.pallas_call(
        _ln_proj_kernel,
        grid=(cols // tn, n // ROW_TILE),
        in_specs=[
            pl.BlockSpec((ROW_TILE, d), lambda j, i: (i, 0)),
            pl.BlockSpec((1, d), lambda j, i: (0, 0)),
            pl.BlockSpec((1, d), lambda j, i: (0, 0)),
            pl.BlockSpec((d, tn), lambda j, i: (0, j)),
        ],
        out_specs=pl.BlockSpec((ROW_TILE, tn), lambda j, i: (i, j)),
        out_shape=jax.ShapeDtypeStruct((n, cols), F32),
        compiler_params=_cparams(("parallel", "parallel")),
        name="ln_proj",
    )(x, g, b, w_bf16)


def _rot_half(x, head):
    w = x.shape[-1]
    half = head // 2
    lane = lax.broadcasted_iota(I32, x.shape, 1)
    left = pltpu.roll(x, w - half, axis=1)
    right = pltpu.roll(x, half, axis=1)
    return jnp.where((lane % head) < half, left, right)


def _rope_kernel(q_ref, iq_ref, ka_ref, ikw_ref, c128_ref, s128_ref, c64_ref, s64_ref,
                 qo_ref, iqo_ref, ko_ref, iko_ref, iwo_ref):
    c128, s128 = c128_ref[...], s128_ref[...]
    c64, s64 = c64_ref[...], s64_ref[...]

    def rope(x, head, c, s):
        rep = x.shape[-1] // LANES
        if rep > 1:
            c = jnp.concatenate([c] * rep, axis=1)
            s = jnp.concatenate([s] * rep, axis=1)
        return x * c + _rot_half(x, head) * s

    q = rope(q_ref[...], A_HEAD, c128, s128)
    qo_ref[...] = (q * (A_HEAD ** -0.5)).astype(BF16)
    iqo_ref[...] = rope(iq_ref[...], IDX_DIM, c64, s64).astype(BF16)
    ko_ref[...] = rope(ka_ref[...], A_HEAD, c128, s128)
    ikw = ikw_ref[...]
    ik = rope(ikw, IDX_DIM, c64, s64)
    iko_ref[...] = ik[:, :IDX_DIM]
    iwo_ref[...] = ikw[:, IDX_DIM:IDX_DIM + IDX_HEADS]


def rope_all(p, c128, s128, c64, s64):
    n = p.shape[0]
    tm = ROW_TILE
    row = lambda w, blk: pl.BlockSpec((tm, w), lambda i: (i, blk))
    return pl.pallas_call(
        _rope_kernel,
        grid=(n // tm,),
        in_specs=[row(A_DIM, C_Q // A_DIM), row(A_DIM, C_IQ // A_DIM),
                  row(A_KV_DIM, C_KA // A_KV_DIM), row(LANES, C_IK // LANES),
                  row(LANES, 0), row(LANES, 0), row(LANES, 0), row(LANES, 0)],
        out_specs=[row(A_DIM, 0), row(A_DIM, 0), row(A_KV_DIM, 0),
                   row(IDX_DIM, 0), row(IDX_HEADS, 0)],
        out_shape=[jax.ShapeDtypeStruct((n, A_DIM), BF16),
                   jax.ShapeDtypeStruct((n, IDX_HEADS * IDX_DIM), BF16),
                   jax.ShapeDtypeStruct((n, A_KV_DIM), F32),
                   jax.ShapeDtypeStruct((n, IDX_DIM), F32),
                   jax.ShapeDtypeStruct((n, IDX_HEADS), F32)],
        compiler_params=_cparams(("parallel",)),
        name="rope",
    )(p, p, p, p, c128, s128, c64, s64)


def _rwkv_pre_kernel(t_real, tp, from_rows, *refs):
    (x_ref, lo_ref, px_ref, plo_ref, mu_ref, mulo_ref, w0_ref, wb_ref, a0_ref, ab_ref,
     gb_ref, kk_ref, ka_ref, rk_ref, e_ref, et_ref,
     r_o, ld_o, k_o, v_o, kk_o, b_o, g_o, bon_o) = refs
    x = x_ref[...]
    lo = lo_ref[...]
    tm = x.shape[0]
    if from_rows:
        i = pl.program_id(0)
        row = lax.broadcasted_iota(I32, (tm, 1), 0)
        t = (i * tm + row) % tp
        first = row == 0
        sx = jnp.where(first, px_ref[SUBLANES - 1:SUBLANES, :], pltpu.roll(x, 1, axis=0))
        slo = jnp.where(first, plo_ref[SUBLANES - 1:SUBLANES, :], pltpu.roll(lo, 1, axis=0))
        sx = jnp.where(t == 0, 0.0, sx)
        slo = jnp.where(t == 0, 0.0, slo)
        live = t < t_real
    else:
        sx = px_ref[...]
        slo = plo_ref[...]
        live = None
    xx = x + (sx - x) * mu_ref[...]
    xlo = lo + (slo - lo) * mulo_ref[...]
    r = xx[:, C_R:C_R + R_DIM]
    k = xx[:, C_K:C_K + R_DIM]
    v = xx[:, C_V:C_V + R_DIM]
    wa = xlo[:, :LANES]
    xg = xlo[:, LANES:]
    z = w0_ref[...] + _dot(jnp.tanh(wa).astype(BF16), wb_ref[...])
    nz = -z
    softplus = jnp.maximum(nz, 0.0) + jnp.log(1.0 + jnp.exp(-jnp.abs(nz)))
    logd = -jnp.exp(-softplus - 0.5)
    a = _sigmoid(a0_ref[...] + _dot(wa.astype(BF16), ab_ref[...]))
    g = _dot(_sigmoid(xg).astype(BF16), gb_ref[...])
    e, et = e_ref[...], et_ref[...]
    kkr = k * kk_ref[...]
    ss = _head_sums(kkr * kkr, e, et)
    kk = kkr / jnp.maximum(jnp.sqrt(ss), 1e-12)
    k2 = k * (1.0 + (a - 1.0) * ka_ref[...])
    bonus = _head_sums(r * k2 * rk_ref[...], e, et) * v
    b = kk * a
    if live is not None:
        zero = lambda y: jnp.where(live, y, 0.0)
        logd, k2s, vs, kk, b = zero(logd), zero(k2), zero(v), zero(kk), zero(b)
    else:
        k2s, vs = k2, v
    r_o[...] = r
    ld_o[...] = logd
    k_o[...] = k2s
    v_o[...] = vs
    kk_o[...] = kk
    b_o[...] = b
    g_o[...] = g
    bon_o[...] = bonus


def rwkv_pre(p, row0, nrows, prev, pw, t_real, tp):
    tm = min(ROW_TILE, nrows)
    blk0 = row0 // tm
    from_rows = prev is None
    xw = 3 * R_DIM
    cur_x = pl.BlockSpec((tm, xw), lambda i: (blk0 + i, 0))
    cur_lo = pl.BlockSpec((tm, LORA_W), lambda i: (blk0 + i, C_LORA // LORA_W))
    if from_rows:
        r8 = tm // SUBLANES
        prev_x = pl.BlockSpec((SUBLANES, xw), lambda i: (jnp.maximum((blk0 + i) * r8 - 1, 0), 0))
        prev_lo = pl.BlockSpec((SUBLANES, LORA_W),
                               lambda i: (jnp.maximum((blk0 + i) * r8 - 1, 0), C_LORA // LORA_W))
        prev_args = (p, p)
    else:
        prev_x = pl.BlockSpec((tm, xw), lambda i: (i, 0))
        prev_lo = pl.BlockSpec((tm, LORA_W), lambda i: (i, 0))
        prev_args = prev
    full = lambda a: pl.BlockSpec(a.shape, lambda i: (0,) * a.ndim)
    params = (pw["mu_x"], pw["mu_lo"], pw["w0"], pw["w_b"], pw["a0"], pw["a_b"], pw["g_b"],
              pw["k_k"], pw["k_a"], pw["r_k"], pw["e"], pw["et"])
    out = pl.BlockSpec((tm, R_DIM), lambda i: (i, 0))
    return pl.pallas_call(
        functools.partial(_rwkv_pre_kernel, t_real, tp, from_rows),
        grid=(nrows // tm,),
        in_specs=[cur_x, cur_lo, prev_x, prev_lo] + [full(a) for a in params],
        out_specs=[out] * 8,
        out_shape=[jax.ShapeDtypeStruct((nrows, R_DIM), F32)] * 8,
        compiler_params=_cparams(("parallel",)),
        name="rwkv_pre_rows" if from_rows else "rwkv_pre_step",
    )(p, p, *prev_args, *params)


def _rwkv_scan_kernel(r_ref, ld_ref, k_ref, v_ref, kk_ref, b_ref, y_ref, s_ref, ss_scr):
    c = pl.program_id(1)

    @pl.when(c == 0)
    def _():
        ss_scr[...] = jnp.zeros_like(ss_scr)

    n = CHUNK
    n2 = 2 * n
    pairs = R_HEADS // 2
    ld_all = ld_ref[...]
    ri = lax.broadcasted_iota(I32, (n, n), 0)
    ci = lax.broadcasted_iota(I32, (n, n), 1)
    cum_all = _dot((ci <= ri).astype(F32), ld_all, HIGHEST)
    head0 = lax.broadcasted_iota(I32, (n, LANES), 1) < R_HEAD
    r4 = lax.broadcasted_iota(I32, (2 * n2, 2 * n2), 0)
    c4 = lax.broadcasted_iota(I32, (2 * n2, 2 * n2), 1)
    tri = (c4 % n) < (r4 % n) + jnp.where(r4 < n2, 0, 1)
    re = lax.broadcasted_iota(I32, (n2, n2), 0)
    ce = lax.broadcasted_iota(I32, (n2, n2), 1)
    eye = (re == ce).astype(F32)

    def stack(x):
        return jnp.concatenate([jnp.where(head0, x, 0.0), jnp.where(head0, 0.0, x)], axis=0)

    ar, bk, v2, ss, e_last = [], [], [], [], []
    for p in range(pairs):
        sl = slice(p * LANES, (p + 1) * LANES)
        cum, ld = cum_all[:, sl], ld_all[:, sl]
        e_pos = jnp.exp(cum)
        e_neg = jnp.exp(-cum)
        at = -kk_ref[:, sl] * jnp.exp(cum - ld)
        ar.append(jnp.concatenate([stack(at), stack(r_ref[:, sl] * e_pos)], axis=0).astype(BF16))
        bk.append(jnp.concatenate([stack(b_ref[:, sl] * e_neg), stack(k_ref[:, sl] * e_neg)],
                                  axis=0).astype(BF16))
        v2.append(stack(v_ref[:, sl]).astype(BF16))
        ss.append(ss_scr[p])
        e_last.append(e_pos[n - 1:n, :])
    xy0 = [_dot_nt(ar[p], ss[p].astype(BF16)) for p in range(pairs)]
    sc = [jnp.where(tri, _dot_nt(ar[p], bk[p]), 0.0) for p in range(pairs)]
    lp = [s[:n2, :n2] for s in sc]
    t = [eye + l for l in lp]
    m = 1
    while 2 * m < n:
        lpb = [l.astype(BF16) for l in lp]
        lp = [_dot(l, l) for l in lpb]
        t = [t[p] + _dot(t[p].astype(BF16), lp[p].astype(BF16)) for p in range(pairs)]
        m *= 2
    w = [xy0[p][:n2] + _dot(sc[p][:n2, n2:].astype(BF16), v2[p]) for p in range(pairs)]
    u = [_dot(t[p].astype(BF16), w[p].astype(BF16)) for p in range(pairs)]
    uv = [jnp.concatenate([u[p].astype(BF16), v2[p]], axis=0) for p in range(pairs)]
    y = [xy0[p][n2:] + _dot(sc[p][n2:].astype(BF16), uv[p]) for p in range(pairs)]
    upd = [_dot_tn(uv[p], bk[p]) for p in range(pairs)]
    for p in range(pairs):
        y_ref[:, p * LANES:(p + 1) * LANES] = y[p][:n] + y[p][n:]
        ss_scr[p] = (ss[p] + upd[p]) * e_last[p]

    @pl.when(c == pl.num_programs(1) - 1)
    def _():
        for p in range(R_HEADS // 2):
            ss = ss_scr[p]
            s_ref[0, 2 * p] = ss[:R_HEAD, :R_HEAD]
            s_ref[0, 2 * p + 1] = ss[R_HEAD:, R_HEAD:]


def rwkv_scan(r, ld, k, v, kk, b, batch, tp):
    nchunk = tp // CHUNK
    blk = pl.BlockSpec((CHUNK, R_DIM), lambda bi, c: (bi * nchunk + c, 0))
    return pl.pallas_call(
        _rwkv_scan_kernel,
        grid=(batch, nchunk),
        in_specs=[blk] * 6,
        out_specs=[blk, pl.BlockSpec((1, R_HEADS, R_HEAD, R_HEAD), lambda bi, c: (bi, 0, 0, 0))],
        out_shape=[jax.ShapeDtypeStruct((batch * tp, R_DIM), F32),
                   jax.ShapeDtypeStruct((batch, R_HEADS, R_HEAD, R_HEAD), F32)],
        scratch_shapes=[pltpu.VMEM((R_HEADS // 2, LANES, LANES), F32)],
        compiler_params=_cparams(("parallel", "arbitrary")),
        name="rwkv_scan",
    )(r, ld, k, v, kk, b)


STEP_ROWS = 16


def _rwkv_step_kernel(r_ref, ld_ref, k_ref, v_ref, kk_ref, b_ref, s_ref, y_ref, so_ref):
    r, k = r_ref[0], k_ref[0]
    dec = jnp.exp(ld_ref[0])
    na = -kk_ref[0]
    b = b_ref[0]
    v = v_ref[0]
    ys = []
    for v0 in range(0, R_HEAD, STEP_ROWS):
        rows = range(v0, v0 + STEP_ROWS)
        s = [s_ref[0, vi] for vi in rows]
        sa = [jnp.sum(x * na, axis=0, keepdims=True) for x in s]
        s_new = [x * dec + a * b + v[vi:vi + 1, :] * k for x, a, vi in zip(s, sa, rows)]
        ys += [jnp.sum(x * r, axis=0, keepdims=True) for x in s_new]
        for x, vi in zip(s_new, rows):
            so_ref[0, vi] = x
    y_ref[0] = jnp.concatenate(ys, axis=0)


def rwkv_step(r, ld, k, v, kk, b, state):
    s = state.shape[-1]
    vec = pl.BlockSpec((1, R_HEAD, s), lambda h: (h, 0, 0))
    st = pl.BlockSpec((1, R_HEAD, R_HEAD, s), lambda h: (h, 0, 0, 0))
    return pl.pallas_call(
        _rwkv_step_kernel,
        grid=(R_HEADS,),
        in_specs=[vec] * 6 + [st],
        out_specs=[vec, st],
        out_shape=[jax.ShapeDtypeStruct((R_HEADS, R_HEAD, s), F32),
                   jax.ShapeDtypeStruct(state.shape, F32)],
        compiler_params=_cparams(("parallel",)),
        name="rwkv_step",
    )(r, ld, k, v, kk, b, state)


def _select_topk(score, allowed, n_sel):
    bits = lax.bitcast_convert_type(score, I32)
    key = jnp.where(bits < 0, bits ^ jnp.int32(0x7FFFFFFF), bits)
    key = jnp.where(allowed, key, jnp.int32(INT_MIN))
    m = score.shape[0]
    gr = m // SEARCH_GROUPS if m % (SEARCH_GROUPS * SUBLANES) == 0 else m
    keys = [key[r0:r0 + gr] for r0 in range(0, m, gr)]

    def body(i, taus):
        step = lax.shift_left(jnp.int32(1), jnp.int32(31) - i)
        cands = [tau + step for tau in taus]
        cnts = [jnp.sum((k >= c).astype(I32), axis=1, keepdims=True) for k, c in zip(keys, cands)]
        return tuple(jnp.where(cnt >= n_sel, c, tau) for cnt, c, tau in zip(cnts, cands, taus))

    taus = lax.fori_loop(0, 32, body, tuple(jnp.full((gr, 1), INT_MIN, I32) for _ in keys))
    tau = jnp.concatenate(taus, axis=0)
    return jnp.logical_and(key >= tau, allowed)


KEY_TILE = 256
Q_TILES_PER_EXTENT = 3


def _dsa_prompt_block(n_sel, tk, i, q_ref, iq_ref, iw_ref, ik_ref, k_ref, v_ref, o_ref, sc_ref):
    tq = q_ref.shape[0]
    iw = iw_ref[...] * ((IDX_HEADS * IDX_DIM) ** -0.5)
    iq = iq_ref[...]
    iq_h = [iq[:, h * IDX_DIM:(h + 1) * IDX_DIM] for h in range(IDX_HEADS)]
    iw_h = [iw[:, h:h + 1] for h in range(IDX_HEADS)]
    for c0 in range(0, tk, KEY_TILE):
        c1 = min(c0 + KEY_TILE, tk)
        ikb = ik_ref[c0:c1, :].astype(BF16)
        acc = jnp.maximum(_dot_nt(iq_h[0], ikb), 0.0) * iw_h[0]
        for h in range(1, IDX_HEADS):
            acc = acc + jnp.maximum(_dot_nt(iq_h[h], ikb), 0.0) * iw_h[h]
        sc_ref[:, c0:c1] = acc
    qpos = i * tq + lax.broadcasted_iota(I32, (tq, 1), 0)
    kpos = lax.broadcasted_iota(I32, (1, tk), 1)
    sel = _select_topk(sc_ref[:, :tk], kpos <= qpos, n_sel)
    bias = jnp.where(sel, 0.0, NEG_BIG)
    q = q_ref[...]
    rep = A_HEADS // A_KV_HEADS
    for g in range(A_KV_HEADS):
        kg = k_ref[:tk, g * A_HEAD:(g + 1) * A_HEAD].astype(BF16)
        vg = v_ref[:tk, g * A_HEAD:(g + 1) * A_HEAD].astype(BF16)
        for rr in range(rep):
            h = g * rep + rr
            s = _dot_nt(q[:, h * A_HEAD:(h + 1) * A_HEAD], kg) + bias
            m = jnp.max(s, axis=1, keepdims=True)
            p = jnp.exp(s - m)
            l = jnp.sum(p, axis=1, keepdims=True)
            o_ref[:, h * A_HEAD:(h + 1) * A_HEAD] = _dot(p.astype(BF16), vg) / l


def _dsa_prompt_kernel(n_sel, *refs):
    i = pl.program_id(1)
    tq = refs[0].shape[0]
    tp = refs[4].shape[0]
    nq = tp // tq
    for lo in range(0, nq, Q_TILES_PER_EXTENT):
        hi = min(lo + Q_TILES_PER_EXTENT, nq)

        @pl.when(jnp.logical_and(i >= lo, i < hi))
        def _(hi=hi):
            _dsa_prompt_block(n_sel, hi * tq, i, *refs)


def dsa_prompt(q, iq, iw, ik, k, p, batch, tp, n_sel):
    nq = tp // Q_TILE
    qrow = lambda w: pl.BlockSpec((Q_TILE, w), lambda b, i: (b * nq + i, 0))
    keys = lambda w, blk: pl.BlockSpec((tp, w), lambda b, i: (b, blk))
    return pl.pallas_call(
        functools.partial(_dsa_prompt_kernel, n_sel),
        grid=(batch, nq),
        in_specs=[qrow(A_DIM), qrow(IDX_HEADS * IDX_DIM), qrow(IDX_HEADS),
                  keys(IDX_DIM, 0), keys(A_KV_DIM, 0), keys(A_KV_DIM, C_VA // A_KV_DIM)],
        out_specs=qrow(A_DIM),
        out_shape=jax.ShapeDtypeStruct((batch * tp, A_DIM), F32),
        scratch_shapes=[pltpu.VMEM((Q_TILE, tp), F32)],
        compiler_params=_cparams(("parallel", "parallel")),
        name="dsa_prompt",
    )(q, iq, iw, ik, k, p)


def _dsa_step_score_kernel(n_pages, page, pt_ref, iq_ref, iw_ref, ikn_ref, *refs):
    pages = refs[:n_pages]
    o_ref = refs[n_pages]
    iq = iq_ref[0]
    iw = iw_ref[0] * ((IDX_HEADS * IDX_DIM) ** -0.5)
    for j in range(n_pages):
        d = _dot(iq, pages[j][...].astype(BF16))
        o_ref[0, :, j * page:(j + 1) * page] = jnp.sum(jnp.maximum(d, 0.0) * iw, axis=0,
                                                       keepdims=True)
    dn = jnp.sum(iq.astype(F32) * ikn_ref[0], axis=1, keepdims=True)
    sn = jnp.sum(jnp.maximum(dn, 0.0) * iw, axis=0, keepdims=True)
    lane = lax.broadcasted_iota(I32, (1, LANES), 1)
    o_ref[0, :, n_pages * page:] = jnp.where(lane == 0, sn, 0.0)


def dsa_step_scores(pt_flat, iq, iw, ik_new, cik2d, n_pages, page):
    s = iq.shape[0]
    kw = n_pages * page + LANES
    page_spec = lambda j: pl.BlockSpec((IDX_DIM, page), lambda i, pt: (pt[i * n_pages + j], 0))
    grid_spec = pltpu.PrefetchScalarGridSpec(
        num_scalar_prefetch=1,
        grid=(s,),
        in_specs=[pl.BlockSpec((1, IDX_HEADS, IDX_DIM), lambda i, pt: (i, 0, 0)),
                  pl.BlockSpec((1, IDX_HEADS, 1), lambda i, pt: (i, 0, 0)),
                  pl.BlockSpec((1, 1, IDX_DIM), lambda i, pt: (i, 0, 0))]
                 + [page_spec(j) for j in range(n_pages)],
        out_specs=pl.BlockSpec((1, 1, kw), lambda i, pt: (i, 0, 0)),
    )
    return pl.pallas_call(
        functools.partial(_dsa_step_score_kernel, n_pages, page),
        grid_spec=grid_spec,
        out_shape=jax.ShapeDtypeStruct((s, 1, kw), F32),
        compiler_params=_cparams(("arbitrary",)),
        name="dsa_step_scores",
    )(pt_flat, iq, iw, ik_new, *([cik2d] * n_pages))


def _dsa_step_select_kernel(n_sel, past, sc_ref, o_ref):
    sc = sc_ref[...]
    kpos = lax.broadcasted_iota(I32, sc.shape, 1)
    sel = _select_topk(sc, kpos <= past, n_sel)
    o_ref[...] = sel.astype(F32)


def dsa_step_select(sc, n_sel, past):
    return pl.pallas_call(
        functools.partial(_dsa_step_select_kernel, n_sel, past),
        out_shape=jax.ShapeDtypeStruct(sc.shape, F32),
        compiler_params=pltpu.CompilerParams(vmem_limit_bytes=VMEM_LIMIT),
        name="dsa_step_select",
    )(sc)


def _dsa_step_attn_kernel(n_pages, page, pt_ref, q_ref, kn_ref, vn_ref, sel_ref, ex_ref, *refs):
    kp = refs[:n_pages]
    vp = refs[n_pages:2 * n_pages]
    o_ref = refs[2 * n_pages]
    q = q_ref[0]
    rep = A_HEADS // A_KV_HEADS
    w2 = page * A_KV_HEADS
    hrow = lax.broadcasted_iota(I32, (A_HEADS, w2), 0)
    col = lax.broadcasted_iota(I32, (A_HEADS, w2), 1)
    own = (col % A_KV_HEADS) == (hrow // rep)
    ex = ex_ref[...]
    logits = []
    for j in range(n_pages):
        s = _dot_nt(q, kp[j][...].astype(BF16))
        selj = _dot(sel_ref[0, :, j * page:(j + 1) * page].astype(BF16), ex)
        logits.append(jnp.where(jnp.logical_and(selj > 0.5, own), s, NEG_BIG))
    h8 = lax.broadcasted_iota(I32, (A_HEADS, A_HEAD), 0)
    kn = jnp.where(h8 < rep, kn_ref[0, 0:1, :], kn_ref[0, 1:2, :])
    vn = jnp.where(h8 < rep, vn_ref[0, 0:1, :], vn_ref[0, 1:2, :])
    sn = jnp.sum(q.astype(F32) * kn, axis=1, keepdims=True)
    seln = sel_ref[0, :, n_pages * page:n_pages * page + 1]
    sn = jnp.where(seln > 0.5, sn, NEG_BIG)
    m = sn
    for s in logits:
        m = jnp.maximum(m, jnp.max(s, axis=1, keepdims=True))
    pn = jnp.exp(sn - m)
    l = pn
    acc = pn * vn
    for j in range(n_pages):
        p = jnp.exp(logits[j] - m)
        l = l + jnp.sum(p, axis=1, keepdims=True)
        acc = acc + _dot(p.astype(BF16), vp[j][...].astype(BF16))
    o_ref[0] = acc / l


def dsa_step_attn(pt_flat, q, k_new, v_new, sel, expand, ck2d, cv2d, n_pages, page):
    s = q.shape[0]
    kw = sel.shape[-1]
    w2 = page * A_KV_HEADS
    page_spec = lambda j: pl.BlockSpec((w2, A_HEAD), lambda i, pt: (pt[i * n_pages + j], 0))
    grid_spec = pltpu.PrefetchScalarGridSpec(
        num_scalar_prefetch=1,
        grid=(s,),
        in_specs=[pl.BlockSpec((1, A_HEADS, A_HEAD), lambda i, pt: (i, 0, 0)),
                  pl.BlockSpec((1, A_KV_HEADS, A_HEAD), lambda i, pt: (i, 0, 0)),
                  pl.BlockSpec((1, A_KV_HEADS, A_HEAD), lambda i, pt: (i, 0, 0)),
                  pl.BlockSpec((1, 1, kw), lambda i, pt: (i, 0, 0)),
                  pl.BlockSpec((page, w2), lambda i, pt: (0, 0))]
                 + [page_spec(j) for j in range(n_pages)] * 2,
        out_specs=pl.BlockSpec((1, A_HEADS, A_HEAD), lambda i, pt: (i, 0, 0)),
    )
    return pl.pallas_call(
        functools.partial(_dsa_step_attn_kernel, n_pages, page),
        grid_spec=grid_spec,
        out_shape=jax.ShapeDtypeStruct((s, A_HEADS, A_HEAD), F32),
        compiler_params=_cparams(("arbitrary",)),
        name="dsa_step_attn",
    )(pt_flat, q, k_new, v_new, sel, expand, *([ck2d] * n_pages), *([cv2d] * n_pages))


def _pack_bf16_pairs(x):
    w = x.shape[1] // 2
    hi = lax.bitcast_convert_type(x[:, :w].astype(BF16).astype(F32), I32)
    lo = lax.bitcast_convert_type(x[:, w:].astype(BF16).astype(F32), I32)
    return hi | lax.shift_right_logical(lo, 16)


def _unpack_bf16_pairs(p):
    hi = lax.bitcast_convert_type(p & jnp.int32(-65536), F32)
    lo = lax.bitcast_convert_type(lax.shift_left(p, 16), F32)
    return hi, lo


def _mix_kernel(alpha, prompt_blocks, x_ref, yp_ref, bonp_ref, gp_ref, ap_ref, ys_ref, bons_ref, gs_ref,
                as_ref, e_ref, et_ref, gng_ref, gnb_ref, l0g_ref, l0b_ref, wo_ref, l1g_ref, l1b_ref,
                wr_ref, h_ref, sc_ref, pk_ref):
    is_prompt = pl.program_id(0) < prompt_blocks
    pick = lambda p_ref, s_ref: jnp.where(is_prompt, p_ref[...], s_ref[...])
    e, et = e_ref[...], et_ref[...]
    y = pick(yp_ref, ys_ref)
    inv = 1.0 / R_HEAD
    mu = _head_sums(y, e, et) * inv
    d = y - mu
    var = _head_sums(d * d, e, et) * inv
    yn = d * lax.rsqrt(var + GN_EPS) * gng_ref[...] + gnb_ref[...]
    r_out = (yn + pick(bonp_ref, bons_ref)) * pick(gp_ref, gs_ref)
    mix = (_dot(r_out.astype(BF16), wo_ref[:R_DIM, :])
           + _dot(pick(ap_ref, as_ref).astype(BF16), wo_ref[R_DIM:, :]))
    h0 = _layer_norm(x_ref[...], l0g_ref[...], l0b_ref[...])
    h1 = _layer_norm(alpha * h0 + mix, l1g_ref[...], l1b_ref[...])
    h_ref[...] = h1
    w_hi, w_lo = _split_bf16(wr_ref[...])
    h_hi, h_lo = _split_bf16(h1)
    sc_ref[...] = _sigmoid(_dot_nt(w_hi, h_hi) + _dot_nt(w_hi, h_lo) + _dot_nt(w_lo, h_hi))
    pk_ref[...] = _pack_bf16_pairs(h1)


def mix_ln1_router(x, prompt_parts, step_parts, pw, alpha):
    n, d = x.shape
    tm = ROW_TILE
    pb = prompt_parts[0].shape[0] // tm
    row = lambda w: pl.BlockSpec((tm, w), lambda i: (i, 0))
    head = lambda w: pl.BlockSpec((tm, w), lambda i: (jnp.minimum(i, pb - 1), 0))
    tail = lambda w: pl.BlockSpec((tm, w), lambda i: (jnp.maximum(i - pb, 0), 0))
    full = lambda a: pl.BlockSpec(a.shape, lambda i: (0,) * a.ndim)
    params = (pw["e"], pw["et"], pw["gn_g"], pw["gn_b"], pw["ln0_g"], pw["ln0_b"], pw["w_out"],
              pw["ln1_g"], pw["ln1_b"], pw["w_router_t"])
    widths = (R_DIM, R_DIM, R_DIM, A_DIM)
    return pl.pallas_call(
        functools.partial(_mix_kernel, alpha, pb),
        grid=(n // tm,),
        in_specs=[row(d)] + [head(w) for w in widths] + [tail(w) for w in widths]
                 + [full(a) for a in params],
        out_specs=[row(d), pl.BlockSpec((N_EXPERTS, tm), lambda i: (0, i)), row(d // 2)],
        out_shape=[jax.ShapeDtypeStruct((n, d), F32),
                   jax.ShapeDtypeStruct((N_EXPERTS, n), F32),
                   jax.ShapeDtypeStruct((n, d // 2), I32)],
        compiler_params=_cparams(("parallel",)),
        name="mix_ln1_router",
    )(x, *prompt_parts, *step_parts, *params)


def _route_kernel(n_prompt, tp, t_real, s_dec, sc_ref, bias_ref, idx_ref, gate_ref, pos_ref, cnt_ref,
                  cnt_scr):
    scores = sc_ref[...]
    biased = scores + bias_ref[...]
    tn = scores.shape[1]
    per = N_EXPERTS // N_EXPERT_GROUPS
    sub = lax.broadcasted_iota(I32, (per, tn), 0)
    grp_rows = []
    for g in range(N_EXPERT_GROUPS):
        xg = biased[g * per:(g + 1) * per, :]
        m1 = jnp.max(xg, axis=0, keepdims=True)
        first = jnp.min(jnp.where(xg == m1, sub, per), axis=0, keepdims=True)
        m2 = jnp.max(jnp.where(sub == first, -jnp.inf, xg), axis=0, keepdims=True)
        grp_rows.append(m1 + m2)
    grp = jnp.concatenate(grp_rows, axis=0)
    gi = lax.broadcasted_iota(I32, (N_EXPERT_GROUPS, tn), 0)
    gsel = jnp.zeros((N_EXPERT_GROUPS, tn), jnp.bool_)
    for _ in range(TOPK_GROUPS):
        m = jnp.max(grp, axis=0, keepdims=True)
        first = jnp.min(jnp.where(grp == m, gi, N_EXPERT_GROUPS), axis=0, keepdims=True)
        hit = gi == first
        gsel = jnp.logical_or(gsel, hit)
        grp = jnp.where(hit, -jnp.inf, grp)
    ei = lax.broadcasted_iota(I32, (N_EXPERTS, tn), 0)
    emask = jnp.concatenate(
        [jnp.broadcast_to(gsel[g:g + 1, :], (per, tn)) for g in range(N_EXPERT_GROUPS)], axis=0)
    cand = jnp.where(emask, biased, -jnp.inf)
    idxs, gates, hits = [], [], []
    for _ in range(TOP_K):
        m = jnp.max(cand, axis=0, keepdims=True)
        first = jnp.min(jnp.where(cand == m, ei, N_EXPERTS), axis=0, keepdims=True)
        hit = ei == first
        idxs.append(first)
        hits.append(hit)
        gates.append(jnp.sum(jnp.where(hit, scores, 0.0), axis=0, keepdims=True))
        cand = jnp.where(hit, -jnp.inf, cand)
    gate = jnp.concatenate(gates, axis=0)
    gate = gate / jnp.sum(gate, axis=0, keepdims=True) * ROUTED_SCALE
    tok = pl.program_id(0) * tn + lax.broadcasted_iota(I32, (1, tn), 1)
    live = (jnp.where(tok < n_prompt, tok % tp, tok - n_prompt)
            < jnp.where(tok < n_prompt, t_real, s_dec))
    idx_ref[...] = jnp.concatenate(idxs, axis=0)
    gate_ref[...] = jnp.where(live, gate, 0.0)
    chosen = hits[0]
    for hit in hits[1:]:
        chosen = jnp.logical_or(chosen, hit)
    onehot = jnp.where(jnp.logical_and(chosen, live), 1.0, 0.0)
    ta = lax.broadcasted_iota(I32, (tn, tn), 0)
    tb = lax.broadcasted_iota(I32, (tn, tn), 1)
    prefix = _dot(onehot.astype(BF16), (ta < tb).astype(BF16))

    @pl.when(pl.program_id(0) == 0)
    def _():
        cnt_scr[...] = jnp.zeros_like(cnt_scr)

    rank = prefix + cnt_scr[:, 0:1]
    pos_ref[...] = jnp.concatenate(
        [jnp.sum(jnp.where(hit, rank, 0.0), axis=0, keepdims=True) for hit in hits], axis=0).astype(I32)
    cnt_scr[...] = cnt_scr[...] + jnp.sum(onehot, axis=1, keepdims=True)
    cnt_ref[...] = cnt_scr[...].astype(I32)


def route(scores_t, e_bias, n_prompt, tp, t_real, s_dec):
    n = scores_t.shape[1]
    tn = ROW_TILE
    tok = pl.BlockSpec((TOP_K, tn), lambda i: (0, i))
    return pl.pallas_call(
        functools.partial(_route_kernel, n_prompt, tp, t_real, s_dec),
        grid=(n // tn,),
        in_specs=[pl.BlockSpec((N_EXPERTS, tn), lambda i: (0, i)),
                  pl.BlockSpec((N_EXPERTS, 1), lambda i: (0, 0))],
        out_specs=[tok, tok, tok, pl.BlockSpec((N_EXPERTS, LANES), lambda i: (0, 0))],
        out_shape=[jax.ShapeDtypeStruct((TOP_K, n), I32), jax.ShapeDtypeStruct((TOP_K, n), F32),
                   jax.ShapeDtypeStruct((TOP_K, n), I32),
                   jax.ShapeDtypeStruct((N_EXPERTS, LANES), I32)],
        scratch_shapes=[pltpu.VMEM((N_EXPERTS, LANES), F32)],
        compiler_params=_cparams(("arbitrary",)),
        name="route",
    )(scores_t, e_bias)


def _dispatch_kernel(dest_ref, x_ref, init_ref, o_ref, sem):
    del init_ref
    tm = x_ref.shape[0]

    def start(i, carry):
        for j in range(TOP_K):
            pltpu.make_async_copy(x_ref.at[pl.ds(i, 1)],
                                  o_ref.at[pl.ds(dest_ref[i * TOP_K + j], 1)], sem).start()
        return carry

    lax.fori_loop(0, tm, start, 0)
    for j in range(TOP_K):
        pltpu.make_async_copy(x_ref, o_ref.at[pl.ds(0, tm)], sem).wait()


def moe_dispatch(dest_flat, xpk, rows_total):
    n, w = xpk.shape
    tm = ROW_TILE
    init = jnp.zeros((rows_total, w), I32)
    return pl.pallas_call(
        _dispatch_kernel,
        grid=(n // tm,),
        in_specs=[pl.BlockSpec((tm * TOP_K,), lambda i: (i,), memory_space=pltpu.SMEM),
                  pl.BlockSpec((tm, w), lambda i: (i, 0)),
                  pl.BlockSpec(memory_space=pl.ANY)],
        out_specs=pl.BlockSpec(memory_space=pl.ANY),
        out_shape=jax.ShapeDtypeStruct((rows_total, w), I32),
        scratch_shapes=[pltpu.SemaphoreType.DMA(())],
        input_output_aliases={2: 0},
        compiler_params=_cparams(("arbitrary",)),
        name="moe_dispatch",
    )(dest_flat, xpk, init)


def _experts_kernel(be_ref, nxt_ref, slot_ref, nu_ref, x_ref, wg_ref, wu_ref, wd_ref, o_ref,
                    wg_f, wu_f, wd_f, wg_s, wu_s, wd_s, sems):
    i = pl.program_id(0)
    used = i < nu_ref[0]
    prev = be_ref[jnp.maximum(i - 1, 0)]
    fresh = jnp.logical_and(used, jnp.logical_or(i == 0, be_ref[i] != prev))

    def weight_copies(e, slot):
        return (pltpu.make_async_copy(wg_ref.at[e], wg_f.at[slot], sems.at[slot, 0]),
                pltpu.make_async_copy(wu_ref.at[e], wu_f.at[slot], sems.at[slot, 1]),
                pltpu.make_async_copy(wd_ref.at[e], wd_f.at[slot], sems.at[slot, 2]))

    @pl.when(jnp.logical_and(used, i == 0))
    def _():
        for c in weight_copies(be_ref[0], 0):
            c.start()

    @pl.when(fresh)
    def _():
        slot = slot_ref[i]
        for c in weight_copies(be_ref[i], slot):
            c.wait()

        @pl.when(nxt_ref[i] >= 0)
        def _():
            for c in weight_copies(nxt_ref[i], 1 - slot):
                c.start()

        wg_s[...] = wg_f[slot].astype(BF16)
        wu_s[...] = wu_f[slot].astype(BF16)
        wd_s[...] = wd_f[slot].astype(BF16)

    @pl.when(used)
    def _():
        hi, lo = _unpack_bf16_pairs(x_ref[...])
        hi, lo = hi.astype(BF16), lo.astype(BF16)
        half = hi.shape[1]
        gp = _dot(hi, wg_s[:half, :]) + _dot(lo, wg_s[half:, :])
        up = _dot(hi, wu_s[:half, :]) + _dot(lo, wu_s[half:, :])
        act = gp * _sigmoid(gp) * up
        o_ref[...] = _pack_bf16_pairs(_dot(act.astype(BF16), wd_s[...]))

    @pl.when(jnp.logical_not(used))
    def _():
        o_ref[...] = jnp.zeros_like(o_ref)


def moe_experts(blk_e, nxt_e, slot, n_used, xs, nb, w_gate, w_up, w_down):
    w = xs.shape[1]
    rows = nb * EXPERT_TILE
    _, d, de = w_gate.shape
    grid_spec = pltpu.PrefetchScalarGridSpec(
        num_scalar_prefetch=4,
        grid=(nb,),
        in_specs=[pl.BlockSpec((EXPERT_TILE, w), lambda i, be, nx, sl, nu: (jnp.minimum(i, nu[0] - 1), 0)),
                  pl.BlockSpec(memory_space=pl.ANY), pl.BlockSpec(memory_space=pl.ANY),
                  pl.BlockSpec(memory_space=pl.ANY)],
        out_specs=pl.BlockSpec((EXPERT_TILE, w), lambda i, be, nx, sl, nu: (i, 0)),
        scratch_shapes=[pltpu.VMEM((2, d, de), F32), pltpu.VMEM((2, d, de), F32),
                        pltpu.VMEM((2, de, d), F32),
                        pltpu.VMEM((d, de), BF16), pltpu.VMEM((d, de), BF16),
                        pltpu.VMEM((de, d), BF16), pltpu.SemaphoreType.DMA((2, 3))],
    )
    return pl.pallas_call(
        _experts_kernel,
        grid_spec=grid_spec,
        out_shape=jax.ShapeDtypeStruct((rows, w), I32),
        compiler_params=_cparams(("arbitrary",)),
        name="moe_experts",
    )(blk_e, nxt_e, slot, n_used, xs, w_gate, w_up, w_down)


def _combine_kernel(alpha, dest_ref, h_ref, gate_ref, wsg_ref, wsu_ref, wsd_ref, l2g_ref, l2b_ref,
                    ys_ref, o_ref, buf, sem):
    tm = h_ref.shape[0]

    def start(i, carry):
        for j in range(TOP_K):
            pltpu.make_async_copy(ys_ref.at[pl.ds(dest_ref[i * TOP_K + j], 1)],
                                  buf.at[j, pl.ds(i, 1)], sem).start(priority=j % 2)
        return carry

    lax.fori_loop(0, tm, start, 0)
    h = h_ref[...]
    hb = h.astype(BF16)
    gp = _dot(hb, wsg_ref[...])
    up = _dot(hb, wsu_ref[...])
    shared = _dot((gp * _sigmoid(gp) * up).astype(BF16), wsd_ref[...])
    for j in range(TOP_K):
        pltpu.make_async_copy(ys_ref.at[pl.ds(0, tm)], buf.at[j], sem).wait()
    gate = gate_ref[...]
    half = buf.shape[2]
    acc_hi = jnp.zeros((tm, half), F32)
    acc_lo = jnp.zeros((tm, half), F32)
    for j in range(TOP_K):
        hi, lo = _unpack_bf16_pairs(buf[j])
        gj = gate[:, j:j + 1]
        acc_hi = acc_hi + gj * hi
        acc_lo = acc_lo + gj * lo
    f = jnp.concatenate([acc_hi, acc_lo], axis=1) + shared
    o_ref[...] = _layer_norm(alpha * h + f, l2g_ref[...], l2b_ref[...])


def moe_combine(dest_flat, h1, gate, ys, pw, alpha):
    n, d = h1.shape
    tm = Q_TILE
    full = lambda a: pl.BlockSpec(a.shape, lambda i: (0,) * a.ndim)
    params = (pw["ws_gate"], pw["ws_up"], pw["ws_down"], pw["ln2_g"], pw["ln2_b"])
    return pl.pallas_call(
        functools.partial(_combine_kernel, alpha),
        grid=(n // tm,),
        in_specs=[pl.BlockSpec((tm * TOP_K,), lambda i: (i,), memory_space=pltpu.SMEM),
                  pl.BlockSpec((tm, d), lambda i: (i, 0)),
                  pl.BlockSpec((tm, TOP_K), lambda i: (i, 0))]
                 + [full(a) for a in params]
                 + [pl.BlockSpec(memory_space=pl.ANY)],
        out_specs=pl.BlockSpec((tm, d), lambda i: (i, 0)),
        out_shape=jax.ShapeDtypeStruct((n, d), F32),
        scratch_shapes=[pltpu.VMEM((TOP_K, tm, d // 2), I32), pltpu.SemaphoreType.DMA(())],
        compiler_params=_cparams(("arbitrary",)),
        name="moe_combine",
    )(dest_flat, h1, gate, *params, ys)


def _round_up(x, m):
    return (x + m - 1) // m * m


def _rope_tables(pos, head):
    half = head // 2
    inv = ROPE_THETA ** (-jnp.arange(half, dtype=F32) / half)
    ang = pos.astype(F32)[:, None] * inv[None, :]
    cos, sin = jnp.cos(ang), jnp.sin(ang)
    rep = LANES // head
    c = jnp.tile(jnp.concatenate([cos, cos], axis=1), (1, rep))
    s = jnp.tile(jnp.concatenate([-sin, sin], axis=1), (1, rep))
    return c, s


def _permute_cols(m):
    a0 = SHIFT_DIM
    pieces = [
        m[..., 0:3 * R_DIM],
        m[..., a0:a0 + A_DIM],
        m[..., a0 + A_DIM + 2 * A_KV_DIM:a0 + A_DIM + 2 * A_KV_DIM + IDX_HEADS * IDX_DIM],
        m[..., a0 + A_DIM:a0 + A_DIM + 2 * A_KV_DIM],
    ]
    i0 = a0 + A_DIM + 2 * A_KV_DIM + IDX_HEADS * IDX_DIM
    pieces.append(m[..., i0:i0 + IDX_DIM + IDX_HEADS])
    pad = lambda w: jnp.zeros(m.shape[:-1] + (w,), m.dtype)
    pieces.append(pad(LANES - IDX_DIM - IDX_HEADS))
    pieces.append(m[..., 3 * R_DIM:SHIFT_DIM])
    pieces.append(pad(LORA_W - (SHIFT_DIM - 3 * R_DIM)))
    return jnp.concatenate(pieces, axis=-1)


def kernel(x_prompt, x_sample, cache_k, cache_v, cache_idx_k, state_wkv, state_shift, page_table,
           meta, ln0_g, ln0_b, w_in, mu_shift, w0, w_b, a0, a_b, g_b, k_k, k_a, r_k, gn_g, gn_b,
           w_out, ln1_g, ln1_b, w_router, e_bias, w_gate, w_up, w_down, ws_gate, ws_up, ws_down,
           ln2_g, ln2_b):
    depth = w_in.shape[0]
    assert depth == 1, "single trunk layer"
    bsz, s_p, d = x_prompt.shape
    s_dec, s_s, _ = x_sample.shape
    assert s_s == 1, "one decode token per sequence"
    t_real = N_META + s_p
    tp = _round_up(t_real, LANES)
    assert (bsz * tp) % ROW_TILE == 0
    sp = _round_up(s_dec, ROW_TILE)
    n_prompt = bsz * tp
    n = n_prompt + sp
    n_pool, page = cache_k.shape[1], cache_k.shape[2]
    n_pages = page_table.shape[1]
    past = n_pages * page
    alpha = float((2 * depth) ** 0.25)
    row2 = lambda a: a.reshape(1, -1)

    meta_rows = jnp.broadcast_to(meta[None], (bsz, N_META, d))
    xp = jnp.concatenate([meta_rows, x_prompt, jnp.zeros((bsz, tp - t_real, d), F32)], axis=1)
    x_all = jnp.concatenate([xp.reshape(n_prompt, d), x_sample.reshape(s_dec, d),
                             jnp.zeros((sp - s_dec, d), F32)], axis=0)
    pos = jnp.concatenate([jnp.tile(jnp.arange(tp), bsz), jnp.full((sp,), past)])
    c128, s128 = _rope_tables(pos, A_HEAD)
    c64, s64 = _rope_tables(pos, IDX_DIM)

    w_in_k = _permute_cols(w_in[0]).astype(BF16)
    mu_k = _permute_cols(
        jnp.concatenate([mu_shift[0], jnp.zeros((w_in.shape[2] - SHIFT_DIM,), F32)])[None, :])
    head_of = jnp.arange(R_DIM) // R_HEAD
    e_mat = (head_of[:, None] == jnp.arange(R_HEADS)[None, :]).astype(F32)
    zpad = lambda a, rows_before, rows_total: jnp.concatenate(
        [jnp.zeros((rows_before, a.shape[1]), a.dtype), a,
         jnp.zeros((rows_total - rows_before - a.shape[0], a.shape[1]), a.dtype)], axis=0)
    pw = {
        "mu_x": mu_k[:, :3 * R_DIM], "mu_lo": mu_k[:, C_LORA:],
        "w0": row2(w0[0]), "a0": row2(a0[0]), "k_k": row2(k_k[0]), "k_a": row2(k_a[0]),
        "r_k": row2(r_k[0]), "gn_g": row2(gn_g[0]), "gn_b": row2(gn_b[0]),
        "w_b": zpad(w_b[0], 0, LANES).astype(BF16),
        "a_b": zpad(a_b[0], D_DECAY_LORA, LANES).astype(BF16),
        "g_b": zpad(g_b[0], 0, LORA_W - LANES).astype(BF16),
        "e": e_mat.astype(BF16), "et": e_mat.T.astype(BF16),
        "ln0_g": row2(ln0_g), "ln0_b": row2(ln0_b),
        "ln1_g": row2(ln1_g[0]), "ln1_b": row2(ln1_b[0]),
        "ln2_g": row2(ln2_g[0]), "ln2_b": row2(ln2_b[0]),
        "w_out": w_out[0].astype(BF16), "w_router_t": w_router[0].T,
        "ws_gate": ws_gate[0].astype(BF16), "ws_up": ws_up[0].astype(BF16),
        "ws_down": ws_down[0].astype(BF16),
    }

    p = ln_proj(x_all, pw["ln0_g"], pw["ln0_b"], w_in_k, tn=P_COLS // 3)
    q_r, iq_r, k_r, ik_r, iw = rope_all(p, c128, s128, c64, s64)

    pre_p = rwkv_pre(p, 0, n_prompt, None, pw, t_real, tp)
    shift_k = _permute_cols(jnp.concatenate(
        [state_shift[0], jnp.zeros((s_dec, w_in.shape[2] - SHIFT_DIM), F32)], axis=1))
    shift_k = jnp.concatenate([shift_k, jnp.zeros((sp - s_dec, P_COLS), F32)], axis=0)
    pre_s = rwkv_pre(p, n_prompt, sp, (shift_k[:, :3 * R_DIM], shift_k[:, C_LORA:]), pw, t_real, tp)
    r_p, ld_p, k_p, v_p, kk_p, b_p, g_p, bon_p = pre_p
    r_s, ld_s, k_s, v_s, kk_s, b_s, g_s, bon_s = pre_s
    y_p, wkv_p = rwkv_scan(r_p, ld_p, k_p, v_p, kk_p, b_p, bsz, tp)
    heads = lambda a: a[:s_dec].reshape(s_dec, R_HEADS, R_HEAD).transpose(1, 2, 0)
    y_hs, wkv_hs = rwkv_step(heads(r_s), heads(ld_s), heads(k_s), heads(v_s), heads(kk_s), heads(b_s),
                             state_wkv[0].transpose(1, 2, 3, 0))
    wkv_s = wkv_hs.transpose(3, 0, 1, 2)
    y_s = jnp.concatenate([y_hs.transpose(2, 0, 1).reshape(s_dec, R_DIM),
                           jnp.zeros((sp - s_dec, R_DIM), F32)], axis=0)

    n_sel_p = min(TOPK_KEYS, t_real // 4)
    a_p = dsa_prompt(q_r, iq_r, iw, ik_r, k_r, p, bsz, tp, n_sel_p)
    n_sel_s = min(TOPK_KEYS, (past + 1) // 4)
    pt_flat = page_table.reshape(-1).astype(I32)
    srow = slice(n_prompt, n_prompt + s_dec)
    sc_s = dsa_step_scores(pt_flat, iq_r[srow].reshape(s_dec, IDX_HEADS, IDX_DIM),
                           iw[srow].reshape(s_dec, IDX_HEADS, 1), ik_r[srow].reshape(s_dec, 1, IDX_DIM),
                           cache_idx_k[0].transpose(0, 2, 1).reshape(n_pool * IDX_DIM, page),
                           n_pages, page)
    sel_s = dsa_step_select(sc_s.reshape(s_dec, -1), n_sel_s, past).reshape(sc_s.shape)
    slot = jnp.arange(page)[:, None]
    expand = (jnp.arange(page * A_KV_HEADS)[None, :] // A_KV_HEADS == slot).astype(BF16)
    a_s = dsa_step_attn(pt_flat, q_r[srow].reshape(s_dec, A_HEADS, A_HEAD),
                        k_r[srow].reshape(s_dec, A_KV_HEADS, A_HEAD),
                        p[srow, C_VA:C_VA + A_KV_DIM].reshape(s_dec, A_KV_HEADS, A_HEAD),
                        sel_s, expand,
                        cache_k[0].reshape(n_pool * page * A_KV_HEADS, A_HEAD),
                        cache_v[0].reshape(n_pool * page * A_KV_HEADS, A_HEAD), n_pages, page)
    a_s = jnp.concatenate([a_s.reshape(s_dec, A_DIM), jnp.zeros((sp - s_dec, A_DIM), F32)], axis=0)

    h1, scores_t, xpk = mix_ln1_router(x_all, (y_p, bon_p, g_p, a_p), (y_s, bon_s, g_s, a_s), pw, alpha)
    eidx_t, gate_t, pos_t, counts = route(scores_t, e_bias[0].reshape(N_EXPERTS, 1),
                                          n_prompt, tp, t_real, s_dec)

    row_id = np.arange(n)
    tok_ok = np.where(row_id < n_prompt, (row_id % tp) < t_real, row_id < n_prompt + s_dec)
    n_dead = int(n - tok_ok.sum())
    nb = (int(tok_ok.sum()) * TOP_K + N_EXPERTS * (EXPERT_TILE - 1)) // EXPERT_TILE + 1
    dead_rows = nb * EXPERT_TILE + (np.cumsum(~tok_ok) - 1)[:, None] * TOP_K + np.arange(TOP_K)[None, :]
    counts = counts[:, 0]
    padded = (counts + EXPERT_TILE - 1) // EXPERT_TILE * EXPERT_TILE
    seg_end = jnp.cumsum(padded)
    seg_start = seg_end - padded
    experts = jnp.arange(N_EXPERTS)
    start_of = jnp.sum(jnp.where(eidx_t[:, :, None] == experts, seg_start, 0), axis=-1)
    dest = (start_of + pos_t).T
    dest_w = jnp.where(tok_ok[:, None], dest, dead_rows).astype(I32).reshape(-1)
    dest_r = jnp.where(tok_ok[:, None], dest, 0).astype(I32).reshape(-1)
    blk_row = jnp.arange(nb) * EXPERT_TILE
    blk_e = jnp.minimum(jnp.sum(seg_end[None, :] <= blk_row[:, None], axis=1),
                        N_EXPERTS - 1).astype(I32)
    n_used = (seg_end[-1] // EXPERT_TILE).astype(I32)
    run_start = jnp.concatenate([jnp.ones((1,), I32), (blk_e[1:] != blk_e[:-1]).astype(I32)])
    slot = ((jnp.cumsum(run_start) - 1) % 2).astype(I32)
    run_end = seg_end[blk_e] // EXPERT_TILE
    nxt_e = jnp.where(run_end < n_used, blk_e[jnp.minimum(run_end, nb - 1)], -1).astype(I32)
    n_used = n_used.reshape(1)

    xs = moe_dispatch(dest_w, xpk, nb * EXPERT_TILE + n_dead * TOP_K)
    ys = moe_experts(blk_e, nxt_e, slot, n_used, xs, nb, w_gate[0], w_up[0], w_down[0])
    h2 = moe_combine(dest_r, h1, gate_t.T, ys, pw, alpha)

    def prompt_rows(a):
        return a[:n_prompt].reshape(bsz, tp, -1)[:, :t_real]

    y_prompt = h2[:n_prompt].reshape(bsz, tp, d)[:, N_META:t_real]
    y_sample = h2[srow].reshape(s_dec, 1, d)
    k_prompt = prompt_rows(k_r).reshape(1, bsz, t_real, A_KV_HEADS, A_HEAD)
    v_prompt = prompt_rows(p[:, C_VA:C_VA + A_KV_DIM]).reshape(1, bsz, t_real, A_KV_HEADS, A_HEAD)
    ik_prompt = prompt_rows(ik_r)[None]
    last = jnp.arange(bsz) * tp + t_real - 1
    unperm = lambda rows: jnp.concatenate([rows[:, :3 * R_DIM],
                                           rows[:, C_LORA:C_LORA + SHIFT_DIM - 3 * R_DIM]], axis=1)
    shift_prompt = unperm(p[last])[None]
    k_sample = k_r[srow].reshape(1, s_dec, 1, A_KV_HEADS, A_HEAD)
    v_sample = p[srow, C_VA:C_VA + A_KV_DIM].reshape(1, s_dec, 1, A_KV_HEADS, A_HEAD)
    ik_sample = ik_r[srow].reshape(1, s_dec, 1, IDX_DIM)
    shift_sample = unperm(p[srow])[None]
    return (y_prompt, y_sample, k_prompt, v_prompt, ik_prompt, wkv_p[None], shift_prompt,
            k_sample, v_sample, ik_sample, wkv_s[None], shift_sample)
```

```python
import functools

import numpy as np
import jax
import jax.numpy as jnp
from jax import lax
from jax.experimental import pallas as pl
from jax.experimental.pallas import tpu as pltpu

F32 = jnp.float32
BF16 = jnp.bfloat16
I32 = jnp.int32
HIGHEST = lax.Precision.HIGHEST

N_META = 16
R_HEADS, R_HEAD = 16, 64
R_DIM = R_HEADS * R_HEAD
D_DECAY_LORA, D_AAA_LORA, D_GATE_LORA = 64, 64, 160
SHIFT_DIM = 3 * R_DIM + D_DECAY_LORA + D_AAA_LORA + D_GATE_LORA
GN_EPS = 64e-5
A_HEADS, A_KV_HEADS, A_HEAD = 8, 2, 128
A_DIM = A_HEADS * A_HEAD
A_KV_DIM = A_KV_HEADS * A_HEAD
IDX_HEADS, IDX_DIM = 16, 64
TOPK_KEYS = 256
ROPE_THETA = 10000.0
N_EXPERTS, N_EXPERT_GROUPS, TOPK_GROUPS, TOP_K = 64, 8, 4, 8
ROUTED_SCALE = 2.5
LN_EPS = 1e-5

LANES = 128
SUBLANES = 8
ROW_TILE = 256
Q_TILE = 128
CHUNK = 64
EXPERT_TILE = 256
VMEM_LIMIT = 56 * 1024 * 1024
SEARCH_GROUPS = 4
NEG_BIG = -1e30
INT_MIN = -2 ** 31

C_R, C_K, C_V = 0, R_DIM, 2 * R_DIM
C_Q = 3 * R_DIM
C_IQ = C_Q + A_DIM
C_KA = C_IQ + IDX_HEADS * IDX_DIM
C_VA = C_KA + A_KV_DIM
C_IK = C_VA + A_KV_DIM
C_LORA = C_IK + LANES
LORA_W = 384
P_COLS = C_LORA + LORA_W


def _cparams(sem):
    return pltpu.CompilerParams(dimension_semantics=sem, vmem_limit_bytes=VMEM_LIMIT)


def _dot(a, b, precision=None):
    return jnp.dot(a, b, preferred_element_type=F32, precision=precision)


def _dot_nt(a, b, precision=None):
    return lax.dot_general(a, b, (((1,), (1,)), ((), ())), preferred_element_type=F32,
                           precision=precision)


def _dot_tn(a, b, precision=None):
    return lax.dot_general(a, b, (((0,), (0,)), ((), ())), preferred_element_type=F32,
                           precision=precision)


def _split_bf16(x):
    hi = x.astype(BF16)
    return hi, (x - hi.astype(F32)).astype(BF16)


def _dot_f32_by_bf16(a, b):
    hi, lo = _split_bf16(a)
    return _dot(hi, b) + _dot(lo, b)


def _head_sums(x, e, et):
    return _dot_f32_by_bf16(_dot_f32_by_bf16(x, e), et)


def _layer_norm(x, g, b):
    mu = jnp.mean(x, axis=-1, keepdims=True)
    xc = x - mu
    var = jnp.mean(xc * xc, axis=-1, keepdims=True)
    return xc * lax.rsqrt(var + LN_EPS) * g + b


def _sigmoid(z):
    return 1.0 / (1.0 + jnp.exp(-z))


def _ln_proj_kernel(x_ref, g_ref, b_ref, w_ref, o_ref):
    h = _layer_norm(x_ref[...], g_ref[...], b_ref[...])
    o_ref[...] = _dot(h.astype(BF16), w_ref[...])


def ln_proj(x, g, b, w_bf16, tn):
    n, d = x.shape
    cols = w_bf16.shape[1]
    return pl.pallas_call(
        _ln_proj_kernel,
        grid=(cols // tn, n // ROW_TILE),
        in_specs=[
            pl.BlockSpec((ROW_TILE, d), lambda j, i: (i, 0)),
            pl.BlockSpec((1, d), lambda j, i: (0, 0)),
            pl.BlockSpec((1, d), lambda j, i: (0, 0)),
            pl.BlockSpec((d, tn), lambda j, i: (0, j)),
        ],
        out_specs=pl.BlockSpec((ROW_TILE, tn), lambda j, i: (i, j)),
        out_shape=jax.ShapeDtypeStruct((n, cols), F32),
        compiler_params=_cparams(("parallel", "parallel")),
        name="ln_proj",
    )(x, g, b, w_bf16)


def _rot_half(x, head):
    w = x.shape[-1]
    half = head // 2
    lane = lax.broadcasted_iota(I32, x.shape, 1)
    left = pltpu.roll(x, w - half, axis=1)
    right = pltpu.roll(x, half, axis=1)
    return jnp.where((lane % head) < half, left, right)


def _rope_kernel(q_ref, iq_ref, ka_ref, ikw_ref, c128_ref, s128_ref, c64_ref, s64_ref,
                 qo_ref, iqo_ref, ko_ref, iko_ref, iwo_ref):
    c128, s128 = c128_ref[...], s128_ref[...]
    c64, s64 = c64_ref[...], s64_ref[...]

    def rope(x, head, c, s):
        rep = x.shape[-1] // LANES
        if rep > 1:
            c = jnp.concatenate([c] * rep, axis=1)
            s = jnp.concatenate([s] * rep, axis=1)
        return x * c + _rot_half(x, head) * s

    q = rope(q_ref[...], A_HEAD, c128, s128)
    qo_ref[...] = (q * (A_HEAD ** -0.5)).astype(BF16)
    iqo_ref[...] = rope(iq_ref[...], IDX_DIM, c64, s64).astype(BF16)
    ko_ref[...] = rope(ka_ref[...], A_HEAD, c128, s128)
    ikw = ikw_ref[...]
    ik = rope(ikw, IDX_DIM, c64, s64)
    iko_ref[...] = ik[:, :IDX_DIM]
    iwo_ref[...] = ikw[:, IDX_DIM:IDX_DIM + IDX_HEADS]


def rope_all(p, c128, s128, c64, s64):
    n = p.shape[0]
    tm = ROW_TILE
    row = lambda w, blk: pl.BlockSpec((tm, w), lambda i: (i, blk))
    return pl.pallas_call(
        _rope_kernel,
        grid=(n // tm,),
        in_specs=[row(A_DIM, C_Q // A_DIM), row(A_DIM, C_IQ // A_DIM),
                  row(A_KV_DIM, C_KA // A_KV_DIM), row(LANES, C_IK // LANES),
                  row(LANES, 0), row(LANES, 0), row(LANES, 0), row(LANES, 0)],
        out_specs=[row(A_DIM, 0), row(A_DIM, 0), row(A_KV_DIM, 0),
                   row(IDX_DIM, 0), row(IDX_HEADS, 0)],
        out_shape=[jax.ShapeDtypeStruct((n, A_DIM), BF16),
                   jax.ShapeDtypeStruct((n, IDX_HEADS * IDX_DIM), BF16),
                   jax.ShapeDtypeStruct((n, A_KV_DIM), F32),
                   jax.ShapeDtypeStruct((n, IDX_DIM), F32),
                   jax.ShapeDtypeStruct((n, IDX_HEADS), F32)],
        compiler_params=_cparams(("parallel",)),
        name="rope",
    )(p, p, p, p, c128, s128, c64, s64)


def _rwkv_pre_kernel(t_real, tp, from_rows, *refs):
    (x_ref, lo_ref, px_ref, plo_ref, mu_ref, mulo_ref, w0_ref, wb_ref, a0_ref, ab_ref,
     gb_ref, kk_ref, ka_ref, rk_ref, e_ref, et_ref,
     r_o, ld_o, k_o, v_o, kk_o, b_o, g_o, bon_o) = refs
    x = x_ref[...]
    lo = lo_ref[...]
    tm = x.shape[0]
    if from_rows:
        i = pl.program_id(0)
        row = lax.broadcasted_iota(I32, (tm, 1), 0)
        t = (i * tm + row) % tp
        first = row == 0
        sx = jnp.where(first, px_ref[SUBLANES - 1:SUBLANES, :], pltpu.roll(x, 1, axis=0))
        slo = jnp.where(first, plo_ref[SUBLANES - 1:SUBLANES, :], pltpu.roll(lo, 1, axis=0))
        sx = jnp.where(t == 0, 0.0, sx)
        slo = jnp.where(t == 0, 0.0, slo)
        live = t < t_real
    else:
        sx = px_ref[...]
        slo = plo_ref[...]
        live = None
    xx = x + (sx - x) * mu_ref[...]
    xlo = lo + (slo - lo) * mulo_ref[...]
    r = xx[:, C_R:C_R + R_DIM]
    k = xx[:, C_K:C_K + R_DIM]
    v = xx[:, C_V:C_V + R_DIM]
    wa = xlo[:, :LANES]
    xg = xlo[:, LANES:]
    z = w0_ref[...] + _dot(jnp.tanh(wa).astype(BF16), wb_ref[...])
    nz = -z
    softplus = jnp.maximum(nz, 0.0) + jnp.log(1.0 + jnp.exp(-jnp.abs(nz)))
    logd = -jnp.exp(-softplus - 0.5)
    a = _sigmoid(a0_ref[...] + _dot(wa.astype(BF16), ab_ref[...]))
    g = _dot(_sigmoid(xg).astype(BF16), gb_ref[...])
    e, et = e_ref[...], et_ref[...]
    kkr = k * kk_ref[...]
    ss = _head_sums(kkr * kkr, e, et)
    kk = kkr / jnp.maximum(jnp.sqrt(ss), 1e-12)
    k2 = k * (1.0 + (a - 1.0) * ka_ref[...])
    bonus = _head_sums(r * k2 * rk_ref[...], e, et) * v
    b = kk * a
    if live is not None:
        zero = lambda y: jnp.where(live, y, 0.0)
        logd, k2s, vs, kk, b = zero(logd), zero(k2), zero(v), zero(kk), zero(b)
    else:
        k2s, vs = k2, v
    r_o[...] = r
    ld_o[...] = logd
    k_o[...] = k2s
    v_o[...] = vs
    kk_o[...] = kk
    b_o[...] = b
    g_o[...] = g
    bon_o[...] = bonus


def rwkv_pre(p, row0, nrows, prev, pw, t_real, tp):
    tm = min(ROW_TILE, nrows)
    blk0 = row0 // tm
    from_rows = prev is None
    xw = 3 * R_DIM
    cur_x = pl.BlockSpec((tm, xw), lambda i: (blk0 + i, 0))
    cur_lo = pl.BlockSpec((tm, LORA_W), lambda i: (blk0 + i, C_LORA // LORA_W))
    if from_rows:
        r8 = tm // SUBLANES
        prev_x = pl.BlockSpec((SUBLANES, xw), lambda i: (jnp.maximum((blk0 + i) * r8 - 1, 0), 0))
        prev_lo = pl.BlockSpec((SUBLANES, LORA_W),
                               lambda i: (jnp.maximum((blk0 + i) * r8 - 1, 0), C_LORA // LORA_W))
        prev_args = (p, p)
    else:
        prev_x = pl.BlockSpec((tm, xw), lambda i: (i, 0))
        prev_lo = pl.BlockSpec((tm, LORA_W), lambda i: (i, 0))
        prev_args = prev
    full = lambda a: pl.BlockSpec(a.shape, lambda i: (0,) * a.ndim)
    params = (pw["mu_x"], pw["mu_lo"], pw["w0"], pw["w_b"], pw["a0"], pw["a_b"], pw["g_b"],
              pw["k_k"], pw["k_a"], pw["r_k"], pw["e"], pw["et"])
    out = pl.BlockSpec((tm, R_DIM), lambda i: (i, 0))
    return pl.pallas_call(
        functools.partial(_rwkv_pre_kernel, t_real, tp, from_rows),
        grid=(nrows // tm,),
        in_specs=[cur_x, cur_lo, prev_x, prev_lo] + [full(a) for a in params],
        out_specs=[out] * 8,
        out_shape=[jax.ShapeDtypeStruct((nrows, R_DIM), F32)] * 8,
        compiler_params=_cparams(("parallel",)),
        name="rwkv_pre_rows" if from_rows else "rwkv_pre_step",
    )(p, p, *prev_args, *params)


def _rwkv_scan_kernel(r_ref, ld_ref, k_ref, v_ref, kk_ref, b_ref, y_ref, s_ref, ss_scr):
    c = pl.program_id(1)

    @pl.when(c == 0)
    def _():
        ss_scr[...] = jnp.zeros_like(ss_scr)

    n = CHUNK
    n2 = 2 * n
    pairs = R_HEADS // 2
    ld_all = ld_ref[...]
    ri = lax.broadcasted_iota(I32, (n, n), 0)
    ci = lax.broadcasted_iota(I32, (n, n), 1)
    cum_all = _dot((ci <= ri).astype(F32), ld_all, HIGHEST)
    head0 = lax.broadcasted_iota(I32, (n, LANES), 1) < R_HEAD
    r4 = lax.broadcasted_iota(I32, (2 * n2, 2 * n2), 0)
    c4 = lax.broadcasted_iota(I32, (2 * n2, 2 * n2), 1)
    tri = (c4 % n) < (r4 % n) + jnp.where(r4 < n2, 0, 1)
    re = lax.broadcasted_iota(I32, (n2, n2), 0)
    ce = lax.broadcasted_iota(I32, (n2, n2), 1)
    eye = (re == ce).astype(F32)

    def stack(x):
        return jnp.concatenate([jnp.where(head0, x, 0.0), jnp.where(head0, 0.0, x)], axis=0)

    ar, bk, v2, ss, e_last = [], [], [], [], []
    for p in range(pairs):
        sl = slice(p * LANES, (p + 1) * LANES)
        cum, ld = cum_all[:, sl], ld_all[:, sl]
        e_pos = jnp.exp(cum)
        e_neg = jnp.exp(-cum)
        at = -kk_ref[:, sl] * jnp.exp(cum - ld)
        ar.append(jnp.concatenate([stack(at), stack(r_ref[:, sl] * e_pos)], axis=0).astype(BF16))
        bk.append(jnp.concatenate([stack(b_ref[:, sl] * e_neg), stack(k_ref[:, sl] * e_neg)],
                                  axis=0).astype(BF16))
        v2.append(stack(v_ref[:, sl]).astype(BF16))
        ss.append(ss_scr[p])
        e_last.append(e_pos[n - 1:n, :])
    xy0 = [_dot_nt(ar[p], ss[p].astype(BF16)) for p in range(pairs)]
    sc = [jnp.where(tri, _dot_nt(ar[p], bk[p]), 0.0) for p in range(pairs)]
    lp = [s[:n2, :n2] for s in sc]
    t = [eye + l for l in lp]
    m = 1
    while 2 * m < n:
        lpb = [l.astype(BF16) for l in lp]
        lp = [_dot(l, l) for l in lpb]
        t = [t[p] + _dot(t[p].astype(BF16), lp[p].astype(BF16)) for p in range(pairs)]
        m *= 2
    w = [xy0[p][:n2] + _dot(sc[p][:n2, n2:].astype(BF16), v2[p]) for p in range(pairs)]
    u = [_dot(t[p].astype(BF16), w[p].astype(BF16)) for p in range(pairs)]
    uv = [jnp.concatenate([u[p].astype(BF16), v2[p]], axis=0) for p in range(pairs)]
    y = [xy0[p][n2:] + _dot(sc[p][n2:].astype(BF16), uv[p]) for p in range(pairs)]
    upd = [_dot_tn(uv[p], bk[p]) for p in range(pairs)]
    for p in range(pairs):
        y_ref[:, p * LANES:(p + 1) * LANES] = y[p][:n] + y[p][n:]
        ss_scr[p] = (ss[p] + upd[p]) * e_last[p]

    @pl.when(c == pl.num_programs(1) - 1)
    def _():
        for p in range(R_HEADS // 2):
            ss = ss_scr[p]
            s_ref[0, 2 * p] = ss[:R_HEAD, :R_HEAD]
            s_ref[0, 2 * p + 1] = ss[R_HEAD:, R_HEAD:]


def rwkv_scan(r, ld, k, v, kk, b, batch, tp):
    nchunk = tp // CHUNK
    blk = pl.BlockSpec((CHUNK, R_DIM), lambda bi, c: (bi * nchunk + c, 0))
    return pl.pallas_call(
        _rwkv_scan_kernel,
        grid=(batch, nchunk),
        in_specs=[blk] * 6,
        out_specs=[blk, pl.BlockSpec((1, R_HEADS, R_HEAD, R_HEAD), lambda bi, c: (bi, 0, 0, 0))],
        out_shape=[jax.ShapeDtypeStruct((batch * tp, R_DIM), F32),
                   jax.ShapeDtypeStruct((batch, R_HEADS, R_HEAD, R_HEAD), F32)],
        scratch_shapes=[pltpu.VMEM((R_HEADS // 2, LANES, LANES), F32)],
        compiler_params=_cparams(("parallel", "arbitrary")),
        name="rwkv_scan",
    )(r, ld, k, v, kk, b)


STEP_ROWS = 16


def _rwkv_step_kernel(r_ref, ld_ref, k_ref, v_ref, kk_ref, b_ref, s_ref, y_ref, so_ref):
    r, k = r_ref[0], k_ref[0]
    dec = jnp.exp(ld_ref[0])
    na = -kk_ref[0]
    b = b_ref[0]
    v = v_ref[0]
    ys = []
    for v0 in range(0, R_HEAD, STEP_ROWS):
        rows = range(v0, v0 + STEP_ROWS)
        s = [s_ref[0, vi] for vi in rows]
        sa = [jnp.sum(x * na, axis=0, keepdims=True) for x in s]
        s_new = [x * dec + a * b + v[vi:vi + 1, :] * k for x, a, vi in zip(s, sa, rows)]
        ys += [jnp.sum(x * r, axis=0, keepdims=True) for x in s_new]
        for x, vi in zip(s_new, rows):
            so_ref[0, vi] = x
    y_ref[0] = jnp.concatenate(ys, axis=0)


def rwkv_step(r, ld, k, v, kk, b, state):
    s = state.shape[-1]
    vec = pl.BlockSpec((1, R_HEAD, s), lambda h: (h, 0, 0))
    st = pl.BlockSpec((1, R_HEAD, R_HEAD, s), lambda h: (h, 0, 0, 0))
    return pl.pallas_call(
        _rwkv_step_kernel,
        grid=(R_HEADS,),
        in_specs=[vec] * 6 + [st],
        out_specs=[vec, st],
        out_shape=[jax.ShapeDtypeStruct((R_HEADS, R_HEAD, s), F32),
                   jax.ShapeDtypeStruct(state.shape, F32)],
        compiler_params=_cparams(("parallel",)),
        name="rwkv_step",
    )(r, ld, k, v, kk, b, state)


def _select_topk(score, allowed, n_sel):
    bits = lax.bitcast_convert_type(score, I32)
    key = jnp.where(bits < 0, bits ^ jnp.int32(0x7FFFFFFF), bits)
    key = jnp.where(allowed, key, jnp.int32(INT_MIN))
    m = score.shape[0]
    gr = m // SEARCH_GROUPS if m % (SEARCH_GROUPS * SUBLANES) == 0 else m
    keys = [key[r0:r0 + gr] for r0 in range(0, m, gr)]

    def body(i, taus):
        step = lax.shift_left(jnp.int32(1), jnp.int32(31) - i)
        cands = [tau + step for tau in taus]
        cnts = [jnp.sum((k >= c).astype(I32), axis=1, keepdims=True) for k, c in zip(keys, cands)]
        return tuple(jnp.where(cnt >= n_sel, c, tau) for cnt, c, tau in zip(cnts, cands, taus))

    taus = lax.fori_loop(0, 32, body, tuple(jnp.full((gr, 1), INT_MIN, I32) for _ in keys))
    tau = jnp.concatenate(taus, axis=0)
    return jnp.logical_and(key >= tau, allowed)


KEY_TILE = 256
Q_TILES_PER_EXTENT = 3


def _dsa_prompt_block(n_sel, tk, i, q_ref, iq_ref, iw_ref, ik_ref, k_ref, v_ref, o_ref, sc_ref):
    tq = q_ref.shape[0]
    iw = iw_ref[...] * ((IDX_HEADS * IDX_DIM) ** -0.5)
    iq = iq_ref[...]
    iq_h = [iq[:, h * IDX_DIM:(h + 1) * IDX_DIM] for h in range(IDX_HEADS)]
    iw_h = [iw[:, h:h + 1] for h in range(IDX_HEADS)]
    for c0 in range(0, tk, KEY_TILE):
        c1 = min(c0 + KEY_TILE, tk)
        ikb = ik_ref[c0:c1, :].astype(BF16)
        acc = jnp.maximum(_dot_nt(iq_h[0], ikb), 0.0) * iw_h[0]
        for h in range(1, IDX_HEADS):
            acc = acc + jnp.maximum(_dot_nt(iq_h[h], ikb), 0.0) * iw_h[h]
        sc_ref[:, c0:c1] = acc
    qpos = i * tq + lax.broadcasted_iota(I32, (tq, 1), 0)
    kpos = lax.broadcasted_iota(I32, (1, tk), 1)
    sel = _select_topk(sc_ref[:, :tk], kpos <= qpos, n_sel)
    bias = jnp.where(sel, 0.0, NEG_BIG)
    q = q_ref[...]
    rep = A_HEADS // A_KV_HEADS
    for g in range(A_KV_HEADS):
        kg = k_ref[:tk, g * A_HEAD:(g + 1) * A_HEAD].astype(BF16)
        vg = v_ref[:tk, g * A_HEAD:(g + 1) * A_HEAD].astype(BF16)
        for rr in range(rep):
            h = g * rep + rr
            s = _dot_nt(q[:, h * A_HEAD:(h + 1) * A_HEAD], kg) + bias
            m = jnp.max(s, axis=1, keepdims=True)
            p = jnp.exp(s - m)
            l = jnp.sum(p, axis=1, keepdims=True)
            o_ref[:, h * A_HEAD:(h + 1) * A_HEAD] = _dot(p.astype(BF16), vg) / l


def _dsa_prompt_kernel(n_sel, *refs):
    i = pl.program_id(1)
    tq = refs[0].shape[0]
    tp = refs[4].shape[0]
    nq = tp // tq
    for lo in range(0, nq, Q_TILES_PER_EXTENT):
        hi = min(lo + Q_TILES_PER_EXTENT, nq)

        @pl.when(jnp.logical_and(i >= lo, i < hi))
        def _(hi=hi):
            _dsa_prompt_block(n_sel, hi * tq, i, *refs)


def dsa_prompt(q, iq, iw, ik, k, p, batch, tp, n_sel):
    nq = tp // Q_TILE
    qrow = lambda w: pl.BlockSpec((Q_TILE, w), lambda b, i: (b * nq + i, 0))
    keys = lambda w, blk: pl.BlockSpec((tp, w), lambda b, i: (b, blk))
    return pl.pallas_call(
        functools.partial(_dsa_prompt_kernel, n_sel),
        grid=(batch, nq),
        in_specs=[qrow(A_DIM), qrow(IDX_HEADS * IDX_DIM), qrow(IDX_HEADS),
                  keys(IDX_DIM, 0), keys(A_KV_DIM, 0), keys(A_KV_DIM, C_VA // A_KV_DIM)],
        out_specs=qrow(A_DIM),
        out_shape=jax.ShapeDtypeStruct((batch * tp, A_DIM), F32),
        scratch_shapes=[pltpu.VMEM((Q_TILE, tp), F32)],
        compiler_params=_cparams(("parallel", "parallel")),
        name="dsa_prompt",
    )(q, iq, iw, ik, k, p)


def _dsa_step_score_kernel(n_pages, page, pt_ref, iq_ref, iw_ref, ikn_ref, *refs):
    pages = refs[:n_pages]
    o_ref = refs[n_pages]
    iq = iq_ref[0]
    iw = iw_ref[0] * ((IDX_HEADS * IDX_DIM) ** -0.5)
    for j in range(n_pages):
        d = _dot(iq, pages[j][...].astype(BF16))
        o_ref[0, :, j * page:(j + 1) * page] = jnp.sum(jnp.maximum(d, 0.0) * iw, axis=0,
                                                       keepdims=True)
    dn = jnp.sum(iq.astype(F32) * ikn_ref[0], axis=1, keepdims=True)
    sn = jnp.sum(jnp.maximum(dn, 0.0) * iw, axis=0, keepdims=True)
    lane = lax.broadcasted_iota(I32, (1, LANES), 1)
    o_ref[0, :, n_pages * page:] = jnp.where(lane == 0, sn, 0.0)


def dsa_step_scores(pt_flat, iq, iw, ik_new, cik2d, n_pages, page):
    s = iq.shape[0]
    kw = n_pages * page + LANES
    page_spec = lambda j: pl.BlockSpec((IDX_DIM, page), lambda i, pt: (pt[i * n_pages + j], 0))
    grid_spec = pltpu.PrefetchScalarGridSpec(
        num_scalar_prefetch=1,
        grid=(s,),
        in_specs=[pl.BlockSpec((1, IDX_HEADS, IDX_DIM), lambda i, pt: (i, 0, 0)),
                  pl.BlockSpec((1, IDX_HEADS, 1), lambda i, pt: (i, 0, 0)),
                  pl.BlockSpec((1, 1, IDX_DIM), lambda i, pt: (i, 0, 0))]
                 + [page_spec(j) for j in range(n_pages)],
        out_specs=pl.BlockSpec((1, 1, kw), lambda i, pt: (i, 0, 0)),
    )
    return pl.pallas_call(
        functools.partial(_dsa_step_score_kernel, n_pages, page),
        grid_spec=grid_spec,
        out_shape=jax.ShapeDtypeStruct((s, 1, kw), F32),
        compiler_params=_cparams(("arbitrary",)),
        name="dsa_step_scores",
    )(pt_flat, iq, iw, ik_new, *([cik2d] * n_pages))


def _dsa_step_select_kernel(n_sel, past, sc_ref, o_ref):
    sc = sc_ref[...]
    kpos = lax.broadcasted_iota(I32, sc.shape, 1)
    sel = _select_topk(sc, kpos <= past, n_sel)
    o_ref[...] = sel.astype(F32)


def dsa_step_select(sc, n_sel, past):
    return pl.pallas_call(
        functools.partial(_dsa_step_select_kernel, n_sel, past),
        out_shape=jax.ShapeDtypeStruct(sc.shape, F32),
        compiler_params=pltpu.CompilerParams(vmem_limit_bytes=VMEM_LIMIT),
        name="dsa_step_select",
    )(sc)


def _dsa_step_attn_kernel(n_pages, page, pt_ref, q_ref, kn_ref, vn_ref, sel_ref, ex_ref, *refs):
    kp = refs[:n_pages]
    vp = refs[n_pages:2 * n_pages]
    o_ref = refs[2 * n_pages]
    q = q_ref[0]
    rep = A_HEADS // A_KV_HEADS
    w2 = page * A_KV_HEADS
    hrow = lax.broadcasted_iota(I32, (A_HEADS, w2), 0)
    col = lax.broadcasted_iota(I32, (A_HEADS, w2), 1)
    own = (col % A_KV_HEADS) == (hrow // rep)
    ex = ex_ref[...]
    logits = []
    for j in range(n_pages):
        s = _dot_nt(q, kp[j][...].astype(BF16))
        selj = _dot(sel_ref[0, :, j * page:(j + 1) * page].astype(BF16), ex)
        logits.append(jnp.where(jnp.logical_and(selj > 0.5, own), s, NEG_BIG))
    h8 = lax.broadcasted_iota(I32, (A_HEADS, A_HEAD), 0)
    kn = jnp.where(h8 < rep, kn_ref[0, 0:1, :], kn_ref[0, 1:2, :])
    vn = jnp.where(h8 < rep, vn_ref[0, 0:1, :], vn_ref[0, 1:2, :])
    sn = jnp.sum(q.astype(F32) * kn, axis=1, keepdims=True)
    seln = sel_ref[0, :, n_pages * page:n_pages * page + 1]
    sn = jnp.where(seln > 0.5, sn, NEG_BIG)
    m = sn
    for s in logits:
        m = jnp.maximum(m, jnp.max(s, axis=1, keepdims=True))
    pn = jnp.exp(sn - m)
    l = pn
    acc = pn * vn
    for j in range(n_pages):
        p = jnp.exp(logits[j] - m)
        l = l + jnp.sum(p, axis=1, keepdims=True)
        acc = acc + _dot(p.astype(BF16), vp[j][...].astype(BF16))
    o_ref[0] = acc / l


def dsa_step_attn(pt_flat, q, k_new, v_new, sel, expand, ck2d, cv2d, n_pages, page):
    s = q.shape[0]
    kw = sel.shape[-1]
    w2 = page * A_KV_HEADS
    page_spec = lambda j: pl.BlockSpec((w2, A_HEAD), lambda i, pt: (pt[i * n_pages + j], 0))
    grid_spec = pltpu.PrefetchScalarGridSpec(
        num_scalar_prefetch=1,
        grid=(s,),
        in_specs=[pl.BlockSpec((1, A_HEADS, A_HEAD), lambda i, pt: (i, 0, 0)),
                  pl.BlockSpec((1, A_KV_HEADS, A_HEAD), lambda i, pt: (i, 0, 0)),
                  pl.BlockSpec((1, A_KV_HEADS, A_HEAD), lambda i, pt: (i, 0, 0)),
                  pl.BlockSpec((1, 1, kw), lambda i, pt: (i, 0, 0)),
                  pl.BlockSpec((page, w2), lambda i, pt: (0, 0))]
                 + [page_spec(j) for j in range(n_pages)] * 2,
        out_specs=pl.BlockSpec((1, A_HEADS, A_HEAD), lambda i, pt: (i, 0, 0)),
    )
    return pl.pallas_call(
        functools.partial(_dsa_step_attn_kernel, n_pages, page),
        grid_spec=grid_spec,
        out_shape=jax.ShapeDtypeStruct((s, A_HEADS, A_HEAD), F32),
        compiler_params=_cparams(("arbitrary",)),
        name="dsa_step_attn",
    )(pt_flat, q, k_new, v_new, sel, expand, *([ck2d] * n_pages), *([cv2d] * n_pages))


def _pack_bf16_pairs(x):
    w = x.shape[1] // 2
    hi = lax.bitcast_convert_type(x[:, :w].astype(BF16).astype(F32), I32)
    lo = lax.bitcast_convert_type(x[:, w:].astype(BF16).astype(F32), I32)
    return hi | lax.shift_right_logical(lo, 16)


def _unpack_bf16_pairs(p):
    hi = lax.bitcast_convert_type(p & jnp.int32(-65536), F32)
    lo = lax.bitcast_convert_type(lax.shift_left(p, 16), F32)
    return hi, lo


def _mix_kernel(alpha, prompt_blocks, x_ref, yp_ref, bonp_ref, gp_ref, ap_ref, ys_ref, bons_ref, gs_ref,
                as_ref, e_ref, et_ref, gng_ref, gnb_ref, l0g_ref, l0b_ref, wo_ref, l1g_ref, l1b_ref,
                wr_ref, h_ref, sc_ref, pk_ref):
    is_prompt = pl.program_id(0) < prompt_blocks
    pick = lambda p_ref, s_ref: jnp.where(is_prompt, p_ref[...], s_ref[...])
    e, et = e_ref[...], et_ref[...]
    y = pick(yp_ref, ys_ref)
    inv = 1.0 / R_HEAD
    mu = _head_sums(y, e, et) * inv
    d = y - mu
    var = _head_sums(d * d, e, et) * inv
    yn = d * lax.rsqrt(var + GN_EPS) * gng_ref[...] + gnb_ref[...]
    r_out = (yn + pick(bonp_ref, bons_ref)) * pick(gp_ref, gs_ref)
    mix = (_dot(r_out.astype(BF16), wo_ref[:R_DIM, :])
           + _dot(pick(ap_ref, as_ref).astype(BF16), wo_ref[R_DIM:, :]))
    h0 = _layer_norm(x_ref[...], l0g_ref[...], l0b_ref[...])
    h1 = _layer_norm(alpha * h0 + mix, l1g_ref[...], l1b_ref[...])
    h_ref[...] = h1
    w_hi, w_lo = _split_bf16(wr_ref[...])
    h_hi, h_lo = _split_bf16(h1)
    sc_ref[...] = _sigmoid(_dot_nt(w_hi, h_hi) + _dot_nt(w_hi, h_lo) + _dot_nt(w_lo, h_hi))
    pk_ref[...] = _pack_bf16_pairs(h1)


def mix_ln1_router(x, prompt_parts, step_parts, pw, alpha):
    n, d = x.shape
    tm = ROW_TILE
    pb = prompt_parts[0].shape[0] // tm
    row = lambda w: pl.BlockSpec((tm, w), lambda i: (i, 0))
    head = lambda w: pl.BlockSpec((tm, w), lambda i: (jnp.minimum(i, pb - 1), 0))
    tail = lambda w: pl.BlockSpec((tm, w), lambda i: (jnp.maximum(i - pb, 0), 0))
    full = lambda a: pl.BlockSpec(a.shape, lambda i: (0,) * a.ndim)
    params = (pw["e"], pw["et"], pw["gn_g"], pw["gn_b"], pw["ln0_g"], pw["ln0_b"], pw["w_out"],
              pw["ln1_g"], pw["ln1_b"], pw["w_router_t"])
    widths = (R_DIM, R_DIM, R_DIM, A_DIM)
    return pl.pallas_call(
        functools.partial(_mix_kernel, alpha, pb),
        grid=(n // tm,),
        in_specs=[row(d)] + [head(w) for w in widths] + [tail(w) for w in widths]
                 + [full(a) for a in params],
        out_specs=[row(d), pl.BlockSpec((N_EXPERTS, tm), lambda i: (0, i)), row(d // 2)],
        out_shape=[jax.ShapeDtypeStruct((n, d), F32),
                   jax.ShapeDtypeStruct((N_EXPERTS, n), F32),
                   jax.ShapeDtypeStruct((n, d // 2), I32)],
        compiler_params=_cparams(("parallel",)),
        name="mix_ln1_router",
    )(x, *prompt_parts, *step_parts, *params)


def _route_kernel(sc_ref, bias_ref, idx_ref, gate_ref, pos_ref, cnt_ref, cnt_scr):
    scores = sc_ref[...]
    biased = scores + bias_ref[...]
    tn = scores.shape[1]
    per = N_EXPERTS // N_EXPERT_GROUPS
    sub = lax.broadcasted_iota(I32, (per, tn), 0)
    grp_rows = []
    for g in range(N_EXPERT_GROUPS):
        xg = biased[g * per:(g + 1) * per, :]
        m1 = jnp.max(xg, axis=0, keepdims=True)
        first = jnp.min(jnp.where(xg == m1, sub, per), axis=0, keepdims=True)
        m2 = jnp.max(jnp.where(sub == first, -jnp.inf, xg), axis=0, keepdims=True)
        grp_rows.append(m1 + m2)
    grp = jnp.concatenate(grp_rows, axis=0)
    gi = lax.broadcasted_iota(I32, (N_EXPERT_GROUPS, tn), 0)
    gsel = jnp.zeros((N_EXPERT_GROUPS, tn), jnp.bool_)
    for _ in range(TOPK_GROUPS):
        m = jnp.max(grp, axis=0, keepdims=True)
        first = jnp.min(jnp.where(grp == m, gi, N_EXPERT_GROUPS), axis=0, keepdims=True)
        hit = gi == first
        gsel = jnp.logical_or(gsel, hit)
        grp = jnp.where(hit, -jnp.inf, grp)
    ei = lax.broadcasted_iota(I32, (N_EXPERTS, tn), 0)
    emask = jnp.concatenate(
        [jnp.broadcast_to(gsel[g:g + 1, :], (per, tn)) for g in range(N_EXPERT_GROUPS)], axis=0)
    cand = jnp.where(emask, biased, -jnp.inf)
    idxs, gates, hits = [], [], []
    for _ in range(TOP_K):
        m = jnp.max(cand, axis=0, keepdims=True)
        first = jnp.min(jnp.where(cand == m, ei, N_EXPERTS), axis=0, keepdims=True)
        hit = ei == first
        idxs.append(first)
        hits.append(hit)
        gates.append(jnp.sum(jnp.where(hit, scores, 0.0), axis=0, keepdims=True))
        cand = jnp.where(hit, -jnp.inf, cand)
    gate = jnp.concatenate(gates, axis=0)
    gate = gate / jnp.sum(gate, axis=0, keepdims=True) * ROUTED_SCALE
    idx_ref[...] = jnp.concatenate(idxs, axis=0)
    gate_ref[...] = gate
    chosen = hits[0]
    for hit in hits[1:]:
        chosen = jnp.logical_or(chosen, hit)
    onehot = jnp.where(chosen, 1.0, 0.0)
    ta = lax.broadcasted_iota(I32, (tn, tn), 0)
    tb = lax.broadcasted_iota(I32, (tn, tn), 1)
    prefix = _dot(onehot.astype(BF16), (ta < tb).astype(BF16))

    @pl.when(pl.program_id(0) == 0)
    def _():
        cnt_scr[...] = jnp.zeros_like(cnt_scr)

    rank = prefix + cnt_scr[:, 0:1]
    pos_ref[...] = jnp.concatenate(
        [jnp.sum(jnp.where(hit, rank, 0.0), axis=0, keepdims=True) for hit in hits], axis=0).astype(I32)
    cnt_scr[...] = cnt_scr[...] + jnp.sum(onehot, axis=1, keepdims=True)
    cnt_ref[...] = cnt_scr[...].astype(I32)


def route(scores_t, e_bias):
    n = scores_t.shape[1]
    tn = ROW_TILE
    tok = pl.BlockSpec((TOP_K, tn), lambda i: (0, i))
    return pl.pallas_call(
        _route_kernel,
        grid=(n // tn,),
        in_specs=[pl.BlockSpec((N_EXPERTS, tn), lambda i: (0, i)),
                  pl.BlockSpec((N_EXPERTS, 1), lambda i: (0, 0))],
        out_specs=[tok, tok, tok, pl.BlockSpec((N_EXPERTS, LANES), lambda i: (0, 0))],
        out_shape=[jax.ShapeDtypeStruct((TOP_K, n), I32), jax.ShapeDtypeStruct((TOP_K, n), F32),
                   jax.ShapeDtypeStruct((TOP_K, n), I32),
                   jax.ShapeDtypeStruct((N_EXPERTS, LANES), I32)],
        scratch_shapes=[pltpu.VMEM((N_EXPERTS, LANES), F32)],
        compiler_params=_cparams(("arbitrary",)),
        name="route",
    )(scores_t, e_bias)


def _dispatch_kernel(nb, n_slots, dest_ref, segend_ref, cnt_ref, nu_ref, x_ref, o_ref, inv_ref,
                     zbuf, sem, zsem):
    tm = x_ref.shape[0]
    row0 = pl.program_id(0) * tm

    @pl.when(pl.program_id(0) == 0)
    def _():
        def per_expert(e, carry):
            hi = segend_ref[e]
            lo = hi - (cnt_ref[e] + EXPERT_TILE - 1) // EXPERT_TILE * EXPERT_TILE + cnt_ref[e]

            def put(r, c):
                inv_ref[r] = n_slots + e * EXPERT_TILE + (r - (hi - EXPERT_TILE))
                return c
            lax.fori_loop(lo, hi, put, 0)
            return carry
        lax.fori_loop(0, N_EXPERTS, per_expert, 0)

        def free_row(r, c):
            inv_ref[r] = n_slots + r % (N_EXPERTS * EXPERT_TILE)
            return c
        lax.fori_loop(nu_ref[0] * EXPERT_TILE, nb * EXPERT_TILE, free_row, 0)

    def fill(start):
        return pltpu.make_async_copy(zbuf, o_ref.at[pl.ds(pl.multiple_of(start, EXPERT_TILE),
                                                          EXPERT_TILE)], zsem)

    @pl.when(pl.program_id(0) == 0)
    def _():
        zbuf[...] = jnp.zeros_like(zbuf)

        def each_expert(fn):
            def body(e, carry):
                @pl.when(cnt_ref[e] > 0)
                def _():
                    fn(fill(segend_ref[e] - EXPERT_TILE))
                return carry
            lax.fori_loop(0, N_EXPERTS, body, 0)

        def each_free_block(fn):
            def body(b, carry):
                fn(fill(b * EXPERT_TILE))
                return carry
            lax.fori_loop(nu_ref[0], nb, body, 0)

        each_expert(lambda c: c.start())
        each_free_block(lambda c: c.start())
        each_expert(lambda c: c.wait())
        each_free_block(lambda c: c.wait())

    def start(i, carry):
        for j in range(TOP_K):
            d = dest_ref[i * TOP_K + j]
            inv_ref[d] = (row0 + i) * TOP_K + j
            pltpu.make_async_copy(x_ref.at[pl.ds(i, 1)], o_ref.at[pl.ds(d, 1)], sem).start()
        return carry

    lax.fori_loop(0, tm, start, 0)
    for j in range(TOP_K):
        pltpu.make_async_copy(x_ref, o_ref.at[pl.ds(0, tm)], sem).wait()


def moe_dispatch(dest_flat, seg_end, counts, n_used, xpk, nb):
    n, w = xpk.shape
    tm = ROW_TILE
    rows = nb * EXPERT_TILE
    smem = lambda: pl.BlockSpec(memory_space=pltpu.SMEM)
    return pl.pallas_call(
        functools.partial(_dispatch_kernel, nb, n * TOP_K),
        grid=(n // tm,),
        in_specs=[pl.BlockSpec((tm * TOP_K,), lambda i: (i,), memory_space=pltpu.SMEM),
                  smem(), smem(), smem(),
                  pl.BlockSpec((tm, w), lambda i: (i, 0))],
        out_specs=[pl.BlockSpec(memory_space=pl.ANY), smem()],
        out_shape=[jax.ShapeDtypeStruct((rows, w), I32), jax.ShapeDtypeStruct((rows,), I32)],
        scratch_shapes=[pltpu.VMEM((EXPERT_TILE, w), I32), pltpu.SemaphoreType.DMA(()),
                        pltpu.SemaphoreType.DMA(())],
        compiler_params=_cparams(("arbitrary",)),
        name="moe_dispatch",
    )(dest_flat, seg_end, counts, n_used, xpk)


def _experts_kernel(n_slots, nb, be_ref, nxt_ref, slot_ref, nu_ref, inv_ref, x_ref, wg_ref, wu_ref,
                    wd_ref, o_ref, wg_f, wu_f, wd_f, wg_s, wu_s, wd_s, obuf_a, obuf_b, sems, osems):
    i = pl.program_id(0)
    n_used = nu_ref[0]
    used = i < n_used
    prev = be_ref[jnp.maximum(i - 1, 0)]
    fresh = jnp.logical_and(used, jnp.logical_or(i == 0, be_ref[i] != prev))
    bm = x_ref.shape[0]

    obufs = (obuf_a, obuf_b)

    def scatter_rows(blk, par):
        for r in range(bm):
            pltpu.make_async_copy(obufs[par].at[pl.ds(r, 1)],
                                  o_ref.at[pl.ds(inv_ref[blk * bm + r], 1)], osems.at[par]).start()

    def wait_rows(par):
        pltpu.make_async_copy(obufs[par], o_ref.at[pl.ds(0, bm)], osems.at[par]).wait()

    def by_parity(cond, fn):
        for par in range(2):
            @pl.when(jnp.logical_and(cond, i % 2 == par))
            def _(par=par):
                fn(par)

    @pl.when(i == 0)
    def _():
        obuf_b[...] = jnp.zeros_like(obuf_b)
        spare = [pltpu.make_async_copy(obuf_b, o_ref.at[pl.ds(n_slots + e * bm, bm)], osems.at[1])
                 for e in range(N_EXPERTS)]
        for c in spare:
            c.start()
        for c in spare:
            c.wait()

    by_parity(jnp.logical_and(i >= 2, i - 2 < n_used), wait_rows)

    def weight_copies(e, slot):
        return (pltpu.make_async_copy(wg_ref.at[e], wg_f.at[slot], sems.at[slot, 0]),
                pltpu.make_async_copy(wu_ref.at[e], wu_f.at[slot], sems.at[slot, 1]),
                pltpu.make_async_copy(wd_ref.at[e], wd_f.at[slot], sems.at[slot, 2]))

    @pl.when(jnp.logical_and(used, i == 0))
    def _():
        for c in weight_copies(be_ref[0], 0):
            c.start()

    @pl.when(fresh)
    def _():
        slot = slot_ref[i]
        for c in weight_copies(be_ref[i], slot):
            c.wait()

        @pl.when(nxt_ref[i] >= 0)
        def _():
            for c in weight_copies(nxt_ref[i], 1 - slot):
                c.start()

        wg_s[...] = wg_f[slot].astype(BF16)
        wu_s[...] = wu_f[slot].astype(BF16)
        wd_s[...] = wd_f[slot].astype(BF16)

    def compute(par):
        hi, lo = _unpack_bf16_pairs(x_ref[...])
        hi, lo = hi.astype(BF16), lo.astype(BF16)
        half = hi.shape[1]
        gp = _dot(hi, wg_s[:half, :]) + _dot(lo, wg_s[half:, :])
        up = _dot(hi, wu_s[:half, :]) + _dot(lo, wu_s[half:, :])
        act = gp * _sigmoid(gp) * up
        obufs[par][...] = _pack_bf16_pairs(_dot(act.astype(BF16), wd_s[...]))

    @pl.when(jnp.logical_and(used, i == 0))
    def _():
        compute(0)

    def send_prev_and_compute(par):
        scatter_rows(i - 1, 1 - par)
        compute(par)

    by_parity(jnp.logical_and(used, i > 0), send_prev_and_compute)
    by_parity(jnp.logical_and(i == n_used, i > 0), lambda par: scatter_rows(i - 1, 1 - par))

    @pl.when(jnp.logical_and(i == nb - 1, n_used == nb - 1))
    def _():
        wait_rows((nb - 2) % 2)


def moe_experts(blk_e, nxt_e, slot, n_used, inv, xs, n_slots, w_gate, w_up, w_down):
    w = xs.shape[1]
    nb = xs.shape[0] // EXPERT_TILE
    _, d, de = w_gate.shape
    grid_spec = pltpu.PrefetchScalarGridSpec(
        num_scalar_prefetch=5,
        grid=(nb,),
        in_specs=[pl.BlockSpec((EXPERT_TILE, w), lambda i, be, nx, sl, nu, iv: (jnp.minimum(i, nu[0] - 1), 0)),
                  pl.BlockSpec(memory_space=pl.ANY), pl.BlockSpec(memory_space=pl.ANY),
                  pl.BlockSpec(memory_space=pl.ANY)],
        out_specs=pl.BlockSpec(memory_space=pl.ANY),
        scratch_shapes=[pltpu.VMEM((2, d, de), F32), pltpu.VMEM((2, d, de), F32),
                        pltpu.VMEM((2, de, d), F32),
                        pltpu.VMEM((d, de), BF16), pltpu.VMEM((d, de), BF16),
                        pltpu.VMEM((de, d), BF16), pltpu.VMEM((EXPERT_TILE, w), I32),
                        pltpu.VMEM((EXPERT_TILE, w), I32),
                        pltpu.SemaphoreType.DMA((2, 3)), pltpu.SemaphoreType.DMA((2,))],
    )
    return pl.pallas_call(
        functools.partial(_experts_kernel, n_slots, nb),
        grid_spec=grid_spec,
        out_shape=jax.ShapeDtypeStruct((n_slots + N_EXPERTS * EXPERT_TILE, w), I32),
        compiler_params=_cparams(("arbitrary",)),
        name="moe_experts",
    )(blk_e, nxt_e, slot, n_used, inv, xs, w_gate, w_up, w_down)


def _combine_kernel(alpha, h_ref, gate_ref, ys_ref, wsg_ref, wsu_ref, wsd_ref, l2g_ref, l2b_ref, o_ref):
    tm = h_ref.shape[0]
    h = h_ref[...]
    hb = h.astype(BF16)
    gp = _dot(hb, wsg_ref[...])
    up = _dot(hb, wsu_ref[...])
    shared = _dot((gp * _sigmoid(gp) * up).astype(BF16), wsd_ref[...])
    row = lax.broadcasted_iota(I32, (tm, tm * TOP_K), 0)
    col = lax.broadcasted_iota(I32, (tm, tm * TOP_K), 1)
    g_hi, g_lo = _split_bf16(jnp.where(col // TOP_K == row, gate_ref[0], 0.0))
    hi, lo = _unpack_bf16_pairs(ys_ref[...])
    hi, lo = hi.astype(BF16), lo.astype(BF16)
    routed = jnp.concatenate([_dot(g_hi, hi) + _dot(g_lo, hi), _dot(g_hi, lo) + _dot(g_lo, lo)], axis=1)
    o_ref[...] = _layer_norm(alpha * h + routed + shared, l2g_ref[...], l2b_ref[...])


def moe_combine(h1, gate_rows, ys, pw, alpha):
    n, d = h1.shape
    tm = gate_rows.shape[2] // TOP_K
    full = lambda a: pl.BlockSpec(a.shape, lambda i: (0,) * a.ndim)
    params = (pw["ws_gate"], pw["ws_up"], pw["ws_down"], pw["ln2_g"], pw["ln2_b"])
    return pl.pallas_call(
        functools.partial(_combine_kernel, alpha),
        grid=(n // tm,),
        in_specs=[pl.BlockSpec((tm, d), lambda i: (i, 0)),
                  pl.BlockSpec((1, 1, tm * TOP_K), lambda i: (i, 0, 0)),
                  pl.BlockSpec((tm * TOP_K, d // 2), lambda i: (i, 0))]
                 + [full(a) for a in params],
        out_specs=pl.BlockSpec((tm, d), lambda i: (i, 0)),
        out_shape=jax.ShapeDtypeStruct((n, d), F32),
        compiler_params=_cparams(("parallel",)),
        name="moe_combine",
    )(h1, gate_rows, ys, *params)


def _round_up(x, m):
    return (x + m - 1) // m * m


def _rope_tables(pos, head):
    half = head // 2
    inv = ROPE_THETA ** (-jnp.arange(half, dtype=F32) / half)
    ang = pos.astype(F32)[:, None] * inv[None, :]
    cos, sin = jnp.cos(ang), jnp.sin(ang)
    rep = LANES // head
    c = jnp.tile(jnp.concatenate([cos, cos], axis=1), (1, rep))
    s = jnp.tile(jnp.concatenate([-sin, sin], axis=1), (1, rep))
    return c, s


def _permute_cols(m):
    a0 = SHIFT_DIM
    pieces = [
        m[..., 0:3 * R_DIM],
        m[..., a0:a0 + A_DIM],
        m[..., a0 + A_DIM + 2 * A_KV_DIM:a0 + A_DIM + 2 * A_KV_DIM + IDX_HEADS * IDX_DIM],
        m[..., a0 + A_DIM:a0 + A_DIM + 2 * A_KV_DIM],
    ]
    i0 = a0 + A_DIM + 2 * A_KV_DIM + IDX_HEADS * IDX_DIM
    pieces.append(m[..., i0:i0 + IDX_DIM + IDX_HEADS])
    pad = lambda w: jnp.zeros(m.shape[:-1] + (w,), m.dtype)
    pieces.append(pad(LANES - IDX_DIM - IDX_HEADS))
    pieces.append(m[..., 3 * R_DIM:SHIFT_DIM])
    pieces.append(pad(LORA_W - (SHIFT_DIM - 3 * R_DIM)))
    return jnp.concatenate(pieces, axis=-1)


def kernel(x_prompt, x_sample, cache_k, cache_v, cache_idx_k, state_wkv, state_shift, page_table,
           meta, ln0_g, ln0_b, w_in, mu_shift, w0, w_b, a0, a_b, g_b, k_k, k_a, r_k, gn_g, gn_b,
           w_out, ln1_g, ln1_b, w_router, e_bias, w_gate, w_up, w_down, ws_gate, ws_up, ws_down,
           ln2_g, ln2_b):
    depth = w_in.shape[0]
    assert depth == 1, "single trunk layer"
    bsz, s_p, d = x_prompt.shape
    s_dec, s_s, _ = x_sample.shape
    assert s_s == 1, "one decode token per sequence"
    t_real = N_META + s_p
    tp = _round_up(t_real, LANES)
    assert (bsz * tp) % ROW_TILE == 0
    sp = _round_up(s_dec, ROW_TILE)
    n_prompt = bsz * tp
    n = n_prompt + sp
    n_pool, page = cache_k.shape[1], cache_k.shape[2]
    n_pages = page_table.shape[1]
    past = n_pages * page
    alpha = float((2 * depth) ** 0.25)
    row2 = lambda a: a.reshape(1, -1)

    meta_rows = jnp.broadcast_to(meta[None], (bsz, N_META, d))
    xp = jnp.concatenate([meta_rows, x_prompt, jnp.zeros((bsz, tp - t_real, d), F32)], axis=1)
    x_all = jnp.concatenate([xp.reshape(n_prompt, d), x_sample.reshape(s_dec, d),
                             jnp.zeros((sp - s_dec, d), F32)], axis=0)
    pos = jnp.concatenate([jnp.tile(jnp.arange(tp), bsz), jnp.full((sp,), past)])
    c128, s128 = _rope_tables(pos, A_HEAD)
    c64, s64 = _rope_tables(pos, IDX_DIM)

    w_in_k = _permute_cols(w_in[0]).astype(BF16)
    mu_k = _permute_cols(
        jnp.concatenate([mu_shift[0], jnp.zeros((w_in.shape[2] - SHIFT_DIM,), F32)])[None, :])
    head_of = jnp.arange(R_DIM) // R_HEAD
    e_mat = (head_of[:, None] == jnp.arange(R_HEADS)[None, :]).astype(F32)
    zpad = lambda a, rows_before, rows_total: jnp.concatenate(
        [jnp.zeros((rows_before, a.shape[1]), a.dtype), a,
         jnp.zeros((rows_total - rows_before - a.shape[0], a.shape[1]), a.dtype)], axis=0)
    pw = {
        "mu_x": mu_k[:, :3 * R_DIM], "mu_lo": mu_k[:, C_LORA:],
        "w0": row2(w0[0]), "a0": row2(a0[0]), "k_k": row2(k_k[0]), "k_a": row2(k_a[0]),
        "r_k": row2(r_k[0]), "gn_g": row2(gn_g[0]), "gn_b": row2(gn_b[0]),
        "w_b": zpad(w_b[0], 0, LANES).astype(BF16),
        "a_b": zpad(a_b[0], D_DECAY_LORA, LANES).astype(BF16),
        "g_b": zpad(g_b[0], 0, LORA_W - LANES).astype(BF16),
        "e": e_mat.astype(BF16), "et": e_mat.T.astype(BF16),
        "ln0_g": row2(ln0_g), "ln0_b": row2(ln0_b),
        "ln1_g": row2(ln1_g[0]), "ln1_b": row2(ln1_b[0]),
        "ln2_g": row2(ln2_g[0]), "ln2_b": row2(ln2_b[0]),
        "w_out": w_out[0].astype(BF16), "w_router_t": w_router[0].T,
        "ws_gate": ws_gate[0].astype(BF16), "ws_up": ws_up[0].astype(BF16),
        "ws_down": ws_down[0].astype(BF16),
    }

    p = ln_proj(x_all, pw["ln0_g"], pw["ln0_b"], w_in_k, tn=P_COLS // 3)
    q_r, iq_r, k_r, ik_r, iw = rope_all(p, c128, s128, c64, s64)

    pre_p = rwkv_pre(p, 0, n_prompt, None, pw, t_real, tp)
    shift_k = _permute_cols(jnp.concatenate(
        [state_shift[0], jnp.zeros((s_dec, w_in.shape[2] - SHIFT_DIM), F32)], axis=1))
    shift_k = jnp.concatenate([shift_k, jnp.zeros((sp - s_dec, P_COLS), F32)], axis=0)
    pre_s = rwkv_pre(p, n_prompt, sp, (shift_k[:, :3 * R_DIM], shift_k[:, C_LORA:]), pw, t_real, tp)
    r_p, ld_p, k_p, v_p, kk_p, b_p, g_p, bon_p = pre_p
    r_s, ld_s, k_s, v_s, kk_s, b_s, g_s, bon_s = pre_s
    y_p, wkv_p = rwkv_scan(r_p, ld_p, k_p, v_p, kk_p, b_p, bsz, tp)
    heads = lambda a: a[:s_dec].reshape(s_dec, R_HEADS, R_HEAD).transpose(1, 2, 0)
    y_hs, wkv_hs = rwkv_step(heads(r_s), heads(ld_s), heads(k_s), heads(v_s), heads(kk_s), heads(b_s),
                             state_wkv[0].transpose(1, 2, 3, 0))
    wkv_s = wkv_hs.transpose(3, 0, 1, 2)
    y_s = jnp.concatenate([y_hs.transpose(2, 0, 1).reshape(s_dec, R_DIM),
                           jnp.zeros((sp - s_dec, R_DIM), F32)], axis=0)

    n_sel_p = min(TOPK_KEYS, t_real // 4)
    a_p = dsa_prompt(q_r, iq_r, iw, ik_r, k_r, p, bsz, tp, n_sel_p)
    n_sel_s = min(TOPK_KEYS, (past + 1) // 4)
    pt_flat = page_table.reshape(-1).astype(I32)
    srow = slice(n_prompt, n_prompt + s_dec)
    sc_s = dsa_step_scores(pt_flat, iq_r[srow].reshape(s_dec, IDX_HEADS, IDX_DIM),
                           iw[srow].reshape(s_dec, IDX_HEADS, 1), ik_r[srow].reshape(s_dec, 1, IDX_DIM),
                           cache_idx_k[0].transpose(0, 2, 1).reshape(n_pool * IDX_DIM, page),
                           n_pages, page)
    sel_s = dsa_step_select(sc_s.reshape(s_dec, -1), n_sel_s, past).reshape(sc_s.shape)
    slot = jnp.arange(page)[:, None]
    expand = (jnp.arange(page * A_KV_HEADS)[None, :] // A_KV_HEADS == slot).astype(BF16)
    a_s = dsa_step_attn(pt_flat, q_r[srow].reshape(s_dec, A_HEADS, A_HEAD),
                        k_r[srow].reshape(s_dec, A_KV_HEADS, A_HEAD),
                        p[srow, C_VA:C_VA + A_KV_DIM].reshape(s_dec, A_KV_HEADS, A_HEAD),
                        sel_s, expand,
                        cache_k[0].reshape(n_pool * page * A_KV_HEADS, A_HEAD),
                        cache_v[0].reshape(n_pool * page * A_KV_HEADS, A_HEAD), n_pages, page)
    a_s = jnp.concatenate([a_s.reshape(s_dec, A_DIM), jnp.zeros((sp - s_dec, A_DIM), F32)], axis=0)

    h1, scores_t, xpk = mix_ln1_router(x_all, (y_p, bon_p, g_p, a_p), (y_s, bon_s, g_s, a_s), pw, alpha)
    eidx_t, gate_t, pos_t, counts = route(scores_t, e_bias[0].reshape(N_EXPERTS, 1))

    n_slots = n * TOP_K
    nb = (n_slots + N_EXPERTS * (EXPERT_TILE - 1)) // EXPERT_TILE + 1
    counts = counts[:, 0]
    padded = (counts + EXPERT_TILE - 1) // EXPERT_TILE * EXPERT_TILE
    seg_end = jnp.cumsum(padded).astype(I32)
    seg_start = seg_end - padded
    experts = jnp.arange(N_EXPERTS)
    start_of = jnp.sum(jnp.where(eidx_t[:, :, None] == experts, seg_start, 0), axis=-1)
    dest = (start_of + pos_t).T.astype(I32).reshape(-1)
    blk_row = jnp.arange(nb) * EXPERT_TILE
    blk_e = jnp.minimum(jnp.sum(seg_end[None, :] <= blk_row[:, None], axis=1),
                        N_EXPERTS - 1).astype(I32)
    n_used = (seg_end[-1] // EXPERT_TILE).astype(I32)
    run_start = jnp.concatenate([jnp.ones((1,), I32), (blk_e[1:] != blk_e[:-1]).astype(I32)])
    slot = ((jnp.cumsum(run_start) - 1) % 2).astype(I32)
    run_end = seg_end[blk_e] // EXPERT_TILE
    nxt_e = jnp.where(run_end < n_used, blk_e[jnp.minimum(run_end, nb - 1)], -1).astype(I32)
    n_used = n_used.reshape(1)
    xs, inv = moe_dispatch(dest, seg_end, counts.astype(I32), n_used, xpk, nb)
    ys = moe_experts(blk_e, nxt_e, slot, n_used, inv, xs, n_slots, w_gate[0], w_up[0], w_down[0])
    tc = Q_TILE
    h2 = moe_combine(h1, gate_t.T.reshape(n // tc, 1, tc * TOP_K), ys, pw, alpha)

    def prompt_rows(a):
        return a[:n_prompt].reshape(bsz, tp, -1)[:, :t_real]

    y_prompt = h2[:n_prompt].reshape(bsz, tp, d)[:, N_META:t_real]
    y_sample = h2[srow].reshape(s_dec, 1, d)
    k_prompt = prompt_rows(k_r).reshape(1, bsz, t_real, A_KV_HEADS, A_HEAD)
    v_prompt = prompt_rows(p[:, C_VA:C_VA + A_KV_DIM]).reshape(1, bsz, t_real, A_KV_HEADS, A_HEAD)
    ik_prompt = prompt_rows(ik_r)[None]
    last = jnp.arange(bsz) * tp + t_real - 1
    unperm = lambda rows: jnp.concatenate([rows[:, :3 * R_DIM],
                                           rows[:, C_LORA:C_LORA + SHIFT_DIM - 3 * R_DIM]], axis=1)
    shift_prompt = unperm(p[last])[None]
    k_sample = k_r[srow].reshape(1, s_dec, 1, A_KV_HEADS, A_HEAD)
    v_sample = p[srow, C_VA:C_VA + A_KV_DIM].reshape(1, s_dec, 1, A_KV_HEADS, A_HEAD)
    ik_sample = ik_r[srow].reshape(1, s_dec, 1, IDX_DIM)
    shift_sample = unperm(p[srow])[None]
    return (y_prompt, y_sample, k_prompt, v_prompt, ik_prompt, wkv_p[None], shift_prompt,
            k_sample, v_sample, ik_sample, wkv_s[None], shift_sample)
```

```python
import functools

import numpy as np
import jax
import jax.numpy as jnp
from jax import lax
from jax.experimental import pallas as pl
from jax.experimental.pallas import tpu as pltpu

F32 = jnp.float32
BF16 = jnp.bfloat16
I32 = jnp.int32
HIGHEST = lax.Precision.HIGHEST

N_META = 16
R_HEADS, R_HEAD = 16, 64
R_DIM = R_HEADS * R_HEAD
D_DECAY_LORA, D_AAA_LORA, D_GATE_LORA = 64, 64, 160
SHIFT_DIM = 3 * R_DIM + D_DECAY_LORA + D_AAA_LORA + D_GATE_LORA
GN_EPS = 64e-5
A_HEADS, A_KV_HEADS, A_HEAD = 8, 2, 128
A_DIM = A_HEADS * A_HEAD
A_KV_DIM = A_KV_HEADS * A_HEAD
IDX_HEADS, IDX_DIM = 16, 64
TOPK_KEYS = 256
ROPE_THETA = 10000.0
N_EXPERTS, N_EXPERT_GROUPS, TOPK_GROUPS, TOP_K = 64, 8, 4, 8
ROUTED_SCALE = 2.5
LN_EPS = 1e-5

LANES = 128
SUBLANES = 8
ROW_TILE = 256
Q_TILE = 128
CHUNK = 64
EXPERT_TILE = 256
VMEM_LIMIT = 56 * 1024 * 1024
NEG_BIG = -1e30
INT_MIN = -2 ** 31

C_R, C_K, C_V = 0, R_DIM, 2 * R_DIM
C_Q = 3 * R_DIM
C_IQ = C_Q + A_DIM
C_KA = C_IQ + IDX_HEADS * IDX_DIM
C_VA = C_KA + A_KV_DIM
C_IK = C_VA + A_KV_DIM
C_LORA = C_IK + LANES
LORA_W = 384
P_COLS = C_LORA + LORA_W


def _cparams(sem):
    return pltpu.CompilerParams(dimension_semantics=sem, vmem_limit_bytes=VMEM_LIMIT)


def _dot(a, b, precision=None):
    return jnp.dot(a, b, preferred_element_type=F32, precision=precision)


def _dot_nt(a, b, precision=None):
    return lax.dot_general(a, b, (((1,), (1,)), ((), ())), preferred_element_type=F32,
                           precision=precision)


def _dot_tn(a, b, precision=None):
    return lax.dot_general(a, b, (((0,), (0,)), ((), ())), preferred_element_type=F32,
                           precision=precision)


def _split_bf16(x):
    hi = x.astype(BF16)
    return hi, (x - hi.astype(F32)).astype(BF16)


def _dot_f32_by_bf16(a, b):
    hi, lo = _split_bf16(a)
    return _dot(hi, b) + _dot(lo, b)


def _head_sums(x, e, et):
    return _dot_f32_by_bf16(_dot_f32_by_bf16(x, e), et)


def _layer_norm(x, g, b):
    mu = jnp.mean(x, axis=-1, keepdims=True)
    xc = x - mu
    var = jnp.mean(xc * xc, axis=-1, keepdims=True)
    return xc * lax.rsqrt(var + LN_EPS) * g + b


def _sigmoid(z):
    return 1.0 / (1.0 + jnp.exp(-z))


def _ln_proj_kernel(x_ref, g_ref, b_ref, w_ref, o_ref):
    h = _layer_norm(x_ref[...], g_ref[...], b_ref[...])
    o_ref[...] = _dot(h.astype(BF16), w_ref[...])


def ln_proj(x, g, b, w_bf16, tn):
    n, d = x.shape
    cols = w_bf16.shape[1]
    return pl.pallas_call(
        _ln_proj_kernel,
        grid=(cols // tn, n // ROW_TILE),
        in_specs=[
            pl.BlockSpec((ROW_TILE, d), lambda j, i: (i, 0)),
            pl.BlockSpec((1, d), lambda j, i: (0, 0)),
            pl.BlockSpec((1, d), lambda j, i: (0, 0)),
            pl.BlockSpec((d, tn), lambda j, i: (0, j)),
        ],
        out_specs=pl.BlockSpec((ROW_TILE, tn), lambda j, i: (i, j)),
        out_shape=jax.ShapeDtypeStruct((n, cols), F32),
        compiler_params=_cparams(("parallel", "parallel")),
        name="ln_proj",
    )(x, g, b, w_bf16)


def _rot_half(x, head):
    w = x.shape[-1]
    half = head // 2
    lane = lax.broadcasted_iota(I32, x.shape, 1)
    left = pltpu.roll(x, w - half, axis=1)
    right = pltpu.roll(x, half, axis=1)
    return jnp.where((lane % head) < half, left, right)


def _rope_kernel(q_ref, iq_ref, ka_ref, ikw_ref, c128_ref, s128_ref, c64_ref, s64_ref,
                 qo_ref, iqo_ref, ko_ref, iko_ref, iwo_ref):
    c128, s128 = c128_ref[...], s128_ref[...]
    c64, s64 = c64_ref[...], s64_ref[...]

    def rope(x, head, c, s):
        rep = x.shape[-1] // LANES
        if rep > 1:
            c = jnp.concatenate([c] * rep, axis=1)
            s = jnp.concatenate([s] * rep, axis=1)
        return x * c + _rot_half(x, head) * s

    q = rope(q_ref[...], A_HEAD, c128, s128)
    qo_ref[...] = (q * (A_HEAD ** -0.5)).astype(BF16)
    iqo_ref[...] = rope(iq_ref[...], IDX_DIM, c64, s64).astype(BF16)
    ko_ref[...] = rope(ka_ref[...], A_HEAD, c128, s128)
    ikw = ikw_ref[...]
    ik = rope(ikw, IDX_DIM, c64, s64)
    iko_ref[...] = ik[:, :IDX_DIM]
    iwo_ref[...] = ikw[:, IDX_DIM:IDX_DIM + IDX_HEADS]


def rope_all(p, c128, s128, c64, s64):
    n = p.shape[0]
    tm = ROW_TILE
    row = lambda w, blk: pl.BlockSpec((tm, w), lambda i: (i, blk))
    return pl.pallas_call(
        _rope_kernel,
        grid=(n // tm,),
        in_specs=[row(A_DIM, C_Q // A_DIM), row(A_DIM, C_IQ // A_DIM),
                  row(A_KV_DIM, C_KA // A_KV_DIM), row(LANES, C_IK // LANES),
                  row(LANES, 0), row(LANES, 0), row(LANES, 0), row(LANES, 0)],
        out_specs=[row(A_DIM, 0), row(A_DIM, 0), row(A_KV_DIM, 0),
                   row(IDX_DIM, 0), row(IDX_HEADS, 0)],
        out_shape=[jax.ShapeDtypeStruct((n, A_DIM), BF16),
                   jax.ShapeDtypeStruct((n, IDX_HEADS * IDX_DIM), BF16),
                   jax.ShapeDtypeStruct((n, A_KV_DIM), F32),
                   jax.ShapeDtypeStruct((n, IDX_DIM), F32),
                   jax.ShapeDtypeStruct((n, IDX_HEADS), F32)],
        compiler_params=_cparams(("parallel",)),
        name="rope",
    )(p, p, p, p, c128, s128, c64, s64)


def _rwkv_pre_kernel(t_real, tp, from_rows, *refs):
    (x_ref, lo_ref, px_ref, plo_ref, mu_ref, mulo_ref, w0_ref, wb_ref, a0_ref, ab_ref,
     gb_ref, kk_ref, ka_ref, rk_ref, e_ref, et_ref,
     r_o, ld_o, k_o, v_o, kk_o, b_o, g_o, bon_o) = refs
    x = x_ref[...]
    lo = lo_ref[...]
    tm = x.shape[0]
    if from_rows:
        i = pl.program_id(0)
        row = lax.broadcasted_iota(I32, (tm, 1), 0)
        t = (i * tm + row) % tp
        first = row == 0
        sx = jnp.where(first, px_ref[SUBLANES - 1:SUBLANES, :], pltpu.roll(x, 1, axis=0))
        slo = jnp.where(first, plo_ref[SUBLANES - 1:SUBLANES, :], pltpu.roll(lo, 1, axis=0))
        sx = jnp.where(t == 0, 0.0, sx)
        slo = jnp.where(t == 0, 0.0, slo)
        live = t < t_real
    else:
        sx = px_ref[...]
        slo = plo_ref[...]
        live = None
    xx = x + (sx - x) * mu_ref[...]
    xlo = lo + (slo - lo) * mulo_ref[...]
    r = xx[:, C_R:C_R + R_DIM]
    k = xx[:, C_K:C_K + R_DIM]
    v = xx[:, C_V:C_V + R_DIM]
    wa = xlo[:, :LANES]
    xg = xlo[:, LANES:]
    z = w0_ref[...] + _dot(jnp.tanh(wa).astype(BF16), wb_ref[...])
    nz = -z
    softplus = jnp.maximum(nz, 0.0) + jnp.log(1.0 + jnp.exp(-jnp.abs(nz)))
    logd = -jnp.exp(-softplus - 0.5)
    a = _sigmoid(a0_ref[...] + _dot(wa.astype(BF16), ab_ref[...]))
    g = _dot(_sigmoid(xg).astype(BF16), gb_ref[...])
    e, et = e_ref[...], et_ref[...]
    kkr = k * kk_ref[...]
    ss = _head_sums(kkr * kkr, e, et)
    kk = kkr / jnp.maximum(jnp.sqrt(ss), 1e-12)
    k2 = k * (1.0 + (a - 1.0) * ka_ref[...])
    bonus = _head_sums(r * k2 * rk_ref[...], e, et) * v
    b = kk * a
    if live is not None:
        zero = lambda y: jnp.where(live, y, 0.0)
        logd, k2s, vs, kk, b = zero(logd), zero(k2), zero(v), zero(kk), zero(b)
    else:
        k2s, vs = k2, v
    r_o[...] = r
    ld_o[...] = logd
    k_o[...] = k2s
    v_o[...] = vs
    kk_o[...] = kk
    b_o[...] = b
    g_o[...] = g
    bon_o[...] = bonus


def rwkv_pre(p, row0, nrows, prev, pw, t_real, tp):
    tm = min(ROW_TILE, nrows)
    blk0 = row0 // tm
    from_rows = prev is None
    xw = 3 * R_DIM
    cur_x = pl.BlockSpec((tm, xw), lambda i: (blk0 + i, 0))
    cur_lo = pl.BlockSpec((tm, LORA_W), lambda i: (blk0 + i, C_LORA // LORA_W))
    if from_rows:
        r8 = tm // SUBLANES
        prev_x = pl.BlockSpec((SUBLANES, xw), lambda i: (jnp.maximum((blk0 + i) * r8 - 1, 0), 0))
        prev_lo = pl.BlockSpec((SUBLANES, LORA_W),
                               lambda i: (jnp.maximum((blk0 + i) * r8 - 1, 0), C_LORA // LORA_W))
        prev_args = (p, p)
    else:
        prev_x = pl.BlockSpec((tm, xw), lambda i: (i, 0))
        prev_lo = pl.BlockSpec((tm, LORA_W), lambda i: (i, 0))
        prev_args = prev
    full = lambda a: pl.BlockSpec(a.shape, lambda i: (0,) * a.ndim)
    params = (pw["mu_x"], pw["mu_lo"], pw["w0"], pw["w_b"], pw["a0"], pw["a_b"], pw["g_b"],
              pw["k_k"], pw["k_a"], pw["r_k"], pw["e"], pw["et"])
    out = pl.BlockSpec((tm, R_DIM), lambda i: (i, 0))
    return pl.pallas_call(
        functools.partial(_rwkv_pre_kernel, t_real, tp, from_rows),
        grid=(nrows // tm,),
        in_specs=[cur_x, cur_lo, prev_x, prev_lo] + [full(a) for a in params],
        out_specs=[out] * 8,
        out_shape=[jax.ShapeDtypeStruct((nrows, R_DIM), F32)] * 8,
        compiler_params=_cparams(("parallel",)),
        name="rwkv_pre_rows" if from_rows else "rwkv_pre_step",
    )(p, p, *prev_args, *params)


def _rwkv_scan_kernel(r_ref, ld_ref, k_ref, v_ref, kk_ref, b_ref, y_ref, s_ref, ss_scr):
    c = pl.program_id(1)

    @pl.when(c == 0)
    def _():
        ss_scr[...] = jnp.zeros_like(ss_scr)

    n = CHUNK
    n2 = 2 * n
    pairs = R_HEADS // 2
    ld_all = ld_ref[...]
    ri = lax.broadcasted_iota(I32, (n, n), 0)
    ci = lax.broadcasted_iota(I32, (n, n), 1)
    cum_all = _dot((ci <= ri).astype(F32), ld_all, HIGHEST)
    head0 = lax.broadcasted_iota(I32, (n, LANES), 1) < R_HEAD
    r4 = lax.broadcasted_iota(I32, (2 * n2, 2 * n2), 0)
    c4 = lax.broadcasted_iota(I32, (2 * n2, 2 * n2), 1)
    tri = (c4 % n) < (r4 % n) + jnp.where(r4 < n2, 0, 1)
    re = lax.broadcasted_iota(I32, (n2, n2), 0)
    ce = lax.broadcasted_iota(I32, (n2, n2), 1)
    eye = (re == ce).astype(F32)

    def stack(x):
        return jnp.concatenate([jnp.where(head0, x, 0.0), jnp.where(head0, 0.0, x)], axis=0)

    ar, bk, v2, ss, e_last = [], [], [], [], []
    for p in range(pairs):
        sl = slice(p * LANES, (p + 1) * LANES)
        cum, ld = cum_all[:, sl], ld_all[:, sl]
        e_pos = jnp.exp(cum)
        e_neg = jnp.exp(-cum)
        at = -kk_ref[:, sl] * jnp.exp(cum - ld)
        ar.append(jnp.concatenate([stack(at), stack(r_ref[:, sl] * e_pos)], axis=0).astype(BF16))
        bk.append(jnp.concatenate([stack(b_ref[:, sl] * e_neg), stack(k_ref[:, sl] * e_neg)],
                                  axis=0).astype(BF16))
        v2.append(stack(v_ref[:, sl]).astype(BF16))
        ss.append(ss_scr[p])
        e_last.append(e_pos[n - 1:n, :])
    xy0 = [_dot_nt(ar[p], ss[p].astype(BF16)) for p in range(pairs)]
    sc = [jnp.where(tri, _dot_nt(ar[p], bk[p]), 0.0) for p in range(pairs)]
    lp = [s[:n2, :n2] for s in sc]
    t = [eye + l for l in lp]
    m = 1
    while 2 * m < n:
        lpb = [l.astype(BF16) for l in lp]
        lp = [_dot(l, l) for l in lpb]
        t = [t[p] + _dot(t[p].astype(BF16), lp[p].astype(BF16)) for p in range(pairs)]
        m *= 2
    w = [xy0[p][:n2] + _dot(sc[p][:n2, n2:].astype(BF16), v2[p]) for p in range(pairs)]
    u = [_dot(t[p].astype(BF16), w[p].astype(BF16)) for p in range(pairs)]
    uv = [jnp.concatenate([u[p].astype(BF16), v2[p]], axis=0) for p in range(pairs)]
    y = [xy0[p][n2:] + _dot(sc[p][n2:].astype(BF16), uv[p]) for p in range(pairs)]
    upd = [_dot_tn(uv[p], bk[p]) for p in range(pairs)]
    for p in range(pairs):
        y_ref[:, p * LANES:(p + 1) * LANES] = y[p][:n] + y[p][n:]
        ss_scr[p] = (ss[p] + upd[p]) * e_last[p]

    @pl.when(c == pl.num_programs(1) - 1)
    def _():
        for p in range(R_HEADS // 2):
            ss = ss_scr[p]
            s_ref[0, 2 * p] = ss[:R_HEAD, :R_HEAD]
            s_ref[0, 2 * p + 1] = ss[R_HEAD:, R_HEAD:]


def rwkv_scan(r, ld, k, v, kk, b, batch, tp):
    nchunk = tp // CHUNK
    blk = pl.BlockSpec((CHUNK, R_DIM), lambda bi, c: (bi * nchunk + c, 0))
    return pl.pallas_call(
        _rwkv_scan_kernel,
        grid=(batch, nchunk),
        in_specs=[blk] * 6,
        out_specs=[blk, pl.BlockSpec((1, R_HEADS, R_HEAD, R_HEAD), lambda bi, c: (bi, 0, 0, 0))],
        out_shape=[jax.ShapeDtypeStruct((batch * tp, R_DIM), F32),
                   jax.ShapeDtypeStruct((batch, R_HEADS, R_HEAD, R_HEAD), F32)],
        scratch_shapes=[pltpu.VMEM((R_HEADS // 2, LANES, LANES), F32)],
        compiler_params=_cparams(("parallel", "arbitrary")),
        name="rwkv_scan",
    )(r, ld, k, v, kk, b)


STEP_ROWS = 16


def _rwkv_step_kernel(r_ref, ld_ref, k_ref, v_ref, kk_ref, b_ref, s_ref, y_ref, so_ref):
    r, k = r_ref[0], k_ref[0]
    dec = jnp.exp(ld_ref[0])
    na = -kk_ref[0]
    b = b_ref[0]
    v = v_ref[0]
    ys = []
    for v0 in range(0, R_HEAD, STEP_ROWS):
        rows = range(v0, v0 + STEP_ROWS)
        s = [s_ref[0, vi] for vi in rows]
        sa = [jnp.sum(x * na, axis=0, keepdims=True) for x in s]
        s_new = [x * dec + a * b + v[vi:vi + 1, :] * k for x, a, vi in zip(s, sa, rows)]
        ys += [jnp.sum(x * r, axis=0, keepdims=True) for x in s_new]
        for x, vi in zip(s_new, rows):
            so_ref[0, vi] = x
    y_ref[0] = jnp.concatenate(ys, axis=0)


def rwkv_step(r, ld, k, v, kk, b, state):
    s = state.shape[-1]
    vec = pl.BlockSpec((1, R_HEAD, s), lambda h: (h, 0, 0))
    st = pl.BlockSpec((1, R_HEAD, R_HEAD, s), lambda h: (h, 0, 0, 0))
    return pl.pallas_call(
        _rwkv_step_kernel,
        grid=(R_HEADS,),
        in_specs=[vec] * 6 + [st],
        out_specs=[vec, st],
        out_shape=[jax.ShapeDtypeStruct((R_HEADS, R_HEAD, s), F32),
                   jax.ShapeDtypeStruct(state.shape, F32)],
        compiler_params=_cparams(("parallel",)),
        name="rwkv_step",
    )(r, ld, k, v, kk, b, state)


def _select_topk(score, allowed, n_sel):
    bits = lax.bitcast_convert_type(score, I32)
    key = jnp.where(bits < 0, bits ^ jnp.int32(0x7FFFFFFF), bits)
    key = jnp.where(allowed, key, jnp.int32(INT_MIN))
    m, w = score.shape
    one, zero = jnp.ones((), BF16), jnp.zeros((), BF16)

    def byte(shift):
        if shift == 24:
            d = lax.shift_right_arithmetic(key, 24) + 128
        else:
            d = lax.shift_right_logical(key, shift) & 255
        return d.astype(F32).astype(BF16)

    def count(flags):
        acc = flags[:, :LANES]
        for c in range(LANES, w, LANES):
            acc = acc + flags[:, c:c + LANES]
        return jnp.sum(acc.astype(F32), axis=1, keepdims=True)

    need = jnp.full((m, 1), float(n_sel), F32)
    x = byte(24)
    tau = jnp.zeros((m, 1), I32)
    for shift in (24, 16, 8, 0):
        def body(it, t, x=x, need=need):
            step = lax.shift_left(jnp.int32(1), jnp.int32(6) - 2 * it).astype(F32)
            cnts = [count(jnp.where(x >= (t + mult * step).astype(BF16), one, zero))
                    for mult in (1.0, 2.0, 3.0)]
            hits = sum(jnp.where(cnt >= need, 1.0, 0.0) for cnt in cnts)
            return t + hits * step

        t = lax.fori_loop(0, 4, body, jnp.zeros((m, 1), F32))
        tb = t.astype(BF16)
        need = need - count(jnp.where(x > tb, one, zero))
        digit = t.astype(I32) - (128 if shift == 24 else 0)
        tau = tau | lax.shift_left(digit, shift)
        if shift:
            x = jnp.where(x == tb, byte(shift - 8), -one)
    return jnp.logical_and(key >= tau, allowed)


KEY_TILE = 256
Q_TILES_PER_EXTENT = 3


def _dsa_prompt_block(n_sel, tk, i, q_ref, iq_ref, iw_ref, ik_ref, k_ref, v_ref, o_ref, sc_ref):
    tq = q_ref.shape[0]
    iw = iw_ref[...] * ((IDX_HEADS * IDX_DIM) ** -0.5)
    iq = iq_ref[...]
    iq_h = [iq[:, h * IDX_DIM:(h + 1) * IDX_DIM] for h in range(IDX_HEADS)]
    iw_h = [iw[:, h:h + 1] for h in range(IDX_HEADS)]
    for c0 in range(0, tk, KEY_TILE):
        c1 = min(c0 + KEY_TILE, tk)
        ikb = ik_ref[c0:c1, :].astype(BF16)
        acc = jnp.maximum(_dot_nt(iq_h[0], ikb), 0.0) * iw_h[0]
        for h in range(1, IDX_HEADS):
            acc = acc + jnp.maximum(_dot_nt(iq_h[h], ikb), 0.0) * iw_h[h]
        sc_ref[:, c0:c1] = acc
    qpos = i * tq + lax.broadcasted_iota(I32, (tq, 1), 0)
    kpos = lax.broadcasted_iota(I32, (1, tk), 1)
    sel = _select_topk(sc_ref[:, :tk], kpos <= qpos, n_sel)
    bias = jnp.where(sel, 0.0, NEG_BIG)
    q = q_ref[...]
    rep = A_HEADS // A_KV_HEADS
    for g in range(A_KV_HEADS):
        kg = k_ref[:tk, g * A_HEAD:(g + 1) * A_HEAD].astype(BF16)
        vg = v_ref[:tk, g * A_HEAD:(g + 1) * A_HEAD].astype(BF16)
        for rr in range(rep):
            h = g * rep + rr
            s = _dot_nt(q[:, h * A_HEAD:(h + 1) * A_HEAD], kg) + bias
            m = jnp.max(s, axis=1, keepdims=True)
            p = jnp.exp(s - m)
            l = jnp.sum(p, axis=1, keepdims=True)
            o_ref[:, h * A_HEAD:(h + 1) * A_HEAD] = _dot(p.astype(BF16), vg) / l


def _dsa_prompt_kernel(n_sel, *refs):
    i = pl.program_id(1)
    tq = refs[0].shape[0]
    tp = refs[4].shape[0]
    nq = tp // tq
    for lo in range(0, nq, Q_TILES_PER_EXTENT):
        hi = min(lo + Q_TILES_PER_EXTENT, nq)

        @pl.when(jnp.logical_and(i >= lo, i < hi))
        def _(hi=hi):
            _dsa_prompt_block(n_sel, hi * tq, i, *refs)


def dsa_prompt(q, iq, iw, ik, k, p, batch, tp, n_sel):
    nq = tp // Q_TILE
    qrow = lambda w: pl.BlockSpec((Q_TILE, w), lambda b, i: (b * nq + i, 0))
    keys = lambda w, blk: pl.BlockSpec((tp, w), lambda b, i: (b, blk))
    return pl.pallas_call(
        functools.partial(_dsa_prompt_kernel, n_sel),
        grid=(batch, nq),
        in_specs=[qrow(A_DIM), qrow(IDX_HEADS * IDX_DIM), qrow(IDX_HEADS),
                  keys(IDX_DIM, 0), keys(A_KV_DIM, 0), keys(A_KV_DIM, C_VA // A_KV_DIM)],
        out_specs=qrow(A_DIM),
        out_shape=jax.ShapeDtypeStruct((batch * tp, A_DIM), F32),
        scratch_shapes=[pltpu.VMEM((Q_TILE, tp), F32)],
        compiler_params=_cparams(("parallel", "parallel")),
        name="dsa_prompt",
    )(q, iq, iw, ik, k, p)


def _dsa_step_score_kernel(n_pages, page, pt_ref, iq_ref, iw_ref, ikn_ref, *refs):
    pages = refs[:n_pages]
    o_ref = refs[n_pages]
    iq = iq_ref[0]
    iw = iw_ref[0] * ((IDX_HEADS * IDX_DIM) ** -0.5)
    for j in range(n_pages):
        d = _dot(iq, pages[j][...].astype(BF16))
        o_ref[0, :, j * page:(j + 1) * page] = jnp.sum(jnp.maximum(d, 0.0) * iw, axis=0,
                                                       keepdims=True)
    dn = jnp.sum(iq.astype(F32) * ikn_ref[0], axis=1, keepdims=True)
    sn = jnp.sum(jnp.maximum(dn, 0.0) * iw, axis=0, keepdims=True)
    lane = lax.broadcasted_iota(I32, (1, LANES), 1)
    o_ref[0, :, n_pages * page:] = jnp.where(lane == 0, sn, 0.0)


def dsa_step_scores(pt_flat, iq, iw, ik_new, cik2d, n_pages, page):
    s = iq.shape[0]
    kw = n_pages * page + LANES
    page_spec = lambda j: pl.BlockSpec((IDX_DIM, page), lambda i, pt: (pt[i * n_pages + j], 0))
    grid_spec = pltpu.PrefetchScalarGridSpec(
        num_scalar_prefetch=1,
        grid=(s,),
        in_specs=[pl.BlockSpec((1, IDX_HEADS, IDX_DIM), lambda i, pt: (i, 0, 0)),
                  pl.BlockSpec((1, IDX_HEADS, 1), lambda i, pt: (i, 0, 0)),
                  pl.BlockSpec((1, 1, IDX_DIM), lambda i, pt: (i, 0, 0))]
                 + [page_spec(j) for j in range(n_pages)],
        out_specs=pl.BlockSpec((1, 1, kw), lambda i, pt: (i, 0, 0)),
    )
    return pl.pallas_call(
        functools.partial(_dsa_step_score_kernel, n_pages, page),
        grid_spec=grid_spec,
        out_shape=jax.ShapeDtypeStruct((s, 1, kw), F32),
        compiler_params=_cparams(("arbitrary",)),
        name="dsa_step_scores",
    )(pt_flat, iq, iw, ik_new, *([cik2d] * n_pages))


def _dsa_step_select_kernel(n_sel, past, sc_ref, o_ref):
    sc = sc_ref[...]
    kpos = lax.broadcasted_iota(I32, sc.shape, 1)
    sel = _select_topk(sc, kpos <= past, n_sel)
    o_ref[...] = sel.astype(F32)


def dsa_step_select(sc, n_sel, past):
    return pl.pallas_call(
        functools.partial(_dsa_step_select_kernel, n_sel, past),
        out_shape=jax.ShapeDtypeStruct(sc.shape, F32),
        compiler_params=pltpu.CompilerParams(vmem_limit_bytes=VMEM_LIMIT),
        name="dsa_step_select",
    )(sc)


def _dsa_step_attn_kernel(n_pages, page, pt_ref, q_ref, kn_ref, vn_ref, sel_ref, ex_ref, *refs):
    kp = refs[:n_pages]
    vp = refs[n_pages:2 * n_pages]
    o_ref = refs[2 * n_pages]
    q = q_ref[0]
    rep = A_HEADS // A_KV_HEADS
    w2 = page * A_KV_HEADS
    hrow = lax.broadcasted_iota(I32, (A_HEADS, w2), 0)
    col = lax.broadcasted_iota(I32, (A_HEADS, w2), 1)
    own = (col % A_KV_HEADS) == (hrow // rep)
    ex = ex_ref[...]
    logits = []
    for j in range(n_pages):
        s = _dot_nt(q, kp[j][...].astype(BF16))
        selj = _dot(sel_ref[0, :, j * page:(j + 1) * page].astype(BF16), ex)
        logits.append(jnp.where(jnp.logical_and(selj > 0.5, own), s, NEG_BIG))
    h8 = lax.broadcasted_iota(I32, (A_HEADS, A_HEAD), 0)
    kn = jnp.where(h8 < rep, kn_ref[0, 0:1, :], kn_ref[0, 1:2, :])
    vn = jnp.where(h8 < rep, vn_ref[0, 0:1, :], vn_ref[0, 1:2, :])
    sn = jnp.sum(q.astype(F32) * kn, axis=1, keepdims=True)
    seln = sel_ref[0, :, n_pages * page:n_pages * page + 1]
    sn = jnp.where(seln > 0.5, sn, NEG_BIG)
    m = sn
    for s in logits:
        m = jnp.maximum(m, jnp.max(s, axis=1, keepdims=True))
    pn = jnp.exp(sn - m)
    l = pn
    acc = pn * vn
    for j in range(n_pages):
        p = jnp.exp(logits[j] - m)
        l = l + jnp.sum(p, axis=1, keepdims=True)
        acc = acc + _dot(p.astype(BF16), vp[j][...].astype(BF16))
    o_ref[0] = acc / l


def dsa_step_attn(pt_flat, q, k_new, v_new, sel, expand, ck2d, cv2d, n_pages, page):
    s = q.shape[0]
    kw = sel.shape[-1]
    w2 = page * A_KV_HEADS
    page_spec = lambda j: pl.BlockSpec((w2, A_HEAD), lambda i, pt: (pt[i * n_pages + j], 0))
    grid_spec = pltpu.PrefetchScalarGridSpec(
        num_scalar_prefetch=1,
        grid=(s,),
        in_specs=[pl.BlockSpec((1, A_HEADS, A_HEAD), lambda i, pt: (i, 0, 0)),
                  pl.BlockSpec((1, A_KV_HEADS, A_HEAD), lambda i, pt: (i, 0, 0)),
                  pl.BlockSpec((1, A_KV_HEADS, A_HEAD), lambda i, pt: (i, 0, 0)),
                  pl.BlockSpec((1, 1, kw), lambda i, pt: (i, 0, 0)),
                  pl.BlockSpec((page, w2), lambda i, pt: (0, 0))]
                 + [page_spec(j) for j in range(n_pages)] * 2,
        out_specs=pl.BlockSpec((1, A_HEADS, A_HEAD), lambda i, pt: (i, 0, 0)),
    )
    return pl.pallas_call(
        functools.partial(_dsa_step_attn_kernel, n_pages, page),
        grid_spec=grid_spec,
        out_shape=jax.ShapeDtypeStruct((s, A_HEADS, A_HEAD), F32),
        compiler_params=_cparams(("arbitrary",)),
        name="dsa_step_attn",
    )(pt_flat, q, k_new, v_new, sel, expand, *([ck2d] * n_pages), *([cv2d] * n_pages))


def _pack_bf16_pairs(x):
    w = x.shape[1] // 2
    hi = lax.bitcast_convert_type(x[:, :w].astype(BF16).astype(F32), I32)
    lo = lax.bitcast_convert_type(x[:, w:].astype(BF16).astype(F32), I32)
    return hi | lax.shift_right_logical(lo, 16)


def _unpack_bf16_pairs(p):
    hi = lax.bitcast_convert_type(p & jnp.int32(-65536), F32)
    lo = lax.bitcast_convert_type(lax.shift_left(p, 16), F32)
    return hi, lo


def _mix_kernel(alpha, prompt_blocks, x_ref, yp_ref, bonp_ref, gp_ref, ap_ref, ys_ref, bons_ref, gs_ref,
                as_ref, e_ref, et_ref, gng_ref, gnb_ref, l0g_ref, l0b_ref, wo_ref, l1g_ref, l1b_ref,
                wr_ref, h_ref, sc_ref, pk_ref):
    is_prompt = pl.program_id(0) < prompt_blocks
    pick = lambda p_ref, s_ref: jnp.where(is_prompt, p_ref[...], s_ref[...])
    e, et = e_ref[...], et_ref[...]
    y = pick(yp_ref, ys_ref)
    inv = 1.0 / R_HEAD
    mu = _head_sums(y, e, et) * inv
    d = y - mu
    var = _head_sums(d * d, e, et) * inv
    yn = d * lax.rsqrt(var + GN_EPS) * gng_ref[...] + gnb_ref[...]
    r_out = (yn + pick(bonp_ref, bons_ref)) * pick(gp_ref, gs_ref)
    mix = (_dot(r_out.astype(BF16), wo_ref[:R_DIM, :])
           + _dot(pick(ap_ref, as_ref).astype(BF16), wo_ref[R_DIM:, :]))
    h0 = _layer_norm(x_ref[...], l0g_ref[...], l0b_ref[...])
    h1 = _layer_norm(alpha * h0 + mix, l1g_ref[...], l1b_ref[...])
    h_ref[...] = h1
    w_hi, w_lo = _split_bf16(wr_ref[...])
    h_hi, h_lo = _split_bf16(h1)
    sc_ref[...] = _sigmoid(_dot_nt(w_hi, h_hi) + _dot_nt(w_hi, h_lo) + _dot_nt(w_lo, h_hi))
    pk_ref[...] = _pack_bf16_pairs(h1)


def mix_ln1_router(x, prompt_parts, step_parts, pw, alpha):
    n, d = x.shape
    tm = ROW_TILE
    pb = prompt_parts[0].shape[0] // tm
    row = lambda w: pl.BlockSpec((tm, w), lambda i: (i, 0))
    head = lambda w: pl.BlockSpec((tm, w), lambda i: (jnp.minimum(i, pb - 1), 0))
    tail = lambda w: pl.BlockSpec((tm, w), lambda i: (jnp.maximum(i - pb, 0), 0))
    full = lambda a: pl.BlockSpec(a.shape, lambda i: (0,) * a.ndim)
    params = (pw["e"], pw["et"], pw["gn_g"], pw["gn_b"], pw["ln0_g"], pw["ln0_b"], pw["w_out"],
              pw["ln1_g"], pw["ln1_b"], pw["w_router_t"])
    widths = (R_DIM, R_DIM, R_DIM, A_DIM)
    return pl.pallas_call(
        functools.partial(_mix_kernel, alpha, pb),
        grid=(n // tm,),
        in_specs=[row(d)] + [head(w) for w in widths] + [tail(w) for w in widths]
                 + [full(a) for a in params],
        out_specs=[row(d), pl.BlockSpec((N_EXPERTS, tm), lambda i: (0, i)), row(d // 2)],
        out_shape=[jax.ShapeDtypeStruct((n, d), F32),
                   jax.ShapeDtypeStruct((N_EXPERTS, n), F32),
                   jax.ShapeDtypeStruct((n, d // 2), I32)],
        compiler_params=_cparams(("parallel",)),
        name="mix_ln1_router",
    )(x, *prompt_parts, *step_parts, *params)


def _route_kernel(sc_ref, bias_ref, idx_ref, gate_ref, pos_ref, cnt_ref, cnt_scr):
    scores = sc_ref[...]
    biased = scores + bias_ref[...]
    tn = scores.shape[1]
    per = N_EXPERTS // N_EXPERT_GROUPS
    sub = lax.broadcasted_iota(I32, (per, tn), 0)
    grp_rows = []
    for g in range(N_EXPERT_GROUPS):
        xg = biased[g * per:(g + 1) * per, :]
        m1 = jnp.max(xg, axis=0, keepdims=True)
        first = jnp.min(jnp.where(xg == m1, sub, per), axis=0, keepdims=True)
        m2 = jnp.max(jnp.where(sub == first, -jnp.inf, xg), axis=0, keepdims=True)
        grp_rows.append(m1 + m2)
    grp = jnp.concatenate(grp_rows, axis=0)
    gi = lax.broadcasted_iota(I32, (N_EXPERT_GROUPS, tn), 0)
    gsel = jnp.zeros((N_EXPERT_GROUPS, tn), jnp.bool_)
    for _ in range(TOPK_GROUPS):
        m = jnp.max(grp, axis=0, keepdims=True)
        first = jnp.min(jnp.where(grp == m, gi, N_EXPERT_GROUPS), axis=0, keepdims=True)
        hit = gi == first
        gsel = jnp.logical_or(gsel, hit)
        grp = jnp.where(hit, -jnp.inf, grp)
    ei = lax.broadcasted_iota(I32, (N_EXPERTS, tn), 0)
    emask = jnp.concatenate(
        [jnp.broadcast_to(gsel[g:g + 1, :], (per, tn)) for g in range(N_EXPERT_GROUPS)], axis=0)
    cand = jnp.where(emask, biased, -jnp.inf)
    idxs, gates, hits = [], [], []
    for _ in range(TOP_K):
        m = jnp.max(cand, axis=0, keepdims=True)
        first = jnp.min(jnp.where(cand == m, ei, N_EXPERTS), axis=0, keepdims=True)
        hit = ei == first
        idxs.append(first)
        hits.append(hit)
        gates.append(jnp.sum(jnp.where(hit, scores, 0.0), axis=0, keepdims=True))
        cand = jnp.where(hit, -jnp.inf, cand)
    gate = jnp.concatenate(gates, axis=0)
    gate = gate / jnp.sum(gate, axis=0, keepdims=True) * ROUTED_SCALE
    idx_ref[...] = jnp.concatenate(idxs, axis=0)
    gate_ref[...] = gate
    chosen = hits[0]
    for hit in hits[1:]:
        chosen = jnp.logical_or(chosen, hit)
    onehot = jnp.where(chosen, 1.0, 0.0)
    ta = lax.broadcasted_iota(I32, (tn, tn), 0)
    tb = lax.broadcasted_iota(I32, (tn, tn), 1)
    prefix = _dot(onehot.astype(BF16), (ta < tb).astype(BF16))

    @pl.when(pl.program_id(0) == 0)
    def _():
        cnt_scr[...] = jnp.zeros_like(cnt_scr)

    rank = prefix + cnt_scr[:, 0:1]
    pos_ref[...] = jnp.concatenate(
        [jnp.sum(jnp.where(hit, rank, 0.0), axis=0, keepdims=True) for hit in hits], axis=0).astype(I32)
    cnt_scr[...] = cnt_scr[...] + jnp.sum(onehot, axis=1, keepdims=True)
    cnt_ref[...] = cnt_scr[...].astype(I32)


def route(scores_t, e_bias):
    n = scores_t.shape[1]
    tn = ROW_TILE
    tok = pl.BlockSpec((TOP_K, tn), lambda i: (0, i))
    return pl.pallas_call(
        _route_kernel,
        grid=(n // tn,),
        in_specs=[pl.BlockSpec((N_EXPERTS, tn), lambda i: (0, i)),
                  pl.BlockSpec((N_EXPERTS, 1), lambda i: (0, 0))],
        out_specs=[tok, tok, tok, pl.BlockSpec((N_EXPERTS, LANES), lambda i: (0, 0))],
        out_shape=[jax.ShapeDtypeStruct((TOP_K, n), I32), jax.ShapeDtypeStruct((TOP_K, n), F32),
                   jax.ShapeDtypeStruct((TOP_K, n), I32),
                   jax.ShapeDtypeStruct((N_EXPERTS, LANES), I32)],
        scratch_shapes=[pltpu.VMEM((N_EXPERTS, LANES), F32)],
        compiler_params=_cparams(("arbitrary",)),
        name="route",
    )(scores_t, e_bias)


def _dispatch_kernel(nb, dest_ref, segend_ref, cnt_ref, nu_ref, x_ref, inv0_ref, o_ref, invo_ref,
                     inv_ref, zbuf, sem, zsem, isem):
    tm = x_ref.shape[0]
    row0 = pl.program_id(0) * tm

    @pl.when(pl.program_id(0) == 0)
    def _():
        c = pltpu.make_async_copy(inv0_ref, inv_ref, isem)
        c.start()
        c.wait()

    def fill(start):
        return pltpu.make_async_copy(zbuf, o_ref.at[pl.ds(pl.multiple_of(start, EXPERT_TILE),
                                                          EXPERT_TILE)], zsem)

    @pl.when(pl.program_id(0) == 0)
    def _():
        zbuf[...] = jnp.zeros_like(zbuf)

        def each_expert(fn):
            def body(e, carry):
                @pl.when(cnt_ref[e] > 0)
                def _():
                    fn(fill(segend_ref[e] - EXPERT_TILE))
                return carry
            lax.fori_loop(0, N_EXPERTS, body, 0)

        def each_free_block(fn):
            def body(b, carry):
                fn(fill(b * EXPERT_TILE))
                return carry
            lax.fori_loop(nu_ref[0], nb, body, 0)

        each_expert(lambda c: c.start())
        each_free_block(lambda c: c.start())
        each_expert(lambda c: c.wait())
        each_free_block(lambda c: c.wait())

    def start(i, carry):
        for j in range(TOP_K):
            d = dest_ref[i * TOP_K + j]
            inv_ref[d] = (row0 + i) * TOP_K + j
            pltpu.make_async_copy(x_ref.at[pl.ds(i, 1)], o_ref.at[pl.ds(d, 1)], sem).start()
        return carry

    lax.fori_loop(0, tm, start, 0)
    for j in range(TOP_K):
        pltpu.make_async_copy(x_ref, o_ref.at[pl.ds(0, tm)], sem).wait()

    @pl.when(pl.program_id(0) == pl.num_programs(0) - 1)
    def _():
        c = pltpu.make_async_copy(inv_ref, invo_ref, isem)
        c.start()
        c.wait()


def moe_dispatch(dest_flat, seg_end, counts, n_used, xpk, inv_default, nb):
    n, w = xpk.shape
    tm = ROW_TILE
    rows = nb * EXPERT_TILE
    smem = lambda: pl.BlockSpec(memory_space=pltpu.SMEM)
    hbm = lambda: pl.BlockSpec(memory_space=pl.ANY)
    return pl.pallas_call(
        functools.partial(_dispatch_kernel, nb),
        grid=(n // tm,),
        in_specs=[pl.BlockSpec((tm * TOP_K,), lambda i: (i,), memory_space=pltpu.SMEM),
                  smem(), smem(), smem(),
                  pl.BlockSpec((tm, w), lambda i: (i, 0)), hbm()],
        out_specs=[hbm(), hbm()],
        out_shape=[jax.ShapeDtypeStruct((rows, w), I32), jax.ShapeDtypeStruct((rows,), I32)],
        scratch_shapes=[pltpu.SMEM((rows,), I32), pltpu.VMEM((EXPERT_TILE, w), I32),
                        pltpu.SemaphoreType.DMA(()), pltpu.SemaphoreType.DMA(()),
                        pltpu.SemaphoreType.DMA(())],
        compiler_params=_cparams(("arbitrary",)),
        name="moe_dispatch",
    )(dest_flat, seg_end, counts, n_used, xpk, inv_default)


def _experts_kernel(n_slots, nb, be_ref, nxt_ref, slot_ref, nu_ref, inv_ref, x_ref, wg_ref, wu_ref,
                    wd_ref, o_ref, wg_f, wu_f, wd_f, wg_s, wu_s, wd_s, obuf_a, obuf_b, sems, osems):
    i = pl.program_id(0)
    n_used = nu_ref[0]
    used = i < n_used
    prev = be_ref[jnp.maximum(i - 1, 0)]
    fresh = jnp.logical_and(used, jnp.logical_or(i == 0, be_ref[i] != prev))
    bm = x_ref.shape[0]

    obufs = (obuf_a, obuf_b)

    def scatter_rows(blk, par):
        for r in range(bm):
            pltpu.make_async_copy(obufs[par].at[pl.ds(r, 1)],
                                  o_ref.at[pl.ds(inv_ref[blk * bm + r], 1)], osems.at[par]).start()

    def wait_rows(par):
        pltpu.make_async_copy(obufs[par], o_ref.at[pl.ds(0, bm)], osems.at[par]).wait()

    def by_parity(cond, fn):
        for par in range(2):
            @pl.when(jnp.logical_and(cond, i % 2 == par))
            def _(par=par):
                fn(par)

    @pl.when(i == 0)
    def _():
        obuf_b[...] = jnp.zeros_like(obuf_b)
        spare = [pltpu.make_async_copy(obuf_b, o_ref.at[pl.ds(n_slots + e * bm, bm)], osems.at[1])
                 for e in range(N_EXPERTS)]
        for c in spare:
            c.start()
        for c in spare:
            c.wait()

    by_parity(jnp.logical_and(i >= 2, i - 2 < n_used), wait_rows)

    def weight_copies(e, slot):
        return (pltpu.make_async_copy(wg_ref.at[e], wg_f.at[slot], sems.at[slot, 0]),
                pltpu.make_async_copy(wu_ref.at[e], wu_f.at[slot], sems.at[slot, 1]),
                pltpu.make_async_copy(wd_ref.at[e], wd_f.at[slot], sems.at[slot, 2]))

    @pl.when(jnp.logical_and(used, i == 0))
    def _():
        for c in weight_copies(be_ref[0], 0):
            c.start()

    @pl.when(fresh)
    def _():
        slot = slot_ref[i]
        for c in weight_copies(be_ref[i], slot):
            c.wait()

        @pl.when(nxt_ref[i] >= 0)
        def _():
            for c in weight_copies(nxt_ref[i], 1 - slot):
                c.start()

        wg_s[...] = wg_f[slot].astype(BF16)
        wu_s[...] = wu_f[slot].astype(BF16)
        wd_s[...] = wd_f[slot].astype(BF16)

    def compute(par):
        hi, lo = _unpack_bf16_pairs(x_ref[...])
        hi, lo = hi.astype(BF16), lo.astype(BF16)
        half = hi.shape[1]
        gp = _dot(hi, wg_s[:half, :]) + _dot(lo, wg_s[half:, :])
        up = _dot(hi, wu_s[:half, :]) + _dot(lo, wu_s[half:, :])
        act = gp * _sigmoid(gp) * up
        obufs[par][...] = _pack_bf16_pairs(_dot(act.astype(BF16), wd_s[...]))

    @pl.when(jnp.logical_and(used, i == 0))
    def _():
        compute(0)

    def send_prev_and_compute(par):
        scatter_rows(i - 1, 1 - par)
        compute(par)

    by_parity(jnp.logical_and(used, i > 0), send_prev_and_compute)
    by_parity(jnp.logical_and(i == n_used, i > 0), lambda par: scatter_rows(i - 1, 1 - par))

    @pl.when(jnp.logical_and(i == nb - 1, n_used == nb - 1))
    def _():
        wait_rows((nb - 2) % 2)


def moe_experts(blk_e, nxt_e, slot, n_used, inv, xs, n_slots, w_gate, w_up, w_down):
    w = xs.shape[1]
    nb = xs.shape[0] // EXPERT_TILE
    _, d, de = w_gate.shape
    grid_spec = pltpu.PrefetchScalarGridSpec(
        num_scalar_prefetch=5,
        grid=(nb,),
        in_specs=[pl.BlockSpec((EXPERT_TILE, w), lambda i, be, nx, sl, nu, iv: (jnp.minimum(i, nu[0] - 1), 0)),
                  pl.BlockSpec(memory_space=pl.ANY), pl.BlockSpec(memory_space=pl.ANY),
                  pl.BlockSpec(memory_space=pl.ANY)],
        out_specs=pl.BlockSpec(memory_space=pl.ANY),
        scratch_shapes=[pltpu.VMEM((2, d, de), F32), pltpu.VMEM((2, d, de), F32),
                        pltpu.VMEM((2, de, d), F32),
                        pltpu.VMEM((d, de), BF16), pltpu.VMEM((d, de), BF16),
                        pltpu.VMEM((de, d), BF16), pltpu.VMEM((EXPERT_TILE, w), I32),
                        pltpu.VMEM((EXPERT_TILE, w), I32),
                        pltpu.SemaphoreType.DMA((2, 3)), pltpu.SemaphoreType.DMA((2,))],
    )
    return pl.pallas_call(
        functools.partial(_experts_kernel, n_slots, nb),
        grid_spec=grid_spec,
        out_shape=jax.ShapeDtypeStruct((n_slots + N_EXPERTS * EXPERT_TILE, w), I32),
        compiler_params=_cparams(("arbitrary",)),
        name="moe_experts",
    )(blk_e, nxt_e, slot, n_used, inv, xs, w_gate, w_up, w_down)


def _combine_kernel(alpha, h_ref, gate_ref, ys_ref, wsg_ref, wsu_ref, wsd_ref, l2g_ref, l2b_ref, o_ref):
    tm = h_ref.shape[0]
    h = h_ref[...]
    hb = h.astype(BF16)
    gp = _dot(hb, wsg_ref[...])
    up = _dot(hb, wsu_ref[...])
    shared = _dot((gp * _sigmoid(gp) * up).astype(BF16), wsd_ref[...])
    row = lax.broadcasted_iota(I32, (tm, tm * TOP_K), 0)
    col = lax.broadcasted_iota(I32, (tm, tm * TOP_K), 1)
    g_hi, g_lo = _split_bf16(jnp.where(col // TOP_K == row, gate_ref[0], 0.0))
    hi, lo = _unpack_bf16_pairs(ys_ref[...])
    hi, lo = hi.astype(BF16), lo.astype(BF16)
    routed = jnp.concatenate([_dot(g_hi, hi) + _dot(g_lo, hi), _dot(g_hi, lo) + _dot(g_lo, lo)], axis=1)
    o_ref[...] = _layer_norm(alpha * h + routed + shared, l2g_ref[...], l2b_ref[...])


def moe_combine(h1, gate_rows, ys, pw, alpha):
    n, d = h1.shape
    tm = gate_rows.shape[2] // TOP_K
    full = lambda a: pl.BlockSpec(a.shape, lambda i: (0,) * a.ndim)
    params = (pw["ws_gate"], pw["ws_up"], pw["ws_down"], pw["ln2_g"], pw["ln2_b"])
    return pl.pallas_call(
        functools.partial(_combine_kernel, alpha),
        grid=(n // tm,),
        in_specs=[pl.BlockSpec((tm, d), lambda i: (i, 0)),
                  pl.BlockSpec((1, 1, tm * TOP_K), lambda i: (i, 0, 0)),
                  pl.BlockSpec((tm * TOP_K, d // 2), lambda i: (i, 0))]
                 + [full(a) for a in params],
        out_specs=pl.BlockSpec((tm, d), lambda i: (i, 0)),
        out_shape=jax.ShapeDtypeStruct((n, d), F32),
        compiler_params=_cparams(("parallel",)),
        name="moe_combine",
    )(h1, gate_rows, ys, *params)


def _round_up(x, m):
    return (x + m - 1) // m * m


def _rope_tables(pos, head):
    half = head // 2
    inv = ROPE_THETA ** (-jnp.arange(half, dtype=F32) / half)
    ang = pos.astype(F32)[:, None] * inv[None, :]
    cos, sin = jnp.cos(ang), jnp.sin(ang)
    rep = LANES // head
    c = jnp.tile(jnp.concatenate([cos, cos], axis=1), (1, rep))
    s = jnp.tile(jnp.concatenate([-sin, sin], axis=1), (1, rep))
    return c, s


def _permute_cols(m):
    a0 = SHIFT_DIM
    pieces = [
        m[..., 0:3 * R_DIM],
        m[..., a0:a0 + A_DIM],
        m[..., a0 + A_DIM + 2 * A_KV_DIM:a0 + A_DIM + 2 * A_KV_DIM + IDX_HEADS * IDX_DIM],
        m[..., a0 + A_DIM:a0 + A_DIM + 2 * A_KV_DIM],
    ]
    i0 = a0 + A_DIM + 2 * A_KV_DIM + IDX_HEADS * IDX_DIM
    pieces.append(m[..., i0:i0 + IDX_DIM + IDX_HEADS])
    pad = lambda w: jnp.zeros(m.shape[:-1] + (w,), m.dtype)
    pieces.append(pad(LANES - IDX_DIM - IDX_HEADS))
    pieces.append(m[..., 3 * R_DIM:SHIFT_DIM])
    pieces.append(pad(LORA_W - (SHIFT_DIM - 3 * R_DIM)))
    return jnp.concatenate(pieces, axis=-1)


def kernel(x_prompt, x_sample, cache_k, cache_v, cache_idx_k, state_wkv, state_shift, page_table,
           meta, ln0_g, ln0_b, w_in, mu_shift, w0, w_b, a0, a_b, g_b, k_k, k_a, r_k, gn_g, gn_b,
           w_out, ln1_g, ln1_b, w_router, e_bias, w_gate, w_up, w_down, ws_gate, ws_up, ws_down,
           ln2_g, ln2_b):
    depth = w_in.shape[0]
    assert depth == 1, "single trunk layer"
    bsz, s_p, d = x_prompt.shape
    s_dec, s_s, _ = x_sample.shape
    assert s_s == 1, "one decode token per sequence"
    t_real = N_META + s_p
    tp = _round_up(t_real, LANES)
    assert (bsz * tp) % ROW_TILE == 0
    sp = _round_up(s_dec, ROW_TILE)
    n_prompt = bsz * tp
    n = n_prompt + sp
    n_pool, page = cache_k.shape[1], cache_k.shape[2]
    n_pages = page_table.shape[1]
    past = n_pages * page
    alpha = float((2 * depth) ** 0.25)
    row2 = lambda a: a.reshape(1, -1)

    meta_rows = jnp.broadcast_to(meta[None], (bsz, N_META, d))
    xp = jnp.concatenate([meta_rows, x_prompt, jnp.zeros((bsz, tp - t_real, d), F32)], axis=1)
    x_all = jnp.concatenate([xp.reshape(n_prompt, d), x_sample.reshape(s_dec, d),
                             jnp.zeros((sp - s_dec, d), F32)], axis=0)
    pos = jnp.concatenate([jnp.tile(jnp.arange(tp), bsz), jnp.full((sp,), past)])
    c128, s128 = _rope_tables(pos, A_HEAD)
    c64, s64 = _rope_tables(pos, IDX_DIM)

    w_in_k = _permute_cols(w_in[0]).astype(BF16)
    mu_k = _permute_cols(
        jnp.concatenate([mu_shift[0], jnp.zeros((w_in.shape[2] - SHIFT_DIM,), F32)])[None, :])
    head_of = jnp.arange(R_DIM) // R_HEAD
    e_mat = (head_of[:, None] == jnp.arange(R_HEADS)[None, :]).astype(F32)
    zpad = lambda a, rows_before, rows_total: jnp.concatenate(
        [jnp.zeros((rows_before, a.shape[1]), a.dtype), a,
         jnp.zeros((rows_total - rows_before - a.shape[0], a.shape[1]), a.dtype)], axis=0)
    pw = {
        "mu_x": mu_k[:, :3 * R_DIM], "mu_lo": mu_k[:, C_LORA:],
        "w0": row2(w0[0]), "a0": row2(a0[0]), "k_k": row2(k_k[0]), "k_a": row2(k_a[0]),
        "r_k": row2(r_k[0]), "gn_g": row2(gn_g[0]), "gn_b": row2(gn_b[0]),
        "w_b": zpad(w_b[0], 0, LANES).astype(BF16),
        "a_b": zpad(a_b[0], D_DECAY_LORA, LANES).astype(BF16),
        "g_b": zpad(g_b[0], 0, LORA_W - LANES).astype(BF16),
        "e": e_mat.astype(BF16), "et": e_mat.T.astype(BF16),
        "ln0_g": row2(ln0_g), "ln0_b": row2(ln0_b),
        "ln1_g": row2(ln1_g[0]), "ln1_b": row2(ln1_b[0]),
        "ln2_g": row2(ln2_g[0]), "ln2_b": row2(ln2_b[0]),
        "w_out": w_out[0].astype(BF16), "w_router_t": w_router[0].T,
        "ws_gate": ws_gate[0].astype(BF16), "ws_up": ws_up[0].astype(BF16),
        "ws_down": ws_down[0].astype(BF16),
    }

    p = ln_proj(x_all, pw["ln0_g"], pw["ln0_b"], w_in_k, tn=P_COLS // 3)
    q_r, iq_r, k_r, ik_r, iw = rope_all(p, c128, s128, c64, s64)

    pre_p = rwkv_pre(p, 0, n_prompt, None, pw, t_real, tp)
    shift_k = _permute_cols(jnp.concatenate(
        [state_shift[0], jnp.zeros((s_dec, w_in.shape[2] - SHIFT_DIM), F32)], axis=1))
    shift_k = jnp.concatenate([shift_k, jnp.zeros((sp - s_dec, P_COLS), F32)], axis=0)
    pre_s = rwkv_pre(p, n_prompt, sp, (shift_k[:, :3 * R_DIM], shift_k[:, C_LORA:]), pw, t_real, tp)
    r_p, ld_p, k_p, v_p, kk_p, b_p, g_p, bon_p = pre_p
    r_s, ld_s, k_s, v_s, kk_s, b_s, g_s, bon_s = pre_s
    y_p, wkv_p = rwkv_scan(r_p, ld_p, k_p, v_p, kk_p, b_p, bsz, tp)
    heads = lambda a: a[:s_dec].reshape(s_dec, R_HEADS, R_HEAD).transpose(1, 2, 0)
    y_hs, wkv_hs = rwkv_step(heads(r_s), heads(ld_s), heads(k_s), heads(v_s), heads(kk_s), heads(b_s),
                             state_wkv[0].transpose(1, 2, 3, 0))
    wkv_s = wkv_hs.transpose(3, 0, 1, 2)
    y_s = jnp.concatenate([y_hs.transpose(2, 0, 1).reshape(s_dec, R_DIM),
                           jnp.zeros((sp - s_dec, R_DIM), F32)], axis=0)

    n_sel_p = min(TOPK_KEYS, t_real // 4)
    a_p = dsa_prompt(q_r, iq_r, iw, ik_r, k_r, p, bsz, tp, n_sel_p)
    n_sel_s = min(TOPK_KEYS, (past + 1) // 4)
    pt_flat = page_table.reshape(-1).astype(I32)
    srow = slice(n_prompt, n_prompt + s_dec)
    sc_s = dsa_step_scores(pt_flat, iq_r[srow].reshape(s_dec, IDX_HEADS, IDX_DIM),
                           iw[srow].reshape(s_dec, IDX_HEADS, 1), ik_r[srow].reshape(s_dec, 1, IDX_DIM),
                           cache_idx_k[0].transpose(0, 2, 1).reshape(n_pool * IDX_DIM, page),
                           n_pages, page)
    sel_s = dsa_step_select(sc_s.reshape(s_dec, -1), n_sel_s, past).reshape(sc_s.shape)
    slot = jnp.arange(page)[:, None]
    expand = (jnp.arange(page * A_KV_HEADS)[None, :] // A_KV_HEADS == slot).astype(BF16)
    a_s = dsa_step_attn(pt_flat, q_r[srow].reshape(s_dec, A_HEADS, A_HEAD),
                        k_r[srow].reshape(s_dec, A_KV_HEADS, A_HEAD),
                        p[srow, C_VA:C_VA + A_KV_DIM].reshape(s_dec, A_KV_HEADS, A_HEAD),
                        sel_s, expand,
                        cache_k[0].reshape(n_pool * page * A_KV_HEADS, A_HEAD),
                        cache_v[0].reshape(n_pool * page * A_KV_HEADS, A_HEAD), n_pages, page)
    a_s = jnp.concatenate([a_s.reshape(s_dec, A_DIM), jnp.zeros((sp - s_dec, A_DIM), F32)], axis=0)

    h1, scores_t, xpk = mix_ln1_router(x_all, (y_p, bon_p, g_p, a_p), (y_s, bon_s, g_s, a_s), pw, alpha)
    eidx_t, gate_t, pos_t, counts = route(scores_t, e_bias[0].reshape(N_EXPERTS, 1))

    n_slots = n * TOP_K
    nb = (n_slots + N_EXPERTS * (EXPERT_TILE - 1)) // EXPERT_TILE + 1
    counts = counts[:, 0]
    padded = (counts + EXPERT_TILE - 1) // EXPERT_TILE * EXPERT_TILE
    seg_end = jnp.cumsum(padded).astype(I32)
    seg_start = seg_end - padded
    experts = jnp.arange(N_EXPERTS)
    start_of = jnp.sum(jnp.where(eidx_t[:, :, None] == experts, seg_start, 0), axis=-1)
    dest = (start_of + pos_t).T.astype(I32).reshape(-1)
    blk_row = jnp.arange(nb) * EXPERT_TILE
    blk_e = jnp.minimum(jnp.sum(seg_end[None, :] <= blk_row[:, None], axis=1),
                        N_EXPERTS - 1).astype(I32)
    n_used = (seg_end[-1] // EXPERT_TILE).astype(I32)
    run_start = jnp.concatenate([jnp.ones((1,), I32), (blk_e[1:] != blk_e[:-1]).astype(I32)])
    slot = ((jnp.cumsum(run_start) - 1) % 2).astype(I32)
    run_end = seg_end[blk_e] // EXPERT_TILE
    nxt_e = jnp.where(run_end < n_used, blk_e[jnp.minimum(run_end, nb - 1)], -1).astype(I32)
    n_used = n_used.reshape(1)
    spare = (n_slots + blk_e[:, None] * EXPERT_TILE + jnp.arange(EXPERT_TILE)[None, :]).astype(I32)
    xs, inv = moe_dispatch(dest, seg_end, counts.astype(I32), n_used, xpk, spare.reshape(-1), nb)
    ys = moe_experts(blk_e, nxt_e, slot, n_used, inv, xs, n_slots, w_gate[0], w_up[0], w_down[0])
    tc = Q_TILE
    h2 = moe_combine(h1, gate_t.T.reshape(n // tc, 1, tc * TOP_K), ys, pw, alpha)

    def prompt_rows(a):
        return a[:n_prompt].reshape(bsz, tp, -1)[:, :t_real]

    y_prompt = h2[:n_prompt].reshape(bsz, tp, d)[:, N_META:t_real]
    y_sample = h2[srow].reshape(s_dec, 1, d)
    k_prompt = prompt_rows(k_r).reshape(1, bsz, t_real, A_KV_HEADS, A_HEAD)
    v_prompt = prompt_rows(p[:, C_VA:C_VA + A_KV_DIM]).reshape(1, bsz, t_real, A_KV_HEADS, A_HEAD)
    ik_prompt = prompt_rows(ik_r)[None]
    last = jnp.arange(bsz) * tp + t_real - 1
    unperm = lambda rows: jnp.concatenate([rows[:, :3 * R_DIM],
                                           rows[:, C_LORA:C_LORA + SHIFT_DIM - 3 * R_DIM]], axis=1)
    shift_prompt = unperm(p[last])[None]
    k_sample = k_r[srow].reshape(1, s_dec, 1, A_KV_HEADS, A_HEAD)
    v_sample = p[srow, C_VA:C_VA + A_KV_DIM].reshape(1, s_dec, 1, A_KV_HEADS, A_HEAD)
    ik_sample = ik_r[srow].reshape(1, s_dec, 1, IDX_DIM)
    shift_sample = unperm(p[srow])[None]
    return (y_prompt, y_sample, k_prompt, v_prompt, ik_prompt, wkv_p[None], shift_prompt,
            k_sample, v_sample, ik_sample, wkv_s[None], shift_sample)
```

```python
import functools

import numpy as np
import jax
import jax.numpy as jnp
from jax import lax
from jax.experimental import pallas as pl
from jax.experimental.pallas import tpu as pltpu

F32 = jnp.float32
BF16 = jnp.bfloat16
I32 = jnp.int32
HIGHEST = lax.Precision.HIGHEST

N_META = 16
R_HEADS, R_HEAD = 16, 64
R_DIM = R_HEADS * R_HEAD
D_DECAY_LORA, D_AAA_LORA, D_GATE_LORA = 64, 64, 160
SHIFT_DIM = 3 * R_DIM + D_DECAY_LORA + D_AAA_LORA + D_GATE_LORA
GN_EPS = 64e-5
A_HEADS, A_KV_HEADS, A_HEAD = 8, 2, 128
A_DIM = A_HEADS * A_HEAD
A_KV_DIM = A_KV_HEADS * A_HEAD
IDX_HEADS, IDX_DIM = 16, 64
TOPK_KEYS = 256
ROPE_THETA = 10000.0
N_EXPERTS, N_EXPERT_GROUPS, TOPK_GROUPS, TOP_K = 64, 8, 4, 8
ROUTED_SCALE = 2.5
LN_EPS = 1e-5

LANES = 128
SUBLANES = 8
ROW_TILE = 256
Q_TILE = 128
CHUNK = 64
EXPERT_TILE = 256
VMEM_LIMIT = 56 * 1024 * 1024
NEG_BIG = -1e30
INT_MIN = -2 ** 31

C_R, C_K, C_V = 0, R_DIM, 2 * R_DIM
C_Q = 3 * R_DIM
C_IQ = C_Q + A_DIM
C_KA = C_IQ + IDX_HEADS * IDX_DIM
C_VA = C_KA + A_KV_DIM
C_IK = C_VA + A_KV_DIM
C_LORA = C_IK + LANES
LORA_W = 384
P_COLS = C_LORA + LORA_W


def _cparams(sem):
    return pltpu.CompilerParams(dimension_semantics=sem, vmem_limit_bytes=VMEM_LIMIT)


def _dot(a, b, precision=None):
    return jnp.dot(a, b, preferred_element_type=F32, precision=precision)


def _dot_nt(a, b, precision=None):
    return lax.dot_general(a, b, (((1,), (1,)), ((), ())), preferred_element_type=F32,
                           precision=precision)


def _dot_tn(a, b, precision=None):
    return lax.dot_general(a, b, (((0,), (0,)), ((), ())), preferred_element_type=F32,
                           precision=precision)


def _split_bf16(x):
    hi = x.astype(BF16)
    return hi, (x - hi.astype(F32)).astype(BF16)


def _dot_f32_by_bf16(a, b):
    hi, lo = _split_bf16(a)
    return _dot(hi, b) + _dot(lo, b)


def _head_sums(x, e, et):
    return _dot_f32_by_bf16(_dot_f32_by_bf16(x, e), et)


def _layer_norm(x, g, b):
    mu = jnp.mean(x, axis=-1, keepdims=True)
    xc = x - mu
    var = jnp.mean(xc * xc, axis=-1, keepdims=True)
    return xc * lax.rsqrt(var + LN_EPS) * g + b


def _sigmoid(z):
    return 1.0 / (1.0 + jnp.exp(-z))


def _ln_proj_kernel(x0_ref, xn_ref, g_ref, b_ref, w_ref, o_ref, h_even, h_odd):
    i = pl.program_id(1)
    norm = lambda ref: _layer_norm(ref[...], g_ref[...], b_ref[...]).astype(BF16)

    @pl.when(i == 0)
    def _():
        h_even[...] = norm(x0_ref)

    for par, (cur, nxt) in enumerate(((h_even, h_odd), (h_odd, h_even))):
        @pl.when(i % 2 == par)
        def _(cur=cur, nxt=nxt):
            o_ref[...] = _dot_nt(cur[...], w_ref[...])
            nxt[...] = norm(xn_ref)


def ln_proj(x, g, b, w_t, tn):
    n, d = x.shape
    cols = w_t.shape[0]
    ni = n // ROW_TILE
    return pl.pallas_call(
        _ln_proj_kernel,
        grid=(cols // tn, ni),
        in_specs=[
            pl.BlockSpec((ROW_TILE, d), lambda j, i: (0, 0)),
            pl.BlockSpec((ROW_TILE, d), lambda j, i: ((i + 1) % ni, 0)),
            pl.BlockSpec((1, d), lambda j, i: (0, 0)),
            pl.BlockSpec((1, d), lambda j, i: (0, 0)),
            pl.BlockSpec((tn, d), lambda j, i: (j, 0)),
        ],
        out_specs=pl.BlockSpec((ROW_TILE, tn), lambda j, i: (i, j)),
        out_shape=jax.ShapeDtypeStruct((n, cols), F32),
        scratch_shapes=[pltpu.VMEM((ROW_TILE, d), BF16), pltpu.VMEM((ROW_TILE, d), BF16)],
        compiler_params=_cparams(("arbitrary", "arbitrary")),
        name="ln_proj",
    )(x, x, g, b, w_t)


def _rot_half(x, head):
    w = x.shape[-1]
    half = head // 2
    lane = lax.broadcasted_iota(I32, x.shape, 1)
    left = pltpu.roll(x, w - half, axis=1)
    right = pltpu.roll(x, half, axis=1)
    return jnp.where((lane % head) < half, left, right)


def _rope_kernel(q_ref, iq_ref, ka_ref, ikw_ref, c128_ref, s128_ref, c64_ref, s64_ref,
                 qo_ref, iqo_ref, ko_ref, iko_ref, iwo_ref):
    c128, s128 = c128_ref[...], s128_ref[...]
    c64, s64 = c64_ref[...], s64_ref[...]

    def rope(x, head, c, s):
        rep = x.shape[-1] // LANES
        if rep > 1:
            c = jnp.concatenate([c] * rep, axis=1)
            s = jnp.concatenate([s] * rep, axis=1)
        return x * c + _rot_half(x, head) * s

    q = rope(q_ref[...], A_HEAD, c128, s128)
    qo_ref[...] = (q * (A_HEAD ** -0.5)).astype(BF16)
    iqo_ref[...] = rope(iq_ref[...], IDX_DIM, c64, s64).astype(BF16)
    ko_ref[...] = rope(ka_ref[...], A_HEAD, c128, s128)
    ikw = ikw_ref[...]
    ik = rope(ikw, IDX_DIM, c64, s64)
    iko_ref[...] = ik[:, :IDX_DIM]
    iwo_ref[...] = ikw[:, IDX_DIM:IDX_DIM + IDX_HEADS]


def rope_all(p, c128, s128, c64, s64):
    n = p.shape[0]
    tm = ROW_TILE
    row = lambda w, blk: pl.BlockSpec((tm, w), lambda i: (i, blk))
    return pl.pallas_call(
        _rope_kernel,
        grid=(n // tm,),
        in_specs=[row(A_DIM, C_Q // A_DIM), row(A_DIM, C_IQ // A_DIM),
                  row(A_KV_DIM, C_KA // A_KV_DIM), row(LANES, C_IK // LANES),
                  row(LANES, 0), row(LANES, 0), row(LANES, 0), row(LANES, 0)],
        out_specs=[row(A_DIM, 0), row(A_DIM, 0), row(A_KV_DIM, 0),
                   row(IDX_DIM, 0), row(IDX_HEADS, 0)],
        out_shape=[jax.ShapeDtypeStruct((n, A_DIM), BF16),
                   jax.ShapeDtypeStruct((n, IDX_HEADS * IDX_DIM), BF16),
                   jax.ShapeDtypeStruct((n, A_KV_DIM), F32),
                   jax.ShapeDtypeStruct((n, IDX_DIM), F32),
                   jax.ShapeDtypeStruct((n, IDX_HEADS), F32)],
        compiler_params=_cparams(("parallel",)),
        name="rope",
    )(p, p, p, p, c128, s128, c64, s64)


def _rwkv_pre_kernel(t_real, tp, from_rows, *refs):
    (x_ref, lo_ref, px_ref, plo_ref, mu_ref, mulo_ref, w0_ref, wb_ref, a0_ref, ab_ref,
     gb_ref, kk_ref, ka_ref, rk_ref, e_ref, et_ref,
     r_o, ld_o, k_o, v_o, kk_o, b_o, g_o, bon_o) = refs
    x = x_ref[...]
    lo = lo_ref[...]
    tm = x.shape[0]
    if from_rows:
        i = pl.program_id(0)
        row = lax.broadcasted_iota(I32, (tm, 1), 0)
        t = (i * tm + row) % tp
        first = row == 0
        sx = jnp.where(first, px_ref[SUBLANES - 1:SUBLANES, :], pltpu.roll(x, 1, axis=0))
        slo = jnp.where(first, plo_ref[SUBLANES - 1:SUBLANES, :], pltpu.roll(lo, 1, axis=0))
        sx = jnp.where(t == 0, 0.0, sx)
        slo = jnp.where(t == 0, 0.0, slo)
        live = t < t_real
    else:
        sx = px_ref[...]
        slo = plo_ref[...]
        live = None
    xx = x + (sx - x) * mu_ref[...]
    xlo = lo + (slo - lo) * mulo_ref[...]
    r = xx[:, C_R:C_R + R_DIM]
    k = xx[:, C_K:C_K + R_DIM]
    v = xx[:, C_V:C_V + R_DIM]
    wa = xlo[:, :LANES]
    xg = xlo[:, LANES:]
    z = w0_ref[...] + _dot(jnp.tanh(wa).astype(BF16), wb_ref[...])
    nz = -z
    softplus = jnp.maximum(nz, 0.0) + jnp.log(1.0 + jnp.exp(-jnp.abs(nz)))
    logd = -jnp.exp(-softplus - 0.5)
    a = _sigmoid(a0_ref[...] + _dot(wa.astype(BF16), ab_ref[...]))
    g = _dot(_sigmoid(xg).astype(BF16), gb_ref[...])
    e, et = e_ref[...], et_ref[...]
    kkr = k * kk_ref[...]
    ss = _head_sums(kkr * kkr, e, et)
    kk = kkr / jnp.maximum(jnp.sqrt(ss), 1e-12)
    k2 = k * (1.0 + (a - 1.0) * ka_ref[...])
    bonus = _head_sums(r * k2 * rk_ref[...], e, et) * v
    b = kk * a
    if live is not None:
        zero = lambda y: jnp.where(live, y, 0.0)
        logd, k2s, vs, kk, b = zero(logd), zero(k2), zero(v), zero(kk), zero(b)
    else:
        k2s, vs = k2, v
    r_o[...] = r
    ld_o[...] = logd
    k_o[...] = k2s
    v_o[...] = vs
    kk_o[...] = kk
    b_o[...] = b
    g_o[...] = g
    bon_o[...] = bonus


def rwkv_pre(p, row0, nrows, prev, pw, t_real, tp):
    tm = min(ROW_TILE, nrows)
    blk0 = row0 // tm
    from_rows = prev is None
    xw = 3 * R_DIM
    cur_x = pl.BlockSpec((tm, xw), lambda i: (blk0 + i, 0))
    cur_lo = pl.BlockSpec((tm, LORA_W), lambda i: (blk0 + i, C_LORA // LORA_W))
    if from_rows:
        r8 = tm // SUBLANES
        prev_x = pl.BlockSpec((SUBLANES, xw), lambda i: (jnp.maximum((blk0 + i) * r8 - 1, 0), 0))
        prev_lo = pl.BlockSpec((SUBLANES, LORA_W),
                               lambda i: (jnp.maximum((blk0 + i) * r8 - 1, 0), C_LORA // LORA_W))
        prev_args = (p, p)
    else:
        prev_x = pl.BlockSpec((tm, xw), lambda i: (i, 0))
        prev_lo = pl.BlockSpec((tm, LORA_W), lambda i: (i, 0))
        prev_args = prev
    full = lambda a: pl.BlockSpec(a.shape, lambda i: (0,) * a.ndim)
    params = (pw["mu_x"], pw["mu_lo"], pw["w0"], pw["w_b"], pw["a0"], pw["a_b"], pw["g_b"],
              pw["k_k"], pw["k_a"], pw["r_k"], pw["e"], pw["et"])
    out = pl.BlockSpec((tm, R_DIM), lambda i: (i, 0))
    return pl.pallas_call(
        functools.partial(_rwkv_pre_kernel, t_real, tp, from_rows),
        grid=(nrows // tm,),
        in_specs=[cur_x, cur_lo, prev_x, prev_lo] + [full(a) for a in params],
        out_specs=[out] * 8,
        out_shape=[jax.ShapeDtypeStruct((nrows, R_DIM), F32)] * 8,
        compiler_params=_cparams(("parallel",)),
        name="rwkv_pre_rows" if from_rows else "rwkv_pre_step",
    )(p, p, *prev_args, *params)


def _rwkv_scan_kernel(r_ref, ld_ref, k_ref, v_ref, kk_ref, b_ref, y_ref, s_ref, ss_scr):
    c = pl.program_id(1)

    @pl.when(c == 0)
    def _():
        ss_scr[...] = jnp.zeros_like(ss_scr)

    n = CHUNK
    n2 = 2 * n
    pairs = R_HEADS // 2
    ld_all = ld_ref[...]
    ri = lax.broadcasted_iota(I32, (n, n), 0)
    ci = lax.broadcasted_iota(I32, (n, n), 1)
    cum_all = _dot((ci <= ri).astype(F32), ld_all, HIGHEST)
    head0 = lax.broadcasted_iota(I32, (n, LANES), 1) < R_HEAD
    r4 = lax.broadcasted_iota(I32, (2 * n2, 2 * n2), 0)
    c4 = lax.broadcasted_iota(I32, (2 * n2, 2 * n2), 1)
    tri = (c4 % n) < (r4 % n) + jnp.where(r4 < n2, 0, 1)
    re = lax.broadcasted_iota(I32, (n2, n2), 0)
    ce = lax.broadcasted_iota(I32, (n2, n2), 1)
    eye = (re == ce).astype(F32)

    def stack(x):
        return jnp.concatenate([jnp.where(head0, x, 0.0), jnp.where(head0, 0.0, x)], axis=0)

    ar, bk, v2, ss, e_last = [], [], [], [], []
    for p in range(pairs):
        sl = slice(p * LANES, (p + 1) * LANES)
        cum, ld = cum_all[:, sl], ld_all[:, sl]
        e_pos = jnp.exp(cum)
        e_neg = jnp.exp(-cum)
        at = -kk_ref[:, sl] * jnp.exp(cum - ld)
        ar.append(jnp.concatenate([stack(at), stack(r_ref[:, sl] * e_pos)], axis=0).astype(BF16))
        bk.append(jnp.concatenate([stack(b_ref[:, sl] * e_neg), stack(k_ref[:, sl] * e_neg)],
                                  axis=0).astype(BF16))
        v2.append(stack(v_ref[:, sl]).astype(BF16))
        ss.append(ss_scr[p])
        e_last.append(e_pos[n - 1:n, :])
    xy0 = [_dot_nt(ar[p], ss[p].astype(BF16)) for p in range(pairs)]
    sc = [jnp.where(tri, _dot_nt(ar[p], bk[p]), 0.0) for p in range(pairs)]
    lp = [s[:n2, :n2] for s in sc]
    t = [eye + l for l in lp]
    m = 1
    while 2 * m < n:
        lpb = [l.astype(BF16) for l in lp]
        lp = [_dot(l, l) for l in lpb]
        t = [t[p] + _dot(t[p].astype(BF16), lp[p].astype(BF16)) for p in range(pairs)]
        m *= 2
    w = [xy0[p][:n2] + _dot(sc[p][:n2, n2:].astype(BF16), v2[p]) for p in range(pairs)]
    u = [_dot(t[p].astype(BF16), w[p].astype(BF16)) for p in range(pairs)]
    uv = [jnp.concatenate([u[p].astype(BF16), v2[p]], axis=0) for p in range(pairs)]
    y = [xy0[p][n2:] + _dot(sc[p][n2:].astype(BF16), uv[p]) for p in range(pairs)]
    upd = [_dot_tn(uv[p], bk[p]) for p in range(pairs)]
    for p in range(pairs):
        y_ref[:, p * LANES:(p + 1) * LANES] = y[p][:n] + y[p][n:]
        ss_scr[p] = (ss[p] + upd[p]) * e_last[p]

    @pl.when(c == pl.num_programs(1) - 1)
    def _():
        for p in range(R_HEADS // 2):
            ss = ss_scr[p]
            s_ref[0, 2 * p] = ss[:R_HEAD, :R_HEAD]
            s_ref[0, 2 * p + 1] = ss[R_HEAD:, R_HEAD:]


def rwkv_scan(r, ld, k, v, kk, b, batch, tp):
    nchunk = tp // CHUNK
    blk = pl.BlockSpec((CHUNK, R_DIM), lambda bi, c: (bi * nchunk + c, 0))
    return pl.pallas_call(
        _rwkv_scan_kernel,
        grid=(batch, nchunk),
        in_specs=[blk] * 6,
        out_specs=[blk, pl.BlockSpec((1, R_HEADS, R_HEAD, R_HEAD), lambda bi, c: (bi, 0, 0, 0))],
        out_shape=[jax.ShapeDtypeStruct((batch * tp, R_DIM), F32),
                   jax.ShapeDtypeStruct((batch, R_HEADS, R_HEAD, R_HEAD), F32)],
        scratch_shapes=[pltpu.VMEM((R_HEADS // 2, LANES, LANES), F32)],
        compiler_params=_cparams(("parallel", "arbitrary")),
        name="rwkv_scan",
    )(r, ld, k, v, kk, b)


STEP_ROWS = 16


def _rwkv_step_kernel(r_ref, ld_ref, k_ref, v_ref, kk_ref, b_ref, s_ref, y_ref, so_ref):
    r, k = r_ref[0], k_ref[0]
    dec = jnp.exp(ld_ref[0])
    na = -kk_ref[0]
    b = b_ref[0]
    v = v_ref[0]
    ys = []
    for v0 in range(0, R_HEAD, STEP_ROWS):
        rows = range(v0, v0 + STEP_ROWS)
        s = [s_ref[0, vi] for vi in rows]
        sa = [jnp.sum(x * na, axis=0, keepdims=True) for x in s]
        s_new = [x * dec + a * b + v[vi:vi + 1, :] * k for x, a, vi in zip(s, sa, rows)]
        ys += [jnp.sum(x * r, axis=0, keepdims=True) for x in s_new]
        for x, vi in zip(s_new, rows):
            so_ref[0, vi] = x
    y_ref[0] = jnp.concatenate(ys, axis=0)


def rwkv_step(r, ld, k, v, kk, b, state):
    s = state.shape[-1]
    vec = pl.BlockSpec((1, R_HEAD, s), lambda h: (h, 0, 0))
    st = pl.BlockSpec((1, R_HEAD, R_HEAD, s), lambda h: (h, 0, 0, 0))
    return pl.pallas_call(
        _rwkv_step_kernel,
        grid=(R_HEADS,),
        in_specs=[vec] * 6 + [st],
        out_specs=[vec, st],
        out_shape=[jax.ShapeDtypeStruct((R_HEADS, R_HEAD, s), F32),
                   jax.ShapeDtypeStruct(state.shape, F32)],
        compiler_params=_cparams(("parallel",)),
        name="rwkv_step",
    )(r, ld, k, v, kk, b, state)


def _select_topk(score, allowed, n_sel):
    bits = lax.bitcast_convert_type(score, I32)
    key = jnp.where(bits < 0, bits ^ jnp.int32(0x7FFFFFFF), bits)
    key = jnp.where(allowed, key, jnp.int32(INT_MIN))
    m, w = score.shape
    one, zero = jnp.ones((), BF16), jnp.zeros((), BF16)

    def byte(shift):
        if shift == 24:
            d = lax.shift_right_arithmetic(key, 24) + 128
        else:
            d = lax.shift_right_logical(key, shift) & 255
        return d.astype(F32).astype(BF16)

    def count(flags):
        acc = flags[:, :LANES]
        for c in range(LANES, w, LANES):
            acc = acc + flags[:, c:c + LANES]
        return jnp.sum(acc.astype(F32), axis=1, keepdims=True)

    need = jnp.full((m, 1), float(n_sel), F32)
    x = byte(24)
    tau = jnp.zeros((m, 1), I32)
    for shift in (24, 16, 8, 0):
        def body(it, t, x=x, need=need):
            step = lax.shift_left(jnp.int32(1), jnp.int32(6) - 2 * it).astype(F32)
            cnts = [count(jnp.where(x >= (t + mult * step).astype(BF16), one, zero))
                    for mult in (1.0, 2.0, 3.0)]
            hits = sum(jnp.where(cnt >= need, 1.0, 0.0) for cnt in cnts)
            return t + hits * step

        t = lax.fori_loop(0, 4, body, jnp.zeros((m, 1), F32))
        tb = t.astype(BF16)
        need = need - count(jnp.where(x > tb, one, zero))
        digit = t.astype(I32) - (128 if shift == 24 else 0)
        tau = tau | lax.shift_left(digit, shift)
        if shift:
            x = jnp.where(x == tb, byte(shift - 8), -one)
    return jnp.logical_and(key >= tau, allowed)


KEY_TILE = 256
Q_TILES_PER_EXTENT = 3


def _dsa_prompt_block(n_sel, tk, i, q_ref, iq_ref, iw_ref, ik_ref, k_ref, v_ref, o_ref, sc_ref):
    tq = q_ref.shape[0]
    iw = iw_ref[...] * ((IDX_HEADS * IDX_DIM) ** -0.5)
    iq = iq_ref[...]
    iq_h = [iq[:, h * IDX_DIM:(h + 1) * IDX_DIM] for h in range(IDX_HEADS)]
    iw_h = [iw[:, h:h + 1] for h in range(IDX_HEADS)]
    for c0 in range(0, tk, KEY_TILE):
        c1 = min(c0 + KEY_TILE, tk)
        ikb = ik_ref[c0:c1, :].astype(BF16)
        acc = jnp.maximum(_dot_nt(iq_h[0], ikb), 0.0) * iw_h[0]
        for h in range(1, IDX_HEADS):
            acc = acc + jnp.maximum(_dot_nt(iq_h[h], ikb), 0.0) * iw_h[h]
        sc_ref[:, c0:c1] = acc
    qpos = i * tq + lax.broadcasted_iota(I32, (tq, 1), 0)
    kpos = lax.broadcasted_iota(I32, (1, tk), 1)
    sel = _select_topk(sc_ref[:, :tk], kpos <= qpos, n_sel)
    bias = jnp.where(sel, 0.0, NEG_BIG)
    q = q_ref[...]
    rep = A_HEADS // A_KV_HEADS
    for g in range(A_KV_HEADS):
        kg = k_ref[:tk, g * A_HEAD:(g + 1) * A_HEAD].astype(BF16)
        vg = v_ref[:tk, g * A_HEAD:(g + 1) * A_HEAD].astype(BF16)
        for rr in range(rep):
            h = g * rep + rr
            s = _dot_nt(q[:, h * A_HEAD:(h + 1) * A_HEAD], kg) + bias
            m = jnp.max(s, axis=1, keepdims=True)
            p = jnp.exp(s - m)
            l = jnp.sum(p, axis=1, keepdims=True)
            o_ref[:, h * A_HEAD:(h + 1) * A_HEAD] = _dot(p.astype(BF16), vg) / l


def _dsa_prompt_kernel(n_sel, *refs):
    i = pl.program_id(1)
    tq = refs[0].shape[0]
    tp = refs[4].shape[0]
    nq = tp // tq
    for lo in range(0, nq, Q_TILES_PER_EXTENT):
        hi = min(lo + Q_TILES_PER_EXTENT, nq)

        @pl.when(jnp.logical_and(i >= lo, i < hi))
        def _(hi=hi):
            _dsa_prompt_block(n_sel, hi * tq, i, *refs)


def dsa_prompt(q, iq, iw, ik, k, p, batch, tp, n_sel):
    nq = tp // Q_TILE
    qrow = lambda w: pl.BlockSpec((Q_TILE, w), lambda b, i: (b * nq + i, 0))
    keys = lambda w, blk: pl.BlockSpec((tp, w), lambda b, i: (b, blk))
    return pl.pallas_call(
        functools.partial(_dsa_prompt_kernel, n_sel),
        grid=(batch, nq),
        in_specs=[qrow(A_DIM), qrow(IDX_HEADS * IDX_DIM), qrow(IDX_HEADS),
                  keys(IDX_DIM, 0), keys(A_KV_DIM, 0), keys(A_KV_DIM, C_VA // A_KV_DIM)],
        out_specs=qrow(A_DIM),
        out_shape=jax.ShapeDtypeStruct((batch * tp, A_DIM), F32),
        scratch_shapes=[pltpu.VMEM((Q_TILE, tp), F32)],
        compiler_params=_cparams(("parallel", "parallel")),
        name="dsa_prompt",
    )(q, iq, iw, ik, k, p)


SCORE_SEQS = 4
ATTN_SEQS = 2


def _dsa_step_score_kernel(n_pages, page, pt_ref, iq_ref, iw_ref, ikn_ref, *refs):
    sb = iq_ref.shape[0]
    pages = refs[:sb * n_pages]
    o_ref = refs[sb * n_pages]
    lane = lax.broadcasted_iota(I32, (1, LANES), 1)
    for q in range(sb):
        iq = iq_ref[q]
        iw = iw_ref[q] * ((IDX_HEADS * IDX_DIM) ** -0.5)
        for j in range(n_pages):
            d = _dot(iq, pages[q * n_pages + j][...].astype(BF16))
            o_ref[q, :, j * page:(j + 1) * page] = jnp.sum(jnp.maximum(d, 0.0) * iw, axis=0,
                                                           keepdims=True)
        dn = jnp.sum(iq.astype(F32) * ikn_ref[q], axis=1, keepdims=True)
        sn = jnp.sum(jnp.maximum(dn, 0.0) * iw, axis=0, keepdims=True)
        o_ref[q, :, n_pages * page:] = jnp.where(lane == 0, sn, 0.0)


def _seqs_per_step(s, want):
    return want if s % want == 0 else 1


def dsa_step_scores(pt_flat, iq, iw, ik_new, cik2d, n_pages, page):
    s = iq.shape[0]
    sb = _seqs_per_step(s, SCORE_SEQS)
    kw = n_pages * page + LANES
    page_spec = lambda q, j: pl.BlockSpec(
        (IDX_DIM, page), lambda i, pt: (pt[(i * sb + q) * n_pages + j], 0))
    grid_spec = pltpu.PrefetchScalarGridSpec(
        num_scalar_prefetch=1,
        grid=(s // sb,),
        in_specs=[pl.BlockSpec((sb, IDX_HEADS, IDX_DIM), lambda i, pt: (i, 0, 0)),
                  pl.BlockSpec((sb, IDX_HEADS, 1), lambda i, pt: (i, 0, 0)),
                  pl.BlockSpec((sb, 1, IDX_DIM), lambda i, pt: (i, 0, 0))]
                 + [page_spec(q, j) for q in range(sb) for j in range(n_pages)],
        out_specs=pl.BlockSpec((sb, 1, kw), lambda i, pt: (i, 0, 0)),
    )
    return pl.pallas_call(
        functools.partial(_dsa_step_score_kernel, n_pages, page),
        grid_spec=grid_spec,
        out_shape=jax.ShapeDtypeStruct((s, 1, kw), F32),
        compiler_params=_cparams(("arbitrary",)),
        name="dsa_step_scores",
    )(pt_flat, iq, iw, ik_new, *([cik2d] * (sb * n_pages)))


def _dsa_step_select_kernel(n_sel, past, sc_ref, o_ref):
    sc = sc_ref[...]
    kpos = lax.broadcasted_iota(I32, sc.shape, 1)
    sel = _select_topk(sc, kpos <= past, n_sel)
    o_ref[...] = sel.astype(F32)


def dsa_step_select(sc, n_sel, past):
    return pl.pallas_call(
        functools.partial(_dsa_step_select_kernel, n_sel, past),
        out_shape=jax.ShapeDtypeStruct(sc.shape, F32),
        compiler_params=pltpu.CompilerParams(vmem_limit_bytes=VMEM_LIMIT),
        name="dsa_step_select",
    )(sc)


def _dsa_step_attn_kernel(n_pages, page, pt_ref, q_ref, kn_ref, vn_ref, sel_ref, ex_ref, *refs):
    sb = q_ref.shape[0]
    kps = refs[:sb * n_pages]
    vps = refs[sb * n_pages:2 * sb * n_pages]
    o_ref = refs[2 * sb * n_pages]
    rep = A_HEADS // A_KV_HEADS
    w2 = page * A_KV_HEADS
    hrow = lax.broadcasted_iota(I32, (A_HEADS, w2), 0)
    col = lax.broadcasted_iota(I32, (A_HEADS, w2), 1)
    own = (col % A_KV_HEADS) == (hrow // rep)
    h8 = lax.broadcasted_iota(I32, (A_HEADS, A_HEAD), 0)
    ex = ex_ref[...]
    for u in range(sb):
        kp = kps[u * n_pages:(u + 1) * n_pages]
        vp = vps[u * n_pages:(u + 1) * n_pages]
        q = q_ref[u]
        logits = []
        for j in range(n_pages):
            s = _dot_nt(q, kp[j][...].astype(BF16))
            selj = _dot(sel_ref[u, :, j * page:(j + 1) * page].astype(BF16), ex)
            logits.append(jnp.where(jnp.logical_and(selj > 0.5, own), s, NEG_BIG))
        kn = jnp.where(h8 < rep, kn_ref[u, 0:1, :], kn_ref[u, 1:2, :])
        vn = jnp.where(h8 < rep, vn_ref[u, 0:1, :], vn_ref[u, 1:2, :])
        sn = jnp.sum(q.astype(F32) * kn, axis=1, keepdims=True)
        seln = sel_ref[u, :, n_pages * page:n_pages * page + 1]
        sn = jnp.where(seln > 0.5, sn, NEG_BIG)
        m = sn
        for s in logits:
            m = jnp.maximum(m, jnp.max(s, axis=1, keepdims=True))
        pn = jnp.exp(sn - m)
        l = pn
        acc = pn * vn
        for j in range(n_pages):
            p = jnp.exp(logits[j] - m)
            l = l + jnp.sum(p, axis=1, keepdims=True)
            acc = acc + _dot(p.astype(BF16), vp[j][...].astype(BF16))
        o_ref[u] = acc / l


def dsa_step_attn(pt_flat, q, k_new, v_new, sel, expand, ck2d, cv2d, n_pages, page):
    s = q.shape[0]
    sb = _seqs_per_step(s, ATTN_SEQS)
    kw = sel.shape[-1]
    w2 = page * A_KV_HEADS
    page_spec = lambda u, j: pl.BlockSpec(
        (w2, A_HEAD), lambda i, pt: (pt[(i * sb + u) * n_pages + j], 0))
    pages = [page_spec(u, j) for u in range(sb) for j in range(n_pages)]
    grid_spec = pltpu.PrefetchScalarGridSpec(
        num_scalar_prefetch=1,
        grid=(s // sb,),
        in_specs=[pl.BlockSpec((sb, A_HEADS, A_HEAD), lambda i, pt: (i, 0, 0)),
                  pl.BlockSpec((sb, A_KV_HEADS, A_HEAD), lambda i, pt: (i, 0, 0)),
                  pl.BlockSpec((sb, A_KV_HEADS, A_HEAD), lambda i, pt: (i, 0, 0)),
                  pl.BlockSpec((sb, 1, kw), lambda i, pt: (i, 0, 0)),
                  pl.BlockSpec((page, w2), lambda i, pt: (0, 0))] + pages * 2,
        out_specs=pl.BlockSpec((sb, A_HEADS, A_HEAD), lambda i, pt: (i, 0, 0)),
    )
    return pl.pallas_call(
        functools.partial(_dsa_step_attn_kernel, n_pages, page),
        grid_spec=grid_spec,
        out_shape=jax.ShapeDtypeStruct((s, A_HEADS, A_HEAD), F32),
        compiler_params=_cparams(("arbitrary",)),
        name="dsa_step_attn",
    )(pt_flat, q, k_new, v_new, sel, expand, *([ck2d] * (sb * n_pages)), *([cv2d] * (sb * n_pages)))


def _pack_bf16_pairs(x):
    w = x.shape[1] // 2
    hi = lax.bitcast_convert_type(x[:, :w].astype(BF16).astype(F32), I32)
    lo = lax.bitcast_convert_type(x[:, w:].astype(BF16).astype(F32), I32)
    return hi | lax.shift_right_logical(lo, 16)


def _unpack_bf16_pairs(p):
    hi = lax.bitcast_convert_type(p & jnp.int32(-65536), F32)
    lo = lax.bitcast_convert_type(lax.shift_left(p, 16), F32)
    return hi, lo


def _mix_kernel(alpha, prompt_blocks, x_ref, yp_ref, bonp_ref, gp_ref, ap_ref, ys_ref, bons_ref, gs_ref,
                as_ref, e_ref, et_ref, gng_ref, gnb_ref, l0g_ref, l0b_ref, wo_ref, l1g_ref, l1b_ref,
                wr_ref, h_ref, sc_ref, pk_ref):
    is_prompt = pl.program_id(0) < prompt_blocks
    pick = lambda p_ref, s_ref: jnp.where(is_prompt, p_ref[...], s_ref[...])
    e, et = e_ref[...], et_ref[...]
    y = pick(yp_ref, ys_ref)
    inv = 1.0 / R_HEAD
    mu = _head_sums(y, e, et) * inv
    d = y - mu
    var = _head_sums(d * d, e, et) * inv
    yn = d * lax.rsqrt(var + GN_EPS) * gng_ref[...] + gnb_ref[...]
    r_out = (yn + pick(bonp_ref, bons_ref)) * pick(gp_ref, gs_ref)
    mix = (_dot(r_out.astype(BF16), wo_ref[:R_DIM, :])
           + _dot(pick(ap_ref, as_ref).astype(BF16), wo_ref[R_DIM:, :]))
    h0 = _layer_norm(x_ref[...], l0g_ref[...], l0b_ref[...])
    h1 = _layer_norm(alpha * h0 + mix, l1g_ref[...], l1b_ref[...])
    h_ref[...] = h1
    w_hi, w_lo = _split_bf16(wr_ref[...])
    h_hi, h_lo = _split_bf16(h1)
    sc_ref[...] = _sigmoid(_dot_nt(w_hi, h_hi) + _dot_nt(w_hi, h_lo) + _dot_nt(w_lo, h_hi))
    pk_ref[...] = _pack_bf16_pairs(h1)


def mix_ln1_router(x, prompt_parts, step_parts, pw, alpha):
    n, d = x.shape
    tm = ROW_TILE
    pb = prompt_parts[0].shape[0] // tm
    row = lambda w: pl.BlockSpec((tm, w), lambda i: (i, 0))
    head = lambda w: pl.BlockSpec((tm, w), lambda i: (jnp.minimum(i, pb - 1), 0))
    tail = lambda w: pl.BlockSpec((tm, w), lambda i: (jnp.maximum(i - pb, 0), 0))
    full = lambda a: pl.BlockSpec(a.shape, lambda i: (0,) * a.ndim)
    params = (pw["e"], pw["et"], pw["gn_g"], pw["gn_b"], pw["ln0_g"], pw["ln0_b"], pw["w_out"],
              pw["ln1_g"], pw["ln1_b"], pw["w_router_t"])
    widths = (R_DIM, R_DIM, R_DIM, A_DIM)
    return pl.pallas_call(
        functools.partial(_mix_kernel, alpha, pb),
        grid=(n // tm,),
        in_specs=[row(d)] + [head(w) for w in widths] + [tail(w) for w in widths]
                 + [full(a) for a in params],
        out_specs=[row(d), pl.BlockSpec((N_EXPERTS, tm), lambda i: (0, i)), row(d // 2)],
        out_shape=[jax.ShapeDtypeStruct((n, d), F32),
                   jax.ShapeDtypeStruct((N_EXPERTS, n), F32),
                   jax.ShapeDtypeStruct((n, d // 2), I32)],
        compiler_params=_cparams(("parallel",)),
        name="mix_ln1_router",
    )(x, *prompt_parts, *step_parts, *params)


def _route_kernel(sc_ref, bias_ref, idx_ref, gate_ref, pos_ref, cnt_ref, cnt_scr):
    scores = sc_ref[...]
    biased = scores + bias_ref[...]
    tn = scores.shape[1]
    per = N_EXPERTS // N_EXPERT_GROUPS
    sub = lax.broadcasted_iota(I32, (per, tn), 0)
    grp_rows = []
    for g in range(N_EXPERT_GROUPS):
        xg = biased[g * per:(g + 1) * per, :]
        m1 = jnp.max(xg, axis=0, keepdims=True)
        first = jnp.min(jnp.where(xg == m1, sub, per), axis=0, keepdims=True)
        m2 = jnp.max(jnp.where(sub == first, -jnp.inf, xg), axis=0, keepdims=True)
        grp_rows.append(m1 + m2)
    grp = jnp.concatenate(grp_rows, axis=0)
    gi = lax.broadcasted_iota(I32, (N_EXPERT_GROUPS, tn), 0)
    gsel = jnp.zeros((N_EXPERT_GROUPS, tn), jnp.bool_)
    for _ in range(TOPK_GROUPS):
        m = jnp.max(grp, axis=0, keepdims=True)
        first = jnp.min(jnp.where(grp == m, gi, N_EXPERT_GROUPS), axis=0, keepdims=True)
        hit = gi == first
        gsel = jnp.logical_or(gsel, hit)
        grp = jnp.where(hit, -jnp.inf, grp)
    ei = lax.broadcasted_iota(I32, (N_EXPERTS, tn), 0)
    emask = jnp.concatenate(
        [jnp.broadcast_to(gsel[g:g + 1, :], (per, tn)) for g in range(N_EXPERT_GROUPS)], axis=0)
    cand = jnp.where(emask, biased, -jnp.inf)
    idxs, gates, hits = [], [], []
    for _ in range(TOP_K):
        m = jnp.max(cand, axis=0, keepdims=True)
        first = jnp.min(jnp.where(cand == m, ei, N_EXPERTS), axis=0, keepdims=True)
        hit = ei == first
        idxs.append(first)
        hits.append(hit)
        gates.append(jnp.sum(jnp.where(hit, scores, 0.0), axis=0, keepdims=True))
        cand = jnp.where(hit, -jnp.inf, cand)
    gate = jnp.concatenate(gates, axis=0)
    gate = gate / jnp.sum(gate, axis=0, keepdims=True) * ROUTED_SCALE
    idx_ref[...] = jnp.concatenate(idxs, axis=0)
    gate_ref[...] = gate
    chosen = hits[0]
    for hit in hits[1:]:
        chosen = jnp.logical_or(chosen, hit)
    onehot = jnp.where(chosen, 1.0, 0.0)
    ta = lax.broadcasted_iota(I32, (tn, tn), 0)
    tb = lax.broadcasted_iota(I32, (tn, tn), 1)
    prefix = _dot(onehot.astype(BF16), (ta < tb).astype(BF16))

    @pl.when(pl.program_id(0) == 0)
    def _():
        cnt_scr[...] = jnp.zeros_like(cnt_scr)

    rank = prefix + cnt_scr[:, 0:1]
    pos_ref[...] = jnp.concatenate(
        [jnp.sum(jnp.where(hit, rank, 0.0), axis=0, keepdims=True) for hit in hits], axis=0).astype(I32)
    cnt_scr[...] = cnt_scr[...] + jnp.sum(onehot, axis=1, keepdims=True)
    cnt_ref[...] = cnt_scr[...].astype(I32)


def route(scores_t, e_bias):
    n = scores_t.shape[1]
    tn = ROW_TILE
    tok = pl.BlockSpec((TOP_K, tn), lambda i: (0, i))
    return pl.pallas_call(
        _route_kernel,
        grid=(n // tn,),
        in_specs=[pl.BlockSpec((N_EXPERTS, tn), lambda i: (0, i)),
                  pl.BlockSpec((N_EXPERTS, 1), lambda i: (0, 0))],
        out_specs=[tok, tok, tok, pl.BlockSpec((N_EXPERTS, LANES), lambda i: (0, 0))],
        out_shape=[jax.ShapeDtypeStruct((TOP_K, n), I32), jax.ShapeDtypeStruct((TOP_K, n), F32),
                   jax.ShapeDtypeStruct((TOP_K, n), I32),
                   jax.ShapeDtypeStruct((N_EXPERTS, LANES), I32)],
        scratch_shapes=[pltpu.VMEM((N_EXPERTS, LANES), F32)],
        compiler_params=_cparams(("arbitrary",)),
        name="route",
    )(scores_t, e_bias)


def _dispatch_kernel(nb, dest_ref, segend_ref, cnt_ref, nu_ref, x_ref, inv0_ref, o_ref, invo_ref,
                     inv_ref, zbuf, sem, zsem, isem):
    tm = x_ref.shape[0]
    row0 = pl.program_id(0) * tm

    @pl.when(pl.program_id(0) == 0)
    def _():
        c = pltpu.make_async_copy(inv0_ref, inv_ref, isem)
        c.start()
        c.wait()

    def fill(start):
        return pltpu.make_async_copy(zbuf, o_ref.at[pl.ds(pl.multiple_of(start, EXPERT_TILE),
                                                          EXPERT_TILE)], zsem)

    @pl.when(pl.program_id(0) == 0)
    def _():
        zbuf[...] = jnp.zeros_like(zbuf)

        def each_expert(fn):
            def body(e, carry):
                @pl.when(cnt_ref[e] > 0)
                def _():
                    fn(fill(segend_ref[e] - EXPERT_TILE))
                return carry
            lax.fori_loop(0, N_EXPERTS, body, 0)

        def each_free_block(fn):
            def body(b, carry):
                fn(fill(b * EXPERT_TILE))
                return carry
            lax.fori_loop(nu_ref[0], nb, body, 0)

        each_expert(lambda c: c.start())
        each_free_block(lambda c: c.start())
        each_expert(lambda c: c.wait())
        each_free_block(lambda c: c.wait())

    def start(i, carry):
        for j in range(TOP_K):
            d = dest_ref[i * TOP_K + j]
            inv_ref[d] = (row0 + i) * TOP_K + j
            pltpu.make_async_copy(x_ref.at[pl.ds(i, 1)], o_ref.at[pl.ds(d, 1)], sem).start()
        return carry

    lax.fori_loop(0, tm, start, 0)
    for j in range(TOP_K):
        pltpu.make_async_copy(x_ref, o_ref.at[pl.ds(0, tm)], sem).wait()

    @pl.when(pl.program_id(0) == pl.num_programs(0) - 1)
    def _():
        c = pltpu.make_async_copy(inv_ref, invo_ref, isem)
        c.start()
        c.wait()


def moe_dispatch(dest_flat, seg_end, counts, n_used, xpk, inv_default, nb):
    n, w = xpk.shape
    tm = ROW_TILE
    rows = nb * EXPERT_TILE
    smem = lambda: pl.BlockSpec(memory_space=pltpu.SMEM)
    hbm = lambda: pl.BlockSpec(memory_space=pl.ANY)
    return pl.pallas_call(
        functools.partial(_dispatch_kernel, nb),
        grid=(n // tm,),
        in_specs=[pl.BlockSpec((tm * TOP_K,), lambda i: (i,), memory_space=pltpu.SMEM),
                  smem(), smem(), smem(),
                  pl.BlockSpec((tm, w), lambda i: (i, 0)), hbm()],
        out_specs=[hbm(), hbm()],
        out_shape=[jax.ShapeDtypeStruct((rows, w), I32), jax.ShapeDtypeStruct((rows,), I32)],
        scratch_shapes=[pltpu.SMEM((rows,), I32), pltpu.VMEM((EXPERT_TILE, w), I32),
                        pltpu.SemaphoreType.DMA(()), pltpu.SemaphoreType.DMA(()),
                        pltpu.SemaphoreType.DMA(())],
        compiler_params=_cparams(("arbitrary",)),
        name="moe_dispatch",
    )(dest_flat, seg_end, counts, n_used, xpk, inv_default)


def _experts_kernel(n_slots, nb, be_ref, nxt_ref, slot_ref, nu_ref, inv_ref, x_ref, wg_ref, wu_ref,
                    wd_ref, o_ref, wg_f, wu_f, wd_f, wg_s, wu_s, wd_s, obuf_a, obuf_b, sems, osems):
    i = pl.program_id(0)
    n_used = nu_ref[0]
    used = i < n_used
    prev = be_ref[jnp.maximum(i - 1, 0)]
    fresh = jnp.logical_and(used, jnp.logical_or(i == 0, be_ref[i] != prev))
    bm = x_ref.shape[0]

    obufs = (obuf_a, obuf_b)

    def scatter_rows(blk, par):
        for r in range(bm):
            pltpu.make_async_copy(obufs[par].at[pl.ds(r, 1)],
                                  o_ref.at[pl.ds(inv_ref[blk * bm + r], 1)], osems.at[par]).start()

    def wait_rows(par):
        pltpu.make_async_copy(obufs[par], o_ref.at[pl.ds(0, bm)], osems.at[par]).wait()

    def by_parity(cond, fn):
        for par in range(2):
            @pl.when(jnp.logical_and(cond, i % 2 == par))
            def _(par=par):
                fn(par)

    @pl.when(i == 0)
    def _():
        obuf_b[...] = jnp.zeros_like(obuf_b)
        spare = [pltpu.make_async_copy(obuf_b, o_ref.at[pl.ds(n_slots + e * bm, bm)], osems.at[1])
                 for e in range(N_EXPERTS)]
        for c in spare:
            c.start()
        for c in spare:
            c.wait()

    by_parity(jnp.logical_and(i >= 2, i - 2 < n_used), wait_rows)

    def weight_copies(e, slot):
        return (pltpu.make_async_copy(wg_ref.at[e], wg_f.at[slot], sems.at[slot, 0]),
                pltpu.make_async_copy(wu_ref.at[e], wu_f.at[slot], sems.at[slot, 1]),
                pltpu.make_async_copy(wd_ref.at[e], wd_f.at[slot], sems.at[slot, 2]))

    @pl.when(jnp.logical_and(used, i == 0))
    def _():
        for c in weight_copies(be_ref[0], 0):
            c.start()

    @pl.when(fresh)
    def _():
        slot = slot_ref[i]
        for c in weight_copies(be_ref[i], slot):
            c.wait()

        @pl.when(nxt_ref[i] >= 0)
        def _():
            for c in weight_copies(nxt_ref[i], 1 - slot):
                c.start()

        wg_s[...] = wg_f[slot].astype(BF16)
        wu_s[...] = wu_f[slot].astype(BF16)
        wd_s[...] = wd_f[slot].astype(BF16)

    def compute(par):
        hi, lo = _unpack_bf16_pairs(x_ref[...])
        hi, lo = hi.astype(BF16), lo.astype(BF16)
        half = hi.shape[1]
        gp = _dot(hi, wg_s[:half, :]) + _dot(lo, wg_s[half:, :])
        up = _dot(hi, wu_s[:half, :]) + _dot(lo, wu_s[half:, :])
        act = gp * _sigmoid(gp) * up
        obufs[par][...] = _pack_bf16_pairs(_dot(act.astype(BF16), wd_s[...]))

    @pl.when(jnp.logical_and(used, i == 0))
    def _():
        compute(0)

    def send_prev_and_compute(par):
        scatter_rows(i - 1, 1 - par)
        compute(par)

    by_parity(jnp.logical_and(used, i > 0), send_prev_and_compute)
    by_parity(jnp.logical_and(i == n_used, i > 0), lambda par: scatter_rows(i - 1, 1 - par))

    @pl.when(jnp.logical_and(i == nb - 1, n_used == nb - 1))
    def _():
        wait_rows((nb - 2) % 2)


def moe_experts(blk_e, nxt_e, slot, n_used, inv, xs, n_slots, w_gate, w_up, w_down):
    w = xs.shape[1]
    nb = xs.shape[0] // EXPERT_TILE
    _, d, de = w_gate.shape
    grid_spec = pltpu.PrefetchScalarGridSpec(
        num_scalar_prefetch=5,
        grid=(nb,),
        in_specs=[pl.BlockSpec((EXPERT_TILE, w), lambda i, be, nx, sl, nu, iv: (jnp.minimum(i, nu[0] - 1), 0)),
                  pl.BlockSpec(memory_space=pl.ANY), pl.BlockSpec(memory_space=pl.ANY),
                  pl.BlockSpec(memory_space=pl.ANY)],
        out_specs=pl.BlockSpec(memory_space=pl.ANY),
        scratch_shapes=[pltpu.VMEM((2, d, de), F32), pltpu.VMEM((2, d, de), F32),
                        pltpu.VMEM((2, de, d), F32),
                        pltpu.VMEM((d, de), BF16), pltpu.VMEM((d, de), BF16),
                        pltpu.VMEM((de, d), BF16), pltpu.VMEM((EXPERT_TILE, w), I32),
                        pltpu.VMEM((EXPERT_TILE, w), I32),
                        pltpu.SemaphoreType.DMA((2, 3)), pltpu.SemaphoreType.DMA((2,))],
    )
    return pl.pallas_call(
        functools.partial(_experts_kernel, n_slots, nb),
        grid_spec=grid_spec,
        out_shape=jax.ShapeDtypeStruct((n_slots + N_EXPERTS * EXPERT_TILE, w), I32),
        compiler_params=_cparams(("arbitrary",)),
        name="moe_experts",
    )(blk_e, nxt_e, slot, n_used, inv, xs, w_gate, w_up, w_down)


def _combine_kernel(alpha, h_ref, gate_ref, ys_ref, wsg_ref, wsu_ref, wsd_ref, l2g_ref, l2b_ref, o_ref):
    tm = h_ref.shape[0]
    h = h_ref[...]
    hb = h.astype(BF16)
    gp = _dot(hb, wsg_ref[...])
    up = _dot(hb, wsu_ref[...])
    shared = _dot((gp * _sigmoid(gp) * up).astype(BF16), wsd_ref[...])
    row = lax.broadcasted_iota(I32, (tm, tm * TOP_K), 0)
    col = lax.broadcasted_iota(I32, (tm, tm * TOP_K), 1)
    g_hi, g_lo = _split_bf16(jnp.where(col // TOP_K == row, gate_ref[0], 0.0))
    hi, lo = _unpack_bf16_pairs(ys_ref[...])
    hi, lo = hi.astype(BF16), lo.astype(BF16)
    routed = jnp.concatenate([_dot(g_hi, hi) + _dot(g_lo, hi), _dot(g_hi, lo) + _dot(g_lo, lo)], axis=1)
    o_ref[...] = _layer_norm(alpha * h + routed + shared, l2g_ref[...], l2b_ref[...])


def moe_combine(h1, gate_rows, ys, pw, alpha):
    n, d = h1.shape
    tm = gate_rows.shape[2] // TOP_K
    full = lambda a: pl.BlockSpec(a.shape, lambda i: (0,) * a.ndim)
    params = (pw["ws_gate"], pw["ws_up"], pw["ws_down"], pw["ln2_g"], pw["ln2_b"])
    return pl.pallas_call(
        functools.partial(_combine_kernel, alpha),
        grid=(n // tm,),
        in_specs=[pl.BlockSpec((tm, d), lambda i: (i, 0)),
                  pl.BlockSpec((1, 1, tm * TOP_K), lambda i: (i, 0, 0)),
                  pl.BlockSpec((tm * TOP_K, d // 2), lambda i: (i, 0))]
                 + [full(a) for a in params],
        out_specs=pl.BlockSpec((tm, d), lambda i: (i, 0)),
        out_shape=jax.ShapeDtypeStruct((n, d), F32),
        compiler_params=_cparams(("parallel",)),
        name="moe_combine",
    )(h1, gate_rows, ys, *params)


def _round_up(x, m):
    return (x + m - 1) // m * m


def _rope_tables(pos, head):
    half = head // 2
    inv = ROPE_THETA ** (-jnp.arange(half, dtype=F32) / half)
    ang = pos.astype(F32)[:, None] * inv[None, :]
    cos, sin = jnp.cos(ang), jnp.sin(ang)
    rep = LANES // head
    c = jnp.tile(jnp.concatenate([cos, cos], axis=1), (1, rep))
    s = jnp.tile(jnp.concatenate([-sin, sin], axis=1), (1, rep))
    return c, s


def _permute_cols(m, axis=-1):
    axis = axis % m.ndim
    cut = lambda lo, hi: lax.slice_in_dim(m, lo, hi, axis=axis)
    a0 = SHIFT_DIM
    i0 = a0 + A_DIM + 2 * A_KV_DIM + IDX_HEADS * IDX_DIM

    def pad(w):
        shape = list(m.shape)
        shape[axis] = w
        return jnp.zeros(shape, m.dtype)

    pieces = [
        cut(0, 3 * R_DIM),
        cut(a0, a0 + A_DIM),
        cut(a0 + A_DIM + 2 * A_KV_DIM, i0),
        cut(a0 + A_DIM, a0 + A_DIM + 2 * A_KV_DIM),
        cut(i0, i0 + IDX_DIM + IDX_HEADS),
        pad(LANES - IDX_DIM - IDX_HEADS),
        cut(3 * R_DIM, SHIFT_DIM),
        pad(LORA_W - (SHIFT_DIM - 3 * R_DIM)),
    ]
    return jnp.concatenate(pieces, axis=axis)


def kernel(x_prompt, x_sample, cache_k, cache_v, cache_idx_k, state_wkv, state_shift, page_table,
           meta, ln0_g, ln0_b, w_in, mu_shift, w0, w_b, a0, a_b, g_b, k_k, k_a, r_k, gn_g, gn_b,
           w_out, ln1_g, ln1_b, w_router, e_bias, w_gate, w_up, w_down, ws_gate, ws_up, ws_down,
           ln2_g, ln2_b):
    depth = w_in.shape[0]
    assert depth == 1, "single trunk layer"
    bsz, s_p, d = x_prompt.shape
    s_dec, s_s, _ = x_sample.shape
    assert s_s == 1, "one decode token per sequence"
    t_real = N_META + s_p
    tp = _round_up(t_real, LANES)
    assert (bsz * tp) % ROW_TILE == 0
    sp = _round_up(s_dec, ROW_TILE)
    n_prompt = bsz * tp
    n = n_prompt + sp
    n_pool, page = cache_k.shape[1], cache_k.shape[2]
    n_pages = page_table.shape[1]
    past = n_pages * page
    alpha = float((2 * depth) ** 0.25)
    row2 = lambda a: a.reshape(1, -1)

    meta_rows = jnp.broadcast_to(meta[None], (bsz, N_META, d))
    xp = jnp.concatenate([meta_rows, x_prompt, jnp.zeros((bsz, tp - t_real, d), F32)], axis=1)
    x_all = jnp.concatenate([xp.reshape(n_prompt, d), x_sample.reshape(s_dec, d),
                             jnp.zeros((sp - s_dec, d), F32)], axis=0)
    pos = jnp.concatenate([jnp.tile(jnp.arange(tp), bsz), jnp.full((sp,), past)])
    c128, s128 = _rope_tables(pos, A_HEAD)
    c64, s64 = _rope_tables(pos, IDX_DIM)

    w_in_t = _permute_cols(w_in[0].T.astype(BF16), axis=0)
    mu_k = _permute_cols(
        jnp.concatenate([mu_shift[0], jnp.zeros((w_in.shape[2] - SHIFT_DIM,), F32)])[None, :])
    head_of = jnp.arange(R_DIM) // R_HEAD
    e_mat = (head_of[:, None] == jnp.arange(R_HEADS)[None, :]).astype(F32)
    zpad = lambda a, rows_before, rows_total: jnp.concatenate(
        [jnp.zeros((rows_before, a.shape[1]), a.dtype), a,
         jnp.zeros((rows_total - rows_before - a.shape[0], a.shape[1]), a.dtype)], axis=0)
    pw = {
        "mu_x": mu_k[:, :3 * R_DIM], "mu_lo": mu_k[:, C_LORA:],
        "w0": row2(w0[0]), "a0": row2(a0[0]), "k_k": row2(k_k[0]), "k_a": row2(k_a[0]),
        "r_k": row2(r_k[0]), "gn_g": row2(gn_g[0]), "gn_b": row2(gn_b[0]),
        "w_b": zpad(w_b[0], 0, LANES).astype(BF16),
        "a_b": zpad(a_b[0], D_DECAY_LORA, LANES).astype(BF16),
        "g_b": zpad(g_b[0], 0, LORA_W - LANES).astype(BF16),
        "e": e_mat.astype(BF16), "et": e_mat.T.astype(BF16),
        "ln0_g": row2(ln0_g), "ln0_b": row2(ln0_b),
        "ln1_g": row2(ln1_g[0]), "ln1_b": row2(ln1_b[0]),
        "ln2_g": row2(ln2_g[0]), "ln2_b": row2(ln2_b[0]),
        "w_out": w_out[0].astype(BF16), "w_router_t": w_router[0].T,
        "ws_gate": ws_gate[0].astype(BF16), "ws_up": ws_up[0].astype(BF16),
        "ws_down": ws_down[0].astype(BF16),
    }

    p = ln_proj(x_all, pw["ln0_g"], pw["ln0_b"], w_in_t, tn=P_COLS // 3)
    q_r, iq_r, k_r, ik_r, iw = rope_all(p, c128, s128, c64, s64)

    pre_p = rwkv_pre(p, 0, n_prompt, None, pw, t_real, tp)
    shift_k = _permute_cols(jnp.concatenate(
        [state_shift[0], jnp.zeros((s_dec, w_in.shape[2] - SHIFT_DIM), F32)], axis=1))
    shift_k = jnp.concatenate([shift_k, jnp.zeros((sp - s_dec, P_COLS), F32)], axis=0)
    pre_s = rwkv_pre(p, n_prompt, sp, (shift_k[:, :3 * R_DIM], shift_k[:, C_LORA:]), pw, t_real, tp)
    r_p, ld_p, k_p, v_p, kk_p, b_p, g_p, bon_p = pre_p
    r_s, ld_s, k_s, v_s, kk_s, b_s, g_s, bon_s = pre_s
    y_p, wkv_p = rwkv_scan(r_p, ld_p, k_p, v_p, kk_p, b_p, bsz, tp)
    heads = lambda a: a[:s_dec].reshape(s_dec, R_HEADS, R_HEAD).transpose(1, 2, 0)
    y_hs, wkv_hs = rwkv_step(heads(r_s), heads(ld_s), heads(k_s), heads(v_s), heads(kk_s), heads(b_s),
                             state_wkv[0].transpose(1, 2, 3, 0))
    wkv_s = wkv_hs.transpose(3, 0, 1, 2)
    y_s = jnp.concatenate([y_hs.transpose(2, 0, 1).reshape(s_dec, R_DIM),
                           jnp.zeros((sp - s_dec, R_DIM), F32)], axis=0)

    n_sel_p = min(TOPK_KEYS, t_real // 4)
    a_p = dsa_prompt(q_r, iq_r, iw, ik_r, k_r, p, bsz, tp, n_sel_p)
    n_sel_s = min(TOPK_KEYS, (past + 1) // 4)
    pt_flat = page_table.reshape(-1).astype(I32)
    srow = slice(n_prompt, n_prompt + s_dec)
    sc_s = dsa_step_scores(pt_flat, iq_r[srow].reshape(s_dec, IDX_HEADS, IDX_DIM),
                           iw[srow].reshape(s_dec, IDX_HEADS, 1), ik_r[srow].reshape(s_dec, 1, IDX_DIM),
                           cache_idx_k[0].transpose(0, 2, 1).reshape(n_pool * IDX_DIM, page),
                           n_pages, page)
    sel_s = dsa_step_select(sc_s.reshape(s_dec, -1), n_sel_s, past).reshape(sc_s.shape)
    slot = jnp.arange(page)[:, None]
    expand = (jnp.arange(page * A_KV_HEADS)[None, :] // A_KV_HEADS == slot).astype(BF16)
    a_s = dsa_step_attn(pt_flat, q_r[srow].reshape(s_dec, A_HEADS, A_HEAD),
                        k_r[srow].reshape(s_dec, A_KV_HEADS, A_HEAD),
                        p[srow, C_VA:C_VA + A_KV_DIM].reshape(s_dec, A_KV_HEADS, A_HEAD),
                        sel_s, expand,
                        cache_k[0].reshape(n_pool * page * A_KV_HEADS, A_HEAD),
                        cache_v[0].reshape(n_pool * page * A_KV_HEADS, A_HEAD), n_pages, page)
    a_s = jnp.concatenate([a_s.reshape(s_dec, A_DIM), jnp.zeros((sp - s_dec, A_DIM), F32)], axis=0)

    h1, scores_t, xpk = mix_ln1_router(x_all, (y_p, bon_p, g_p, a_p), (y_s, bon_s, g_s, a_s), pw, alpha)
    eidx_t, gate_t, pos_t, counts = route(scores_t, e_bias[0].reshape(N_EXPERTS, 1))

    n_slots = n * TOP_K
    nb = (n_slots + N_EXPERTS * (EXPERT_TILE - 1)) // EXPERT_TILE + 1
    counts = counts[:, 0]
    padded = (counts + EXPERT_TILE - 1) // EXPERT_TILE * EXPERT_TILE
    seg_end = jnp.cumsum(padded).astype(I32)
    seg_start = seg_end - padded
    experts = jnp.arange(N_EXPERTS)
    start_of = jnp.sum(jnp.where(eidx_t[:, :, None] == experts, seg_start, 0), axis=-1)
    dest = (start_of + pos_t).T.astype(I32).reshape(-1)
    blk_row = jnp.arange(nb) * EXPERT_TILE
    blk_e = jnp.minimum(jnp.sum(seg_end[None, :] <= blk_row[:, None], axis=1),
                        N_EXPERTS - 1).astype(I32)
    n_used = (seg_end[-1] // EXPERT_TILE).astype(I32)
    run_start = jnp.concatenate([jnp.ones((1,), I32), (blk_e[1:] != blk_e[:-1]).astype(I32)])
    slot = ((jnp.cumsum(run_start) - 1) % 2).astype(I32)
    run_end = seg_end[blk_e] // EXPERT_TILE
    nxt_e = jnp.where(run_end < n_used, blk_e[jnp.minimum(run_end, nb - 1)], -1).astype(I32)
    n_used = n_used.reshape(1)
    spare = (n_slots + blk_e[:, None] * EXPERT_TILE + jnp.arange(EXPERT_TILE)[None, :]).astype(I32)
    xs, inv = moe_dispatch(dest, seg_end, counts.astype(I32), n_used, xpk, spare.reshape(-1), nb)
    ys = moe_experts(blk_e, nxt_e, slot, n_used, inv, xs, n_slots, w_gate[0], w_up[0], w_down[0])
    tc = Q_TILE
    h2 = moe_combine(h1, gate_t.T.reshape(n // tc, 1, tc * TOP_K), ys, pw, alpha)

    def prompt_rows(a):
        return a[:n_prompt].reshape(bsz, tp, -1)[:, :t_real]

    y_prompt = h2[:n_prompt].reshape(bsz, tp, d)[:, N_META:t_real]
    y_sample = h2[srow].reshape(s_dec, 1, d)
    k_prompt = prompt_rows(k_r).reshape(1, bsz, t_real, A_KV_HEADS, A_HEAD)
    v_prompt = prompt_rows(p[:, C_VA:C_VA + A_KV_DIM]).reshape(1, bsz, t_real, A_KV_HEADS, A_HEAD)
    ik_prompt = prompt_rows(ik_r)[None]
    last = jnp.arange(bsz) * tp + t_real - 1
    unperm = lambda rows: jnp.concatenate([rows[:, :3 * R_DIM],
                                           rows[:, C_LORA:C_LORA + SHIFT_DIM - 3 * R_DIM]], axis=1)
    shift_prompt = unperm(p[last])[None]
    k_sample = k_r[srow].reshape(1, s_dec, 1, A_KV_HEADS, A_HEAD)
    v_sample = p[srow, C_VA:C_VA + A_KV_DIM].reshape(1, s_dec, 1, A_KV_HEADS, A_HEAD)
    ik_sample = ik_r[srow].reshape(1, s_dec, 1, IDX_DIM)
    shift_sample = unperm(p[srow])[None]
    return (y_prompt, y_sample, k_prompt, v_prompt, ik_prompt, wkv_p[None], shift_prompt,
            k_sample, v_sample, ik_sample, wkv_s[None], shift_sample)
```

```python
import functools

import numpy as np
import jax
import jax.numpy as jnp
from jax import lax
from jax.experimental import pallas as pl
from jax.experimental.pallas import tpu as pltpu

F32 = jnp.float32
BF16 = jnp.bfloat16
I32 = jnp.int32
HIGHEST = lax.Precision.HIGHEST

N_META = 16
R_HEADS, R_HEAD = 16, 64
R_DIM = R_HEADS * R_HEAD
D_DECAY_LORA, D_AAA_LORA, D_GATE_LORA = 64, 64, 160
SHIFT_DIM = 3 * R_DIM + D_DECAY_LORA + D_AAA_LORA + D_GATE_LORA
GN_EPS = 64e-5
A_HEADS, A_KV_HEADS, A_HEAD = 8, 2, 128
A_DIM = A_HEADS * A_HEAD
A_KV_DIM = A_KV_HEADS * A_HEAD
IDX_HEADS, IDX_DIM = 16, 64
TOPK_KEYS = 256
ROPE_THETA = 10000.0
N_EXPERTS, N_EXPERT_GROUPS, TOPK_GROUPS, TOP_K = 64, 8, 4, 8
ROUTED_SCALE = 2.5
LN_EPS = 1e-5

LANES = 128
SUBLANES = 8
ROW_TILE = 256
Q_TILE = 128
CHUNK = 64
EXPERT_TILE = 256
VMEM_LIMIT = 56 * 1024 * 1024
NEG_BIG = -1e30
INT_MIN = -2 ** 31

C_R, C_K, C_V = 0, R_DIM, 2 * R_DIM
C_Q = 3 * R_DIM
C_IQ = C_Q + A_DIM
C_KA = C_IQ + IDX_HEADS * IDX_DIM
C_VA = C_KA + A_KV_DIM
C_IK = C_VA + A_KV_DIM
C_LORA = C_IK + LANES
LORA_W = 384
P_COLS = C_LORA + LORA_W


def _cparams(sem):
    return pltpu.CompilerParams(dimension_semantics=sem, vmem_limit_bytes=VMEM_LIMIT)


def _dot(a, b, precision=None):
    return jnp.dot(a, b, preferred_element_type=F32, precision=precision)


def _dot_nt(a, b, precision=None):
    return lax.dot_general(a, b, (((1,), (1,)), ((), ())), preferred_element_type=F32,
                           precision=precision)


def _dot_tn(a, b, precision=None):
    return lax.dot_general(a, b, (((0,), (0,)), ((), ())), preferred_element_type=F32,
                           precision=precision)


def _split_bf16(x):
    hi = x.astype(BF16)
    return hi, (x - hi.astype(F32)).astype(BF16)


def _dot_f32_by_bf16(a, b):
    hi, lo = _split_bf16(a)
    return _dot(hi, b) + _dot(lo, b)


def _head_sums(x, e, et):
    return _dot_f32_by_bf16(_dot_f32_by_bf16(x, e), et)


def _layer_norm(x, g, b):
    mu = jnp.mean(x, axis=-1, keepdims=True)
    xc = x - mu
    var = jnp.mean(xc * xc, axis=-1, keepdims=True)
    return xc * lax.rsqrt(var + LN_EPS) * g + b


def _sigmoid(z):
    return 1.0 / (1.0 + jnp.exp(-z))


def _ln_proj_kernel(x0_ref, xn_ref, g_ref, b_ref, w_ref, o_ref, h_even, h_odd):
    i = pl.program_id(1)
    norm = lambda ref: _layer_norm(ref[...], g_ref[...], b_ref[...]).astype(BF16)

    @pl.when(i == 0)
    def _():
        h_even[...] = norm(x0_ref)

    for par, (cur, nxt) in enumerate(((h_even, h_odd), (h_odd, h_even))):
        @pl.when(i % 2 == par)
        def _(cur=cur, nxt=nxt):
            o_ref[...] = _dot_nt(cur[...], w_ref[...])
            nxt[...] = norm(xn_ref)


def ln_proj(x, g, b, w_t, tn):
    n, d = x.shape
    cols = w_t.shape[0]
    ni = n // ROW_TILE
    return pl.pallas_call(
        _ln_proj_kernel,
        grid=(cols // tn, ni),
        in_specs=[
            pl.BlockSpec((ROW_TILE, d), lambda j, i: (0, 0)),
            pl.BlockSpec((ROW_TILE, d), lambda j, i: ((i + 1) % ni, 0)),
            pl.BlockSpec((1, d), lambda j, i: (0, 0)),
            pl.BlockSpec((1, d), lambda j, i: (0, 0)),
            pl.BlockSpec((tn, d), lambda j, i: (j, 0)),
        ],
        out_specs=pl.BlockSpec((ROW_TILE, tn), lambda j, i: (i, j)),
        out_shape=jax.ShapeDtypeStruct((n, cols), F32),
        scratch_shapes=[pltpu.VMEM((ROW_TILE, d), BF16), pltpu.VMEM((ROW_TILE, d), BF16)],
        compiler_params=_cparams(("arbitrary", "arbitrary")),
        name="ln_proj",
    )(x, x, g, b, w_t)


def _rot_half(x, head):
    w = x.shape[-1]
    half = head // 2
    lane = lax.broadcasted_iota(I32, x.shape, 1)
    left = pltpu.roll(x, w - half, axis=1)
    right = pltpu.roll(x, half, axis=1)
    return jnp.where((lane % head) < half, left, right)


def _rope_kernel(q_ref, iq_ref, ka_ref, ikw_ref, c128_ref, s128_ref, c64_ref, s64_ref,
                 qo_ref, iqo_ref, ko_ref, iko_ref, iwo_ref, kt_ref, ikt_ref):
    c128, s128 = c128_ref[...], s128_ref[...]
    c64, s64 = c64_ref[...], s64_ref[...]

    def rope(x, head, c, s):
        rep = x.shape[-1] // LANES
        if rep > 1:
            c = jnp.concatenate([c] * rep, axis=1)
            s = jnp.concatenate([s] * rep, axis=1)
        return x * c + _rot_half(x, head) * s

    q = rope(q_ref[...], A_HEAD, c128, s128)
    qo_ref[...] = (q * (A_HEAD ** -0.5)).astype(BF16)
    iqo_ref[...] = rope(iq_ref[...], IDX_DIM, c64, s64).astype(BF16)
    k = rope(ka_ref[...], A_HEAD, c128, s128)
    ko_ref[...] = k
    ikw = ikw_ref[...]
    ik = rope(ikw, IDX_DIM, c64, s64)
    iko_ref[...] = ik[:, :IDX_DIM]
    iwo_ref[...] = ikw[:, IDX_DIM:IDX_DIM + IDX_HEADS]
    kt_ref[...] = k.T.astype(BF16)
    ikt_ref[...] = ik.T[:IDX_DIM, :].astype(BF16)


def rope_all(p, c128, s128, c64, s64):
    n = p.shape[0]
    tm = ROW_TILE
    row = lambda w, blk: pl.BlockSpec((tm, w), lambda i: (i, blk))
    return pl.pallas_call(
        _rope_kernel,
        grid=(n // tm,),
        in_specs=[row(A_DIM, C_Q // A_DIM), row(A_DIM, C_IQ // A_DIM),
                  row(A_KV_DIM, C_KA // A_KV_DIM), row(LANES, C_IK // LANES),
                  row(LANES, 0), row(LANES, 0), row(LANES, 0), row(LANES, 0)],
        out_specs=[row(A_DIM, 0), row(A_DIM, 0), row(A_KV_DIM, 0),
                   row(IDX_DIM, 0), row(IDX_HEADS, 0),
                   pl.BlockSpec((A_KV_DIM, tm), lambda i: (0, i)),
                   pl.BlockSpec((IDX_DIM, tm), lambda i: (0, i))],
        out_shape=[jax.ShapeDtypeStruct((n, A_DIM), BF16),
                   jax.ShapeDtypeStruct((n, IDX_HEADS * IDX_DIM), BF16),
                   jax.ShapeDtypeStruct((n, A_KV_DIM), F32),
                   jax.ShapeDtypeStruct((n, IDX_DIM), F32),
                   jax.ShapeDtypeStruct((n, IDX_HEADS), F32),
                   jax.ShapeDtypeStruct((A_KV_DIM, n), BF16),
                   jax.ShapeDtypeStruct((IDX_DIM, n), BF16)],
        compiler_params=_cparams(("parallel",)),
        name="rope",
    )(p, p, p, p, c128, s128, c64, s64)


def _rwkv_pre_kernel(t_real, tp, from_rows, *refs):
    (x_ref, lo_ref, px_ref, plo_ref, mu_ref, mulo_ref, w0_ref, wb_ref, a0_ref, ab_ref,
     gb_ref, kk_ref, ka_ref, rk_ref, e_ref, et_ref,
     r_o, ld_o, k_o, v_o, kk_o, b_o, g_o, bon_o) = refs
    x = x_ref[...]
    lo = lo_ref[...]
    tm = x.shape[0]
    if from_rows:
        i = pl.program_id(0)
        row = lax.broadcasted_iota(I32, (tm, 1), 0)
        t = (i * tm + row) % tp
        first = row == 0
        sx = jnp.where(first, px_ref[SUBLANES - 1:SUBLANES, :], pltpu.roll(x, 1, axis=0))
        slo = jnp.where(first, plo_ref[SUBLANES - 1:SUBLANES, :], pltpu.roll(lo, 1, axis=0))
        sx = jnp.where(t == 0, 0.0, sx)
        slo = jnp.where(t == 0, 0.0, slo)
        live = t < t_real
    else:
        sx = px_ref[...]
        slo = plo_ref[...]
        live = None
    xx = x + (sx - x) * mu_ref[...]
    xlo = lo + (slo - lo) * mulo_ref[...]
    r = xx[:, C_R:C_R + R_DIM]
    k = xx[:, C_K:C_K + R_DIM]
    v = xx[:, C_V:C_V + R_DIM]
    wa = xlo[:, :LANES]
    xg = xlo[:, LANES:]
    z = w0_ref[...] + _dot(jnp.tanh(wa).astype(BF16), wb_ref[...])
    nz = -z
    softplus = jnp.maximum(nz, 0.0) + jnp.log(1.0 + jnp.exp(-jnp.abs(nz)))
    logd = -jnp.exp(-softplus - 0.5)
    a = _sigmoid(a0_ref[...] + _dot(wa.astype(BF16), ab_ref[...]))
    g = _dot(_sigmoid(xg).astype(BF16), gb_ref[...])
    e, et = e_ref[...], et_ref[...]
    kkr = k * kk_ref[...]
    ss = _head_sums(kkr * kkr, e, et)
    kk = kkr / jnp.maximum(jnp.sqrt(ss), 1e-12)
    k2 = k * (1.0 + (a - 1.0) * ka_ref[...])
    bonus = _head_sums(r * k2 * rk_ref[...], e, et) * v
    b = kk * a
    if live is not None:
        zero = lambda y: jnp.where(live, y, 0.0)
        logd, k2s, vs, kk, b = zero(logd), zero(k2), zero(v), zero(kk), zero(b)
    else:
        k2s, vs = k2, v
    r_o[...] = r
    ld_o[...] = logd
    k_o[...] = k2s
    v_o[...] = vs
    kk_o[...] = kk
    b_o[...] = b
    g_o[...] = g
    bon_o[...] = bonus


def rwkv_pre(p, row0, nrows, prev, pw, t_real, tp):
    tm = min(ROW_TILE, nrows)
    blk0 = row0 // tm
    from_rows = prev is None
    xw = 3 * R_DIM
    cur_x = pl.BlockSpec((tm, xw), lambda i: (blk0 + i, 0))
    cur_lo = pl.BlockSpec((tm, LORA_W), lambda i: (blk0 + i, C_LORA // LORA_W))
    if from_rows:
        r8 = tm // SUBLANES
        prev_x = pl.BlockSpec((SUBLANES, xw), lambda i: (jnp.maximum((blk0 + i) * r8 - 1, 0), 0))
        prev_lo = pl.BlockSpec((SUBLANES, LORA_W),
                               lambda i: (jnp.maximum((blk0 + i) * r8 - 1, 0), C_LORA // LORA_W))
        prev_args = (p, p)
    else:
        prev_x = pl.BlockSpec((tm, xw), lambda i: (i, 0))
        prev_lo = pl.BlockSpec((tm, LORA_W), lambda i: (i, 0))
        prev_args = prev
    full = lambda a: pl.BlockSpec(a.shape, lambda i: (0,) * a.ndim)
    params = (pw["mu_x"], pw["mu_lo"], pw["w0"], pw["w_b"], pw["a0"], pw["a_b"], pw["g_b"],
              pw["k_k"], pw["k_a"], pw["r_k"], pw["e"], pw["et"])
    out = pl.BlockSpec((tm, R_DIM), lambda i: (i, 0))
    return pl.pallas_call(
        functools.partial(_rwkv_pre_kernel, t_real, tp, from_rows),
        grid=(nrows // tm,),
        in_specs=[cur_x, cur_lo, prev_x, prev_lo] + [full(a) for a in params],
        out_specs=[out] * 8,
        out_shape=[jax.ShapeDtypeStruct((nrows, R_DIM), F32)] * 8,
        compiler_params=_cparams(("parallel",)),
        name="rwkv_pre_rows" if from_rows else "rwkv_pre_step",
    )(p, p, *prev_args, *params)


def _rwkv_scan_kernel(r_ref, ld_ref, k_ref, v_ref, kk_ref, b_ref, y_ref, s_ref, ss_scr):
    c = pl.program_id(1)

    @pl.when(c == 0)
    def _():
        ss_scr[...] = jnp.zeros_like(ss_scr)

    n = CHUNK
    n2 = 2 * n
    pairs = R_HEADS // 2
    ld_all = ld_ref[...]
    ri = lax.broadcasted_iota(I32, (n, n), 0)
    ci = lax.broadcasted_iota(I32, (n, n), 1)
    cum_all = _dot((ci <= ri).astype(F32), ld_all, HIGHEST)
    head0 = lax.broadcasted_iota(I32, (n, LANES), 1) < R_HEAD
    r4 = lax.broadcasted_iota(I32, (2 * n2, 2 * n2), 0)
    c4 = lax.broadcasted_iota(I32, (2 * n2, 2 * n2), 1)
    tri = (c4 % n) < (r4 % n) + jnp.where(r4 < n2, 0, 1)
    re = lax.broadcasted_iota(I32, (n2, n2), 0)
    ce = lax.broadcasted_iota(I32, (n2, n2), 1)
    eye = (re == ce).astype(F32)

    def stack(x):
        return jnp.concatenate([jnp.where(head0, x, 0.0), jnp.where(head0, 0.0, x)], axis=0)

    ar, bk, v2, ss, e_last = [], [], [], [], []
    for p in range(pairs):
        sl = slice(p * LANES, (p + 1) * LANES)
        cum, ld = cum_all[:, sl], ld_all[:, sl]
        e_pos = jnp.exp(cum)
        e_neg = jnp.exp(-cum)
        at = -kk_ref[:, sl] * jnp.exp(cum - ld)
        ar.append(jnp.concatenate([stack(at), stack(r_ref[:, sl] * e_pos)], axis=0).astype(BF16))
        bk.append(jnp.concatenate([stack(b_ref[:, sl] * e_neg), stack(k_ref[:, sl] * e_neg)],
                                  axis=0).astype(BF16))
        v2.append(stack(v_ref[:, sl]).astype(BF16))
        ss.append(ss_scr[p])
        e_last.append(e_pos[n - 1:n, :])
    xy0 = [_dot_nt(ar[p], ss[p].astype(BF16)) for p in range(pairs)]
    sc = [jnp.where(tri, _dot_nt(ar[p], bk[p]), 0.0) for p in range(pairs)]
    lp = [s[:n2, :n2] for s in sc]
    t = [eye + l for l in lp]
    m = 1
    while 2 * m < n:
        lpb = [l.astype(BF16) for l in lp]
        lp = [_dot(l, l) for l in lpb]
        t = [t[p] + _dot(t[p].astype(BF16), lp[p].astype(BF16)) for p in range(pairs)]
        m *= 2
    w = [xy0[p][:n2] + _dot(sc[p][:n2, n2:].astype(BF16), v2[p]) for p in range(pairs)]
    u = [_dot(t[p].astype(BF16), w[p].astype(BF16)) for p in range(pairs)]
    uv = [jnp.concatenate([u[p].astype(BF16), v2[p]], axis=0) for p in range(pairs)]
    y = [xy0[p][n2:] + _dot(sc[p][n2:].astype(BF16), uv[p]) for p in range(pairs)]
    upd = [_dot_tn(uv[p], bk[p]) for p in range(pairs)]
    for p in range(pairs):
        y_ref[:, p * LANES:(p + 1) * LANES] = y[p][:n] + y[p][n:]
        ss_scr[p] = (ss[p] + upd[p]) * e_last[p]

    @pl.when(c == pl.num_programs(1) - 1)
    def _():
        for p in range(R_HEADS // 2):
            ss = ss_scr[p]
            s_ref[0, 2 * p] = ss[:R_HEAD, :R_HEAD]
            s_ref[0, 2 * p + 1] = ss[R_HEAD:, R_HEAD:]


def rwkv_scan(r, ld, k, v, kk, b, batch, tp):
    nchunk = tp // CHUNK
    blk = pl.BlockSpec((CHUNK, R_DIM), lambda bi, c: (bi * nchunk + c, 0))
    return pl.pallas_call(
        _rwkv_scan_kernel,
        grid=(batch, nchunk),
        in_specs=[blk] * 6,
        out_specs=[blk, pl.BlockSpec((1, R_HEADS, R_HEAD, R_HEAD), lambda bi, c: (bi, 0, 0, 0))],
        out_shape=[jax.ShapeDtypeStruct((batch * tp, R_DIM), F32),
                   jax.ShapeDtypeStruct((batch, R_HEADS, R_HEAD, R_HEAD), F32)],
        scratch_shapes=[pltpu.VMEM((R_HEADS // 2, LANES, LANES), F32)],
        compiler_params=_cparams(("parallel", "arbitrary")),
        name="rwkv_scan",
    )(r, ld, k, v, kk, b)


STEP_ROWS = 16


def _rwkv_step_kernel(r_ref, ld_ref, k_ref, v_ref, kk_ref, b_ref, s_ref, y_ref, so_ref):
    r, k = r_ref[0], k_ref[0]
    dec = jnp.exp(ld_ref[0])
    na = -kk_ref[0]
    b = b_ref[0]
    v = v_ref[0]
    ys = []
    for v0 in range(0, R_HEAD, STEP_ROWS):
        rows = range(v0, v0 + STEP_ROWS)
        s = [s_ref[0, vi] for vi in rows]
        sa = [jnp.sum(x * na, axis=0, keepdims=True) for x in s]
        s_new = [x * dec + a * b + v[vi:vi + 1, :] * k for x, a, vi in zip(s, sa, rows)]
        ys += [jnp.sum(x * r, axis=0, keepdims=True) for x in s_new]
        for x, vi in zip(s_new, rows):
            so_ref[0, vi] = x
    y_ref[0] = jnp.concatenate(ys, axis=0)


def rwkv_step(r, ld, k, v, kk, b, state):
    s = state.shape[-1]
    vec = pl.BlockSpec((1, R_HEAD, s), lambda h: (h, 0, 0))
    st = pl.BlockSpec((1, R_HEAD, R_HEAD, s), lambda h: (h, 0, 0, 0))
    return pl.pallas_call(
        _rwkv_step_kernel,
        grid=(R_HEADS,),
        in_specs=[vec] * 6 + [st],
        out_specs=[vec, st],
        out_shape=[jax.ShapeDtypeStruct((R_HEADS, R_HEAD, s), F32),
                   jax.ShapeDtypeStruct(state.shape, F32)],
        compiler_params=_cparams(("parallel",)),
        name="rwkv_step",
    )(r, ld, k, v, kk, b, state)


def _select_topk(score, allowed, n_sel):
    bits = lax.bitcast_convert_type(score, I32)
    key = jnp.where(bits < 0, bits ^ jnp.int32(0x7FFFFFFF), bits)
    key = jnp.where(allowed, key, jnp.int32(INT_MIN))
    m, w = score.shape
    one, zero = jnp.ones((), BF16), jnp.zeros((), BF16)

    def byte(shift):
        if shift == 24:
            d = lax.shift_right_arithmetic(key, 24) + 128
        else:
            d = lax.shift_right_logical(key, shift) & 255
        return d.astype(F32).astype(BF16)

    def count(flags):
        acc = flags[:, :LANES]
        for c in range(LANES, w, LANES):
            acc = acc + flags[:, c:c + LANES]
        return jnp.sum(acc.astype(F32), axis=1, keepdims=True)

    need = jnp.full((m, 1), float(n_sel), F32)
    x = byte(24)
    tau = jnp.zeros((m, 1), I32)
    for shift in (24, 16, 8, 0):
        def body(it, t, x=x, need=need):
            step = lax.shift_left(jnp.int32(1), jnp.int32(6) - 2 * it).astype(F32)
            cnts = [count(jnp.where(x >= (t + mult * step).astype(BF16), one, zero))
                    for mult in (1.0, 2.0, 3.0)]
            hits = sum(jnp.where(cnt >= need, 1.0, 0.0) for cnt in cnts)
            return t + hits * step

        t = lax.fori_loop(0, 4, body, jnp.zeros((m, 1), F32))
        tb = t.astype(BF16)
        need = need - count(jnp.where(x > tb, one, zero))
        digit = t.astype(I32) - (128 if shift == 24 else 0)
        tau = tau | lax.shift_left(digit, shift)
        if shift:
            x = jnp.where(x == tb, byte(shift - 8), -one)
    return jnp.logical_and(key >= tau, allowed)


KEY_TILE = 256
Q_TILES_PER_EXTENT = 3


def _dsa_prompt_block(n_sel, tk, i, q_ref, iq_ref, iw_ref, ikt_ref, kt_ref, v_ref, o_ref, sc_ref):
    tq = q_ref.shape[0]
    iw = iw_ref[...] * ((IDX_HEADS * IDX_DIM) ** -0.5)
    iq = iq_ref[...]
    iq_h = [iq[:, h * IDX_DIM:(h + 1) * IDX_DIM] for h in range(IDX_HEADS)]
    iw_h = [iw[:, h:h + 1] for h in range(IDX_HEADS)]
    for c0 in range(0, tk, KEY_TILE):
        c1 = min(c0 + KEY_TILE, tk)
        ikb = ikt_ref[:, c0:c1]
        acc = jnp.maximum(_dot(iq_h[0], ikb), 0.0) * iw_h[0]
        for h in range(1, IDX_HEADS):
            acc = acc + jnp.maximum(_dot(iq_h[h], ikb), 0.0) * iw_h[h]
        sc_ref[:, c0:c1] = acc
    qpos = i * tq + lax.broadcasted_iota(I32, (tq, 1), 0)
    kpos = lax.broadcasted_iota(I32, (1, tk), 1)
    sel = _select_topk(sc_ref[:, :tk], kpos <= qpos, n_sel)
    bias = jnp.where(sel, 0.0, NEG_BIG)
    q = q_ref[...]
    rep = A_HEADS // A_KV_HEADS
    for g in range(A_KV_HEADS):
        kg = kt_ref[g * A_HEAD:(g + 1) * A_HEAD, :tk]
        vg = v_ref[:tk, g * A_HEAD:(g + 1) * A_HEAD].astype(BF16)
        for rr in range(rep):
            h = g * rep + rr
            s = _dot(q[:, h * A_HEAD:(h + 1) * A_HEAD], kg) + bias
            m = jnp.max(s, axis=1, keepdims=True)
            p = jnp.exp(s - m)
            l = jnp.sum(p, axis=1, keepdims=True)
            o_ref[:, h * A_HEAD:(h + 1) * A_HEAD] = _dot(p.astype(BF16), vg) / l


def _dsa_prompt_kernel(n_sel, *refs):
    i = pl.program_id(1)
    tq = refs[0].shape[0]
    tp = refs[5].shape[0]
    nq = tp // tq
    for lo in range(0, nq, Q_TILES_PER_EXTENT):
        hi = min(lo + Q_TILES_PER_EXTENT, nq)

        @pl.when(jnp.logical_and(i >= lo, i < hi))
        def _(hi=hi):
            _dsa_prompt_block(n_sel, hi * tq, i, *refs)


def dsa_prompt(q, iq, iw, ikt, kt, p, batch, tp, n_sel):
    nq = tp // Q_TILE
    qrow = lambda w: pl.BlockSpec((Q_TILE, w), lambda b, i: (b * nq + i, 0))
    keys = lambda w, blk: pl.BlockSpec((tp, w), lambda b, i: (b, blk))
    keys_t = lambda w: pl.BlockSpec((w, tp), lambda b, i: (0, b))
    return pl.pallas_call(
        functools.partial(_dsa_prompt_kernel, n_sel),
        grid=(batch, nq),
        in_specs=[qrow(A_DIM), qrow(IDX_HEADS * IDX_DIM), qrow(IDX_HEADS),
                  keys_t(IDX_DIM), keys_t(A_KV_DIM), keys(A_KV_DIM, C_VA // A_KV_DIM)],
        out_specs=qrow(A_DIM),
        out_shape=jax.ShapeDtypeStruct((batch * tp, A_DIM), F32),
        scratch_shapes=[pltpu.VMEM((Q_TILE, tp), F32)],
        compiler_params=_cparams(("parallel", "parallel")),
        name="dsa_prompt",
    )(q, iq, iw, ikt, kt, p)


SCORE_SEQS = 4
ATTN_SEQS = 2


def _dsa_step_score_kernel(n_pages, page, pt_ref, iq_ref, iw_ref, ikn_ref, *refs):
    sb = iq_ref.shape[0]
    pages = refs[:sb * n_pages]
    o_ref = refs[sb * n_pages]
    lane = lax.broadcasted_iota(I32, (1, LANES), 1)
    for q in range(sb):
        iq = iq_ref[q]
        iw = iw_ref[q] * ((IDX_HEADS * IDX_DIM) ** -0.5)
        for j in range(n_pages):
            d = _dot(iq, pages[q * n_pages + j][...].astype(BF16))
            o_ref[q, :, j * page:(j + 1) * page] = jnp.sum(jnp.maximum(d, 0.0) * iw, axis=0,
                                                           keepdims=True)
        dn = jnp.sum(iq.astype(F32) * ikn_ref[q], axis=1, keepdims=True)
        sn = jnp.sum(jnp.maximum(dn, 0.0) * iw, axis=0, keepdims=True)
        o_ref[q, :, n_pages * page:] = jnp.where(lane == 0, sn, 0.0)


def _seqs_per_step(s, want):
    return want if s % want == 0 else 1


def dsa_step_scores(pt_flat, iq, iw, ik_new, cik2d, n_pages, page):
    s = iq.shape[0]
    sb = _seqs_per_step(s, SCORE_SEQS)
    kw = n_pages * page + LANES
    page_spec = lambda q, j: pl.BlockSpec(
        (IDX_DIM, page), lambda i, pt: (pt[(i * sb + q) * n_pages + j], 0))
    grid_spec = pltpu.PrefetchScalarGridSpec(
        num_scalar_prefetch=1,
        grid=(s // sb,),
        in_specs=[pl.BlockSpec((sb, IDX_HEADS, IDX_DIM), lambda i, pt: (i, 0, 0)),
                  pl.BlockSpec((sb, IDX_HEADS, 1), lambda i, pt: (i, 0, 0)),
                  pl.BlockSpec((sb, 1, IDX_DIM), lambda i, pt: (i, 0, 0))]
                 + [page_spec(q, j) for q in range(sb) for j in range(n_pages)],
        out_specs=pl.BlockSpec((sb, 1, kw), lambda i, pt: (i, 0, 0)),
    )
    return pl.pallas_call(
        functools.partial(_dsa_step_score_kernel, n_pages, page),
        grid_spec=grid_spec,
        out_shape=jax.ShapeDtypeStruct((s, 1, kw), F32),
        compiler_params=_cparams(("arbitrary",)),
        name="dsa_step_scores",
    )(pt_flat, iq, iw, ik_new, *([cik2d] * (sb * n_pages)))


def _dsa_step_select_kernel(n_sel, past, sc_ref, o_ref):
    sc = sc_ref[...]
    kpos = lax.broadcasted_iota(I32, sc.shape, 1)
    sel = _select_topk(sc, kpos <= past, n_sel)
    o_ref[...] = sel.astype(F32)


def dsa_step_select(sc, n_sel, past):
    return pl.pallas_call(
        functools.partial(_dsa_step_select_kernel, n_sel, past),
        out_shape=jax.ShapeDtypeStruct(sc.shape, F32),
        compiler_params=pltpu.CompilerParams(vmem_limit_bytes=VMEM_LIMIT),
        name="dsa_step_select",
    )(sc)


def _dsa_step_attn_kernel(n_pages, page, pt_ref, q_ref, kn_ref, vn_ref, sel_ref, ex_ref, *refs):
    sb = q_ref.shape[0]
    kps = refs[:sb * n_pages]
    vps = refs[sb * n_pages:2 * sb * n_pages]
    o_ref = refs[2 * sb * n_pages]
    rep = A_HEADS // A_KV_HEADS
    w2 = page * A_KV_HEADS
    hrow = lax.broadcasted_iota(I32, (A_HEADS, w2), 0)
    col = lax.broadcasted_iota(I32, (A_HEADS, w2), 1)
    own = (col % A_KV_HEADS) == (hrow // rep)
    h8 = lax.broadcasted_iota(I32, (A_HEADS, A_HEAD), 0)
    ex = ex_ref[...]
    for u in range(sb):
        kp = kps[u * n_pages:(u + 1) * n_pages]
        vp = vps[u * n_pages:(u + 1) * n_pages]
        q = q_ref[u]
        logits = []
        for j in range(n_pages):
            s = _dot_nt(q, kp[j][...].astype(BF16))
            selj = _dot(sel_ref[u, :, j * page:(j + 1) * page].astype(BF16), ex)
            logits.append(jnp.where(jnp.logical_and(selj > 0.5, own), s, NEG_BIG))
        kn = jnp.where(h8 < rep, kn_ref[u, 0:1, :], kn_ref[u, 1:2, :])
        vn = jnp.where(h8 < rep, vn_ref[u, 0:1, :], vn_ref[u, 1:2, :])
        sn = jnp.sum(q.astype(F32) * kn, axis=1, keepdims=True)
        seln = sel_ref[u, :, n_pages * page:n_pages * page + 1]
        sn = jnp.where(seln > 0.5, sn, NEG_BIG)
        m = sn
        for s in logits:
            m = jnp.maximum(m, jnp.max(s, axis=1, keepdims=True))
        pn = jnp.exp(sn - m)
        l = pn
        acc = pn * vn
        for j in range(n_pages):
            p = jnp.exp(logits[j] - m)
            l = l + jnp.sum(p, axis=1, keepdims=True)
            acc = acc + _dot(p.astype(BF16), vp[j][...].astype(BF16))
        o_ref[u] = acc / l


def dsa_step_attn(pt_flat, q, k_new, v_new, sel, expand, ck2d, cv2d, n_pages, page):
    s = q.shape[0]
    sb = _seqs_per_step(s, ATTN_SEQS)
    kw = sel.shape[-1]
    w2 = page * A_KV_HEADS
    page_spec = lambda u, j: pl.BlockSpec(
        (w2, A_HEAD), lambda i, pt: (pt[(i * sb + u) * n_pages + j], 0))
    pages = [page_spec(u, j) for u in range(sb) for j in range(n_pages)]
    grid_spec = pltpu.PrefetchScalarGridSpec(
        num_scalar_prefetch=1,
        grid=(s // sb,),
        in_specs=[pl.BlockSpec((sb, A_HEADS, A_HEAD), lambda i, pt: (i, 0, 0)),
                  pl.BlockSpec((sb, A_KV_HEADS, A_HEAD), lambda i, pt: (i, 0, 0)),
                  pl.BlockSpec((sb, A_KV_HEADS, A_HEAD), lambda i, pt: (i, 0, 0)),
                  pl.BlockSpec((sb, 1, kw), lambda i, pt: (i, 0, 0)),
                  pl.BlockSpec((page, w2), lambda i, pt: (0, 0))] + pages * 2,
        out_specs=pl.BlockSpec((sb, A_HEADS, A_HEAD), lambda i, pt: (i, 0, 0)),
    )
    return pl.pallas_call(
        functools.partial(_dsa_step_attn_kernel, n_pages, page),
        grid_spec=grid_spec,
        out_shape=jax.ShapeDtypeStruct((s, A_HEADS, A_HEAD), F32),
        compiler_params=_cparams(("arbitrary",)),
        name="dsa_step_attn",
    )(pt_flat, q, k_new, v_new, sel, expand, *([ck2d] * (sb * n_pages)), *([cv2d] * (sb * n_pages)))


def _pack_bf16_pairs(x):
    w = x.shape[1] // 2
    hi = lax.bitcast_convert_type(x[:, :w].astype(BF16).astype(F32), I32)
    lo = lax.bitcast_convert_type(x[:, w:].astype(BF16).astype(F32), I32)
    return hi | lax.shift_right_logical(lo, 16)


def _unpack_bf16_pairs(p):
    hi = lax.bitcast_convert_type(p & jnp.int32(-65536), F32)
    lo = lax.bitcast_convert_type(lax.shift_left(p, 16), F32)
    return hi, lo


def _mix_kernel(alpha, prompt_blocks, x_ref, yp_ref, bonp_ref, gp_ref, ap_ref, ys_ref, bons_ref, gs_ref,
                as_ref, e_ref, et_ref, gng_ref, gnb_ref, l0g_ref, l0b_ref, wo_ref, l1g_ref, l1b_ref,
                wr_ref, h_ref, sc_ref, pk_ref):
    is_prompt = pl.program_id(0) < prompt_blocks
    pick = lambda p_ref, s_ref: jnp.where(is_prompt, p_ref[...], s_ref[...])
    e, et = e_ref[...], et_ref[...]
    y = pick(yp_ref, ys_ref)
    inv = 1.0 / R_HEAD
    mu = _head_sums(y, e, et) * inv
    d = y - mu
    var = _head_sums(d * d, e, et) * inv
    yn = d * lax.rsqrt(var + GN_EPS) * gng_ref[...] + gnb_ref[...]
    r_out = (yn + pick(bonp_ref, bons_ref)) * pick(gp_ref, gs_ref)
    mix = (_dot(r_out.astype(BF16), wo_ref[:R_DIM, :])
           + _dot(pick(ap_ref, as_ref).astype(BF16), wo_ref[R_DIM:, :]))
    h0 = _layer_norm(x_ref[...], l0g_ref[...], l0b_ref[...])
    h1 = _layer_norm(alpha * h0 + mix, l1g_ref[...], l1b_ref[...])
    h_ref[...] = h1
    w_hi, w_lo = _split_bf16(wr_ref[...])
    h_hi, h_lo = _split_bf16(h1)
    logits = _dot(h_hi, w_hi) + _dot(h_lo, w_hi) + _dot(h_hi, w_lo)
    sc_ref[...] = _sigmoid(logits.T[:N_EXPERTS, :])
    pk_ref[...] = _pack_bf16_pairs(h1)


def mix_ln1_router(x, prompt_parts, step_parts, pw, alpha):
    n, d = x.shape
    tm = ROW_TILE
    pb = prompt_parts[0].shape[0] // tm
    row = lambda w: pl.BlockSpec((tm, w), lambda i: (i, 0))
    head = lambda w: pl.BlockSpec((tm, w), lambda i: (jnp.minimum(i, pb - 1), 0))
    tail = lambda w: pl.BlockSpec((tm, w), lambda i: (jnp.maximum(i - pb, 0), 0))
    full = lambda a: pl.BlockSpec(a.shape, lambda i: (0,) * a.ndim)
    params = (pw["e"], pw["et"], pw["gn_g"], pw["gn_b"], pw["ln0_g"], pw["ln0_b"], pw["w_out"],
              pw["ln1_g"], pw["ln1_b"], pw["w_router"])
    widths = (R_DIM, R_DIM, R_DIM, A_DIM)
    return pl.pallas_call(
        functools.partial(_mix_kernel, alpha, pb),
        grid=(n // tm,),
        in_specs=[row(d)] + [head(w) for w in widths] + [tail(w) for w in widths]
                 + [full(a) for a in params],
        out_specs=[row(d), pl.BlockSpec((N_EXPERTS, tm), lambda i: (0, i)), row(d // 2)],
        out_shape=[jax.ShapeDtypeStruct((n, d), F32),
                   jax.ShapeDtypeStruct((N_EXPERTS, n), F32),
                   jax.ShapeDtypeStruct((n, d // 2), I32)],
        compiler_params=_cparams(("parallel",)),
        name="mix_ln1_router",
    )(x, *prompt_parts, *step_parts, *params)


def _route_kernel(sc_ref, bias_ref, idx_ref, gate_ref, pos_ref, cnt_ref, cnt_scr):
    scores = sc_ref[...]
    biased = scores + bias_ref[...]
    tn = scores.shape[1]
    per = N_EXPERTS // N_EXPERT_GROUPS
    sub = lax.broadcasted_iota(I32, (per, tn), 0)
    grp_rows = []
    for g in range(N_EXPERT_GROUPS):
        xg = biased[g * per:(g + 1) * per, :]
        m1 = jnp.max(xg, axis=0, keepdims=True)
        first = jnp.min(jnp.where(xg == m1, sub, per), axis=0, keepdims=True)
        m2 = jnp.max(jnp.where(sub == first, -jnp.inf, xg), axis=0, keepdims=True)
        grp_rows.append(m1 + m2)
    grp = jnp.concatenate(grp_rows, axis=0)
    gi = lax.broadcasted_iota(I32, (N_EXPERT_GROUPS, tn), 0)
    gsel = jnp.zeros((N_EXPERT_GROUPS, tn), jnp.bool_)
    for _ in range(TOPK_GROUPS):
        m = jnp.max(grp, axis=0, keepdims=True)
        first = jnp.min(jnp.where(grp == m, gi, N_EXPERT_GROUPS), axis=0, keepdims=True)
        hit = gi == first
        gsel = jnp.logical_or(gsel, hit)
        grp = jnp.where(hit, -jnp.inf, grp)
    ei = lax.broadcasted_iota(I32, (N_EXPERTS, tn), 0)
    emask = jnp.concatenate(
        [jnp.broadcast_to(gsel[g:g + 1, :], (per, tn)) for g in range(N_EXPERT_GROUPS)], axis=0)
    cand = jnp.where(emask, biased, -jnp.inf)
    idxs, gates, hits = [], [], []
    for _ in range(TOP_K):
        m = jnp.max(cand, axis=0, keepdims=True)
        first = jnp.min(jnp.where(cand == m, ei, N_EXPERTS), axis=0, keepdims=True)
        hit = ei == first
        idxs.append(first)
        hits.append(hit)
        gates.append(jnp.sum(jnp.where(hit, scores, 0.0), axis=0, keepdims=True))
        cand = jnp.where(hit, -jnp.inf, cand)
    gate = jnp.concatenate(gates, axis=0)
    gate = gate / jnp.sum(gate, axis=0, keepdims=True) * ROUTED_SCALE
    idx_ref[...] = jnp.concatenate(idxs, axis=0)
    gate_ref[...] = gate
    chosen = hits[0]
    for hit in hits[1:]:
        chosen = jnp.logical_or(chosen, hit)
    onehot = jnp.where(chosen, 1.0, 0.0)
    ta = lax.broadcasted_iota(I32, (tn, tn), 0)
    tb = lax.broadcasted_iota(I32, (tn, tn), 1)
    prefix = _dot(onehot.astype(BF16), (ta < tb).astype(BF16))

    @pl.when(pl.program_id(0) == 0)
    def _():
        cnt_scr[...] = jnp.zeros_like(cnt_scr)

    rank = prefix + cnt_scr[:, 0:1]
    pos_ref[...] = jnp.concatenate(
        [jnp.sum(jnp.where(hit, rank, 0.0), axis=0, keepdims=True) for hit in hits], axis=0).astype(I32)
    cnt_scr[...] = cnt_scr[...] + jnp.sum(onehot, axis=1, keepdims=True)
    cnt_ref[...] = cnt_scr[...].astype(I32)


def route(scores_t, e_bias):
    n = scores_t.shape[1]
    tn = ROW_TILE
    tok = pl.BlockSpec((TOP_K, tn), lambda i: (0, i))
    return pl.pallas_call(
        _route_kernel,
        grid=(n // tn,),
        in_specs=[pl.BlockSpec((N_EXPERTS, tn), lambda i: (0, i)),
                  pl.BlockSpec((N_EXPERTS, 1), lambda i: (0, 0))],
        out_specs=[tok, tok, tok, pl.BlockSpec((N_EXPERTS, LANES), lambda i: (0, 0))],
        out_shape=[jax.ShapeDtypeStruct((TOP_K, n), I32), jax.ShapeDtypeStruct((TOP_K, n), F32),
                   jax.ShapeDtypeStruct((TOP_K, n), I32),
                   jax.ShapeDtypeStruct((N_EXPERTS, LANES), I32)],
        scratch_shapes=[pltpu.VMEM((N_EXPERTS, LANES), F32)],
        compiler_params=_cparams(("arbitrary",)),
        name="route",
    )(scores_t, e_bias)


def _dispatch_kernel(nb, dest_ref, segend_ref, cnt_ref, nu_ref, x_ref, inv0_ref, o_ref, invo_ref,
                     inv_ref, zbuf, sem, zsem, isem):
    tm = x_ref.shape[0]
    row0 = pl.program_id(0) * tm

    @pl.when(pl.program_id(0) == 0)
    def _():
        c = pltpu.make_async_copy(inv0_ref, inv_ref, isem)
        c.start()
        c.wait()

    def fill(start):
        return pltpu.make_async_copy(zbuf, o_ref.at[pl.ds(pl.multiple_of(start, EXPERT_TILE),
                                                          EXPERT_TILE)], zsem)

    @pl.when(pl.program_id(0) == 0)
    def _():
        zbuf[...] = jnp.zeros_like(zbuf)

        def each_expert(fn):
            def body(e, carry):
                @pl.when(cnt_ref[e] > 0)
                def _():
                    fn(fill(segend_ref[e] - EXPERT_TILE))
                return carry
            lax.fori_loop(0, N_EXPERTS, body, 0)

        def each_free_block(fn):
            def body(b, carry):
                fn(fill(b * EXPERT_TILE))
                return carry
            lax.fori_loop(nu_ref[0], nb, body, 0)

        each_expert(lambda c: c.start())
        each_free_block(lambda c: c.start())
        each_expert(lambda c: c.wait())
        each_free_block(lambda c: c.wait())

    def start(i, carry):
        for j in range(TOP_K):
            d = dest_ref[i * TOP_K + j]
            inv_ref[d] = (row0 + i) * TOP_K + j
            pltpu.make_async_copy(x_ref.at[pl.ds(i, 1)], o_ref.at[pl.ds(d, 1)], sem).start()
        return carry

    lax.fori_loop(0, tm, start, 0)
    for j in range(TOP_K):
        pltpu.make_async_copy(x_ref, o_ref.at[pl.ds(0, tm)], sem).wait()

    @pl.when(pl.program_id(0) == pl.num_programs(0) - 1)
    def _():
        c = pltpu.make_async_copy(inv_ref, invo_ref, isem)
        c.start()
        c.wait()


def moe_dispatch(dest_flat, seg_end, counts, n_used, xpk, inv_default, nb):
    n, w = xpk.shape
    tm = ROW_TILE
    rows = nb * EXPERT_TILE
    smem = lambda: pl.BlockSpec(memory_space=pltpu.SMEM)
    hbm = lambda: pl.BlockSpec(memory_space=pl.ANY)
    return pl.pallas_call(
        functools.partial(_dispatch_kernel, nb),
        grid=(n // tm,),
        in_specs=[pl.BlockSpec((tm * TOP_K,), lambda i: (i,), memory_space=pltpu.SMEM),
                  smem(), smem(), smem(),
                  pl.BlockSpec((tm, w), lambda i: (i, 0)), hbm()],
        out_specs=[hbm(), hbm()],
        out_shape=[jax.ShapeDtypeStruct((rows, w), I32), jax.ShapeDtypeStruct((rows,), I32)],
        scratch_shapes=[pltpu.SMEM((rows,), I32), pltpu.VMEM((EXPERT_TILE, w), I32),
                        pltpu.SemaphoreType.DMA(()), pltpu.SemaphoreType.DMA(()),
                        pltpu.SemaphoreType.DMA(())],
        compiler_params=_cparams(("arbitrary",)),
        name="moe_dispatch",
    )(dest_flat, seg_end, counts, n_used, xpk, inv_default)


def _experts_kernel(n_slots, nb, be_ref, nxt_ref, slot_ref, nu_ref, inv_ref, x_ref, wg_ref, wu_ref,
                    wd_ref, o_ref, wg_f, wu_f, wd_f, wg_s, wu_s, wd_s, obuf_a, obuf_b, sems, osems):
    i = pl.program_id(0)
    n_used = nu_ref[0]
    used = i < n_used
    prev = be_ref[jnp.maximum(i - 1, 0)]
    fresh = jnp.logical_and(used, jnp.logical_or(i == 0, be_ref[i] != prev))
    bm = x_ref.shape[0]

    obufs = (obuf_a, obuf_b)

    def scatter_rows(blk, par):
        for r in range(bm):
            pltpu.make_async_copy(obufs[par].at[pl.ds(r, 1)],
                                  o_ref.at[pl.ds(inv_ref[blk * bm + r], 1)], osems.at[par]).start()

    def wait_rows(par):
        pltpu.make_async_copy(obufs[par], o_ref.at[pl.ds(0, bm)], osems.at[par]).wait()

    def by_parity(cond, fn):
        for par in range(2):
            @pl.when(jnp.logical_and(cond, i % 2 == par))
            def _(par=par):
                fn(par)

    @pl.when(i == 0)
    def _():
        obuf_b[...] = jnp.zeros_like(obuf_b)
        spare = [pltpu.make_async_copy(obuf_b, o_ref.at[pl.ds(n_slots + e * bm, bm)], osems.at[1])
                 for e in range(N_EXPERTS)]
        for c in spare:
            c.start()
        for c in spare:
            c.wait()

    by_parity(jnp.logical_and(i >= 2, i - 2 < n_used), wait_rows)

    def weight_copies(e, slot):
        return (pltpu.make_async_copy(wg_ref.at[e], wg_f.at[slot], sems.at[slot, 0]),
                pltpu.make_async_copy(wu_ref.at[e], wu_f.at[slot], sems.at[slot, 1]),
                pltpu.make_async_copy(wd_ref.at[e], wd_f.at[slot], sems.at[slot, 2]))

    @pl.when(jnp.logical_and(used, i == 0))
    def _():
        for c in weight_copies(be_ref[0], 0):
            c.start()

    @pl.when(fresh)
    def _():
        slot = slot_ref[i]
        for c in weight_copies(be_ref[i], slot):
            c.wait()

        @pl.when(nxt_ref[i] >= 0)
        def _():
            for c in weight_copies(nxt_ref[i], 1 - slot):
                c.start()

        wg_s[...] = wg_f[slot].astype(BF16)
        wu_s[...] = wu_f[slot].astype(BF16)
        wd_s[...] = wd_f[slot].astype(BF16)

    def compute(par):
        hi, lo = _unpack_bf16_pairs(x_ref[...])
        hi, lo = hi.astype(BF16), lo.astype(BF16)
        half = hi.shape[1]
        gp = _dot(hi, wg_s[:half, :]) + _dot(lo, wg_s[half:, :])
        up = _dot(hi, wu_s[:half, :]) + _dot(lo, wu_s[half:, :])
        act = gp * _sigmoid(gp) * up
        obufs[par][...] = _pack_bf16_pairs(_dot(act.astype(BF16), wd_s[...]))

    @pl.when(jnp.logical_and(used, i == 0))
    def _():
        compute(0)

    def send_prev_and_compute(par):
        scatter_rows(i - 1, 1 - par)
        compute(par)

    by_parity(jnp.logical_and(used, i > 0), send_prev_and_compute)
    by_parity(jnp.logical_and(i == n_used, i > 0), lambda par: scatter_rows(i - 1, 1 - par))

    @pl.when(jnp.logical_and(i == nb - 1, n_used == nb - 1))
    def _():
        wait_rows((nb - 2) % 2)


def moe_experts(blk_e, nxt_e, slot, n_used, inv, xs, n_slots, w_gate, w_up, w_down):
    w = xs.shape[1]
    nb = xs.shape[0] // EXPERT_TILE
    _, d, de = w_gate.shape
    grid_spec = pltpu.PrefetchScalarGridSpec(
        num_scalar_prefetch=5,
        grid=(nb,),
        in_specs=[pl.BlockSpec((EXPERT_TILE, w), lambda i, be, nx, sl, nu, iv: (jnp.minimum(i, nu[0] - 1), 0)),
                  pl.BlockSpec(memory_space=pl.ANY), pl.BlockSpec(memory_space=pl.ANY),
                  pl.BlockSpec(memory_space=pl.ANY)],
        out_specs=pl.BlockSpec(memory_space=pl.ANY),
        scratch_shapes=[pltpu.VMEM((2, d, de), F32), pltpu.VMEM((2, d, de), F32),
                        pltpu.VMEM((2, de, d), F32),
                        pltpu.VMEM((d, de), BF16), pltpu.VMEM((d, de), BF16),
                        pltpu.VMEM((de, d), BF16), pltpu.VMEM((EXPERT_TILE, w), I32),
                        pltpu.VMEM((EXPERT_TILE, w), I32),
                        pltpu.SemaphoreType.DMA((2, 3)), pltpu.SemaphoreType.DMA((2,))],
    )
    return pl.pallas_call(
        functools.partial(_experts_kernel, n_slots, nb),
        grid_spec=grid_spec,
        out_shape=jax.ShapeDtypeStruct((n_slots + N_EXPERTS * EXPERT_TILE, w), I32),
        compiler_params=_cparams(("arbitrary",)),
        name="moe_experts",
    )(blk_e, nxt_e, slot, n_used, inv, xs, w_gate, w_up, w_down)


def _combine_kernel(alpha, h_ref, gate_ref, ys_ref, wsg_ref, wsu_ref, wsd_ref, l2g_ref, l2b_ref, o_ref):
    tm = h_ref.shape[0]
    h = h_ref[...]
    hb = h.astype(BF16)
    gp = _dot(hb, wsg_ref[...])
    up = _dot(hb, wsu_ref[...])
    shared = _dot((gp * _sigmoid(gp) * up).astype(BF16), wsd_ref[...])
    row = lax.broadcasted_iota(I32, (tm, tm * TOP_K), 0)
    col = lax.broadcasted_iota(I32, (tm, tm * TOP_K), 1)
    g_hi, g_lo = _split_bf16(jnp.where(col // TOP_K == row, gate_ref[0], 0.0))
    hi, lo = _unpack_bf16_pairs(ys_ref[...])
    hi, lo = hi.astype(BF16), lo.astype(BF16)
    routed = jnp.concatenate([_dot(g_hi, hi) + _dot(g_lo, hi), _dot(g_hi, lo) + _dot(g_lo, lo)], axis=1)
    o_ref[...] = _layer_norm(alpha * h + routed + shared, l2g_ref[...], l2b_ref[...])


def moe_combine(h1, gate_rows, ys, pw, alpha):
    n, d = h1.shape
    tm = gate_rows.shape[2] // TOP_K
    full = lambda a: pl.BlockSpec(a.shape, lambda i: (0,) * a.ndim)
    params = (pw["ws_gate"], pw["ws_up"], pw["ws_down"], pw["ln2_g"], pw["ln2_b"])
    return pl.pallas_call(
        functools.partial(_combine_kernel, alpha),
        grid=(n // tm,),
        in_specs=[pl.BlockSpec((tm, d), lambda i: (i, 0)),
                  pl.BlockSpec((1, 1, tm * TOP_K), lambda i: (i, 0, 0)),
                  pl.BlockSpec((tm * TOP_K, d // 2), lambda i: (i, 0))]
                 + [full(a) for a in params],
        out_specs=pl.BlockSpec((tm, d), lambda i: (i, 0)),
        out_shape=jax.ShapeDtypeStruct((n, d), F32),
        compiler_params=_cparams(("parallel",)),
        name="moe_combine",
    )(h1, gate_rows, ys, *params)


def _round_up(x, m):
    return (x + m - 1) // m * m


def _rope_tables(pos, head):
    half = head // 2
    inv = ROPE_THETA ** (-jnp.arange(half, dtype=F32) / half)
    ang = pos.astype(F32)[:, None] * inv[None, :]
    cos, sin = jnp.cos(ang), jnp.sin(ang)
    rep = LANES // head
    c = jnp.tile(jnp.concatenate([cos, cos], axis=1), (1, rep))
    s = jnp.tile(jnp.concatenate([-sin, sin], axis=1), (1, rep))
    return c, s


def _permute_cols(m, axis=-1):
    axis = axis % m.ndim
    cut = lambda lo, hi: lax.slice_in_dim(m, lo, hi, axis=axis)
    a0 = SHIFT_DIM
    i0 = a0 + A_DIM + 2 * A_KV_DIM + IDX_HEADS * IDX_DIM

    def pad(w):
        shape = list(m.shape)
        shape[axis] = w
        return jnp.zeros(shape, m.dtype)

    pieces = [
        cut(0, 3 * R_DIM),
        cut(a0, a0 + A_DIM),
        cut(a0 + A_DIM + 2 * A_KV_DIM, i0),
        cut(a0 + A_DIM, a0 + A_DIM + 2 * A_KV_DIM),
        cut(i0, i0 + IDX_DIM + IDX_HEADS),
        pad(LANES - IDX_DIM - IDX_HEADS),
        cut(3 * R_DIM, SHIFT_DIM),
        pad(LORA_W - (SHIFT_DIM - 3 * R_DIM)),
    ]
    return jnp.concatenate(pieces, axis=axis)


def kernel(x_prompt, x_sample, cache_k, cache_v, cache_idx_k, state_wkv, state_shift, page_table,
           meta, ln0_g, ln0_b, w_in, mu_shift, w0, w_b, a0, a_b, g_b, k_k, k_a, r_k, gn_g, gn_b,
           w_out, ln1_g, ln1_b, w_router, e_bias, w_gate, w_up, w_down, ws_gate, ws_up, ws_down,
           ln2_g, ln2_b):
    depth = w_in.shape[0]
    assert depth == 1, "single trunk layer"
    bsz, s_p, d = x_prompt.shape
    s_dec, s_s, _ = x_sample.shape
    assert s_s == 1, "one decode token per sequence"
    t_real = N_META + s_p
    tp = _round_up(t_real, LANES)
    assert (bsz * tp) % ROW_TILE == 0
    sp = _round_up(s_dec, ROW_TILE)
    n_prompt = bsz * tp
    n = n_prompt + sp
    n_pool, page = cache_k.shape[1], cache_k.shape[2]
    n_pages = page_table.shape[1]
    past = n_pages * page
    alpha = float((2 * depth) ** 0.25)
    row2 = lambda a: a.reshape(1, -1)

    meta_rows = jnp.broadcast_to(meta[None], (bsz, N_META, d))
    xp = jnp.concatenate([meta_rows, x_prompt, jnp.zeros((bsz, tp - t_real, d), F32)], axis=1)
    x_all = jnp.concatenate([xp.reshape(n_prompt, d), x_sample.reshape(s_dec, d),
                             jnp.zeros((sp - s_dec, d), F32)], axis=0)
    pos = jnp.concatenate([jnp.tile(jnp.arange(tp), bsz), jnp.full((sp,), past)])
    c128, s128 = _rope_tables(pos, A_HEAD)
    c64, s64 = _rope_tables(pos, IDX_DIM)

    w_in_t = _permute_cols(w_in[0].T.astype(BF16), axis=0)
    mu_k = _permute_cols(
        jnp.concatenate([mu_shift[0], jnp.zeros((w_in.shape[2] - SHIFT_DIM,), F32)])[None, :])
    head_of = jnp.arange(R_DIM) // R_HEAD
    e_mat = (head_of[:, None] == jnp.arange(R_HEADS)[None, :]).astype(F32)
    zpad = lambda a, rows_before, rows_total: jnp.concatenate(
        [jnp.zeros((rows_before, a.shape[1]), a.dtype), a,
         jnp.zeros((rows_total - rows_before - a.shape[0], a.shape[1]), a.dtype)], axis=0)
    pw = {
        "mu_x": mu_k[:, :3 * R_DIM], "mu_lo": mu_k[:, C_LORA:],
        "w0": row2(w0[0]), "a0": row2(a0[0]), "k_k": row2(k_k[0]), "k_a": row2(k_a[0]),
        "r_k": row2(r_k[0]), "gn_g": row2(gn_g[0]), "gn_b": row2(gn_b[0]),
        "w_b": zpad(w_b[0], 0, LANES).astype(BF16),
        "a_b": zpad(a_b[0], D_DECAY_LORA, LANES).astype(BF16),
        "g_b": zpad(g_b[0], 0, LORA_W - LANES).astype(BF16),
        "e": e_mat.astype(BF16), "et": e_mat.T.astype(BF16),
        "ln0_g": row2(ln0_g), "ln0_b": row2(ln0_b),
        "ln1_g": row2(ln1_g[0]), "ln1_b": row2(ln1_b[0]),
        "ln2_g": row2(ln2_g[0]), "ln2_b": row2(ln2_b[0]),
        "w_out": w_out[0].astype(BF16),
        "w_router": jnp.pad(w_router[0], ((0, 0), (0, LANES - N_EXPERTS))),
        "ws_gate": ws_gate[0].astype(BF16), "ws_up": ws_up[0].astype(BF16),
        "ws_down": ws_down[0].astype(BF16),
    }

    p = ln_proj(x_all, pw["ln0_g"], pw["ln0_b"], w_in_t, tn=P_COLS // 3)
    q_r, iq_r, k_r, ik_r, iw, kt_r, ikt_r = rope_all(p, c128, s128, c64, s64)

    pre_p = rwkv_pre(p, 0, n_prompt, None, pw, t_real, tp)
    shift_k = _permute_cols(jnp.concatenate(
        [state_shift[0], jnp.zeros((s_dec, w_in.shape[2] - SHIFT_DIM), F32)], axis=1))
    shift_k = jnp.concatenate([shift_k, jnp.zeros((sp - s_dec, P_COLS), F32)], axis=0)
    pre_s = rwkv_pre(p, n_prompt, sp, (shift_k[:, :3 * R_DIM], shift_k[:, C_LORA:]), pw, t_real, tp)
    r_p, ld_p, k_p, v_p, kk_p, b_p, g_p, bon_p = pre_p
    r_s, ld_s, k_s, v_s, kk_s, b_s, g_s, bon_s = pre_s
    y_p, wkv_p = rwkv_scan(r_p, ld_p, k_p, v_p, kk_p, b_p, bsz, tp)
    heads = lambda a: a[:s_dec].reshape(s_dec, R_HEADS, R_HEAD).transpose(1, 2, 0)
    y_hs, wkv_hs = rwkv_step(heads(r_s), heads(ld_s), heads(k_s), heads(v_s), heads(kk_s), heads(b_s),
                             state_wkv[0].transpose(1, 2, 3, 0))
    wkv_s = wkv_hs.transpose(3, 0, 1, 2)
    y_s = jnp.concatenate([y_hs.transpose(2, 0, 1).reshape(s_dec, R_DIM),
                           jnp.zeros((sp - s_dec, R_DIM), F32)], axis=0)

    n_sel_p = min(TOPK_KEYS, t_real // 4)
    a_p = dsa_prompt(q_r, iq_r, iw, ikt_r, kt_r, p, bsz, tp, n_sel_p)
    n_sel_s = min(TOPK_KEYS, (past + 1) // 4)
    pt_flat = page_table.reshape(-1).astype(I32)
    srow = slice(n_prompt, n_prompt + s_dec)
    sc_s = dsa_step_scores(pt_flat, iq_r[srow].reshape(s_dec, IDX_HEADS, IDX_DIM),
                           iw[srow].reshape(s_dec, IDX_HEADS, 1), ik_r[srow].reshape(s_dec, 1, IDX_DIM),
                           cache_idx_k[0].transpose(0, 2, 1).reshape(n_pool * IDX_DIM, page),
                           n_pages, page)
    sel_s = dsa_step_select(sc_s.reshape(s_dec, -1), n_sel_s, past).reshape(sc_s.shape)
    slot = jnp.arange(page)[:, None]
    expand = (jnp.arange(page * A_KV_HEADS)[None, :] // A_KV_HEADS == slot).astype(BF16)
    a_s = dsa_step_attn(pt_flat, q_r[srow].reshape(s_dec, A_HEADS, A_HEAD),
                        k_r[srow].reshape(s_dec, A_KV_HEADS, A_HEAD),
                        p[srow, C_VA:C_VA + A_KV_DIM].reshape(s_dec, A_KV_HEADS, A_HEAD),
                        sel_s, expand,
                        cache_k[0].reshape(n_pool * page * A_KV_HEADS, A_HEAD),
                        cache_v[0].reshape(n_pool * page * A_KV_HEADS, A_HEAD), n_pages, page)
    a_s = jnp.concatenate([a_s.reshape(s_dec, A_DIM), jnp.zeros((sp - s_dec, A_DIM), F32)], axis=0)

    h1, scores_t, xpk = mix_ln1_router(x_all, (y_p, bon_p, g_p, a_p), (y_s, bon_s, g_s, a_s), pw, alpha)
    eidx_t, gate_t, pos_t, counts = route(scores_t, e_bias[0].reshape(N_EXPERTS, 1))

    n_slots = n * TOP_K
    nb = (n_slots + N_EXPERTS * (EXPERT_TILE - 1)) // EXPERT_TILE + 1
    counts = counts[:, 0]
    padded = (counts + EXPERT_TILE - 1) // EXPERT_TILE * EXPERT_TILE
    seg_end = jnp.cumsum(padded).astype(I32)
    seg_start = seg_end - padded
    experts = jnp.arange(N_EXPERTS)
    start_of = jnp.sum(jnp.where(eidx_t[:, :, None] == experts, seg_start, 0), axis=-1)
    dest = (start_of + pos_t).T.astype(I32).reshape(-1)
    blk_row = jnp.arange(nb) * EXPERT_TILE
    blk_e = jnp.minimum(jnp.sum(seg_end[None, :] <= blk_row[:, None], axis=1),
                        N_EXPERTS - 1).astype(I32)
    n_used = (seg_end[-1] // EXPERT_TILE).astype(I32)
    run_start = jnp.concatenate([jnp.ones((1,), I32), (blk_e[1:] != blk_e[:-1]).astype(I32)])
    slot = ((jnp.cumsum(run_start) - 1) % 2).astype(I32)
    run_end = seg_end[blk_e] // EXPERT_TILE
    nxt_e = jnp.where(run_end < n_used, blk_e[jnp.minimum(run_end, nb - 1)], -1).astype(I32)
    n_used = n_used.reshape(1)
    spare = (n_slots + blk_e[:, None] * EXPERT_TILE + jnp.arange(EXPERT_TILE)[None, :]).astype(I32)
    xs, inv = moe_dispatch(dest, seg_end, counts.astype(I32), n_used, xpk, spare.reshape(-1), nb)
    ys = moe_experts(blk_e, nxt_e, slot, n_used, inv, xs, n_slots, w_gate[0], w_up[0], w_down[0])
    tc = Q_TILE
    h2 = moe_combine(h1, gate_t.T.reshape(n // tc, 1, tc * TOP_K), ys, pw, alpha)

    def prompt_rows(a):
        return a[:n_prompt].reshape(bsz, tp, -1)[:, :t_real]

    y_prompt = h2[:n_prompt].reshape(bsz, tp, d)[:, N_META:t_real]
    y_sample = h2[srow].reshape(s_dec, 1, d)
    k_prompt = prompt_rows(k_r).reshape(1, bsz, t_real, A_KV_HEADS, A_HEAD)
    v_prompt = prompt_rows(p[:, C_VA:C_VA + A_KV_DIM]).reshape(1, bsz, t_real, A_KV_HEADS, A_HEAD)
    ik_prompt = prompt_rows(ik_r)[None]
    last = jnp.arange(bsz) * tp + t_real - 1
    unperm = lambda rows: jnp.concatenate([rows[:, :3 * R_DIM],
                                           rows[:, C_LORA:C_LORA + SHIFT_DIM - 3 * R_DIM]], axis=1)
    shift_prompt = unperm(p[last])[None]
    k_sample = k_r[srow].reshape(1, s_dec, 1, A_KV_HEADS, A_HEAD)
    v_sample = p[srow, C_VA:C_VA + A_KV_DIM].reshape(1, s_dec, 1, A_KV_HEADS, A_HEAD)
    ik_sample = ik_r[srow].reshape(1, s_dec, 1, IDX_DIM)
    shift_sample = unperm(p[srow])[None]
    return (y_prompt, y_sample, k_prompt, v_prompt, ik_prompt, wkv_p[None], shift_prompt,
            k_sample, v_sample, ik_sample, wkv_s[None], shift_sample)
```

```python
import functools

import numpy as np
import jax
import jax.numpy as jnp
from jax import lax
from jax.experimental import pallas as pl
from jax.experimental.pallas import tpu as pltpu

F32 = jnp.float32
BF16 = jnp.bfloat16
I32 = jnp.int32
HIGHEST = lax.Precision.HIGHEST

N_META = 16
R_HEADS, R_HEAD = 16, 64
R_DIM = R_HEADS * R_HEAD
D_DECAY_LORA, D_AAA_LORA, D_GATE_LORA = 64, 64, 160
SHIFT_DIM = 3 * R_DIM + D_DECAY_LORA + D_AAA_LORA + D_GATE_LORA
GN_EPS = 64e-5
A_HEADS, A_KV_HEADS, A_HEAD = 8, 2, 128
A_DIM = A_HEADS * A_HEAD
A_KV_DIM = A_KV_HEADS * A_HEAD
IDX_HEADS, IDX_DIM = 16, 64
TOPK_KEYS = 256
ROPE_THETA = 10000.0
N_EXPERTS, N_EXPERT_GROUPS, TOPK_GROUPS, TOP_K = 64, 8, 4, 8
ROUTED_SCALE = 2.5
LN_EPS = 1e-5

LANES = 128
SUBLANES = 8
ROW_TILE = 256
Q_TILE = 128
CHUNK = 64
EXPERT_TILE = 256
VMEM_LIMIT = 56 * 1024 * 1024
NEG_BIG = -1e30
INT_MIN = -2 ** 31

C_R, C_K, C_V = 0, R_DIM, 2 * R_DIM
C_Q = 3 * R_DIM
C_IQ = C_Q + A_DIM
C_KA = C_IQ + IDX_HEADS * IDX_DIM
C_VA = C_KA + A_KV_DIM
C_IK = C_VA + A_KV_DIM
C_LORA = C_IK + LANES
LORA_W = 384
P_COLS = C_LORA + LORA_W


def _cparams(sem):
    return pltpu.CompilerParams(dimension_semantics=sem, vmem_limit_bytes=VMEM_LIMIT)


def _dot(a, b, precision=None):
    return jnp.dot(a, b, preferred_element_type=F32, precision=precision)


def _dot_nt(a, b, precision=None):
    return lax.dot_general(a, b, (((1,), (1,)), ((), ())), preferred_element_type=F32,
                           precision=precision)


def _dot_tn(a, b, precision=None):
    return lax.dot_general(a, b, (((0,), (0,)), ((), ())), preferred_element_type=F32,
                           precision=precision)


def _split_bf16(x):
    hi = x.astype(BF16)
    return hi, (x - hi.astype(F32)).astype(BF16)


def _dot_f32_by_bf16(a, b):
    hi, lo = _split_bf16(a)
    return _dot(hi, b) + _dot(lo, b)


def _head_sums(x, e, et):
    return _dot_f32_by_bf16(_dot_f32_by_bf16(x, e), et)


def _layer_norm(x, g, b):
    mu = jnp.mean(x, axis=-1, keepdims=True)
    xc = x - mu
    var = jnp.mean(xc * xc, axis=-1, keepdims=True)
    return xc * lax.rsqrt(var + LN_EPS) * g + b


def _sigmoid(z):
    return 1.0 / (1.0 + jnp.exp(-z))


def _ln_proj_kernel(x0_ref, xn_ref, g_ref, b_ref, w_ref, o_ref, h_even, h_odd):
    i = pl.program_id(1)
    norm = lambda ref: _layer_norm(ref[...], g_ref[...], b_ref[...]).astype(BF16)

    @pl.when(i == 0)
    def _():
        h_even[...] = norm(x0_ref)

    for par, (cur, nxt) in enumerate(((h_even, h_odd), (h_odd, h_even))):
        @pl.when(i % 2 == par)
        def _(cur=cur, nxt=nxt):
            o_ref[...] = _dot_nt(cur[...], w_ref[...])
            nxt[...] = norm(xn_ref)


PROJ_ROWS = 256
PROJ_COLS = 2048


def ln_proj(x, g, b, w_t):
    n, d = x.shape
    cols = w_t.shape[0]
    tm = max(t for t in range(LANES, PROJ_ROWS + 1, LANES) if n % t == 0)
    tn = PROJ_COLS
    ni = n // tm
    return pl.pallas_call(
        _ln_proj_kernel,
        grid=(cols // tn, ni),
        in_specs=[
            pl.BlockSpec((tm, d), lambda j, i: (0, 0)),
            pl.BlockSpec((tm, d), lambda j, i: ((i + 1) % ni, 0)),
            pl.BlockSpec((1, d), lambda j, i: (0, 0)),
            pl.BlockSpec((1, d), lambda j, i: (0, 0)),
            pl.BlockSpec((tn, d), lambda j, i: (j, 0)),
        ],
        out_specs=pl.BlockSpec((tm, tn), lambda j, i: (i, j)),
        out_shape=jax.ShapeDtypeStruct((n, cols), F32),
        scratch_shapes=[pltpu.VMEM((tm, d), BF16), pltpu.VMEM((tm, d), BF16)],
        compiler_params=_cparams(("arbitrary", "arbitrary")),
        name="ln_proj",
    )(x, x, g, b, w_t)


def _rot_half(x, head):
    w = x.shape[-1]
    half = head // 2
    lane = lax.broadcasted_iota(I32, x.shape, 1)
    left = pltpu.roll(x, w - half, axis=1)
    right = pltpu.roll(x, half, axis=1)
    return jnp.where((lane % head) < half, left, right)


def _rope_kernel(q_ref, iq_ref, ka_ref, ikw_ref, c128_ref, s128_ref, c64_ref, s64_ref,
                 qo_ref, iqo_ref, ko_ref, iko_ref, iwo_ref, kt_ref, ikt_ref):
    c128, s128 = c128_ref[...], s128_ref[...]
    c64, s64 = c64_ref[...], s64_ref[...]

    def rope(x, head, c, s):
        rep = x.shape[-1] // LANES
        if rep > 1:
            c = jnp.concatenate([c] * rep, axis=1)
            s = jnp.concatenate([s] * rep, axis=1)
        return x * c + _rot_half(x, head) * s

    q = rope(q_ref[...], A_HEAD, c128, s128)
    qo_ref[...] = (q * (A_HEAD ** -0.5)).astype(BF16)
    iqo_ref[...] = rope(iq_ref[...], IDX_DIM, c64, s64).astype(BF16)
    k = rope(ka_ref[...], A_HEAD, c128, s128)
    ko_ref[...] = k
    ikw = ikw_ref[...]
    ik = rope(ikw, IDX_DIM, c64, s64)
    iko_ref[...] = ik[:, :IDX_DIM]
    iwo_ref[...] = ikw[:, IDX_DIM:IDX_DIM + IDX_HEADS]
    kt_ref[...] = k.T.astype(BF16)
    ikt_ref[...] = ik.T[:IDX_DIM, :].astype(BF16)


def rope_all(p, c128, s128, c64, s64):
    n = p.shape[0]
    tm = ROW_TILE
    row = lambda w, blk: pl.BlockSpec((tm, w), lambda i: (i, blk))
    return pl.pallas_call(
        _rope_kernel,
        grid=(n // tm,),
        in_specs=[row(A_DIM, C_Q // A_DIM), row(A_DIM, C_IQ // A_DIM),
                  row(A_KV_DIM, C_KA // A_KV_DIM), row(LANES, C_IK // LANES),
                  row(LANES, 0), row(LANES, 0), row(LANES, 0), row(LANES, 0)],
        out_specs=[row(A_DIM, 0), row(A_DIM, 0), row(A_KV_DIM, 0),
                   row(IDX_DIM, 0), row(IDX_HEADS, 0),
                   pl.BlockSpec((A_KV_DIM, tm), lambda i: (0, i)),
                   pl.BlockSpec((IDX_DIM, tm), lambda i: (0, i))],
        out_shape=[jax.ShapeDtypeStruct((n, A_DIM), BF16),
                   jax.ShapeDtypeStruct((n, IDX_HEADS * IDX_DIM), BF16),
                   jax.ShapeDtypeStruct((n, A_KV_DIM), F32),
                   jax.ShapeDtypeStruct((n, IDX_DIM), F32),
                   jax.ShapeDtypeStruct((n, IDX_HEADS), F32),
                   jax.ShapeDtypeStruct((A_KV_DIM, n), BF16),
                   jax.ShapeDtypeStruct((IDX_DIM, n), BF16)],
        compiler_params=_cparams(("parallel",)),
        name="rope",
    )(p, p, p, p, c128, s128, c64, s64)


def _rwkv_pre_kernel(t_real, tp, from_rows, *refs):
    (x_ref, lo_ref, px_ref, plo_ref, mu_ref, mulo_ref, w0_ref, wb_ref, a0_ref, ab_ref,
     gb_ref, kk_ref, ka_ref, rk_ref, e_ref, et_ref,
     r_o, ld_o, k_o, v_o, kk_o, b_o, g_o, bon_o) = refs
    x = x_ref[...]
    lo = lo_ref[...]
    tm = x.shape[0]
    if from_rows:
        i = pl.program_id(0)
        row = lax.broadcasted_iota(I32, (tm, 1), 0)
        t = (i * tm + row) % tp
        first = row == 0
        sx = jnp.where(first, px_ref[SUBLANES - 1:SUBLANES, :], pltpu.roll(x, 1, axis=0))
        slo = jnp.where(first, plo_ref[SUBLANES - 1:SUBLANES, :], pltpu.roll(lo, 1, axis=0))
        sx = jnp.where(t == 0, 0.0, sx)
        slo = jnp.where(t == 0, 0.0, slo)
        live = t < t_real
    else:
        sx = px_ref[...]
        slo = plo_ref[...]
        live = None
    xx = x + (sx - x) * mu_ref[...]
    xlo = lo + (slo - lo) * mulo_ref[...]
    r = xx[:, C_R:C_R + R_DIM]
    k = xx[:, C_K:C_K + R_DIM]
    v = xx[:, C_V:C_V + R_DIM]
    wa = xlo[:, :LANES]
    xg = xlo[:, LANES:]
    z = w0_ref[...] + _dot(jnp.tanh(wa).astype(BF16), wb_ref[...])
    nz = -z
    softplus = jnp.maximum(nz, 0.0) + jnp.log(1.0 + jnp.exp(-jnp.abs(nz)))
    logd = -jnp.exp(-softplus - 0.5)
    a = _sigmoid(a0_ref[...] + _dot(wa.astype(BF16), ab_ref[...]))
    g = _dot(_sigmoid(xg).astype(BF16), gb_ref[...])
    e, et = e_ref[...], et_ref[...]
    kkr = k * kk_ref[...]
    ss = _head_sums(kkr * kkr, e, et)
    kk = kkr / jnp.maximum(jnp.sqrt(ss), 1e-12)
    k2 = k * (1.0 + (a - 1.0) * ka_ref[...])
    bonus = _head_sums(r * k2 * rk_ref[...], e, et) * v
    b = kk * a
    if live is not None:
        zero = lambda y: jnp.where(live, y, 0.0)
        logd, k2s, vs, kk, b = zero(logd), zero(k2), zero(v), zero(kk), zero(b)
    else:
        k2s, vs = k2, v
    r_o[...] = r
    ld_o[...] = logd
    k_o[...] = k2s
    v_o[...] = vs
    kk_o[...] = kk
    b_o[...] = b
    g_o[...] = g
    bon_o[...] = bonus


def rwkv_pre(p, row0, nrows, prev, pw, t_real, tp):
    tm = min(ROW_TILE, nrows)
    blk0 = row0 // tm
    from_rows = prev is None
    xw = 3 * R_DIM
    cur_x = pl.BlockSpec((tm, xw), lambda i: (blk0 + i, 0))
    cur_lo = pl.BlockSpec((tm, LORA_W), lambda i: (blk0 + i, C_LORA // LORA_W))
    if from_rows:
        r8 = tm // SUBLANES
        prev_x = pl.BlockSpec((SUBLANES, xw), lambda i: (jnp.maximum((blk0 + i) * r8 - 1, 0), 0))
        prev_lo = pl.BlockSpec((SUBLANES, LORA_W),
                               lambda i: (jnp.maximum((blk0 + i) * r8 - 1, 0), C_LORA // LORA_W))
        prev_args = (p, p)
    else:
        prev_x = pl.BlockSpec((tm, xw), lambda i: (i, 0))
        prev_lo = pl.BlockSpec((tm, LORA_W), lambda i: (i, 0))
        prev_args = prev
    full = lambda a: pl.BlockSpec(a.shape, lambda i: (0,) * a.ndim)
    params = (pw["mu_x"], pw["mu_lo"], pw["w0"], pw["w_b"], pw["a0"], pw["a_b"], pw["g_b"],
              pw["k_k"], pw["k_a"], pw["r_k"], pw["e"], pw["et"])
    out = pl.BlockSpec((tm, R_DIM), lambda i: (i, 0))
    return pl.pallas_call(
        functools.partial(_rwkv_pre_kernel, t_real, tp, from_rows),
        grid=(nrows // tm,),
        in_specs=[cur_x, cur_lo, prev_x, prev_lo] + [full(a) for a in params],
        out_specs=[out] * 8,
        out_shape=[jax.ShapeDtypeStruct((nrows, R_DIM), F32)] * 8,
        compiler_params=_cparams(("parallel",)),
        name="rwkv_pre_rows" if from_rows else "rwkv_pre_step",
    )(p, p, *prev_args, *params)


def _rwkv_scan_kernel(r_ref, ld_ref, k_ref, v_ref, kk_ref, b_ref, y_ref, s_ref, ss_scr):
    c = pl.program_id(1)

    @pl.when(c == 0)
    def _():
        ss_scr[...] = jnp.zeros_like(ss_scr)

    n = CHUNK
    n2 = 2 * n
    pairs = R_HEADS // 2
    ld_all = ld_ref[...]
    ri = lax.broadcasted_iota(I32, (n, n), 0)
    ci = lax.broadcasted_iota(I32, (n, n), 1)
    cum_all = _dot((ci <= ri).astype(F32), ld_all, HIGHEST)
    head0 = lax.broadcasted_iota(I32, (n, LANES), 1) < R_HEAD
    r4 = lax.broadcasted_iota(I32, (2 * n2, 2 * n2), 0)
    c4 = lax.broadcasted_iota(I32, (2 * n2, 2 * n2), 1)
    tri = (c4 % n) < (r4 % n) + jnp.where(r4 < n2, 0, 1)
    re = lax.broadcasted_iota(I32, (n2, n2), 0)
    ce = lax.broadcasted_iota(I32, (n2, n2), 1)
    eye = (re == ce).astype(F32)

    def stack(x):
        return jnp.concatenate([jnp.where(head0, x, 0.0), jnp.where(head0, 0.0, x)], axis=0)

    ar, bk, v2, ss, e_last = [], [], [], [], []
    for p in range(pairs):
        sl = slice(p * LANES, (p + 1) * LANES)
        cum, ld = cum_all[:, sl], ld_all[:, sl]
        e_pos = jnp.exp(cum)
        e_neg = jnp.exp(-cum)
        at = -kk_ref[:, sl] * jnp.exp(cum - ld)
        ar.append(jnp.concatenate([stack(at), stack(r_ref[:, sl] * e_pos)], axis=0).astype(BF16))
        bk.append(jnp.concatenate([stack(b_ref[:, sl] * e_neg), stack(k_ref[:, sl] * e_neg)],
                                  axis=0).astype(BF16))
        v2.append(stack(v_ref[:, sl]).astype(BF16))
        ss.append(ss_scr[p])
        e_last.append(e_pos[n - 1:n, :])
    xy0 = [_dot_nt(ar[p], ss[p].astype(BF16)) for p in range(pairs)]
    sc = [jnp.where(tri, _dot_nt(ar[p], bk[p]), 0.0) for p in range(pairs)]
    lp = [s[:n2, :n2] for s in sc]
    t = [eye + l for l in lp]
    m = 1
    while 2 * m < n:
        lpb = [l.astype(BF16) for l in lp]
        lp = [_dot(l, l) for l in lpb]
        t = [t[p] + _dot(t[p].astype(BF16), lp[p].astype(BF16)) for p in range(pairs)]
        m *= 2
    w = [xy0[p][:n2] + _dot(sc[p][:n2, n2:].astype(BF16), v2[p]) for p in range(pairs)]
    u = [_dot(t[p].astype(BF16), w[p].astype(BF16)) for p in range(pairs)]
    uv = [jnp.concatenate([u[p].astype(BF16), v2[p]], axis=0) for p in range(pairs)]
    y = [xy0[p][n2:] + _dot(sc[p][n2:].astype(BF16), uv[p]) for p in range(pairs)]
    upd = [_dot_tn(uv[p], bk[p]) for p in range(pairs)]
    for p in range(pairs):
        y_ref[:, p * LANES:(p + 1) * LANES] = y[p][:n] + y[p][n:]
        ss_scr[p] = (ss[p] + upd[p]) * e_last[p]

    @pl.when(c == pl.num_programs(1) - 1)
    def _():
        for p in range(R_HEADS // 2):
            ss = ss_scr[p]
            s_ref[0, 2 * p] = ss[:R_HEAD, :R_HEAD]
            s_ref[0, 2 * p + 1] = ss[R_HEAD:, R_HEAD:]


def rwkv_scan(r, ld, k, v, kk, b, batch, tp):
    nchunk = tp // CHUNK
    blk = pl.BlockSpec((CHUNK, R_DIM), lambda bi, c: (bi * nchunk + c, 0))
    return pl.pallas_call(
        _rwkv_scan_kernel,
        grid=(batch, nchunk),
        in_specs=[blk] * 6,
        out_specs=[blk, pl.BlockSpec((1, R_HEADS, R_HEAD, R_HEAD), lambda bi, c: (bi, 0, 0, 0))],
        out_shape=[jax.ShapeDtypeStruct((batch * tp, R_DIM), F32),
                   jax.ShapeDtypeStruct((batch, R_HEADS, R_HEAD, R_HEAD), F32)],
        scratch_shapes=[pltpu.VMEM((R_HEADS // 2, LANES, LANES), F32)],
        compiler_params=_cparams(("parallel", "arbitrary")),
        name="rwkv_scan",
    )(r, ld, k, v, kk, b)


STEP_ROWS = 16


def _rwkv_step_kernel(r_ref, ld_ref, k_ref, v_ref, kk_ref, b_ref, s_ref, y_ref, so_ref):
    r, k = r_ref[0], k_ref[0]
    dec = jnp.exp(ld_ref[0])
    na = -kk_ref[0]
    b = b_ref[0]
    v = v_ref[0]
    ys = []
    for v0 in range(0, R_HEAD, STEP_ROWS):
        rows = range(v0, v0 + STEP_ROWS)
        s = [s_ref[0, vi] for vi in rows]
        sa = [jnp.sum(x * na, axis=0, keepdims=True) for x in s]
        s_new = [x * dec + a * b + v[vi:vi + 1, :] * k for x, a, vi in zip(s, sa, rows)]
        ys += [jnp.sum(x * r, axis=0, keepdims=True) for x in s_new]
        for x, vi in zip(s_new, rows):
            so_ref[0, vi] = x
    y_ref[0] = jnp.concatenate(ys, axis=0)


def rwkv_step(r, ld, k, v, kk, b, state):
    s = state.shape[-1]
    vec = pl.BlockSpec((1, R_HEAD, s), lambda h: (h, 0, 0))
    st = pl.BlockSpec((1, R_HEAD, R_HEAD, s), lambda h: (h, 0, 0, 0))
    return pl.pallas_call(
        _rwkv_step_kernel,
        grid=(R_HEADS,),
        in_specs=[vec] * 6 + [st],
        out_specs=[vec, st],
        out_shape=[jax.ShapeDtypeStruct((R_HEADS, R_HEAD, s), F32),
                   jax.ShapeDtypeStruct(state.shape, F32)],
        compiler_params=_cparams(("parallel",)),
        name="rwkv_step",
    )(r, ld, k, v, kk, b, state)


def _select_topk(score, allowed, n_sel, store):
    bits = lax.bitcast_convert_type(score, I32)
    key = jnp.where(bits < 0, bits ^ jnp.int32(0x7FFFFFFF), bits)
    key = jnp.where(allowed, key, jnp.int32(INT_MIN))
    m, w = score.shape
    one, zero = jnp.ones((), BF16), jnp.zeros((), BF16)

    def byte(shift):
        if shift == 24:
            d = lax.shift_right_arithmetic(key, 24) + 128
        else:
            d = lax.shift_right_logical(key, shift) & 255
        return d.astype(F32).astype(BF16)

    def count(flags):
        acc = flags[:, :LANES]
        for c in range(LANES, w, LANES):
            acc = acc + flags[:, c:c + LANES]
        return jnp.sum(acc.astype(F32), axis=1, keepdims=True)

    need = jnp.full((m, 1), float(n_sel), F32)
    x = byte(24)
    tau = jnp.zeros((m, 1), I32)
    for shift in (24, 16, 8, 0):
        def body(it, t, x=x, need=need):
            step = lax.shift_left(jnp.int32(1), jnp.int32(6) - 2 * it).astype(F32)
            cnts = [count(jnp.where(x >= (t + mult * step).astype(BF16), one, zero))
                    for mult in (1.0, 2.0, 3.0)]
            hits = sum(jnp.where(cnt >= need, 1.0, 0.0) for cnt in cnts)
            return t + hits * step

        t = lax.fori_loop(0, 4, body, jnp.zeros((m, 1), F32))
        tb = t.astype(BF16)
        need = need - count(jnp.where(x > tb, one, zero))
        digit = t.astype(I32) - (128 if shift == 24 else 0)
        tau = tau | lax.shift_left(digit, shift)
        if shift:
            x = jnp.where(x == tb, byte(shift - 8), -one)
    store(jnp.logical_and(key >= tau, allowed))
    ties = count(jnp.where(x == tb, one, zero))
    crowded = jnp.logical_and(ties > need, tau > INT_MIN)

    @pl.when(jnp.max(jnp.where(crowded, 1.0, 0.0)) > 0.0)
    def _():
        above = jnp.logical_and(key > tau, allowed)
        equal = jnp.logical_and(key == tau, allowed)
        ra = lax.broadcasted_iota(I32, (KEY_TILE, KEY_TILE), 0)
        rb = lax.broadcasted_iota(I32, (KEY_TILE, KEY_TILE), 1)
        upto = (ra <= rb).astype(BF16)
        seen = jnp.zeros((m, 1), F32)
        keep = []
        for c0 in range(0, w, KEY_TILE):
            c1 = min(c0 + KEY_TILE, w)
            eq = equal[:, c0:c1]
            flags = jnp.where(eq, 1.0, 0.0)
            rank = seen + _dot(flags.astype(BF16), upto[:c1 - c0, :c1 - c0])
            keep.append(jnp.logical_and(eq, rank <= need))
            seen = seen + jnp.sum(flags, axis=1, keepdims=True)
        store(jnp.logical_or(above, jnp.concatenate(keep, axis=1)))


KEY_TILE = 256
Q_TILES_PER_EXTENT = 2


def _dsa_prompt_block(n_sel, tk, i, q_ref, iq_ref, iw_ref, ikt_ref, kt_ref, v_ref, o_ref, sc_ref):
    tq = q_ref.shape[0]
    iw = iw_ref[...] * ((IDX_HEADS * IDX_DIM) ** -0.5)
    iq = iq_ref[...]
    iq_h = [iq[:, h * IDX_DIM:(h + 1) * IDX_DIM] for h in range(IDX_HEADS)]
    iw_h = [iw[:, h:h + 1] for h in range(IDX_HEADS)]
    for c0 in range(0, tk, KEY_TILE):
        c1 = min(c0 + KEY_TILE, tk)
        ikb = ikt_ref[:, c0:c1]
        acc = jnp.maximum(_dot(iq_h[0], ikb), 0.0) * iw_h[0]
        for h in range(1, IDX_HEADS):
            acc = acc + jnp.maximum(_dot(iq_h[h], ikb), 0.0) * iw_h[h]
        sc_ref[:, c0:c1] = acc
    qpos = i * tq + lax.broadcasted_iota(I32, (tq, 1), 0)
    kpos = lax.broadcasted_iota(I32, (1, tk), 1)

    def store_bias(mask):
        sc_ref[:, :tk] = jnp.where(mask, 0.0, NEG_BIG)

    _select_topk(sc_ref[:, :tk], kpos <= qpos, n_sel, store_bias)
    bias = sc_ref[:, :tk]
    q = q_ref[...]
    rep = A_HEADS // A_KV_HEADS
    for g in range(A_KV_HEADS):
        kg = kt_ref[g * A_HEAD:(g + 1) * A_HEAD, :tk]
        vg = v_ref[:tk, g * A_HEAD:(g + 1) * A_HEAD].astype(BF16)
        for rr in range(rep):
            h = g * rep + rr
            s = _dot(q[:, h * A_HEAD:(h + 1) * A_HEAD], kg) + bias
            m = jnp.max(s, axis=1, keepdims=True)
            p = jnp.exp(s - m)
            l = jnp.sum(p, axis=1, keepdims=True)
            o_ref[:, h * A_HEAD:(h + 1) * A_HEAD] = _dot(p.astype(BF16), vg) / l


def _dsa_prompt_kernel(n_sel, *refs):
    i = pl.program_id(1)
    tq = refs[0].shape[0]
    tp = refs[5].shape[0]
    nq = tp // tq
    for lo in range(0, nq, Q_TILES_PER_EXTENT):
        hi = min(lo + Q_TILES_PER_EXTENT, nq)

        @pl.when(jnp.logical_and(i >= lo, i < hi))
        def _(hi=hi):
            _dsa_prompt_block(n_sel, hi * tq, i, *refs)


def dsa_prompt(q, iq, iw, ikt, kt, p, batch, tp, n_sel):
    nq = tp // Q_TILE
    qrow = lambda w: pl.BlockSpec((Q_TILE, w), lambda b, i: (b * nq + i, 0))
    keys = lambda w, blk: pl.BlockSpec((tp, w), lambda b, i: (b, blk))
    keys_t = lambda w: pl.BlockSpec((w, tp), lambda b, i: (0, b))
    return pl.pallas_call(
        functools.partial(_dsa_prompt_kernel, n_sel),
        grid=(batch, nq),
        in_specs=[qrow(A_DIM), qrow(IDX_HEADS * IDX_DIM), qrow(IDX_HEADS),
                  keys_t(IDX_DIM), keys_t(A_KV_DIM), keys(A_KV_DIM, C_VA // A_KV_DIM)],
        out_specs=qrow(A_DIM),
        out_shape=jax.ShapeDtypeStruct((batch * tp, A_DIM), F32),
        scratch_shapes=[pltpu.VMEM((Q_TILE, tp), F32)],
        compiler_params=_cparams(("parallel", "parallel")),
        name="dsa_prompt",
    )(q, iq, iw, ikt, kt, p)


SCORE_SEQS = 4
ATTN_SEQS = 2


def _dsa_step_score_kernel(n_pages, page, pt_ref, iq_ref, iw_ref, ikn_ref, *refs):
    sb = iq_ref.shape[0]
    pages = refs[:sb * n_pages]
    o_ref = refs[sb * n_pages]
    lane = lax.broadcasted_iota(I32, (1, LANES), 1)
    for q in range(sb):
        iq = iq_ref[q]
        iw = iw_ref[q] * ((IDX_HEADS * IDX_DIM) ** -0.5)
        for j in range(n_pages):
            d = _dot(iq, pages[q * n_pages + j][...].astype(BF16))
            o_ref[q, :, j * page:(j + 1) * page] = jnp.sum(jnp.maximum(d, 0.0) * iw, axis=0,
                                                           keepdims=True)
        dn = jnp.sum(iq.astype(F32) * ikn_ref[q], axis=1, keepdims=True)
        sn = jnp.sum(jnp.maximum(dn, 0.0) * iw, axis=0, keepdims=True)
        o_ref[q, :, n_pages * page:] = jnp.where(lane == 0, sn, 0.0)


def _seqs_per_step(s, want):
    return want if s % want == 0 else 1


def dsa_step_scores(pt_flat, iq, iw, ik_new, cik2d, n_pages, page):
    s = iq.shape[0]
    sb = _seqs_per_step(s, SCORE_SEQS)
    kw = n_pages * page + LANES
    page_spec = lambda q, j: pl.BlockSpec(
        (IDX_DIM, page), lambda i, pt: (pt[(i * sb + q) * n_pages + j], 0))
    grid_spec = pltpu.PrefetchScalarGridSpec(
        num_scalar_prefetch=1,
        grid=(s // sb,),
        in_specs=[pl.BlockSpec((sb, IDX_HEADS, IDX_DIM), lambda i, pt: (i, 0, 0)),
                  pl.BlockSpec((sb, IDX_HEADS, 1), lambda i, pt: (i, 0, 0)),
                  pl.BlockSpec((sb, 1, IDX_DIM), lambda i, pt: (i, 0, 0))]
                 + [page_spec(q, j) for q in range(sb) for j in range(n_pages)],
        out_specs=pl.BlockSpec((sb, 1, kw), lambda i, pt: (i, 0, 0)),
    )
    return pl.pallas_call(
        functools.partial(_dsa_step_score_kernel, n_pages, page),
        grid_spec=grid_spec,
        out_shape=jax.ShapeDtypeStruct((s, 1, kw), F32),
        compiler_params=_cparams(("arbitrary",)),
        name="dsa_step_scores",
    )(pt_flat, iq, iw, ik_new, *([cik2d] * (sb * n_pages)))


def _dsa_step_select_kernel(n_sel, past, sc_ref, o_ref):
    sc = sc_ref[...]
    kpos = lax.broadcasted_iota(I32, sc.shape, 1)

    def store(mask):
        o_ref[...] = jnp.where(mask, 1.0, 0.0)

    _select_topk(sc, kpos <= past, n_sel, store)


def dsa_step_select(sc, n_sel, past):
    return pl.pallas_call(
        functools.partial(_dsa_step_select_kernel, n_sel, past),
        out_shape=jax.ShapeDtypeStruct(sc.shape, F32),
        compiler_params=pltpu.CompilerParams(vmem_limit_bytes=VMEM_LIMIT),
        name="dsa_step_select",
    )(sc)


def _dsa_step_attn_kernel(n_pages, page, pt_ref, q_ref, kn_ref, vn_ref, sel_ref, ex_ref, *refs):
    sb = q_ref.shape[0]
    kps = refs[:sb * n_pages]
    vps = refs[sb * n_pages:2 * sb * n_pages]
    o_ref = refs[2 * sb * n_pages]
    rep = A_HEADS // A_KV_HEADS
    w2 = page * A_KV_HEADS
    hrow = lax.broadcasted_iota(I32, (A_HEADS, w2), 0)
    col = lax.broadcasted_iota(I32, (A_HEADS, w2), 1)
    own = (col % A_KV_HEADS) == (hrow // rep)
    h8 = lax.broadcasted_iota(I32, (A_HEADS, A_HEAD), 0)
    ex = ex_ref[...]
    for u in range(sb):
        kp = kps[u * n_pages:(u + 1) * n_pages]
        vp = vps[u * n_pages:(u + 1) * n_pages]
        q = q_ref[u]
        logits = []
        for j in range(n_pages):
            s = _dot_nt(q, kp[j][...].astype(BF16))
            selj = _dot(sel_ref[u, :, j * page:(j + 1) * page].astype(BF16), ex)
            logits.append(jnp.where(jnp.logical_and(selj > 0.5, own), s, NEG_BIG))
        kn = jnp.where(h8 < rep, kn_ref[u, 0:1, :], kn_ref[u, 1:2, :])
        vn = jnp.where(h8 < rep, vn_ref[u, 0:1, :], vn_ref[u, 1:2, :])
        sn = jnp.sum(q.astype(F32) * kn, axis=1, keepdims=True)
        seln = sel_ref[u, :, n_pages * page:n_pages * page + 1]
        sn = jnp.where(seln > 0.5, sn, NEG_BIG)
        m = sn
        for s in logits:
            m = jnp.maximum(m, jnp.max(s, axis=1, keepdims=True))
        pn = jnp.exp(sn - m)
        l = pn
        acc = pn * vn
        for j in range(n_pages):
            p = jnp.exp(logits[j] - m)
            l = l + jnp.sum(p, axis=1, keepdims=True)
            acc = acc + _dot(p.astype(BF16), vp[j][...].astype(BF16))
        o_ref[u] = acc / l


def dsa_step_attn(pt_flat, q, k_new, v_new, sel, expand, ck2d, cv2d, n_pages, page):
    s = q.shape[0]
    sb = _seqs_per_step(s, ATTN_SEQS)
    kw = sel.shape[-1]
    w2 = page * A_KV_HEADS
    page_spec = lambda u, j: pl.BlockSpec(
        (w2, A_HEAD), lambda i, pt: (pt[(i * sb + u) * n_pages + j], 0))
    pages = [page_spec(u, j) for u in range(sb) for j in range(n_pages)]
    grid_spec = pltpu.PrefetchScalarGridSpec(
        num_scalar_prefetch=1,
        grid=(s // sb,),
        in_specs=[pl.BlockSpec((sb, A_HEADS, A_HEAD), lambda i, pt: (i, 0, 0)),
                  pl.BlockSpec((sb, A_KV_HEADS, A_HEAD), lambda i, pt: (i, 0, 0)),
                  pl.BlockSpec((sb, A_KV_HEADS, A_HEAD), lambda i, pt: (i, 0, 0)),
                  pl.BlockSpec((sb, 1, kw), lambda i, pt: (i, 0, 0)),
                  pl.BlockSpec((page, w2), lambda i, pt: (0, 0))] + pages * 2,
        out_specs=pl.BlockSpec((sb, A_HEADS, A_HEAD), lambda i, pt: (i, 0, 0)),
    )
    return pl.pallas_call(
        functools.partial(_dsa_step_attn_kernel, n_pages, page),
        grid_spec=grid_spec,
        out_shape=jax.ShapeDtypeStruct((s, A_HEADS, A_HEAD), F32),
        compiler_params=_cparams(("arbitrary",)),
        name="dsa_step_attn",
    )(pt_flat, q, k_new, v_new, sel, expand, *([ck2d] * (sb * n_pages)), *([cv2d] * (sb * n_pages)))


def _pack_bf16_pairs(x):
    w = x.shape[1] // 2
    hi = lax.bitcast_convert_type(x[:, :w].astype(BF16).astype(F32), I32)
    lo = lax.bitcast_convert_type(x[:, w:].astype(BF16).astype(F32), I32)
    return hi | lax.shift_right_logical(lo, 16)


def _unpack_bf16_pairs(p):
    hi = lax.bitcast_convert_type(p & jnp.int32(-65536), F32)
    lo = lax.bitcast_convert_type(lax.shift_left(p, 16), F32)
    return hi, lo


def _mix_kernel(alpha, prompt_blocks, x_ref, yp_ref, bonp_ref, gp_ref, ap_ref, ys_ref, bons_ref, gs_ref,
                as_ref, e_ref, et_ref, gng_ref, gnb_ref, l0g_ref, l0b_ref, wo_ref, l1g_ref, l1b_ref,
                wr_ref, h_ref, sc_ref, pk_ref):
    is_prompt = pl.program_id(0) < prompt_blocks
    pick = lambda p_ref, s_ref: jnp.where(is_prompt, p_ref[...], s_ref[...])
    e, et = e_ref[...], et_ref[...]
    y = pick(yp_ref, ys_ref)
    inv = 1.0 / R_HEAD
    mu = _head_sums(y, e, et) * inv
    d = y - mu
    var = _head_sums(d * d, e, et) * inv
    yn = d * lax.rsqrt(var + GN_EPS) * gng_ref[...] + gnb_ref[...]
    r_out = (yn + pick(bonp_ref, bons_ref)) * pick(gp_ref, gs_ref)
    mix = (_dot(r_out.astype(BF16), wo_ref[:R_DIM, :])
           + _dot(pick(ap_ref, as_ref).astype(BF16), wo_ref[R_DIM:, :]))
    h0 = _layer_norm(x_ref[...], l0g_ref[...], l0b_ref[...])
    h1 = _layer_norm(alpha * h0 + mix, l1g_ref[...], l1b_ref[...])
    h_ref[...] = h1
    w_hi, w_lo = _split_bf16(wr_ref[...])
    h_hi, h_lo = _split_bf16(h1)
    logits = _dot(h_hi, w_hi) + _dot(h_lo, w_hi) + _dot(h_hi, w_lo)
    sc_ref[...] = _sigmoid(logits.T[:N_EXPERTS, :])
    pk_ref[...] = _pack_bf16_pairs(h1)


def mix_ln1_router(x, prompt_parts, step_parts, pw, alpha):
    n, d = x.shape
    tm = ROW_TILE
    pb = prompt_parts[0].shape[0] // tm
    row = lambda w: pl.BlockSpec((tm, w), lambda i: (i, 0))
    head = lambda w: pl.BlockSpec((tm, w), lambda i: (jnp.minimum(i, pb - 1), 0))
    tail = lambda w: pl.BlockSpec((tm, w), lambda i: (jnp.maximum(i - pb, 0), 0))
    full = lambda a: pl.BlockSpec(a.shape, lambda i: (0,) * a.ndim)
    params = (pw["e"], pw["et"], pw["gn_g"], pw["gn_b"], pw["ln0_g"], pw["ln0_b"], pw["w_out"],
              pw["ln1_g"], pw["ln1_b"], pw["w_router"])
    widths = (R_DIM, R_DIM, R_DIM, A_DIM)
    return pl.pallas_call(
        functools.partial(_mix_kernel, alpha, pb),
        grid=(n // tm,),
        in_specs=[row(d)] + [head(w) for w in widths] + [tail(w) for w in widths]
                 + [full(a) for a in params],
        out_specs=[row(d), pl.BlockSpec((N_EXPERTS, tm), lambda i: (0, i)), row(d // 2)],
        out_shape=[jax.ShapeDtypeStruct((n, d), F32),
                   jax.ShapeDtypeStruct((N_EXPERTS, n), F32),
                   jax.ShapeDtypeStruct((n, d // 2), I32)],
        compiler_params=_cparams(("parallel",)),
        name="mix_ln1_router",
    )(x, *prompt_parts, *step_parts, *params)


def _route_kernel(sc_ref, bias_ref, idx_ref, gate_ref, pos_ref, cnt_ref, cnt_scr):
    scores = sc_ref[...]
    biased = scores + bias_ref[...]
    tn = scores.shape[1]
    per = N_EXPERTS // N_EXPERT_GROUPS
    sub = lax.broadcasted_iota(I32, (per, tn), 0)
    grp_rows = []
    for g in range(N_EXPERT_GROUPS):
        xg = biased[g * per:(g + 1) * per, :]
        m1 = jnp.max(xg, axis=0, keepdims=True)
        first = jnp.min(jnp.where(xg == m1, sub, per), axis=0, keepdims=True)
        m2 = jnp.max(jnp.where(sub == first, -jnp.inf, xg), axis=0, keepdims=True)
        grp_rows.append(m1 + m2)
    grp = jnp.concatenate(grp_rows, axis=0)
    gi = lax.broadcasted_iota(I32, (N_EXPERT_GROUPS, tn), 0)
    gsel = jnp.zeros((N_EXPERT_GROUPS, tn), jnp.bool_)
    for _ in range(TOPK_GROUPS):
        m = jnp.max(grp, axis=0, keepdims=True)
        first = jnp.min(jnp.where(grp == m, gi, N_EXPERT_GROUPS), axis=0, keepdims=True)
        hit = gi == first
        gsel = jnp.logical_or(gsel, hit)
        grp = jnp.where(hit, -jnp.inf, grp)
    ei = lax.broadcasted_iota(I32, (N_EXPERTS, tn), 0)
    emask = jnp.concatenate(
        [jnp.broadcast_to(gsel[g:g + 1, :], (per, tn)) for g in range(N_EXPERT_GROUPS)], axis=0)
    cand = jnp.where(emask, biased, -jnp.inf)
    idxs, gates, hits = [], [], []
    for _ in range(TOP_K):
        m = jnp.max(cand, axis=0, keepdims=True)
        first = jnp.min(jnp.where(cand == m, ei, N_EXPERTS), axis=0, keepdims=True)
        hit = ei == first
        idxs.append(first)
        hits.append(hit)
        gates.append(jnp.sum(jnp.where(hit, scores, 0.0), axis=0, keepdims=True))
        cand = jnp.where(hit, -jnp.inf, cand)
    gate = jnp.concatenate(gates, axis=0)
    gate = gate / jnp.sum(gate, axis=0, keepdims=True) * ROUTED_SCALE
    idx_ref[...] = jnp.concatenate(idxs, axis=0)
    gate_ref[...] = gate
    chosen = hits[0]
    for hit in hits[1:]:
        chosen = jnp.logical_or(chosen, hit)
    onehot = jnp.where(chosen, 1.0, 0.0)
    ta = lax.broadcasted_iota(I32, (tn, tn), 0)
    tb = lax.broadcasted_iota(I32, (tn, tn), 1)
    prefix = _dot(onehot.astype(BF16), (ta < tb).astype(BF16))

    @pl.when(pl.program_id(0) == 0)
    def _():
        cnt_scr[...] = jnp.zeros_like(cnt_scr)

    rank = prefix + cnt_scr[:, 0:1]
    pos_ref[...] = jnp.concatenate(
        [jnp.sum(jnp.where(hit, rank, 0.0), axis=0, keepdims=True) for hit in hits], axis=0).astype(I32)
    cnt_scr[...] = cnt_scr[...] + jnp.sum(onehot, axis=1, keepdims=True)
    cnt_ref[...] = cnt_scr[...].astype(I32)


def route(scores_t, e_bias):
    n = scores_t.shape[1]
    tn = ROW_TILE
    tok = pl.BlockSpec((TOP_K, tn), lambda i: (0, i))
    return pl.pallas_call(
        _route_kernel,
        grid=(n // tn,),
        in_specs=[pl.BlockSpec((N_EXPERTS, tn), lambda i: (0, i)),
                  pl.BlockSpec((N_EXPERTS, 1), lambda i: (0, 0))],
        out_specs=[tok, tok, tok, pl.BlockSpec((N_EXPERTS, LANES), lambda i: (0, 0))],
        out_shape=[jax.ShapeDtypeStruct((TOP_K, n), I32), jax.ShapeDtypeStruct((TOP_K, n), F32),
                   jax.ShapeDtypeStruct((TOP_K, n), I32),
                   jax.ShapeDtypeStruct((N_EXPERTS, LANES), I32)],
        scratch_shapes=[pltpu.VMEM((N_EXPERTS, LANES), F32)],
        compiler_params=_cparams(("arbitrary",)),
        name="route",
    )(scores_t, e_bias)


def _dispatch_kernel(nb, dest_ref, segend_ref, cnt_ref, nu_ref, x_ref, inv0_ref, o_ref, invo_ref,
                     inv_ref, zbuf, sem, zsem, isem):
    tm = x_ref.shape[0]
    row0 = pl.program_id(0) * tm

    @pl.when(pl.program_id(0) == 0)
    def _():
        c = pltpu.make_async_copy(inv0_ref, inv_ref, isem)
        c.start()
        c.wait()

    def fill(start):
        return pltpu.make_async_copy(zbuf, o_ref.at[pl.ds(pl.multiple_of(start, EXPERT_TILE),
                                                          EXPERT_TILE)], zsem)

    @pl.when(pl.program_id(0) == 0)
    def _():
        zbuf[...] = jnp.zeros_like(zbuf)

        def each_expert(fn):
            def body(e, carry):
                @pl.when(cnt_ref[e] > 0)
                def _():
                    fn(fill(segend_ref[e] - EXPERT_TILE))
                return carry
            lax.fori_loop(0, N_EXPERTS, body, 0)

        def each_free_block(fn):
            def body(b, carry):
                fn(fill(b * EXPERT_TILE))
                return carry
            lax.fori_loop(nu_ref[0], nb, body, 0)

        each_expert(lambda c: c.start())
        each_free_block(lambda c: c.start())
        each_expert(lambda c: c.wait())
        each_free_block(lambda c: c.wait())

    def start(i, carry):
        for j in range(TOP_K):
            d = dest_ref[i * TOP_K + j]
            inv_ref[d] = (row0 + i) * TOP_K + j
            pltpu.make_async_copy(x_ref.at[pl.ds(i, 1)], o_ref.at[pl.ds(d, 1)], sem).start()
        return carry

    lax.fori_loop(0, tm, start, 0)
    for j in range(TOP_K):
        pltpu.make_async_copy(x_ref, o_ref.at[pl.ds(0, tm)], sem).wait()

    @pl.when(pl.program_id(0) == pl.num_programs(0) - 1)
    def _():
        c = pltpu.make_async_copy(inv_ref, invo_ref, isem)
        c.start()
        c.wait()


def moe_dispatch(dest_flat, seg_end, counts, n_used, xpk, inv_default, nb):
    n, w = xpk.shape
    tm = ROW_TILE
    rows = nb * EXPERT_TILE
    smem = lambda: pl.BlockSpec(memory_space=pltpu.SMEM)
    hbm = lambda: pl.BlockSpec(memory_space=pl.ANY)
    return pl.pallas_call(
        functools.partial(_dispatch_kernel, nb),
        grid=(n // tm,),
        in_specs=[pl.BlockSpec((tm * TOP_K,), lambda i: (i,), memory_space=pltpu.SMEM),
                  smem(), smem(), smem(),
                  pl.BlockSpec((tm, w), lambda i: (i, 0)), hbm()],
        out_specs=[hbm(), hbm()],
        out_shape=[jax.ShapeDtypeStruct((rows, w), I32), jax.ShapeDtypeStruct((rows,), I32)],
        scratch_shapes=[pltpu.SMEM((rows,), I32), pltpu.VMEM((EXPERT_TILE, w), I32),
                        pltpu.SemaphoreType.DMA(()), pltpu.SemaphoreType.DMA(()),
                        pltpu.SemaphoreType.DMA(())],
        compiler_params=_cparams(("arbitrary",)),
        name="moe_dispatch",
    )(dest_flat, seg_end, counts, n_used, xpk, inv_default)


def _experts_kernel(n_slots, nb, be_ref, nxt_ref, slot_ref, nu_ref, inv_ref, x_ref, wg_ref, wu_ref,
                    wd_ref, o_ref, wg_f, wu_f, wd_f, wg_s, wu_s, wd_s, obuf_a, obuf_b, sems, osems):
    i = pl.program_id(0)
    n_used = nu_ref[0]
    used = i < n_used
    prev = be_ref[jnp.maximum(i - 1, 0)]
    fresh = jnp.logical_and(used, jnp.logical_or(i == 0, be_ref[i] != prev))
    bm = x_ref.shape[0]

    obufs = (obuf_a, obuf_b)

    def scatter_rows(blk, par):
        for r in range(bm):
            pltpu.make_async_copy(obufs[par].at[pl.ds(r, 1)],
                                  o_ref.at[pl.ds(inv_ref[blk * bm + r], 1)], osems.at[par]).start()

    def wait_rows(par):
        pltpu.make_async_copy(obufs[par], o_ref.at[pl.ds(0, bm)], osems.at[par]).wait()

    def by_parity(cond, fn):
        for par in range(2):
            @pl.when(jnp.logical_and(cond, i % 2 == par))
            def _(par=par):
                fn(par)

    @pl.when(i == 0)
    def _():
        obuf_b[...] = jnp.zeros_like(obuf_b)
        spare = [pltpu.make_async_copy(obuf_b, o_ref.at[pl.ds(n_slots + e * bm, bm)], osems.at[1])
                 for e in range(N_EXPERTS)]
        for c in spare:
            c.start()
        for c in spare:
            c.wait()

    by_parity(jnp.logical_and(i >= 2, i - 2 < n_used), wait_rows)

    def weight_copies(e, slot):
        return (pltpu.make_async_copy(wg_ref.at[e], wg_f.at[slot], sems.at[slot, 0]),
                pltpu.make_async_copy(wu_ref.at[e], wu_f.at[slot], sems.at[slot, 1]),
                pltpu.make_async_copy(wd_ref.at[e], wd_f.at[slot], sems.at[slot, 2]))

    @pl.when(jnp.logical_and(used, i == 0))
    def _():
        for c in weight_copies(be_ref[0], 0):
            c.start()

    @pl.when(fresh)
    def _():
        slot = slot_ref[i]
        for c in weight_copies(be_ref[i], slot):
            c.wait()

        @pl.when(nxt_ref[i] >= 0)
        def _():
            for c in weight_copies(nxt_ref[i], 1 - slot):
                c.start()

        wg_s[...] = wg_f[slot].astype(BF16)
        wu_s[...] = wu_f[slot].astype(BF16)
        wd_s[...] = wd_f[slot].astype(BF16)

    def compute(par):
        hi, lo = _unpack_bf16_pairs(x_ref[...])
        hi, lo = hi.astype(BF16), lo.astype(BF16)
        half = hi.shape[1]
        gp = _dot(hi, wg_s[:half, :]) + _dot(lo, wg_s[half:, :])
        up = _dot(hi, wu_s[:half, :]) + _dot(lo, wu_s[half:, :])
        act = gp * _sigmoid(gp) * up
        obufs[par][...] = _pack_bf16_pairs(_dot(act.astype(BF16), wd_s[...]))

    @pl.when(jnp.logical_and(used, i == 0))
    def _():
        compute(0)

    def send_prev_and_compute(par):
        scatter_rows(i - 1, 1 - par)
        compute(par)

    by_parity(jnp.logical_and(used, i > 0), send_prev_and_compute)
    by_parity(jnp.logical_and(i == n_used, i > 0), lambda par: scatter_rows(i - 1, 1 - par))

    @pl.when(jnp.logical_and(i == nb - 1, n_used == nb - 1))
    def _():
        wait_rows((nb - 2) % 2)


def moe_experts(blk_e, nxt_e, slot, n_used, inv, xs, n_slots, w_gate, w_up, w_down):
    w = xs.shape[1]
    nb = xs.shape[0] // EXPERT_TILE
    _, d, de = w_gate.shape
    grid_spec = pltpu.PrefetchScalarGridSpec(
        num_scalar_prefetch=5,
        grid=(nb,),
        in_specs=[pl.BlockSpec((EXPERT_TILE, w), lambda i, be, nx, sl, nu, iv: (jnp.minimum(i, nu[0] - 1), 0)),
                  pl.BlockSpec(memory_space=pl.ANY), pl.BlockSpec(memory_space=pl.ANY),
                  pl.BlockSpec(memory_space=pl.ANY)],
        out_specs=pl.BlockSpec(memory_space=pl.ANY),
        scratch_shapes=[pltpu.VMEM((2, d, de), F32), pltpu.VMEM((2, d, de), F32),
                        pltpu.VMEM((2, de, d), F32),
                        pltpu.VMEM((d, de), BF16), pltpu.VMEM((d, de), BF16),
                        pltpu.VMEM((de, d), BF16), pltpu.VMEM((EXPERT_TILE, w), I32),
                        pltpu.VMEM((EXPERT_TILE, w), I32),
                        pltpu.SemaphoreType.DMA((2, 3)), pltpu.SemaphoreType.DMA((2,))],
    )
    return pl.pallas_call(
        functools.partial(_experts_kernel, n_slots, nb),
        grid_spec=grid_spec,
        out_shape=jax.ShapeDtypeStruct((n_slots + N_EXPERTS * EXPERT_TILE, w), I32),
        compiler_params=_cparams(("arbitrary",)),
        name="moe_experts",
    )(blk_e, nxt_e, slot, n_used, inv, xs, w_gate, w_up, w_down)


def _combine_kernel(alpha, h_ref, gate_ref, ys_ref, wsg_ref, wsu_ref, wsd_ref, l2g_ref, l2b_ref, o_ref):
    tm = h_ref.shape[0]
    h = h_ref[...]
    hb = h.astype(BF16)
    gp = _dot(hb, wsg_ref[...])
    up = _dot(hb, wsu_ref[...])
    shared = _dot((gp * _sigmoid(gp) * up).astype(BF16), wsd_ref[...])
    row = lax.broadcasted_iota(I32, (tm, tm * TOP_K), 0)
    col = lax.broadcasted_iota(I32, (tm, tm * TOP_K), 1)
    g_hi, g_lo = _split_bf16(jnp.where(col // TOP_K == row, gate_ref[0], 0.0))
    hi, lo = _unpack_bf16_pairs(ys_ref[...])
    hi, lo = hi.astype(BF16), lo.astype(BF16)
    routed = jnp.concatenate([_dot(g_hi, hi) + _dot(g_lo, hi), _dot(g_hi, lo) + _dot(g_lo, lo)], axis=1)
    o_ref[...] = _layer_norm(alpha * h + routed + shared, l2g_ref[...], l2b_ref[...])


def moe_combine(h1, gate_rows, ys, pw, alpha):
    n, d = h1.shape
    tm = gate_rows.shape[2] // TOP_K
    full = lambda a: pl.BlockSpec(a.shape, lambda i: (0,) * a.ndim)
    params = (pw["ws_gate"], pw["ws_up"], pw["ws_down"], pw["ln2_g"], pw["ln2_b"])
    return pl.pallas_call(
        functools.partial(_combine_kernel, alpha),
        grid=(n // tm,),
        in_specs=[pl.BlockSpec((tm, d), lambda i: (i, 0)),
                  pl.BlockSpec((1, 1, tm * TOP_K), lambda i: (i, 0, 0)),
                  pl.BlockSpec((tm * TOP_K, d // 2), lambda i: (i, 0))]
                 + [full(a) for a in params],
        out_specs=pl.BlockSpec((tm, d), lambda i: (i, 0)),
        out_shape=jax.ShapeDtypeStruct((n, d), F32),
        compiler_params=_cparams(("parallel",)),
        name="moe_combine",
    )(h1, gate_rows, ys, *params)


def _round_up(x, m):
    return (x + m - 1) // m * m


def _rope_tables(pos, head):
    half = head // 2
    inv = ROPE_THETA ** (-jnp.arange(half, dtype=F32) / half)
    ang = pos.astype(F32)[:, None] * inv[None, :]
    cos, sin = jnp.cos(ang), jnp.sin(ang)
    rep = LANES // head
    c = jnp.tile(jnp.concatenate([cos, cos], axis=1), (1, rep))
    s = jnp.tile(jnp.concatenate([-sin, sin], axis=1), (1, rep))
    return c, s


def _permute_cols(m, axis=-1):
    axis = axis % m.ndim
    cut = lambda lo, hi: lax.slice_in_dim(m, lo, hi, axis=axis)
    a0 = SHIFT_DIM
    i0 = a0 + A_DIM + 2 * A_KV_DIM + IDX_HEADS * IDX_DIM

    def pad(w):
        shape = list(m.shape)
        shape[axis] = w
        return jnp.zeros(shape, m.dtype)

    pieces = [
        cut(0, 3 * R_DIM),
        cut(a0, a0 + A_DIM),
        cut(a0 + A_DIM + 2 * A_KV_DIM, i0),
        cut(a0 + A_DIM, a0 + A_DIM + 2 * A_KV_DIM),
        cut(i0, i0 + IDX_DIM + IDX_HEADS),
        pad(LANES - IDX_DIM - IDX_HEADS),
        cut(3 * R_DIM, SHIFT_DIM),
        pad(LORA_W - (SHIFT_DIM - 3 * R_DIM)),
    ]
    return jnp.concatenate(pieces, axis=axis)


def kernel(x_prompt, x_sample, cache_k, cache_v, cache_idx_k, state_wkv, state_shift, page_table,
           meta, ln0_g, ln0_b, w_in, mu_shift, w0, w_b, a0, a_b, g_b, k_k, k_a, r_k, gn_g, gn_b,
           w_out, ln1_g, ln1_b, w_router, e_bias, w_gate, w_up, w_down, ws_gate, ws_up, ws_down,
           ln2_g, ln2_b):
    depth = w_in.shape[0]
    assert depth == 1, "single trunk layer"
    bsz, s_p, d = x_prompt.shape
    s_dec, s_s, _ = x_sample.shape
    assert s_s == 1, "one decode token per sequence"
    t_real = N_META + s_p
    tp = _round_up(t_real, LANES)
    assert (bsz * tp) % ROW_TILE == 0
    sp = _round_up(s_dec, ROW_TILE)
    n_prompt = bsz * tp
    n = n_prompt + sp
    n_pool, page = cache_k.shape[1], cache_k.shape[2]
    n_pages = page_table.shape[1]
    past = n_pages * page
    alpha = float((2 * depth) ** 0.25)
    row2 = lambda a: a.reshape(1, -1)

    meta_rows = jnp.broadcast_to(meta[None], (bsz, N_META, d))
    xp = jnp.concatenate([meta_rows, x_prompt, jnp.zeros((bsz, tp - t_real, d), F32)], axis=1)
    x_all = jnp.concatenate([xp.reshape(n_prompt, d), x_sample.reshape(s_dec, d),
                             jnp.zeros((sp - s_dec, d), F32)], axis=0)
    pos = jnp.concatenate([jnp.tile(jnp.arange(tp), bsz), jnp.full((sp,), past)])
    c128, s128 = _rope_tables(pos, A_HEAD)
    c64, s64 = _rope_tables(pos, IDX_DIM)

    w_in_t = _permute_cols(w_in[0].T.astype(BF16), axis=0)
    mu_k = _permute_cols(
        jnp.concatenate([mu_shift[0], jnp.zeros((w_in.shape[2] - SHIFT_DIM,), F32)])[None, :])
    head_of = jnp.arange(R_DIM) // R_HEAD
    e_mat = (head_of[:, None] == jnp.arange(R_HEADS)[None, :]).astype(F32)
    zpad = lambda a, rows_before, rows_total: jnp.concatenate(
        [jnp.zeros((rows_before, a.shape[1]), a.dtype), a,
         jnp.zeros((rows_total - rows_before - a.shape[0], a.shape[1]), a.dtype)], axis=0)
    pw = {
        "mu_x": mu_k[:, :3 * R_DIM], "mu_lo": mu_k[:, C_LORA:],
        "w0": row2(w0[0]), "a0": row2(a0[0]), "k_k": row2(k_k[0]), "k_a": row2(k_a[0]),
        "r_k": row2(r_k[0]), "gn_g": row2(gn_g[0]), "gn_b": row2(gn_b[0]),
        "w_b": zpad(w_b[0], 0, LANES).astype(BF16),
        "a_b": zpad(a_b[0], D_DECAY_LORA, LANES).astype(BF16),
        "g_b": zpad(g_b[0], 0, LORA_W - LANES).astype(BF16),
        "e": e_mat.astype(BF16), "et": e_mat.T.astype(BF16),
        "ln0_g": row2(ln0_g), "ln0_b": row2(ln0_b),
        "ln1_g": row2(ln1_g[0]), "ln1_b": row2(ln1_b[0]),
        "ln2_g": row2(ln2_g[0]), "ln2_b": row2(ln2_b[0]),
        "w_out": w_out[0].astype(BF16),
        "w_router": jnp.pad(w_router[0], ((0, 0), (0, LANES - N_EXPERTS))),
        "ws_gate": ws_gate[0].astype(BF16), "ws_up": ws_up[0].astype(BF16),
        "ws_down": ws_down[0].astype(BF16),
    }

    p = ln_proj(x_all, pw["ln0_g"], pw["ln0_b"], w_in_t)
    q_r, iq_r, k_r, ik_r, iw, kt_r, ikt_r = rope_all(p, c128, s128, c64, s64)

    pre_p = rwkv_pre(p, 0, n_prompt, None, pw, t_real, tp)
    shift_k = _permute_cols(jnp.concatenate(
        [state_shift[0], jnp.zeros((s_dec, w_in.shape[2] - SHIFT_DIM), F32)], axis=1))
    shift_k = jnp.concatenate([shift_k, jnp.zeros((sp - s_dec, P_COLS), F32)], axis=0)
    pre_s = rwkv_pre(p, n_prompt, sp, (shift_k[:, :3 * R_DIM], shift_k[:, C_LORA:]), pw, t_real, tp)
    r_p, ld_p, k_p, v_p, kk_p, b_p, g_p, bon_p = pre_p
    r_s, ld_s, k_s, v_s, kk_s, b_s, g_s, bon_s = pre_s
    y_p, wkv_p = rwkv_scan(r_p, ld_p, k_p, v_p, kk_p, b_p, bsz, tp)
    heads = lambda a: a[:s_dec].reshape(s_dec, R_HEADS, R_HEAD).transpose(1, 2, 0)
    y_hs, wkv_hs = rwkv_step(heads(r_s), heads(ld_s), heads(k_s), heads(v_s), heads(kk_s), heads(b_s),
                             state_wkv[0].transpose(1, 2, 3, 0))
    wkv_s = wkv_hs.transpose(3, 0, 1, 2)
    y_s = jnp.concatenate([y_hs.transpose(2, 0, 1).reshape(s_dec, R_DIM),
                           jnp.zeros((sp - s_dec, R_DIM), F32)], axis=0)

    n_sel_p = min(TOPK_KEYS, t_real // 4)
    a_p = dsa_prompt(q_r, iq_r, iw, ikt_r, kt_r, p, bsz, tp, n_sel_p)
    n_sel_s = min(TOPK_KEYS, (past + 1) // 4)
    pt_flat = page_table.reshape(-1).astype(I32)
    srow = slice(n_prompt, n_prompt + s_dec)
    sc_s = dsa_step_scores(pt_flat, iq_r[srow].reshape(s_dec, IDX_HEADS, IDX_DIM),
                           iw[srow].reshape(s_dec, IDX_HEADS, 1), ik_r[srow].reshape(s_dec, 1, IDX_DIM),
                           cache_idx_k[0].transpose(0, 2, 1).reshape(n_pool * IDX_DIM, page),
                           n_pages, page)
    sel_s = dsa_step_select(sc_s.reshape(s_dec, -1), n_sel_s, past).reshape(sc_s.shape)
    slot = jnp.arange(page)[:, None]
    expand = (jnp.arange(page * A_KV_HEADS)[None, :] // A_KV_HEADS == slot).astype(BF16)
    a_s = dsa_step_attn(pt_flat, q_r[srow].reshape(s_dec, A_HEADS, A_HEAD),
                        k_r[srow].reshape(s_dec, A_KV_HEADS, A_HEAD),
                        p[srow, C_VA:C_VA + A_KV_DIM].reshape(s_dec, A_KV_HEADS, A_HEAD),
                        sel_s, expand,
                        cache_k[0].reshape(n_pool * page * A_KV_HEADS, A_HEAD),
                        cache_v[0].reshape(n_pool * page * A_KV_HEADS, A_HEAD), n_pages, page)
    a_s = jnp.concatenate([a_s.reshape(s_dec, A_DIM), jnp.zeros((sp - s_dec, A_DIM), F32)], axis=0)

    h1, scores_t, xpk = mix_ln1_router(x_all, (y_p, bon_p, g_p, a_p), (y_s, bon_s, g_s, a_s), pw, alpha)
    eidx_t, gate_t, pos_t, counts = route(scores_t, e_bias[0].reshape(N_EXPERTS, 1))

    n_slots = n * TOP_K
    nb = (n_slots + N_EXPERTS * (EXPERT_TILE - 1)) // EXPERT_TILE + 1
    counts = counts[:, 0]
    padded = (counts + EXPERT_TILE - 1) // EXPERT_TILE * EXPERT_TILE
    seg_end = jnp.cumsum(padded).astype(I32)
    seg_start = seg_end - padded
    experts = jnp.arange(N_EXPERTS)
    start_of = jnp.sum(jnp.where(eidx_t[:, :, None] == experts, seg_start, 0), axis=-1)
    dest = (start_of + pos_t).T.astype(I32).reshape(-1)
    blk_row = jnp.arange(nb) * EXPERT_TILE
    blk_e = jnp.minimum(jnp.sum(seg_end[None, :] <= blk_row[:, None], axis=1),
                        N_EXPERTS - 1).astype(I32)
    n_used = (seg_end[-1] // EXPERT_TILE).astype(I32)
    run_start = jnp.concatenate([jnp.ones((1,), I32), (blk_e[1:] != blk_e[:-1]).astype(I32)])
    slot = ((jnp.cumsum(run_start) - 1) % 2).astype(I32)
    run_end = seg_end[blk_e] // EXPERT_TILE
    nxt_e = jnp.where(run_end < n_used, blk_e[jnp.minimum(run_end, nb - 1)], -1).astype(I32)
    n_used = n_used.reshape(1)
    spare = (n_slots + blk_e[:, None] * EXPERT_TILE + jnp.arange(EXPERT_TILE)[None, :]).astype(I32)
    xs, inv = moe_dispatch(dest, seg_end, counts.astype(I32), n_used, xpk, spare.reshape(-1), nb)
    ys = moe_experts(blk_e, nxt_e, slot, n_used, inv, xs, n_slots, w_gate[0], w_up[0], w_down[0])
    tc = Q_TILE
    h2 = moe_combine(h1, gate_t.T.reshape(n // tc, 1, tc * TOP_K), ys, pw, alpha)

    def prompt_rows(a):
        return a[:n_prompt].reshape(bsz, tp, -1)[:, :t_real]

    y_prompt = h2[:n_prompt].reshape(bsz, tp, d)[:, N_META:t_real]
    y_sample = h2[srow].reshape(s_dec, 1, d)
    k_prompt = prompt_rows(k_r).reshape(1, bsz, t_real, A_KV_HEADS, A_HEAD)
    v_prompt = prompt_rows(p[:, C_VA:C_VA + A_KV_DIM]).reshape(1, bsz, t_real, A_KV_HEADS, A_HEAD)
    ik_prompt = prompt_rows(ik_r)[None]
    last = jnp.arange(bsz) * tp + t_real - 1
    unperm = lambda rows: jnp.concatenate([rows[:, :3 * R_DIM],
                                           rows[:, C_LORA:C_LORA + SHIFT_DIM - 3 * R_DIM]], axis=1)
    shift_prompt = unperm(p[last])[None]
    k_sample = k_r[srow].reshape(1, s_dec, 1, A_KV_HEADS, A_HEAD)
    v_sample = p[srow, C_VA:C_VA + A_KV_DIM].reshape(1, s_dec, 1, A_KV_HEADS, A_HEAD)
    ik_sample = ik_r[srow].reshape(1, s_dec, 1, IDX_DIM)
    shift_sample = unperm(p[srow])[None]
    return (y_prompt, y_sample, k_prompt, v_prompt, ik_prompt, wkv_p[None], shift_prompt,
            k_sample, v_sample, ik_sample, wkv_s[None], shift_sample)
```

```python
import functools

import numpy as np
import jax
import jax.numpy as jnp
from jax import lax
from jax.experimental import pallas as pl
from jax.experimental.pallas import tpu as pltpu

F32 = jnp.float32
BF16 = jnp.bfloat16
I32 = jnp.int32
HIGHEST = lax.Precision.HIGHEST

N_META = 16
R_HEADS, R_HEAD = 16, 64
R_DIM = R_HEADS * R_HEAD
D_DECAY_LORA, D_AAA_LORA, D_GATE_LORA = 64, 64, 160
SHIFT_DIM = 3 * R_DIM + D_DECAY_LORA + D_AAA_LORA + D_GATE_LORA
GN_EPS = 64e-5
A_HEADS, A_KV_HEADS, A_HEAD = 8, 2, 128
A_DIM = A_HEADS * A_HEAD
A_KV_DIM = A_KV_HEADS * A_HEAD
IDX_HEADS, IDX_DIM = 16, 64
TOPK_KEYS = 256
ROPE_THETA = 10000.0
N_EXPERTS, N_EXPERT_GROUPS, TOPK_GROUPS, TOP_K = 64, 8, 4, 8
ROUTED_SCALE = 2.5
LN_EPS = 1e-5

LANES = 128
SUBLANES = 8
ROW_TILE = 256
Q_TILE = 128
CHUNK = 64
EXPERT_TILE = 256
VMEM_LIMIT = 56 * 1024 * 1024
NEG_BIG = -1e30
INT_MIN = -2 ** 31

C_R, C_K, C_V = 0, R_DIM, 2 * R_DIM
C_Q = 3 * R_DIM
C_IQ = C_Q + A_DIM
C_KA = C_IQ + IDX_HEADS * IDX_DIM
C_VA = C_KA + A_KV_DIM
C_IK = C_VA + A_KV_DIM
C_LORA = C_IK + LANES
LORA_W = 384
P_COLS = C_LORA + LORA_W


def _cparams(sem):
    return pltpu.CompilerParams(dimension_semantics=sem, vmem_limit_bytes=VMEM_LIMIT)


def _dot(a, b, precision=None):
    return jnp.dot(a, b, preferred_element_type=F32, precision=precision)


def _dot_nt(a, b, precision=None):
    return lax.dot_general(a, b, (((1,), (1,)), ((), ())), preferred_element_type=F32,
                           precision=precision)


def _dot_tn(a, b, precision=None):
    return lax.dot_general(a, b, (((0,), (0,)), ((), ())), preferred_element_type=F32,
                           precision=precision)


def _split_bf16(x):
    hi = x.astype(BF16)
    return hi, (x - hi.astype(F32)).astype(BF16)


def _dot_f32_by_bf16(a, b):
    hi, lo = _split_bf16(a)
    return _dot(hi, b) + _dot(lo, b)


def _head_sums(x, e, et):
    return _dot_f32_by_bf16(_dot_f32_by_bf16(x, e), et)


def _layer_norm(x, g, b):
    mu = jnp.mean(x, axis=-1, keepdims=True)
    xc = x - mu
    var = jnp.mean(xc * xc, axis=-1, keepdims=True)
    return xc * lax.rsqrt(var + LN_EPS) * g + b


def _sigmoid(z):
    return 1.0 / (1.0 + jnp.exp(-z))


def _ln_proj_kernel(x0_ref, xn_ref, g_ref, b_ref, w_ref, o_ref, h_even, h_odd):
    i = pl.program_id(1)
    norm = lambda ref: _layer_norm(ref[...], g_ref[...], b_ref[...]).astype(BF16)

    @pl.when(i == 0)
    def _():
        h_even[...] = norm(x0_ref)

    for par, (cur, nxt) in enumerate(((h_even, h_odd), (h_odd, h_even))):
        @pl.when(i % 2 == par)
        def _(cur=cur, nxt=nxt):
            o_ref[...] = _dot_nt(cur[...], w_ref[...])
            nxt[...] = norm(xn_ref)


PROJ_ROWS = 256
PROJ_COLS = 2048


def ln_proj(x, g, b, w_t):
    n, d = x.shape
    cols = w_t.shape[0]
    tm = max(t for t in range(LANES, PROJ_ROWS + 1, LANES) if n % t == 0)
    tn = PROJ_COLS
    ni = n // tm
    return pl.pallas_call(
        _ln_proj_kernel,
        grid=(cols // tn, ni),
        in_specs=[
            pl.BlockSpec((tm, d), lambda j, i: (0, 0)),
            pl.BlockSpec((tm, d), lambda j, i: ((i + 1) % ni, 0)),
            pl.BlockSpec((1, d), lambda j, i: (0, 0)),
            pl.BlockSpec((1, d), lambda j, i: (0, 0)),
            pl.BlockSpec((tn, d), lambda j, i: (j, 0)),
        ],
        out_specs=pl.BlockSpec((tm, tn), lambda j, i: (i, j)),
        out_shape=jax.ShapeDtypeStruct((n, cols), F32),
        scratch_shapes=[pltpu.VMEM((tm, d), BF16), pltpu.VMEM((tm, d), BF16)],
        compiler_params=_cparams(("arbitrary", "arbitrary")),
        name="ln_proj",
    )(x, x, g, b, w_t)


def _rot_half(x, head):
    w = x.shape[-1]
    half = head // 2
    lane = lax.broadcasted_iota(I32, x.shape, 1)
    left = pltpu.roll(x, w - half, axis=1)
    right = pltpu.roll(x, half, axis=1)
    return jnp.where((lane % head) < half, left, right)


def _rope_kernel(q_ref, iq_ref, ka_ref, ikw_ref, c128_ref, s128_ref, c64_ref, s64_ref,
                 qo_ref, iqo_ref, ko_ref, iko_ref, iwo_ref, kt_ref, ikt_ref):
    c128, s128 = c128_ref[...], s128_ref[...]
    c64, s64 = c64_ref[...], s64_ref[...]

    def rope(x, head, c, s):
        rep = x.shape[-1] // LANES
        if rep > 1:
            c = jnp.concatenate([c] * rep, axis=1)
            s = jnp.concatenate([s] * rep, axis=1)
        return x * c + _rot_half(x, head) * s

    q = rope(q_ref[...], A_HEAD, c128, s128)
    qo_ref[...] = (q * (A_HEAD ** -0.5)).astype(BF16)
    iqo_ref[...] = rope(iq_ref[...], IDX_DIM, c64, s64).astype(BF16)
    k = rope(ka_ref[...], A_HEAD, c128, s128)
    ko_ref[...] = k
    ikw = ikw_ref[...]
    ik = rope(ikw, IDX_DIM, c64, s64)
    iko_ref[...] = ik[:, :IDX_DIM]
    iwo_ref[...] = ikw[:, IDX_DIM:IDX_DIM + IDX_HEADS]
    kt_ref[...] = k.T.astype(BF16)
    ikt_ref[...] = ik.T[:IDX_DIM, :].astype(BF16)


def rope_all(p, c128, s128, c64, s64):
    n = p.shape[0]
    tm = ROW_TILE
    row = lambda w, blk: pl.BlockSpec((tm, w), lambda i: (i, blk))
    return pl.pallas_call(
        _rope_kernel,
        grid=(n // tm,),
        in_specs=[row(A_DIM, C_Q // A_DIM), row(A_DIM, C_IQ // A_DIM),
                  row(A_KV_DIM, C_KA // A_KV_DIM), row(LANES, C_IK // LANES),
                  row(LANES, 0), row(LANES, 0), row(LANES, 0), row(LANES, 0)],
        out_specs=[row(A_DIM, 0), row(A_DIM, 0), row(A_KV_DIM, 0),
                   row(IDX_DIM, 0), row(IDX_HEADS, 0),
                   pl.BlockSpec((A_KV_DIM, tm), lambda i: (0, i)),
                   pl.BlockSpec((IDX_DIM, tm), lambda i: (0, i))],
        out_shape=[jax.ShapeDtypeStruct((n, A_DIM), BF16),
                   jax.ShapeDtypeStruct((n, IDX_HEADS * IDX_DIM), BF16),
                   jax.ShapeDtypeStruct((n, A_KV_DIM), F32),
                   jax.ShapeDtypeStruct((n, IDX_DIM), F32),
                   jax.ShapeDtypeStruct((n, IDX_HEADS), F32),
                   jax.ShapeDtypeStruct((A_KV_DIM, n), BF16),
                   jax.ShapeDtypeStruct((IDX_DIM, n), BF16)],
        compiler_params=_cparams(("parallel",)),
        name="rope",
    )(p, p, p, p, c128, s128, c64, s64)


def _rwkv_pre_kernel(t_real, tp, from_rows, *refs):
    (x_ref, lo_ref, px_ref, plo_ref, mu_ref, mulo_ref, w0_ref, wb_ref, a0_ref, ab_ref,
     gb_ref, kk_ref, ka_ref, rk_ref, e_ref, et_ref,
     r_o, ld_o, k_o, v_o, kk_o, b_o, g_o, bon_o) = refs
    x = x_ref[...]
    lo = lo_ref[...]
    tm = x.shape[0]
    if from_rows:
        i = pl.program_id(0)
        row = lax.broadcasted_iota(I32, (tm, 1), 0)
        t = (i * tm + row) % tp
        first = row == 0
        sx = jnp.where(first, px_ref[SUBLANES - 1:SUBLANES, :], pltpu.roll(x, 1, axis=0))
        slo = jnp.where(first, plo_ref[SUBLANES - 1:SUBLANES, :], pltpu.roll(lo, 1, axis=0))
        sx = jnp.where(t == 0, 0.0, sx)
        slo = jnp.where(t == 0, 0.0, slo)
        live = t < t_real
    else:
        sx = px_ref[...]
        slo = plo_ref[...]
        live = None
    xx = x + (sx - x) * mu_ref[...]
    xlo = lo + (slo - lo) * mulo_ref[...]
    r = xx[:, C_R:C_R + R_DIM]
    k = xx[:, C_K:C_K + R_DIM]
    v = xx[:, C_V:C_V + R_DIM]
    wa = xlo[:, :LANES]
    xg = xlo[:, LANES:]
    z = w0_ref[...] + _dot(jnp.tanh(wa).astype(BF16), wb_ref[...])
    nz = -z
    softplus = jnp.maximum(nz, 0.0) + jnp.log(1.0 + jnp.exp(-jnp.abs(nz)))
    logd = -jnp.exp(-softplus - 0.5)
    a = _sigmoid(a0_ref[...] + _dot(wa.astype(BF16), ab_ref[...]))
    g = _dot(_sigmoid(xg).astype(BF16), gb_ref[...])
    e, et = e_ref[...], et_ref[...]
    kkr = k * kk_ref[...]
    ss = _head_sums(kkr * kkr, e, et)
    kk = kkr / jnp.maximum(jnp.sqrt(ss), 1e-12)
    k2 = k * (1.0 + (a - 1.0) * ka_ref[...])
    bonus = _head_sums(r * k2 * rk_ref[...], e, et) * v
    b = kk * a
    if live is not None:
        zero = lambda y: jnp.where(live, y, 0.0)
        logd, k2s, vs, kk, b = zero(logd), zero(k2), zero(v), zero(kk), zero(b)
    else:
        k2s, vs = k2, v
    r_o[...] = r
    ld_o[...] = logd
    k_o[...] = k2s
    v_o[...] = vs
    kk_o[...] = kk
    b_o[...] = b
    g_o[...] = g
    bon_o[...] = bonus


def rwkv_pre(p, row0, nrows, prev, pw, t_real, tp):
    tm = min(ROW_TILE, nrows)
    blk0 = row0 // tm
    from_rows = prev is None
    xw = 3 * R_DIM
    cur_x = pl.BlockSpec((tm, xw), lambda i: (blk0 + i, 0))
    cur_lo = pl.BlockSpec((tm, LORA_W), lambda i: (blk0 + i, C_LORA // LORA_W))
    if from_rows:
        r8 = tm // SUBLANES
        prev_x = pl.BlockSpec((SUBLANES, xw), lambda i: (jnp.maximum((blk0 + i) * r8 - 1, 0), 0))
        prev_lo = pl.BlockSpec((SUBLANES, LORA_W),
                               lambda i: (jnp.maximum((blk0 + i) * r8 - 1, 0), C_LORA // LORA_W))
        prev_args = (p, p)
    else:
        prev_x = pl.BlockSpec((tm, xw), lambda i: (i, 0))
        prev_lo = pl.BlockSpec((tm, LORA_W), lambda i: (i, 0))
        prev_args = prev
    full = lambda a: pl.BlockSpec(a.shape, lambda i: (0,) * a.ndim)
    params = (pw["mu_x"], pw["mu_lo"], pw["w0"], pw["w_b"], pw["a0"], pw["a_b"], pw["g_b"],
              pw["k_k"], pw["k_a"], pw["r_k"], pw["e"], pw["et"])
    out = pl.BlockSpec((tm, R_DIM), lambda i: (i, 0))
    return pl.pallas_call(
        functools.partial(_rwkv_pre_kernel, t_real, tp, from_rows),
        grid=(nrows // tm,),
        in_specs=[cur_x, cur_lo, prev_x, prev_lo] + [full(a) for a in params],
        out_specs=[out] * 8,
        out_shape=[jax.ShapeDtypeStruct((nrows, R_DIM), F32)] * 8,
        compiler_params=_cparams(("parallel",)),
        name="rwkv_pre_rows" if from_rows else "rwkv_pre_step",
    )(p, p, *prev_args, *params)


def _rwkv_scan_kernel(r_ref, ld_ref, k_ref, v_ref, kk_ref, b_ref, y_ref, s_ref, ss_scr):
    c = pl.program_id(1)

    @pl.when(c == 0)
    def _():
        ss_scr[...] = jnp.zeros_like(ss_scr)

    n = CHUNK
    n2 = 2 * n
    pairs = R_HEADS // 2
    ld_all = ld_ref[...]
    ri = lax.broadcasted_iota(I32, (n, n), 0)
    ci = lax.broadcasted_iota(I32, (n, n), 1)
    cum_all = _dot((ci <= ri).astype(F32), ld_all, HIGHEST)
    head0 = lax.broadcasted_iota(I32, (n, LANES), 1) < R_HEAD
    r4 = lax.broadcasted_iota(I32, (2 * n2, 2 * n2), 0)
    c4 = lax.broadcasted_iota(I32, (2 * n2, 2 * n2), 1)
    tri = (c4 % n) < (r4 % n) + jnp.where(r4 < n2, 0, 1)
    re = lax.broadcasted_iota(I32, (n2, n2), 0)
    ce = lax.broadcasted_iota(I32, (n2, n2), 1)
    eye = (re == ce).astype(F32)

    def stack(x):
        return jnp.concatenate([jnp.where(head0, x, 0.0), jnp.where(head0, 0.0, x)], axis=0)

    ar, bk, v2, ss, e_last = [], [], [], [], []
    for p in range(pairs):
        sl = slice(p * LANES, (p + 1) * LANES)
        cum, ld = cum_all[:, sl], ld_all[:, sl]
        e_pos = jnp.exp(cum)
        e_neg = jnp.exp(-cum)
        at = -kk_ref[:, sl] * jnp.exp(cum - ld)
        ar.append(jnp.concatenate([stack(at), stack(r_ref[:, sl] * e_pos)], axis=0).astype(BF16))
        bk.append(jnp.concatenate([stack(b_ref[:, sl] * e_neg), stack(k_ref[:, sl] * e_neg)],
                                  axis=0).astype(BF16))
        v2.append(stack(v_ref[:, sl]).astype(BF16))
        ss.append(ss_scr[p])
        e_last.append(e_pos[n - 1:n, :])
    xy0 = [_dot_nt(ar[p], ss[p].astype(BF16)) for p in range(pairs)]
    sc = [jnp.where(tri, _dot_nt(ar[p], bk[p]), 0.0) for p in range(pairs)]
    lp = [s[:n2, :n2] for s in sc]
    t = [eye + l for l in lp]
    m = 1
    while 2 * m < n:
        lpb = [l.astype(BF16) for l in lp]
        lp = [_dot(l, l) for l in lpb]
        t = [t[p] + _dot(t[p].astype(BF16), lp[p].astype(BF16)) for p in range(pairs)]
        m *= 2
    w = [xy0[p][:n2] + _dot(sc[p][:n2, n2:].astype(BF16), v2[p]) for p in range(pairs)]
    u = [_dot(t[p].astype(BF16), w[p].astype(BF16)) for p in range(pairs)]
    uv = [jnp.concatenate([u[p].astype(BF16), v2[p]], axis=0) for p in range(pairs)]
    y = [xy0[p][n2:] + _dot(sc[p][n2:].astype(BF16), uv[p]) for p in range(pairs)]
    upd = [_dot_tn(uv[p], bk[p]) for p in range(pairs)]
    for p in range(pairs):
        y_ref[:, p * LANES:(p + 1) * LANES] = y[p][:n] + y[p][n:]
        ss_scr[p] = (ss[p] + upd[p]) * e_last[p]

    @pl.when(c == pl.num_programs(1) - 1)
    def _():
        for p in range(R_HEADS // 2):
            ss = ss_scr[p]
            s_ref[0, 2 * p] = ss[:R_HEAD, :R_HEAD]
            s_ref[0, 2 * p + 1] = ss[R_HEAD:, R_HEAD:]


def rwkv_scan(r, ld, k, v, kk, b, batch, tp):
    nchunk = tp // CHUNK
    blk = pl.BlockSpec((CHUNK, R_DIM), lambda bi, c: (bi * nchunk + c, 0))
    return pl.pallas_call(
        _rwkv_scan_kernel,
        grid=(batch, nchunk),
        in_specs=[blk] * 6,
        out_specs=[blk, pl.BlockSpec((1, R_HEADS, R_HEAD, R_HEAD), lambda bi, c: (bi, 0, 0, 0))],
        out_shape=[jax.ShapeDtypeStruct((batch * tp, R_DIM), F32),
                   jax.ShapeDtypeStruct((batch, R_HEADS, R_HEAD, R_HEAD), F32)],
        scratch_shapes=[pltpu.VMEM((R_HEADS // 2, LANES, LANES), F32)],
        compiler_params=_cparams(("parallel", "arbitrary")),
        name="rwkv_scan",
    )(r, ld, k, v, kk, b)


STEP_ROWS = 16


def _rwkv_step_kernel(r_ref, ld_ref, k_ref, v_ref, kk_ref, b_ref, s_ref, y_ref, so_ref):
    r, k = r_ref[0], k_ref[0]
    dec = jnp.exp(ld_ref[0])
    na = -kk_ref[0]
    b = b_ref[0]
    v = v_ref[0]
    ys = []
    for v0 in range(0, R_HEAD, STEP_ROWS):
        rows = range(v0, v0 + STEP_ROWS)
        s = [s_ref[0, vi] for vi in rows]
        sa = [jnp.sum(x * na, axis=0, keepdims=True) for x in s]
        s_new = [x * dec + a * b + v[vi:vi + 1, :] * k for x, a, vi in zip(s, sa, rows)]
        ys += [jnp.sum(x * r, axis=0, keepdims=True) for x in s_new]
        for x, vi in zip(s_new, rows):
            so_ref[0, vi] = x
    y_ref[0] = jnp.concatenate(ys, axis=0)


def rwkv_step(r, ld, k, v, kk, b, state):
    s = state.shape[-1]
    vec = pl.BlockSpec((1, R_HEAD, s), lambda h: (h, 0, 0))
    st = pl.BlockSpec((1, R_HEAD, R_HEAD, s), lambda h: (h, 0, 0, 0))
    return pl.pallas_call(
        _rwkv_step_kernel,
        grid=(R_HEADS,),
        in_specs=[vec] * 6 + [st],
        out_specs=[vec, st],
        out_shape=[jax.ShapeDtypeStruct((R_HEADS, R_HEAD, s), F32),
                   jax.ShapeDtypeStruct(state.shape, F32)],
        compiler_params=_cparams(("parallel",)),
        name="rwkv_step",
    )(r, ld, k, v, kk, b, state)


def _select_topk(score, allowed, n_sel, store):
    bits = lax.bitcast_convert_type(score, I32)
    key = jnp.where(bits < 0, bits ^ jnp.int32(0x7FFFFFFF), bits)
    key = jnp.where(allowed, key, jnp.int32(INT_MIN))
    m, w = score.shape
    one, zero = jnp.ones((), BF16), jnp.zeros((), BF16)

    def byte(shift):
        if shift == 24:
            d = lax.shift_right_arithmetic(key, 24) + 128
        else:
            d = lax.shift_right_logical(key, shift) & 255
        return d.astype(F32).astype(BF16)

    def count(flags):
        acc = flags[:, :LANES]
        for c in range(LANES, w, LANES):
            acc = acc + flags[:, c:c + LANES]
        return jnp.sum(acc.astype(F32), axis=1, keepdims=True)

    need = jnp.full((m, 1), float(n_sel), F32)
    x = byte(24)
    tau = jnp.zeros((m, 1), I32)
    for shift in (24, 16, 8, 0):
        def body(it, t, x=x, need=need):
            step = lax.shift_left(jnp.int32(1), jnp.int32(6) - 2 * it).astype(F32)
            cnts = [count(jnp.where(x >= (t + mult * step).astype(BF16), one, zero))
                    for mult in (1.0, 2.0, 3.0)]
            hits = sum(jnp.where(cnt >= need, 1.0, 0.0) for cnt in cnts)
            return t + hits * step

        t = lax.fori_loop(0, 4, body, jnp.zeros((m, 1), F32))
        tb = t.astype(BF16)
        need = need - count(jnp.where(x > tb, one, zero))
        digit = t.astype(I32) - (128 if shift == 24 else 0)
        tau = tau | lax.shift_left(digit, shift)
        if shift:
            x = jnp.where(x == tb, byte(shift - 8), -one)
    store(jnp.logical_and(key >= tau, allowed))
    ties = count(jnp.where(x == tb, one, zero))
    crowded = jnp.logical_and(ties > need, tau > INT_MIN)

    @pl.when(jnp.max(jnp.where(crowded, 1.0, 0.0)) > 0.0)
    def _():
        above = jnp.logical_and(key > tau, allowed)
        equal = jnp.logical_and(key == tau, allowed)
        ra = lax.broadcasted_iota(I32, (KEY_TILE, KEY_TILE), 0)
        rb = lax.broadcasted_iota(I32, (KEY_TILE, KEY_TILE), 1)
        upto = (ra <= rb).astype(BF16)
        seen = jnp.zeros((m, 1), F32)
        keep = []
        for c0 in range(0, w, KEY_TILE):
            c1 = min(c0 + KEY_TILE, w)
            eq = equal[:, c0:c1]
            flags = jnp.where(eq, 1.0, 0.0)
            rank = seen + _dot(flags.astype(BF16), upto[:c1 - c0, :c1 - c0])
            keep.append(jnp.logical_and(eq, rank <= need))
            seen = seen + jnp.sum(flags, axis=1, keepdims=True)
        store(jnp.logical_or(above, jnp.concatenate(keep, axis=1)))


KEY_TILE = 256
Q_TILES_PER_EXTENT = 4


def _dsa_prompt_block(n_sel, tk, i, q_ref, iq_ref, iw_ref, ikt_ref, kt_ref, v_ref, o_ref, sc_ref):
    tq = q_ref.shape[0]
    iw = iw_ref[...] * ((IDX_HEADS * IDX_DIM) ** -0.5)
    iq = iq_ref[...]
    iq_h = [iq[:, h * IDX_DIM:(h + 1) * IDX_DIM] for h in range(IDX_HEADS)]
    iw_h = [iw[:, h:h + 1] for h in range(IDX_HEADS)]
    for c0 in range(0, tk, KEY_TILE):
        c1 = min(c0 + KEY_TILE, tk)
        ikb = ikt_ref[:, c0:c1]
        acc = jnp.maximum(_dot(iq_h[0], ikb), 0.0) * iw_h[0]
        for h in range(1, IDX_HEADS):
            acc = acc + jnp.maximum(_dot(iq_h[h], ikb), 0.0) * iw_h[h]
        sc_ref[:, c0:c1] = acc
    qpos = i * tq + lax.broadcasted_iota(I32, (tq, 1), 0)
    kpos = lax.broadcasted_iota(I32, (1, tk), 1)

    def store_bias(mask):
        sc_ref[:, :tk] = jnp.where(mask, 0.0, NEG_BIG)

    _select_topk(sc_ref[:, :tk], kpos <= qpos, n_sel, store_bias)
    bias = sc_ref[:, :tk]
    q = q_ref[...]
    rep = A_HEADS // A_KV_HEADS
    for g in range(A_KV_HEADS):
        kg = kt_ref[g * A_HEAD:(g + 1) * A_HEAD, :tk]
        vg = v_ref[:tk, g * A_HEAD:(g + 1) * A_HEAD].astype(BF16)
        for rr in range(rep):
            h = g * rep + rr
            s = _dot(q[:, h * A_HEAD:(h + 1) * A_HEAD], kg) + bias
            m = jnp.max(s, axis=1, keepdims=True)
            p = jnp.exp(s - m)
            l = jnp.sum(p, axis=1, keepdims=True)
            o_ref[:, h * A_HEAD:(h + 1) * A_HEAD] = _dot(p.astype(BF16), vg) / l


def _dsa_prompt_kernel(n_sel, *refs):
    i = pl.program_id(1)
    tq = refs[0].shape[0]
    tp = refs[5].shape[0]
    nq = tp // tq
    for lo in range(0, nq, Q_TILES_PER_EXTENT):
        hi = min(lo + Q_TILES_PER_EXTENT, nq)

        @pl.when(jnp.logical_and(i >= lo, i < hi))
        def _(hi=hi):
            _dsa_prompt_block(n_sel, hi * tq, i, *refs)


def dsa_prompt(q, iq, iw, ikt, kt, p, batch, tp, n_sel):
    nq = tp // Q_TILE
    qrow = lambda w: pl.BlockSpec((Q_TILE, w), lambda b, i: (b * nq + i, 0))
    keys = lambda w, blk: pl.BlockSpec((tp, w), lambda b, i: (b, blk))
    keys_t = lambda w: pl.BlockSpec((w, tp), lambda b, i: (0, b))
    return pl.pallas_call(
        functools.partial(_dsa_prompt_kernel, n_sel),
        grid=(batch, nq),
        in_specs=[qrow(A_DIM), qrow(IDX_HEADS * IDX_DIM), qrow(IDX_HEADS),
                  keys_t(IDX_DIM), keys_t(A_KV_DIM), keys(A_KV_DIM, C_VA // A_KV_DIM)],
        out_specs=qrow(A_DIM),
        out_shape=jax.ShapeDtypeStruct((batch * tp, A_DIM), F32),
        scratch_shapes=[pltpu.VMEM((Q_TILE, tp), F32)],
        compiler_params=_cparams(("parallel", "parallel")),
        name="dsa_prompt",
    )(q, iq, iw, ikt, kt, p)


SCORE_SEQS = 4
ATTN_SEQS = 2


def _dsa_step_score_kernel(n_pages, page, pt_ref, iq_ref, iw_ref, ikn_ref, *refs):
    sb = iq_ref.shape[0]
    pages = refs[:sb * n_pages]
    o_ref = refs[sb * n_pages]
    lane = lax.broadcasted_iota(I32, (1, LANES), 1)
    for q in range(sb):
        iq = iq_ref[q]
        iw = iw_ref[q] * ((IDX_HEADS * IDX_DIM) ** -0.5)
        for j in range(n_pages):
            d = _dot(iq, pages[q * n_pages + j][...].astype(BF16))
            o_ref[q, :, j * page:(j + 1) * page] = jnp.sum(jnp.maximum(d, 0.0) * iw, axis=0,
                                                           keepdims=True)
        dn = jnp.sum(iq.astype(F32) * ikn_ref[q], axis=1, keepdims=True)
        sn = jnp.sum(jnp.maximum(dn, 0.0) * iw, axis=0, keepdims=True)
        o_ref[q, :, n_pages * page:] = jnp.where(lane == 0, sn, 0.0)


def _seqs_per_step(s, want):
    return want if s % want == 0 else 1


def dsa_step_scores(pt_flat, iq, iw, ik_new, cik2d, n_pages, page):
    s = iq.shape[0]
    sb = _seqs_per_step(s, SCORE_SEQS)
    kw = n_pages * page + LANES
    page_spec = lambda q, j: pl.BlockSpec(
        (IDX_DIM, page), lambda i, pt: (pt[(i * sb + q) * n_pages + j], 0))
    grid_spec = pltpu.PrefetchScalarGridSpec(
        num_scalar_prefetch=1,
        grid=(s // sb,),
        in_specs=[pl.BlockSpec((sb, IDX_HEADS, IDX_DIM), lambda i, pt: (i, 0, 0)),
                  pl.BlockSpec((sb, IDX_HEADS, 1), lambda i, pt: (i, 0, 0)),
                  pl.BlockSpec((sb, 1, IDX_DIM), lambda i, pt: (i, 0, 0))]
                 + [page_spec(q, j) for q in range(sb) for j in range(n_pages)],
        out_specs=pl.BlockSpec((sb, 1, kw), lambda i, pt: (i, 0, 0)),
    )
    return pl.pallas_call(
        functools.partial(_dsa_step_score_kernel, n_pages, page),
        grid_spec=grid_spec,
        out_shape=jax.ShapeDtypeStruct((s, 1, kw), F32),
        compiler_params=_cparams(("arbitrary",)),
        name="dsa_step_scores",
    )(pt_flat, iq, iw, ik_new, *([cik2d] * (sb * n_pages)))


def _dsa_step_select_kernel(n_sel, past, sc_ref, o_ref):
    sc = sc_ref[...]
    kpos = lax.broadcasted_iota(I32, sc.shape, 1)

    def store(mask):
        o_ref[...] = jnp.where(mask, 1.0, 0.0)

    _select_topk(sc, kpos <= past, n_sel, store)


def dsa_step_select(sc, n_sel, past):
    return pl.pallas_call(
        functools.partial(_dsa_step_select_kernel, n_sel, past),
        out_shape=jax.ShapeDtypeStruct(sc.shape, F32),
        compiler_params=pltpu.CompilerParams(vmem_limit_bytes=VMEM_LIMIT),
        name="dsa_step_select",
    )(sc)


def _dsa_step_attn_kernel(n_pages, page, pt_ref, q_ref, kn_ref, vn_ref, sel_ref, ex_ref, *refs):
    sb = q_ref.shape[0]
    kps = refs[:sb * n_pages]
    vps = refs[sb * n_pages:2 * sb * n_pages]
    o_ref = refs[2 * sb * n_pages]
    rep = A_HEADS // A_KV_HEADS
    w2 = page * A_KV_HEADS
    hrow = lax.broadcasted_iota(I32, (A_HEADS, w2), 0)
    col = lax.broadcasted_iota(I32, (A_HEADS, w2), 1)
    own = (col % A_KV_HEADS) == (hrow // rep)
    h8 = lax.broadcasted_iota(I32, (A_HEADS, A_HEAD), 0)
    ex = ex_ref[...]
    for u in range(sb):
        kp = kps[u * n_pages:(u + 1) * n_pages]
        vp = vps[u * n_pages:(u + 1) * n_pages]
        q = q_ref[u]
        logits = []
        for j in range(n_pages):
            s = _dot_nt(q, kp[j][...].astype(BF16))
            selj = _dot(sel_ref[u, :, j * page:(j + 1) * page].astype(BF16), ex)
            logits.append(jnp.where(jnp.logical_and(selj > 0.5, own), s, NEG_BIG))
        kn = jnp.where(h8 < rep, kn_ref[u, 0:1, :], kn_ref[u, 1:2, :])
        vn = jnp.where(h8 < rep, vn_ref[u, 0:1, :], vn_ref[u, 1:2, :])
        sn = jnp.sum(q.astype(F32) * kn, axis=1, keepdims=True)
        seln = sel_ref[u, :, n_pages * page:n_pages * page + 1]
        sn = jnp.where(seln > 0.5, sn, NEG_BIG)
        m = sn
        for s in logits:
            m = jnp.maximum(m, jnp.max(s, axis=1, keepdims=True))
        pn = jnp.exp(sn - m)
        l = pn
        acc = pn * vn
        for j in range(n_pages):
            p = jnp.exp(logits[j] - m)
            l = l + jnp.sum(p, axis=1, keepdims=True)
            acc = acc + _dot(p.astype(BF16), vp[j][...].astype(BF16))
        o_ref[u] = acc / l


def dsa_step_attn(pt_flat, q, k_new, v_new, sel, expand, ck2d, cv2d, n_pages, page):
    s = q.shape[0]
    sb = _seqs_per_step(s, ATTN_SEQS)
    kw = sel.shape[-1]
    w2 = page * A_KV_HEADS
    page_spec = lambda u, j: pl.BlockSpec(
        (w2, A_HEAD), lambda i, pt: (pt[(i * sb + u) * n_pages + j], 0))
    pages = [page_spec(u, j) for u in range(sb) for j in range(n_pages)]
    grid_spec = pltpu.PrefetchScalarGridSpec(
        num_scalar_prefetch=1,
        grid=(s // sb,),
        in_specs=[pl.BlockSpec((sb, A_HEADS, A_HEAD), lambda i, pt: (i, 0, 0)),
                  pl.BlockSpec((sb, A_KV_HEADS, A_HEAD), lambda i, pt: (i, 0, 0)),
                  pl.BlockSpec((sb, A_KV_HEADS, A_HEAD), lambda i, pt: (i, 0, 0)),
                  pl.BlockSpec((sb, 1, kw), lambda i, pt: (i, 0, 0)),
                  pl.BlockSpec((page, w2), lambda i, pt: (0, 0))] + pages * 2,
        out_specs=pl.BlockSpec((sb, A_HEADS, A_HEAD), lambda i, pt: (i, 0, 0)),
    )
    return pl.pallas_call(
        functools.partial(_dsa_step_attn_kernel, n_pages, page),
        grid_spec=grid_spec,
        out_shape=jax.ShapeDtypeStruct((s, A_HEADS, A_HEAD), F32),
        compiler_params=_cparams(("arbitrary",)),
        name="dsa_step_attn",
    )(pt_flat, q, k_new, v_new, sel, expand, *([ck2d] * (sb * n_pages)), *([cv2d] * (sb * n_pages)))


def _pack_bf16_pairs(x):
    w = x.shape[1] // 2
    hi = lax.bitcast_convert_type(x[:, :w].astype(BF16).astype(F32), I32)
    lo = lax.bitcast_convert_type(x[:, w:].astype(BF16).astype(F32), I32)
    return hi | lax.shift_right_logical(lo, 16)


def _unpack_bf16_pairs(p):
    hi = lax.bitcast_convert_type(p & jnp.int32(-65536), F32)
    lo = lax.bitcast_convert_type(lax.shift_left(p, 16), F32)
    return hi, lo


def _mix_kernel(alpha, prompt_blocks, x_ref, yp_ref, bonp_ref, gp_ref, ap_ref, ys_ref, bons_ref, gs_ref,
                as_ref, e_ref, et_ref, gng_ref, gnb_ref, l0g_ref, l0b_ref, wo_ref, l1g_ref, l1b_ref,
                wr_ref, h_ref, sc_ref, pk_ref):
    is_prompt = pl.program_id(0) < prompt_blocks
    pick = lambda p_ref, s_ref: jnp.where(is_prompt, p_ref[...], s_ref[...])
    e, et = e_ref[...], et_ref[...]
    y = pick(yp_ref, ys_ref)
    inv = 1.0 / R_HEAD
    mu = _head_sums(y, e, et) * inv
    d = y - mu
    var = _head_sums(d * d, e, et) * inv
    yn = d * lax.rsqrt(var + GN_EPS) * gng_ref[...] + gnb_ref[...]
    r_out = (yn + pick(bonp_ref, bons_ref)) * pick(gp_ref, gs_ref)
    mix = (_dot(r_out.astype(BF16), wo_ref[:R_DIM, :])
           + _dot(pick(ap_ref, as_ref).astype(BF16), wo_ref[R_DIM:, :]))
    h0 = _layer_norm(x_ref[...], l0g_ref[...], l0b_ref[...])
    h1 = _layer_norm(alpha * h0 + mix, l1g_ref[...], l1b_ref[...])
    h_ref[...] = h1
    w_hi, w_lo = _split_bf16(wr_ref[...])
    h_hi, h_lo = _split_bf16(h1)
    logits = _dot(h_hi, w_hi) + _dot(h_lo, w_hi) + _dot(h_hi, w_lo)
    sc_ref[...] = _sigmoid(logits.T[:N_EXPERTS, :])
    pk_ref[...] = _pack_bf16_pairs(h1)


def mix_ln1_router(x, prompt_parts, step_parts, pw, alpha):
    n, d = x.shape
    tm = ROW_TILE
    pb = prompt_parts[0].shape[0] // tm
    row = lambda w: pl.BlockSpec((tm, w), lambda i: (i, 0))
    head = lambda w: pl.BlockSpec((tm, w), lambda i: (jnp.minimum(i, pb - 1), 0))
    tail = lambda w: pl.BlockSpec((tm, w), lambda i: (jnp.maximum(i - pb, 0), 0))
    full = lambda a: pl.BlockSpec(a.shape, lambda i: (0,) * a.ndim)
    params = (pw["e"], pw["et"], pw["gn_g"], pw["gn_b"], pw["ln0_g"], pw["ln0_b"], pw["w_out"],
              pw["ln1_g"], pw["ln1_b"], pw["w_router"])
    widths = (R_DIM, R_DIM, R_DIM, A_DIM)
    return pl.pallas_call(
        functools.partial(_mix_kernel, alpha, pb),
        grid=(n // tm,),
        in_specs=[row(d)] + [head(w) for w in widths] + [tail(w) for w in widths]
                 + [full(a) for a in params],
        out_specs=[row(d), pl.BlockSpec((N_EXPERTS, tm), lambda i: (0, i)), row(d // 2)],
        out_shape=[jax.ShapeDtypeStruct((n, d), F32),
                   jax.ShapeDtypeStruct((N_EXPERTS, n), F32),
                   jax.ShapeDtypeStruct((n, d // 2), I32)],
        compiler_params=_cparams(("parallel",)),
        name="mix_ln1_router",
    )(x, *prompt_parts, *step_parts, *params)


def _route_kernel(sc_ref, bias_ref, idx_ref, gate_ref, pos_ref, cnt_ref, cnt_scr):
    scores = sc_ref[...]
    biased = scores + bias_ref[...]
    tn = scores.shape[1]
    per = N_EXPERTS // N_EXPERT_GROUPS
    sub = lax.broadcasted_iota(I32, (per, tn), 0)
    grp_rows = []
    for g in range(N_EXPERT_GROUPS):
        xg = biased[g * per:(g + 1) * per, :]
        m1 = jnp.max(xg, axis=0, keepdims=True)
        first = jnp.min(jnp.where(xg == m1, sub, per), axis=0, keepdims=True)
        m2 = jnp.max(jnp.where(sub == first, -jnp.inf, xg), axis=0, keepdims=True)
        grp_rows.append(m1 + m2)
    grp = jnp.concatenate(grp_rows, axis=0)
    gi = lax.broadcasted_iota(I32, (N_EXPERT_GROUPS, tn), 0)
    gsel = jnp.zeros((N_EXPERT_GROUPS, tn), jnp.bool_)
    for _ in range(TOPK_GROUPS):
        m = jnp.max(grp, axis=0, keepdims=True)
        first = jnp.min(jnp.where(grp == m, gi, N_EXPERT_GROUPS), axis=0, keepdims=True)
        hit = gi == first
        gsel = jnp.logical_or(gsel, hit)
        grp = jnp.where(hit, -jnp.inf, grp)
    ei = lax.broadcasted_iota(I32, (N_EXPERTS, tn), 0)
    emask = jnp.concatenate(
        [jnp.broadcast_to(gsel[g:g + 1, :], (per, tn)) for g in range(N_EXPERT_GROUPS)], axis=0)
    cand = jnp.where(emask, biased, -jnp.inf)
    idxs, gates, hits = [], [], []
    for _ in range(TOP_K):
        m = jnp.max(cand, axis=0, keepdims=True)
        first = jnp.min(jnp.where(cand == m, ei, N_EXPERTS), axis=0, keepdims=True)
        hit = ei == first
        idxs.append(first)
        hits.append(hit)
        gates.append(jnp.sum(jnp.where(hit, scores, 0.0), axis=0, keepdims=True))
        cand = jnp.where(hit, -jnp.inf, cand)
    gate = jnp.concatenate(gates, axis=0)
    gate = gate / jnp.sum(gate, axis=0, keepdims=True) * ROUTED_SCALE
    idx_ref[...] = jnp.concatenate(idxs, axis=0)
    gate_ref[...] = gate
    chosen = hits[0]
    for hit in hits[1:]:
        chosen = jnp.logical_or(chosen, hit)
    onehot = jnp.where(chosen, 1.0, 0.0)
    ta = lax.broadcasted_iota(I32, (tn, tn), 0)
    tb = lax.broadcasted_iota(I32, (tn, tn), 1)
    prefix = _dot(onehot.astype(BF16), (ta < tb).astype(BF16))

    @pl.when(pl.program_id(0) == 0)
    def _():
        cnt_scr[...] = jnp.zeros_like(cnt_scr)

    rank = prefix + cnt_scr[:, 0:1]
    pos_ref[...] = jnp.concatenate(
        [jnp.sum(jnp.where(hit, rank, 0.0), axis=0, keepdims=True) for hit in hits], axis=0).astype(I32)
    cnt_scr[...] = cnt_scr[...] + jnp.sum(onehot, axis=1, keepdims=True)
    cnt_ref[...] = cnt_scr[...].astype(I32)


def route(scores_t, e_bias):
    n = scores_t.shape[1]
    tn = ROW_TILE
    tok = pl.BlockSpec((TOP_K, tn), lambda i: (0, i))
    return pl.pallas_call(
        _route_kernel,
        grid=(n // tn,),
        in_specs=[pl.BlockSpec((N_EXPERTS, tn), lambda i: (0, i)),
                  pl.BlockSpec((N_EXPERTS, 1), lambda i: (0, 0))],
        out_specs=[tok, tok, tok, pl.BlockSpec((N_EXPERTS, LANES), lambda i: (0, 0))],
        out_shape=[jax.ShapeDtypeStruct((TOP_K, n), I32), jax.ShapeDtypeStruct((TOP_K, n), F32),
                   jax.ShapeDtypeStruct((TOP_K, n), I32),
                   jax.ShapeDtypeStruct((N_EXPERTS, LANES), I32)],
        scratch_shapes=[pltpu.VMEM((N_EXPERTS, LANES), F32)],
        compiler_params=_cparams(("arbitrary",)),
        name="route",
    )(scores_t, e_bias)


def _dispatch_kernel(nb, dest_ref, segend_ref, cnt_ref, nu_ref, x_ref, inv0_ref, o_ref, invo_ref,
                     inv_ref, zbuf, sem, zsem, isem):
    tm = x_ref.shape[0]
    row0 = pl.program_id(0) * tm

    @pl.when(pl.program_id(0) == 0)
    def _():
        c = pltpu.make_async_copy(inv0_ref, inv_ref, isem)
        c.start()
        c.wait()

    def fill(start):
        return pltpu.make_async_copy(zbuf, o_ref.at[pl.ds(pl.multiple_of(start, EXPERT_TILE),
                                                          EXPERT_TILE)], zsem)

    @pl.when(pl.program_id(0) == 0)
    def _():
        zbuf[...] = jnp.zeros_like(zbuf)

        def each_expert(fn):
            def body(e, carry):
                @pl.when(cnt_ref[e] > 0)
                def _():
                    fn(fill(segend_ref[e] - EXPERT_TILE))
                return carry
            lax.fori_loop(0, N_EXPERTS, body, 0)

        def each_free_block(fn):
            def body(b, carry):
                fn(fill(b * EXPERT_TILE))
                return carry
            lax.fori_loop(nu_ref[0], nb, body, 0)

        each_expert(lambda c: c.start())
        each_free_block(lambda c: c.start())
        each_expert(lambda c: c.wait())
        each_free_block(lambda c: c.wait())

    def start(i, carry):
        for j in range(TOP_K):
            d = dest_ref[i * TOP_K + j]
            inv_ref[d] = (row0 + i) * TOP_K + j
            pltpu.make_async_copy(x_ref.at[pl.ds(i, 1)], o_ref.at[pl.ds(d, 1)], sem).start()
        return carry

    lax.fori_loop(0, tm, start, 0)
    for j in range(TOP_K):
        pltpu.make_async_copy(x_ref, o_ref.at[pl.ds(0, tm)], sem).wait()

    @pl.when(pl.program_id(0) == pl.num_programs(0) - 1)
    def _():
        c = pltpu.make_async_copy(inv_ref, invo_ref, isem)
        c.start()
        c.wait()


def moe_dispatch(dest_flat, seg_end, counts, n_used, xpk, inv_default, nb):
    n, w = xpk.shape
    tm = ROW_TILE
    rows = nb * EXPERT_TILE
    smem = lambda: pl.BlockSpec(memory_space=pltpu.SMEM)
    hbm = lambda: pl.BlockSpec(memory_space=pl.ANY)
    return pl.pallas_call(
        functools.partial(_dispatch_kernel, nb),
        grid=(n // tm,),
        in_specs=[pl.BlockSpec((tm * TOP_K,), lambda i: (i,), memory_space=pltpu.SMEM),
                  smem(), smem(), smem(),
                  pl.BlockSpec((tm, w), lambda i: (i, 0)), hbm()],
        out_specs=[hbm(), hbm()],
        out_shape=[jax.ShapeDtypeStruct((rows, w), I32), jax.ShapeDtypeStruct((rows,), I32)],
        scratch_shapes=[pltpu.SMEM((rows,), I32), pltpu.VMEM((EXPERT_TILE, w), I32),
                        pltpu.SemaphoreType.DMA(()), pltpu.SemaphoreType.DMA(()),
                        pltpu.SemaphoreType.DMA(())],
        compiler_params=_cparams(("arbitrary",)),
        name="moe_dispatch",
    )(dest_flat, seg_end, counts, n_used, xpk, inv_default)


def _experts_kernel(n_slots, nb, be_ref, nxt_ref, slot_ref, nu_ref, inv_ref, x_ref, wg_ref, wu_ref,
                    wd_ref, o_ref, wg_f, wu_f, wd_f, wg_s, wu_s, wd_s, obuf_a, obuf_b, sems, osems):
    i = pl.program_id(0)
    n_used = nu_ref[0]
    used = i < n_used
    prev = be_ref[jnp.maximum(i - 1, 0)]
    fresh = jnp.logical_and(used, jnp.logical_or(i == 0, be_ref[i] != prev))
    bm = x_ref.shape[0]

    obufs = (obuf_a, obuf_b)

    def scatter_rows(blk, par):
        for r in range(bm):
            pltpu.make_async_copy(obufs[par].at[pl.ds(r, 1)],
                                  o_ref.at[pl.ds(inv_ref[blk * bm + r], 1)], osems.at[par]).start()

    def wait_rows(par):
        pltpu.make_async_copy(obufs[par], o_ref.at[pl.ds(0, bm)], osems.at[par]).wait()

    def by_parity(cond, fn):
        for par in range(2):
            @pl.when(jnp.logical_and(cond, i % 2 == par))
            def _(par=par):
                fn(par)

    @pl.when(i == 0)
    def _():
        obuf_b[...] = jnp.zeros_like(obuf_b)
        spare = [pltpu.make_async_copy(obuf_b, o_ref.at[pl.ds(n_slots + e * bm, bm)], osems.at[1])
                 for e in range(N_EXPERTS)]
        for c in spare:
            c.start()
        for c in spare:
            c.wait()

    by_parity(jnp.logical_and(i >= 2, i - 2 < n_used), wait_rows)

    def weight_copies(e, slot):
        return (pltpu.make_async_copy(wg_ref.at[e], wg_f.at[slot], sems.at[slot, 0]),
                pltpu.make_async_copy(wu_ref.at[e], wu_f.at[slot], sems.at[slot, 1]),
                pltpu.make_async_copy(wd_ref.at[e], wd_f.at[slot], sems.at[slot, 2]))

    @pl.when(jnp.logical_and(used, i == 0))
    def _():
        for c in weight_copies(be_ref[0], 0):
            c.start()

    @pl.when(fresh)
    def _():
        slot = slot_ref[i]
        for c in weight_copies(be_ref[i], slot):
            c.wait()

        @pl.when(nxt_ref[i] >= 0)
        def _():
            for c in weight_copies(nxt_ref[i], 1 - slot):
                c.start()

        wg_s[...] = wg_f[slot].astype(BF16)
        wu_s[...] = wu_f[slot].astype(BF16)
        wd_s[...] = wd_f[slot].astype(BF16)

    def compute(par):
        hi, lo = _unpack_bf16_pairs(x_ref[...])
        hi, lo = hi.astype(BF16), lo.astype(BF16)
        half = hi.shape[1]
        gp = _dot(hi, wg_s[:half, :]) + _dot(lo, wg_s[half:, :])
        up = _dot(hi, wu_s[:half, :]) + _dot(lo, wu_s[half:, :])
        act = gp * _sigmoid(gp) * up
        obufs[par][...] = _pack_bf16_pairs(_dot(act.astype(BF16), wd_s[...]))

    @pl.when(jnp.logical_and(used, i == 0))
    def _():
        compute(0)

    def send_prev_and_compute(par):
        scatter_rows(i - 1, 1 - par)
        compute(par)

    by_parity(jnp.logical_and(used, i > 0), send_prev_and_compute)
    by_parity(jnp.logical_and(i == n_used, i > 0), lambda par: scatter_rows(i - 1, 1 - par))

    @pl.when(jnp.logical_and(i == nb - 1, n_used == nb - 1))
    def _():
        wait_rows((nb - 2) % 2)


def moe_experts(blk_e, nxt_e, slot, n_used, inv, xs, n_slots, w_gate, w_up, w_down):
    w = xs.shape[1]
    nb = xs.shape[0] // EXPERT_TILE
    _, d, de = w_gate.shape
    grid_spec = pltpu.PrefetchScalarGridSpec(
        num_scalar_prefetch=5,
        grid=(nb,),
        in_specs=[pl.BlockSpec((EXPERT_TILE, w), lambda i, be, nx, sl, nu, iv: (jnp.minimum(i, nu[0] - 1), 0)),
                  pl.BlockSpec(memory_space=pl.ANY), pl.BlockSpec(memory_space=pl.ANY),
                  pl.BlockSpec(memory_space=pl.ANY)],
        out_specs=pl.BlockSpec(memory_space=pl.ANY),
        scratch_shapes=[pltpu.VMEM((2, d, de), F32), pltpu.VMEM((2, d, de), F32),
                        pltpu.VMEM((2, de, d), F32),
                        pltpu.VMEM((d, de), BF16), pltpu.VMEM((d, de), BF16),
                        pltpu.VMEM((de, d), BF16), pltpu.VMEM((EXPERT_TILE, w), I32),
                        pltpu.VMEM((EXPERT_TILE, w), I32),
                        pltpu.SemaphoreType.DMA((2, 3)), pltpu.SemaphoreType.DMA((2,))],
    )
    return pl.pallas_call(
        functools.partial(_experts_kernel, n_slots, nb),
        grid_spec=grid_spec,
        out_shape=jax.ShapeDtypeStruct((n_slots + N_EXPERTS * EXPERT_TILE, w), I32),
        compiler_params=_cparams(("arbitrary",)),
        name="moe_experts",
    )(blk_e, nxt_e, slot, n_used, inv, xs, w_gate, w_up, w_down)


def _combine_kernel(alpha, h_ref, gate_ref, ys_ref, wsg_ref, wsu_ref, wsd_ref, l2g_ref, l2b_ref, o_ref):
    tm = h_ref.shape[0]
    h = h_ref[...]
    hb = h.astype(BF16)
    gp = _dot(hb, wsg_ref[...])
    up = _dot(hb, wsu_ref[...])
    shared = _dot((gp * _sigmoid(gp) * up).astype(BF16), wsd_ref[...])
    row = lax.broadcasted_iota(I32, (tm, tm * TOP_K), 0)
    col = lax.broadcasted_iota(I32, (tm, tm * TOP_K), 1)
    g_hi, g_lo = _split_bf16(jnp.where(col // TOP_K == row, gate_ref[0], 0.0))
    hi, lo = _unpack_bf16_pairs(ys_ref[...])
    hi, lo = hi.astype(BF16), lo.astype(BF16)
    routed = jnp.concatenate([_dot(g_hi, hi) + _dot(g_lo, hi), _dot(g_hi, lo) + _dot(g_lo, lo)], axis=1)
    o_ref[...] = _layer_norm(alpha * h + routed + shared, l2g_ref[...], l2b_ref[...])


def moe_combine(h1, gate_rows, ys, pw, alpha):
    n, d = h1.shape
    tm = gate_rows.shape[2] // TOP_K
    full = lambda a: pl.BlockSpec(a.shape, lambda i: (0,) * a.ndim)
    params = (pw["ws_gate"], pw["ws_up"], pw["ws_down"], pw["ln2_g"], pw["ln2_b"])
    return pl.pallas_call(
        functools.partial(_combine_kernel, alpha),
        grid=(n // tm,),
        in_specs=[pl.BlockSpec((tm, d), lambda i: (i, 0)),
                  pl.BlockSpec((1, 1, tm * TOP_K), lambda i: (i, 0, 0)),
                  pl.BlockSpec((tm * TOP_K, d // 2), lambda i: (i, 0))]
                 + [full(a) for a in params],
        out_specs=pl.BlockSpec((tm, d), lambda i: (i, 0)),
        out_shape=jax.ShapeDtypeStruct((n, d), F32),
        compiler_params=_cparams(("parallel",)),
        name="moe_combine",
    )(h1, gate_rows, ys, *params)


def _round_up(x, m):
    return (x + m - 1) // m * m


def _rope_tables(pos, head):
    half = head // 2
    inv = ROPE_THETA ** (-jnp.arange(half, dtype=F32) / half)
    ang = pos.astype(F32)[:, None] * inv[None, :]
    cos, sin = jnp.cos(ang), jnp.sin(ang)
    rep = LANES // head
    c = jnp.tile(jnp.concatenate([cos, cos], axis=1), (1, rep))
    s = jnp.tile(jnp.concatenate([-sin, sin], axis=1), (1, rep))
    return c, s


def _permute_cols(m, axis=-1):
    axis = axis % m.ndim
    cut = lambda lo, hi: lax.slice_in_dim(m, lo, hi, axis=axis)
    a0 = SHIFT_DIM
    i0 = a0 + A_DIM + 2 * A_KV_DIM + IDX_HEADS * IDX_DIM

    def pad(w):
        shape = list(m.shape)
        shape[axis] = w
        return jnp.zeros(shape, m.dtype)

    pieces = [
        cut(0, 3 * R_DIM),
        cut(a0, a0 + A_DIM),
        cut(a0 + A_DIM + 2 * A_KV_DIM, i0),
        cut(a0 + A_DIM, a0 + A_DIM + 2 * A_KV_DIM),
        cut(i0, i0 + IDX_DIM + IDX_HEADS),
        pad(LANES - IDX_DIM - IDX_HEADS),
        cut(3 * R_DIM, SHIFT_DIM),
        pad(LORA_W - (SHIFT_DIM - 3 * R_DIM)),
    ]
    return jnp.concatenate(pieces, axis=axis)


def kernel(x_prompt, x_sample, cache_k, cache_v, cache_idx_k, state_wkv, state_shift, page_table,
           meta, ln0_g, ln0_b, w_in, mu_shift, w0, w_b, a0, a_b, g_b, k_k, k_a, r_k, gn_g, gn_b,
           w_out, ln1_g, ln1_b, w_router, e_bias, w_gate, w_up, w_down, ws_gate, ws_up, ws_down,
           ln2_g, ln2_b):
    depth = w_in.shape[0]
    assert depth == 1, "single trunk layer"
    bsz, s_p, d = x_prompt.shape
    s_dec, s_s, _ = x_sample.shape
    assert s_s == 1, "one decode token per sequence"
    t_real = N_META + s_p
    tp = _round_up(t_real, LANES)
    assert (bsz * tp) % ROW_TILE == 0
    sp = _round_up(s_dec, ROW_TILE)
    n_prompt = bsz * tp
    n = n_prompt + sp
    n_pool, page = cache_k.shape[1], cache_k.shape[2]
    n_pages = page_table.shape[1]
    past = n_pages * page
    alpha = float((2 * depth) ** 0.25)
    row2 = lambda a: a.reshape(1, -1)

    meta_rows = jnp.broadcast_to(meta[None], (bsz, N_META, d))
    xp = jnp.concatenate([meta_rows, x_prompt, jnp.zeros((bsz, tp - t_real, d), F32)], axis=1)
    x_all = jnp.concatenate([xp.reshape(n_prompt, d), x_sample.reshape(s_dec, d),
                             jnp.zeros((sp - s_dec, d), F32)], axis=0)
    pos = jnp.concatenate([jnp.tile(jnp.arange(tp), bsz), jnp.full((sp,), past)])
    c128, s128 = _rope_tables(pos, A_HEAD)
    c64, s64 = _rope_tables(pos, IDX_DIM)

    w_in_t = _permute_cols(w_in[0].T.astype(BF16), axis=0)
    mu_k = _permute_cols(
        jnp.concatenate([mu_shift[0], jnp.zeros((w_in.shape[2] - SHIFT_DIM,), F32)])[None, :])
    head_of = jnp.arange(R_DIM) // R_HEAD
    e_mat = (head_of[:, None] == jnp.arange(R_HEADS)[None, :]).astype(F32)
    zpad = lambda a, rows_before, rows_total: jnp.concatenate(
        [jnp.zeros((rows_before, a.shape[1]), a.dtype), a,
         jnp.zeros((rows_total - rows_before - a.shape[0], a.shape[1]), a.dtype)], axis=0)
    pw = {
        "mu_x": mu_k[:, :3 * R_DIM], "mu_lo": mu_k[:, C_LORA:],
        "w0": row2(w0[0]), "a0": row2(a0[0]), "k_k": row2(k_k[0]), "k_a": row2(k_a[0]),
        "r_k": row2(r_k[0]), "gn_g": row2(gn_g[0]), "gn_b": row2(gn_b[0]),
        "w_b": zpad(w_b[0], 0, LANES).astype(BF16),
        "a_b": zpad(a_b[0], D_DECAY_LORA, LANES).astype(BF16),
        "g_b": zpad(g_b[0], 0, LORA_W - LANES).astype(BF16),
        "e": e_mat.astype(BF16), "et": e_mat.T.astype(BF16),
        "ln0_g": row2(ln0_g), "ln0_b": row2(ln0_b),
        "ln1_g": row2(ln1_g[0]), "ln1_b": row2(ln1_b[0]),
        "ln2_g": row2(ln2_g[0]), "ln2_b": row2(ln2_b[0]),
        "w_out": w_out[0].astype(BF16),
        "w_router": jnp.pad(w_router[0], ((0, 0), (0, LANES - N_EXPERTS))),
        "ws_gate": ws_gate[0].astype(BF16), "ws_up": ws_up[0].astype(BF16),
        "ws_down": ws_down[0].astype(BF16),
    }

    p = ln_proj(x_all, pw["ln0_g"], pw["ln0_b"], w_in_t)
    q_r, iq_r, k_r, ik_r, iw, kt_r, ikt_r = rope_all(p, c128, s128, c64, s64)

    pre_p = rwkv_pre(p, 0, n_prompt, None, pw, t_real, tp)
    shift_k = _permute_cols(jnp.concatenate(
        [state_shift[0], jnp.zeros((s_dec, w_in.shape[2] - SHIFT_DIM), F32)], axis=1))
    shift_k = jnp.concatenate([shift_k, jnp.zeros((sp - s_dec, P_COLS), F32)], axis=0)
    pre_s = rwkv_pre(p, n_prompt, sp, (shift_k[:, :3 * R_DIM], shift_k[:, C_LORA:]), pw, t_real, tp)
    r_p, ld_p, k_p, v_p, kk_p, b_p, g_p, bon_p = pre_p
    r_s, ld_s, k_s, v_s, kk_s, b_s, g_s, bon_s = pre_s
    y_p, wkv_p = rwkv_scan(r_p, ld_p, k_p, v_p, kk_p, b_p, bsz, tp)
    heads = lambda a: a[:s_dec].reshape(s_dec, R_HEADS, R_HEAD).transpose(1, 2, 0)
    y_hs, wkv_hs = rwkv_step(heads(r_s), heads(ld_s), heads(k_s), heads(v_s), heads(kk_s), heads(b_s),
                             state_wkv[0].transpose(1, 2, 3, 0))
    wkv_s = wkv_hs.transpose(3, 0, 1, 2)
    y_s = jnp.concatenate([y_hs.transpose(2, 0, 1).reshape(s_dec, R_DIM),
                           jnp.zeros((sp - s_dec, R_DIM), F32)], axis=0)

    n_sel_p = min(TOPK_KEYS, t_real // 4)
    a_p = dsa_prompt(q_r, iq_r, iw, ikt_r, kt_r, p, bsz, tp, n_sel_p)
    n_sel_s = min(TOPK_KEYS, (past + 1) // 4)
    pt_flat = page_table.reshape(-1).astype(I32)
    srow = slice(n_prompt, n_prompt + s_dec)
    sc_s = dsa_step_scores(pt_flat, iq_r[srow].reshape(s_dec, IDX_HEADS, IDX_DIM),
                           iw[srow].reshape(s_dec, IDX_HEADS, 1), ik_r[srow].reshape(s_dec, 1, IDX_DIM),
                           cache_idx_k[0].transpose(0, 2, 1).reshape(n_pool * IDX_DIM, page),
                           n_pages, page)
    sel_s = dsa_step_select(sc_s.reshape(s_dec, -1), n_sel_s, past).reshape(sc_s.shape)
    slot = jnp.arange(page)[:, None]
    expand = (jnp.arange(page * A_KV_HEADS)[None, :] // A_KV_HEADS == slot).astype(BF16)
    a_s = dsa_step_attn(pt_flat, q_r[srow].reshape(s_dec, A_HEADS, A_HEAD),
                        k_r[srow].reshape(s_dec, A_KV_HEADS, A_HEAD),
                        p[srow, C_VA:C_VA + A_KV_DIM].reshape(s_dec, A_KV_HEADS, A_HEAD),
                        sel_s, expand,
                        cache_k[0].reshape(n_pool * page * A_KV_HEADS, A_HEAD),
                        cache_v[0].reshape(n_pool * page * A_KV_HEADS, A_HEAD), n_pages, page)
    a_s = jnp.concatenate([a_s.reshape(s_dec, A_DIM), jnp.zeros((sp - s_dec, A_DIM), F32)], axis=0)

    h1, scores_t, xpk = mix_ln1_router(x_all, (y_p, bon_p, g_p, a_p), (y_s, bon_s, g_s, a_s), pw, alpha)
    eidx_t, gate_t, pos_t, counts = route(scores_t, e_bias[0].reshape(N_EXPERTS, 1))

    n_slots = n * TOP_K
    nb = (n_slots + N_EXPERTS * (EXPERT_TILE - 1)) // EXPERT_TILE + 1
    counts = counts[:, 0]
    padded = (counts + EXPERT_TILE - 1) // EXPERT_TILE * EXPERT_TILE
    seg_end = jnp.cumsum(padded).astype(I32)
    seg_start = seg_end - padded
    experts = jnp.arange(N_EXPERTS)
    start_of = jnp.sum(jnp.where(eidx_t[:, :, None] == experts, seg_start, 0), axis=-1)
    dest = (start_of + pos_t).T.astype(I32).reshape(-1)
    blk_row = jnp.arange(nb) * EXPERT_TILE
    blk_e = jnp.minimum(jnp.sum(seg_end[None, :] <= blk_row[:, None], axis=1),
                        N_EXPERTS - 1).astype(I32)
    n_used = (seg_end[-1] // EXPERT_TILE).astype(I32)
    run_start = jnp.concatenate([jnp.ones((1,), I32), (blk_e[1:] != blk_e[:-1]).astype(I32)])
    slot = ((jnp.cumsum(run_start) - 1) % 2).astype(I32)
    run_end = seg_end[blk_e] // EXPERT_TILE
    nxt_e = jnp.where(run_end < n_used, blk_e[jnp.minimum(run_end, nb - 1)], -1).astype(I32)
    n_used = n_used.reshape(1)
    spare = (n_slots + blk_e[:, None] * EXPERT_TILE + jnp.arange(EXPERT_TILE)[None, :]).astype(I32)
    xs, inv = moe_dispatch(dest, seg_end, counts.astype(I32), n_used, xpk, spare.reshape(-1), nb)
    ys = moe_experts(blk_e, nxt_e, slot, n_used, inv, xs, n_slots, w_gate[0], w_up[0], w_down[0])
    tc = Q_TILE
    h2 = moe_combine(h1, gate_t.T.reshape(n // tc, 1, tc * TOP_K), ys, pw, alpha)

    def prompt_rows(a):
        return a[:n_prompt].reshape(bsz, tp, -1)[:, :t_real]

    y_prompt = h2[:n_prompt].reshape(bsz, tp, d)[:, N_META:t_real]
    y_sample = h2[srow].reshape(s_dec, 1, d)
    k_prompt = prompt_rows(k_r).reshape(1, bsz, t_real, A_KV_HEADS, A_HEAD)
    v_prompt = prompt_rows(p[:, C_VA:C_VA + A_KV_DIM]).reshape(1, bsz, t_real, A_KV_HEADS, A_HEAD)
    ik_prompt = prompt_rows(ik_r)[None]
    last = jnp.arange(bsz) * tp + t_real - 1
    unperm = lambda rows: jnp.concatenate([rows[:, :3 * R_DIM],
                                           rows[:, C_LORA:C_LORA + SHIFT_DIM - 3 * R_DIM]], axis=1)
    shift_prompt = unperm(p[last])[None]
    k_sample = k_r[srow].reshape(1, s_dec, 1, A_KV_HEADS, A_HEAD)
    v_sample = p[srow, C_VA:C_VA + A_KV_DIM].reshape(1, s_dec, 1, A_KV_HEADS, A_HEAD)
    ik_sample = ik_r[srow].reshape(1, s_dec, 1, IDX_DIM)
    shift_sample = unperm(p[srow])[None]
    return (y_prompt, y_sample, k_prompt, v_prompt, ik_prompt, wkv_p[None], shift_prompt,
            k_sample, v_sample, ik_sample, wkv_s[None], shift_sample)
```

```python
import functools

import numpy as np
import jax
import jax.numpy as jnp
from jax import lax
from jax.experimental import pallas as pl
from jax.experimental.pallas import tpu as pltpu

F32 = jnp.float32
BF16 = jnp.bfloat16
I32 = jnp.int32
HIGHEST = lax.Precision.HIGHEST

N_META = 16
R_HEADS, R_HEAD = 16, 64
R_DIM = R_HEADS * R_HEAD
D_DECAY_LORA, D_AAA_LORA, D_GATE_LORA = 64, 64, 160
SHIFT_DIM = 3 * R_DIM + D_DECAY_LORA + D_AAA_LORA + D_GATE_LORA
GN_EPS = 64e-5
A_HEADS, A_KV_HEADS, A_HEAD = 8, 2, 128
A_DIM = A_HEADS * A_HEAD
A_KV_DIM = A_KV_HEADS * A_HEAD
IDX_HEADS, IDX_DIM = 16, 64
TOPK_KEYS = 256
ROPE_THETA = 10000.0
N_EXPERTS, N_EXPERT_GROUPS, TOPK_GROUPS, TOP_K = 64, 8, 4, 8
ROUTED_SCALE = 2.5
LN_EPS = 1e-5

LANES = 128
SUBLANES = 8
ROW_TILE = 256
Q_TILE = 128
CHUNK = 64
EXPERT_TILE = 256
VMEM_LIMIT = 56 * 1024 * 1024
NEG_BIG = -1e30
INT_MIN = -2 ** 31

C_R, C_K, C_V = 0, R_DIM, 2 * R_DIM
C_Q = 3 * R_DIM
C_IQ = C_Q + A_DIM
C_KA = C_IQ + IDX_HEADS * IDX_DIM
C_VA = C_KA + A_KV_DIM
C_IK = C_VA + A_KV_DIM
C_LORA = C_IK + LANES
LORA_W = 384
P_COLS = C_LORA + LORA_W


def _cparams(sem):
    return pltpu.CompilerParams(dimension_semantics=sem, vmem_limit_bytes=VMEM_LIMIT)


def _dot(a, b, precision=None):
    return jnp.dot(a, b, preferred_element_type=F32, precision=precision)


def _dot_nt(a, b, precision=None):
    return lax.dot_general(a, b, (((1,), (1,)), ((), ())), preferred_element_type=F32,
                           precision=precision)


def _dot_tn(a, b, precision=None):
    return lax.dot_general(a, b, (((0,), (0,)), ((), ())), preferred_element_type=F32,
                           precision=precision)


def _split_bf16(x):
    hi = x.astype(BF16)
    return hi, (x - hi.astype(F32)).astype(BF16)


def _dot_f32_by_bf16(a, b):
    hi, lo = _split_bf16(a)
    return _dot(hi, b) + _dot(lo, b)


def _head_sums(x, e, et):
    return _dot_f32_by_bf16(_dot_f32_by_bf16(x, e), et)


def _layer_norm(x, g, b):
    mu = jnp.mean(x, axis=-1, keepdims=True)
    xc = x - mu
    var = jnp.mean(xc * xc, axis=-1, keepdims=True)
    return xc * lax.rsqrt(var + LN_EPS) * g + b


def _sigmoid(z):
    return 1.0 / (1.0 + jnp.exp(-z))


def _ln_proj_kernel(x0_ref, xn_ref, g_ref, b_ref, w_ref, o_ref, h_even, h_odd):
    i = pl.program_id(1)
    norm = lambda ref: _layer_norm(ref[...], g_ref[...], b_ref[...]).astype(BF16)

    @pl.when(i == 0)
    def _():
        h_even[...] = norm(x0_ref)

    for par, (cur, nxt) in enumerate(((h_even, h_odd), (h_odd, h_even))):
        @pl.when(i % 2 == par)
        def _(cur=cur, nxt=nxt):
            o_ref[...] = _dot_nt(cur[...], w_ref[...])
            nxt[...] = norm(xn_ref)


PROJ_ROWS = 256
PROJ_COLS = 2048


def ln_proj(x, g, b, w_t):
    n, d = x.shape
    cols = w_t.shape[0]
    tm = max(t for t in range(LANES, PROJ_ROWS + 1, LANES) if n % t == 0)
    tn = PROJ_COLS
    ni = n // tm
    return pl.pallas_call(
        _ln_proj_kernel,
        grid=(cols // tn, ni),
        in_specs=[
            pl.BlockSpec((tm, d), lambda j, i: (0, 0)),
            pl.BlockSpec((tm, d), lambda j, i: ((i + 1) % ni, 0)),
            pl.BlockSpec((1, d), lambda j, i: (0, 0)),
            pl.BlockSpec((1, d), lambda j, i: (0, 0)),
            pl.BlockSpec((tn, d), lambda j, i: (j, 0)),
        ],
        out_specs=pl.BlockSpec((tm, tn), lambda j, i: (i, j)),
        out_shape=jax.ShapeDtypeStruct((n, cols), F32),
        scratch_shapes=[pltpu.VMEM((tm, d), BF16), pltpu.VMEM((tm, d), BF16)],
        compiler_params=_cparams(("arbitrary", "arbitrary")),
        name="ln_proj",
    )(x, x, g, b, w_t)


def _rot_half(x, head):
    w = x.shape[-1]
    half = head // 2
    lane = lax.broadcasted_iota(I32, x.shape, 1)
    left = pltpu.roll(x, w - half, axis=1)
    right = pltpu.roll(x, half, axis=1)
    return jnp.where((lane % head) < half, left, right)


def _rope_kernel(q_ref, iq_ref, ka_ref, ikw_ref, c128_ref, s128_ref, c64_ref, s64_ref,
                 qo_ref, iqo_ref, ko_ref, iko_ref, iwo_ref, kt_ref, ikt_ref):
    c128, s128 = c128_ref[...], s128_ref[...]
    c64, s64 = c64_ref[...], s64_ref[...]

    def rope(x, head, c, s):
        rep = x.shape[-1] // LANES
        if rep > 1:
            c = jnp.concatenate([c] * rep, axis=1)
            s = jnp.concatenate([s] * rep, axis=1)
        return x * c + _rot_half(x, head) * s

    q = rope(q_ref[...], A_HEAD, c128, s128)
    qo_ref[...] = (q * (A_HEAD ** -0.5)).astype(BF16)
    iqo_ref[...] = rope(iq_ref[...], IDX_DIM, c64, s64).astype(BF16)
    k = rope(ka_ref[...], A_HEAD, c128, s128)
    ko_ref[...] = k
    ikw = ikw_ref[...]
    ik = rope(ikw, IDX_DIM, c64, s64)
    iko_ref[...] = ik[:, :IDX_DIM]
    iwo_ref[...] = ikw[:, IDX_DIM:IDX_DIM + IDX_HEADS]
    kt_ref[...] = k.T.astype(BF16)
    ikt_ref[...] = ik.T[:IDX_DIM, :].astype(BF16)


def rope_all(p, c128, s128, c64, s64):
    n = p.shape[0]
    tm = ROW_TILE
    row = lambda w, blk: pl.BlockSpec((tm, w), lambda i: (i, blk))
    return pl.pallas_call(
        _rope_kernel,
        grid=(n // tm,),
        in_specs=[row(A_DIM, C_Q // A_DIM), row(A_DIM, C_IQ // A_DIM),
                  row(A_KV_DIM, C_KA // A_KV_DIM), row(LANES, C_IK // LANES),
                  row(LANES, 0), row(LANES, 0), row(LANES, 0), row(LANES, 0)],
        out_specs=[row(A_DIM, 0), row(A_DIM, 0), row(A_KV_DIM, 0),
                   row(IDX_DIM, 0), row(IDX_HEADS, 0),
                   pl.BlockSpec((A_KV_DIM, tm), lambda i: (0, i)),
                   pl.BlockSpec((IDX_DIM, tm), lambda i: (0, i))],
        out_shape=[jax.ShapeDtypeStruct((n, A_DIM), BF16),
                   jax.ShapeDtypeStruct((n, IDX_HEADS * IDX_DIM), BF16),
                   jax.ShapeDtypeStruct((n, A_KV_DIM), F32),
                   jax.ShapeDtypeStruct((n, IDX_DIM), F32),
                   jax.ShapeDtypeStruct((n, IDX_HEADS), F32),
                   jax.ShapeDtypeStruct((A_KV_DIM, n), BF16),
                   jax.ShapeDtypeStruct((IDX_DIM, n), BF16)],
        compiler_params=_cparams(("parallel",)),
        name="rope",
    )(p, p, p, p, c128, s128, c64, s64)


def _rwkv_pre_kernel(t_real, tp, from_rows, *refs):
    (x_ref, lo_ref, px_ref, plo_ref, mu_ref, mulo_ref, w0_ref, wb_ref, a0_ref, ab_ref,
     gb_ref, kk_ref, ka_ref, rk_ref, e_ref, et_ref,
     r_o, ld_o, k_o, v_o, kk_o, b_o, g_o, bon_o) = refs
    x = x_ref[...]
    lo = lo_ref[...]
    tm = x.shape[0]
    if from_rows:
        i = pl.program_id(0)
        row = lax.broadcasted_iota(I32, (tm, 1), 0)
        t = (i * tm + row) % tp
        first = row == 0
        sx = jnp.where(first, px_ref[SUBLANES - 1:SUBLANES, :], pltpu.roll(x, 1, axis=0))
        slo = jnp.where(first, plo_ref[SUBLANES - 1:SUBLANES, :], pltpu.roll(lo, 1, axis=0))
        sx = jnp.where(t == 0, 0.0, sx)
        slo = jnp.where(t == 0, 0.0, slo)
        live = t < t_real
    else:
        sx = px_ref[...]
        slo = plo_ref[...]
        live = None
    xx = x + (sx - x) * mu_ref[...]
    xlo = lo + (slo - lo) * mulo_ref[...]
    r = xx[:, C_R:C_R + R_DIM]
    k = xx[:, C_K:C_K + R_DIM]
    v = xx[:, C_V:C_V + R_DIM]
    wa = xlo[:, :LANES]
    xg = xlo[:, LANES:]
    z = w0_ref[...] + _dot(jnp.tanh(wa).astype(BF16), wb_ref[...])
    nz = -z
    softplus = jnp.maximum(nz, 0.0) + jnp.log(1.0 + jnp.exp(-jnp.abs(nz)))
    logd = -jnp.exp(-softplus - 0.5)
    a = _sigmoid(a0_ref[...] + _dot(wa.astype(BF16), ab_ref[...]))
    g = _dot(_sigmoid(xg).astype(BF16), gb_ref[...])
    e, et = e_ref[...], et_ref[...]
    kkr = k * kk_ref[...]
    ss = _head_sums(kkr * kkr, e, et)
    kk = kkr / jnp.maximum(jnp.sqrt(ss), 1e-12)
    k2 = k * (1.0 + (a - 1.0) * ka_ref[...])
    bonus = _head_sums(r * k2 * rk_ref[...], e, et) * v
    b = kk * a
    if live is not None:
        zero = lambda y: jnp.where(live, y, 0.0)
        logd, k2s, vs, kk, b = zero(logd), zero(k2), zero(v), zero(kk), zero(b)
    else:
        k2s, vs = k2, v
    r_o[...] = r
    ld_o[...] = logd
    k_o[...] = k2s
    v_o[...] = vs
    kk_o[...] = kk
    b_o[...] = b
    g_o[...] = g
    bon_o[...] = bonus


def rwkv_pre(p, row0, nrows, prev, pw, t_real, tp):
    tm = min(ROW_TILE, nrows)
    blk0 = row0 // tm
    from_rows = prev is None
    xw = 3 * R_DIM
    cur_x = pl.BlockSpec((tm, xw), lambda i: (blk0 + i, 0))
    cur_lo = pl.BlockSpec((tm, LORA_W), lambda i: (blk0 + i, C_LORA // LORA_W))
    if from_rows:
        r8 = tm // SUBLANES
        prev_x = pl.BlockSpec((SUBLANES, xw), lambda i: (jnp.maximum((blk0 + i) * r8 - 1, 0), 0))
        prev_lo = pl.BlockSpec((SUBLANES, LORA_W),
                               lambda i: (jnp.maximum((blk0 + i) * r8 - 1, 0), C_LORA // LORA_W))
        prev_args = (p, p)
    else:
        prev_x = pl.BlockSpec((tm, xw), lambda i: (i, 0))
        prev_lo = pl.BlockSpec((tm, LORA_W), lambda i: (i, 0))
        prev_args = prev
    full = lambda a: pl.BlockSpec(a.shape, lambda i: (0,) * a.ndim)
    params = (pw["mu_x"], pw["mu_lo"], pw["w0"], pw["w_b"], pw["a0"], pw["a_b"], pw["g_b"],
              pw["k_k"], pw["k_a"], pw["r_k"], pw["e"], pw["et"])
    out = pl.BlockSpec((tm, R_DIM), lambda i: (i, 0))
    return pl.pallas_call(
        functools.partial(_rwkv_pre_kernel, t_real, tp, from_rows),
        grid=(nrows // tm,),
        in_specs=[cur_x, cur_lo, prev_x, prev_lo] + [full(a) for a in params],
        out_specs=[out] * 8,
        out_shape=[jax.ShapeDtypeStruct((nrows, R_DIM), F32)] * 8,
        compiler_params=_cparams(("parallel",)),
        name="rwkv_pre_rows" if from_rows else "rwkv_pre_step",
    )(p, p, *prev_args, *params)


def _rwkv_scan_kernel(r_ref, ld_ref, k_ref, v_ref, kk_ref, b_ref, y_ref, s_ref, ss_scr):
    c = pl.program_id(1)

    @pl.when(c == 0)
    def _():
        ss_scr[...] = jnp.zeros_like(ss_scr)

    n = CHUNK
    n2 = 2 * n
    pairs = R_HEADS // 2
    ld_all = ld_ref[...]
    ri = lax.broadcasted_iota(I32, (n, n), 0)
    ci = lax.broadcasted_iota(I32, (n, n), 1)
    cum_all = _dot((ci <= ri).astype(F32), ld_all, HIGHEST)
    head0 = lax.broadcasted_iota(I32, (n, LANES), 1) < R_HEAD
    r4 = lax.broadcasted_iota(I32, (2 * n2, 2 * n2), 0)
    c4 = lax.broadcasted_iota(I32, (2 * n2, 2 * n2), 1)
    tri = (c4 % n) < (r4 % n) + jnp.where(r4 < n2, 0, 1)
    re = lax.broadcasted_iota(I32, (n2, n2), 0)
    ce = lax.broadcasted_iota(I32, (n2, n2), 1)
    eye = (re == ce).astype(F32)

    def stack(x):
        return jnp.concatenate([jnp.where(head0, x, 0.0), jnp.where(head0, 0.0, x)], axis=0)

    ar, bk, v2, ss, e_last = [], [], [], [], []
    for p in range(pairs):
        sl = slice(p * LANES, (p + 1) * LANES)
        cum, ld = cum_all[:, sl], ld_all[:, sl]
        e_pos = jnp.exp(cum)
        e_neg = jnp.exp(-cum)
        at = -kk_ref[:, sl] * jnp.exp(cum - ld)
        ar.append(jnp.concatenate([stack(at), stack(r_ref[:, sl] * e_pos)], axis=0).astype(BF16))
        bk.append(jnp.concatenate([stack(b_ref[:, sl] * e_neg), stack(k_ref[:, sl] * e_neg)],
                                  axis=0).astype(BF16))
        v2.append(stack(v_ref[:, sl]).astype(BF16))
        ss.append(ss_scr[p])
        e_last.append(e_pos[n - 1:n, :])
    xy0 = [_dot_nt(ar[p], ss[p].astype(BF16)) for p in range(pairs)]
    sc = [jnp.where(tri, _dot_nt(ar[p], bk[p]), 0.0) for p in range(pairs)]
    lp = [s[:n2, :n2] for s in sc]
    t = [eye + l for l in lp]
    m = 1
    while 2 * m < n:
        lpb = [l.astype(BF16) for l in lp]
        lp = [_dot(l, l) for l in lpb]
        t = [t[p] + _dot(t[p].astype(BF16), lp[p].astype(BF16)) for p in range(pairs)]
        m *= 2
    w = [xy0[p][:n2] + _dot(sc[p][:n2, n2:].astype(BF16), v2[p]) for p in range(pairs)]
    u = [_dot(t[p].astype(BF16), w[p].astype(BF16)) for p in range(pairs)]
    uv = [jnp.concatenate([u[p].astype(BF16), v2[p]], axis=0) for p in range(pairs)]
    y = [xy0[p][n2:] + _dot(sc[p][n2:].astype(BF16), uv[p]) for p in range(pairs)]
    upd = [_dot_tn(uv[p], bk[p]) for p in range(pairs)]
    for p in range(pairs):
        y_ref[:, p * LANES:(p + 1) * LANES] = y[p][:n] + y[p][n:]
        ss_scr[p] = (ss[p] + upd[p]) * e_last[p]

    @pl.when(c == pl.num_programs(1) - 1)
    def _():
        for p in range(R_HEADS // 2):
            ss = ss_scr[p]
            s_ref[0, 2 * p] = ss[:R_HEAD, :R_HEAD]
            s_ref[0, 2 * p + 1] = ss[R_HEAD:, R_HEAD:]


def rwkv_scan(r, ld, k, v, kk, b, batch, tp):
    nchunk = tp // CHUNK
    blk = pl.BlockSpec((CHUNK, R_DIM), lambda bi, c: (bi * nchunk + c, 0))
    return pl.pallas_call(
        _rwkv_scan_kernel,
        grid=(batch, nchunk),
        in_specs=[blk] * 6,
        out_specs=[blk, pl.BlockSpec((1, R_HEADS, R_HEAD, R_HEAD), lambda bi, c: (bi, 0, 0, 0))],
        out_shape=[jax.ShapeDtypeStruct((batch * tp, R_DIM), F32),
                   jax.ShapeDtypeStruct((batch, R_HEADS, R_HEAD, R_HEAD), F32)],
        scratch_shapes=[pltpu.VMEM((R_HEADS // 2, LANES, LANES), F32)],
        compiler_params=_cparams(("parallel", "arbitrary")),
        name="rwkv_scan",
    )(r, ld, k, v, kk, b)


STEP_ROWS = 16


def _rwkv_step_kernel(r_ref, ld_ref, k_ref, v_ref, kk_ref, b_ref, s_ref, y_ref, so_ref):
    r, k = r_ref[0], k_ref[0]
    dec = jnp.exp(ld_ref[0])
    na = -kk_ref[0]
    b = b_ref[0]
    v = v_ref[0]
    ys = []
    for v0 in range(0, R_HEAD, STEP_ROWS):
        rows = range(v0, v0 + STEP_ROWS)
        s = [s_ref[0, vi] for vi in rows]
        sa = [jnp.sum(x * na, axis=0, keepdims=True) for x in s]
        s_new = [x * dec + a * b + v[vi:vi + 1, :] * k for x, a, vi in zip(s, sa, rows)]
        ys += [jnp.sum(x * r, axis=0, keepdims=True) for x in s_new]
        for x, vi in zip(s_new, rows):
            so_ref[0, vi] = x
    y_ref[0] = jnp.concatenate(ys, axis=0)


def rwkv_step(r, ld, k, v, kk, b, state):
    s = state.shape[-1]
    vec = pl.BlockSpec((1, R_HEAD, s), lambda h: (h, 0, 0))
    st = pl.BlockSpec((1, R_HEAD, R_HEAD, s), lambda h: (h, 0, 0, 0))
    return pl.pallas_call(
        _rwkv_step_kernel,
        grid=(R_HEADS,),
        in_specs=[vec] * 6 + [st],
        out_specs=[vec, st],
        out_shape=[jax.ShapeDtypeStruct((R_HEADS, R_HEAD, s), F32),
                   jax.ShapeDtypeStruct(state.shape, F32)],
        compiler_params=_cparams(("parallel",)),
        name="rwkv_step",
    )(r, ld, k, v, kk, b, state)


def _select_topk(score, allowed, n_sel, store):
    bits = lax.bitcast_convert_type(score, I32)
    key = jnp.where(bits < 0, bits ^ jnp.int32(0x7FFFFFFF), bits)
    key = jnp.where(allowed, key, jnp.int32(INT_MIN))
    m, w = score.shape
    one, zero = jnp.ones((), BF16), jnp.zeros((), BF16)

    def byte(shift):
        if shift == 24:
            d = lax.shift_right_arithmetic(key, 24) + 128
        else:
            d = lax.shift_right_logical(key, shift) & 255
        return d.astype(F32).astype(BF16)

    def count(flags):
        acc = flags[:, :LANES]
        for c in range(LANES, w, LANES):
            acc = acc + flags[:, c:c + LANES]
        return jnp.sum(acc.astype(F32), axis=1, keepdims=True)

    need = jnp.full((m, 1), float(n_sel), F32)
    x = byte(24)
    tau = jnp.zeros((m, 1), I32)
    for shift in (24, 16, 8, 0):
        def body(it, t, x=x, need=need):
            step = lax.shift_left(jnp.int32(1), jnp.int32(6) - 2 * it).astype(F32)
            cnts = [count(jnp.where(x >= (t + mult * step).astype(BF16), one, zero))
                    for mult in (1.0, 2.0, 3.0)]
            hits = sum(jnp.where(cnt >= need, 1.0, 0.0) for cnt in cnts)
            return t + hits * step

        t = lax.fori_loop(0, 4, body, jnp.zeros((m, 1), F32))
        tb = t.astype(BF16)
        need = need - count(jnp.where(x > tb, one, zero))
        digit = t.astype(I32) - (128 if shift == 24 else 0)
        tau = tau | lax.shift_left(digit, shift)
        if shift:
            x = jnp.where(x == tb, byte(shift - 8), -one)
    store(jnp.logical_and(key >= tau, allowed))
    ties = count(jnp.where(x == tb, one, zero))
    crowded = jnp.logical_and(ties > need, tau > INT_MIN)

    @pl.when(jnp.max(jnp.where(crowded, 1.0, 0.0)) > 0.0)
    def _():
        above = jnp.logical_and(key > tau, allowed)
        equal = jnp.logical_and(key == tau, allowed)
        ra = lax.broadcasted_iota(I32, (KEY_TILE, KEY_TILE), 0)
        rb = lax.broadcasted_iota(I32, (KEY_TILE, KEY_TILE), 1)
        upto = (ra <= rb).astype(BF16)
        seen = jnp.zeros((m, 1), F32)
        keep = []
        for c0 in range(0, w, KEY_TILE):
            c1 = min(c0 + KEY_TILE, w)
            eq = equal[:, c0:c1]
            flags = jnp.where(eq, 1.0, 0.0)
            rank = seen + _dot(flags.astype(BF16), upto[:c1 - c0, :c1 - c0])
            keep.append(jnp.logical_and(eq, rank <= need))
            seen = seen + jnp.sum(flags, axis=1, keepdims=True)
        store(jnp.logical_or(above, jnp.concatenate(keep, axis=1)))


KEY_TILE = 256
Q_TILES_PER_EXTENT = 4


def _dsa_prompt_block(n_sel, tk, i, q_ref, iq_ref, iw_ref, ikt_ref, kt_ref, v_ref, o_ref, sc_ref):
    tq = q_ref.shape[0]
    iw = iw_ref[...] * ((IDX_HEADS * IDX_DIM) ** -0.5)
    iq = iq_ref[...]
    iq_h = [iq[:, h * IDX_DIM:(h + 1) * IDX_DIM] for h in range(IDX_HEADS)]
    iw_h = [iw[:, h:h + 1] for h in range(IDX_HEADS)]
    for c0 in range(0, tk, KEY_TILE):
        c1 = min(c0 + KEY_TILE, tk)
        ikb = ikt_ref[:, c0:c1]
        acc = jnp.maximum(_dot(iq_h[0], ikb), 0.0) * iw_h[0]
        for h in range(1, IDX_HEADS):
            acc = acc + jnp.maximum(_dot(iq_h[h], ikb), 0.0) * iw_h[h]
        sc_ref[:, c0:c1] = acc
    qpos = i * tq + lax.broadcasted_iota(I32, (tq, 1), 0)
    kpos = lax.broadcasted_iota(I32, (1, tk), 1)

    def store_bias(mask):
        sc_ref[:, :tk] = jnp.where(mask, 0.0, NEG_BIG)

    _select_topk(sc_ref[:, :tk], kpos <= qpos, n_sel, store_bias)
    bias = sc_ref[:, :tk]
    q = q_ref[...]
    rep = A_HEADS // A_KV_HEADS
    for g in range(A_KV_HEADS):
        kg = kt_ref[g * A_HEAD:(g + 1) * A_HEAD, :tk]
        vg = v_ref[:tk, g * A_HEAD:(g + 1) * A_HEAD].astype(BF16)
        for rr in range(rep):
            h = g * rep + rr
            s = _dot(q[:, h * A_HEAD:(h + 1) * A_HEAD], kg) + bias
            m = jnp.max(s, axis=1, keepdims=True)
            p = jnp.exp(s - m)
            l = jnp.sum(p, axis=1, keepdims=True)
            o_ref[:, h * A_HEAD:(h + 1) * A_HEAD] = _dot(p.astype(BF16), vg) / l


def _dsa_prompt_kernel(n_sel, *refs):
    i = pl.program_id(1)
    tq = refs[0].shape[0]
    tp = refs[5].shape[0]
    nq = tp // tq
    for lo in range(0, nq, Q_TILES_PER_EXTENT):
        hi = min(lo + Q_TILES_PER_EXTENT, nq)

        @pl.when(jnp.logical_and(i >= lo, i < hi))
        def _(hi=hi):
            _dsa_prompt_block(n_sel, hi * tq, i, *refs)


def dsa_prompt(q, iq, iw, ikt, kt, p, batch, tp, n_sel):
    nq = tp // Q_TILE
    qrow = lambda w: pl.BlockSpec((Q_TILE, w), lambda b, i: (b * nq + i, 0))
    keys = lambda w, blk: pl.BlockSpec((tp, w), lambda b, i: (b, blk))
    keys_t = lambda w: pl.BlockSpec((w, tp), lambda b, i: (0, b))
    return pl.pallas_call(
        functools.partial(_dsa_prompt_kernel, n_sel),
        grid=(batch, nq),
        in_specs=[qrow(A_DIM), qrow(IDX_HEADS * IDX_DIM), qrow(IDX_HEADS),
                  keys_t(IDX_DIM), keys_t(A_KV_DIM), keys(A_KV_DIM, C_VA // A_KV_DIM)],
        out_specs=qrow(A_DIM),
        out_shape=jax.ShapeDtypeStruct((batch * tp, A_DIM), F32),
        scratch_shapes=[pltpu.VMEM((Q_TILE, tp), F32)],
        compiler_params=_cparams(("parallel", "parallel")),
        name="dsa_prompt",
    )(q, iq, iw, ikt, kt, p)


SCORE_SEQS = 4
ATTN_SEQS = 2


def _dsa_step_score_kernel(n_pages, page, pt_ref, iq_ref, iw_ref, ikn_ref, *refs):
    sb = iq_ref.shape[0]
    pages = refs[:sb * n_pages]
    o_ref = refs[sb * n_pages]
    lane = lax.broadcasted_iota(I32, (1, LANES), 1)
    for q in range(sb):
        iq = iq_ref[q]
        iw = iw_ref[q] * ((IDX_HEADS * IDX_DIM) ** -0.5)
        for j in range(n_pages):
            d = _dot(iq, pages[q * n_pages + j][...].astype(BF16))
            o_ref[q, :, j * page:(j + 1) * page] = jnp.sum(jnp.maximum(d, 0.0) * iw, axis=0,
                                                           keepdims=True)
        dn = jnp.sum(iq.astype(F32) * ikn_ref[q], axis=1, keepdims=True)
        sn = jnp.sum(jnp.maximum(dn, 0.0) * iw, axis=0, keepdims=True)
        o_ref[q, :, n_pages * page:] = jnp.where(lane == 0, sn, 0.0)


def _seqs_per_step(s, want):
    return want if s % want == 0 else 1


def dsa_step_scores(pt_flat, iq, iw, ik_new, cik2d, n_pages, page):
    s = iq.shape[0]
    sb = _seqs_per_step(s, SCORE_SEQS)
    kw = n_pages * page + LANES
    page_spec = lambda q, j: pl.BlockSpec(
        (IDX_DIM, page), lambda i, pt: (pt[(i * sb + q) * n_pages + j], 0))
    grid_spec = pltpu.PrefetchScalarGridSpec(
        num_scalar_prefetch=1,
        grid=(s // sb,),
        in_specs=[pl.BlockSpec((sb, IDX_HEADS, IDX_DIM), lambda i, pt: (i, 0, 0)),
                  pl.BlockSpec((sb, IDX_HEADS, 1), lambda i, pt: (i, 0, 0)),
                  pl.BlockSpec((sb, 1, IDX_DIM), lambda i, pt: (i, 0, 0))]
                 + [page_spec(q, j) for q in range(sb) for j in range(n_pages)],
        out_specs=pl.BlockSpec((sb, 1, kw), lambda i, pt: (i, 0, 0)),
    )
    return pl.pallas_call(
        functools.partial(_dsa_step_score_kernel, n_pages, page),
        grid_spec=grid_spec,
        out_shape=jax.ShapeDtypeStruct((s, 1, kw), F32),
        compiler_params=_cparams(("arbitrary",)),
        name="dsa_step_scores",
    )(pt_flat, iq, iw, ik_new, *([cik2d] * (sb * n_pages)))


def _dsa_step_select_kernel(n_sel, past, sc_ref, o_ref):
    sc = sc_ref[...]
    kpos = lax.broadcasted_iota(I32, sc.shape, 1)

    def store(mask):
        o_ref[...] = jnp.where(mask, 1.0, 0.0)

    _select_topk(sc, kpos <= past, n_sel, store)


def dsa_step_select(sc, n_sel, past):
    return pl.pallas_call(
        functools.partial(_dsa_step_select_kernel, n_sel, past),
        out_shape=jax.ShapeDtypeStruct(sc.shape, F32),
        compiler_params=pltpu.CompilerParams(vmem_limit_bytes=VMEM_LIMIT),
        name="dsa_step_select",
    )(sc)


def _dsa_step_attn_kernel(n_pages, page, pt_ref, q_ref, kn_ref, vn_ref, sel_ref, ex_ref, *refs):
    sb = q_ref.shape[0]
    kps = refs[:sb * n_pages]
    vps = refs[sb * n_pages:2 * sb * n_pages]
    o_ref = refs[2 * sb * n_pages]
    rep = A_HEADS // A_KV_HEADS
    w2 = page * A_KV_HEADS
    hrow = lax.broadcasted_iota(I32, (A_HEADS, w2), 0)
    col = lax.broadcasted_iota(I32, (A_HEADS, w2), 1)
    own = (col % A_KV_HEADS) == (hrow // rep)
    h8 = lax.broadcasted_iota(I32, (A_HEADS, A_HEAD), 0)
    ex = ex_ref[...]
    for u in range(sb):
        kp = kps[u * n_pages:(u + 1) * n_pages]
        vp = vps[u * n_pages:(u + 1) * n_pages]
        q = q_ref[u]
        logits = []
        for j in range(n_pages):
            s = _dot_nt(q, kp[j][...].astype(BF16))
            selj = _dot(sel_ref[u, :, j * page:(j + 1) * page].astype(BF16), ex)
            logits.append(jnp.where(jnp.logical_and(selj > 0.5, own), s, NEG_BIG))
        kn = jnp.where(h8 < rep, kn_ref[u, 0:1, :], kn_ref[u, 1:2, :])
        vn = jnp.where(h8 < rep, vn_ref[u, 0:1, :], vn_ref[u, 1:2, :])
        sn = jnp.sum(q.astype(F32) * kn, axis=1, keepdims=True)
        seln = sel_ref[u, :, n_pages * page:n_pages * page + 1]
        sn = jnp.where(seln > 0.5, sn, NEG_BIG)
        m = sn
        for s in logits:
            m = jnp.maximum(m, jnp.max(s, axis=1, keepdims=True))
        pn = jnp.exp(sn - m)
        l = pn
        acc = pn * vn
        for j in range(n_pages):
            p = jnp.exp(logits[j] - m)
            l = l + jnp.sum(p, axis=1, keepdims=True)
            acc = acc + _dot(p.astype(BF16), vp[j][...].astype(BF16))
        o_ref[u] = acc / l


def dsa_step_attn(pt_flat, q, k_new, v_new, sel, expand, ck2d, cv2d, n_pages, page):
    s = q.shape[0]
    sb = _seqs_per_step(s, ATTN_SEQS)
    kw = sel.shape[-1]
    w2 = page * A_KV_HEADS
    page_spec = lambda u, j: pl.BlockSpec(
        (w2, A_HEAD), lambda i, pt: (pt[(i * sb + u) * n_pages + j], 0))
    pages = [page_spec(u, j) for u in range(sb) for j in range(n_pages)]
    grid_spec = pltpu.PrefetchScalarGridSpec(
        num_scalar_prefetch=1,
        grid=(s // sb,),
        in_specs=[pl.BlockSpec((sb, A_HEADS, A_HEAD), lambda i, pt: (i, 0, 0)),
                  pl.BlockSpec((sb, A_KV_HEADS, A_HEAD), lambda i, pt: (i, 0, 0)),
                  pl.BlockSpec((sb, A_KV_HEADS, A_HEAD), lambda i, pt: (i, 0, 0)),
                  pl.BlockSpec((sb, 1, kw), lambda i, pt: (i, 0, 0)),
                  pl.BlockSpec((page, w2), lambda i, pt: (0, 0))] + pages * 2,
        out_specs=pl.BlockSpec((sb, A_HEADS, A_HEAD), lambda i, pt: (i, 0, 0)),
    )
    return pl.pallas_call(
        functools.partial(_dsa_step_attn_kernel, n_pages, page),
        grid_spec=grid_spec,
        out_shape=jax.ShapeDtypeStruct((s, A_HEADS, A_HEAD), F32),
        compiler_params=_cparams(("arbitrary",)),
        name="dsa_step_attn",
    )(pt_flat, q, k_new, v_new, sel, expand, *([ck2d] * (sb * n_pages)), *([cv2d] * (sb * n_pages)))


def _pack_bf16_pairs(x):
    w = x.shape[1] // 2
    hi = lax.bitcast_convert_type(x[:, :w].astype(BF16).astype(F32), I32)
    lo = lax.bitcast_convert_type(x[:, w:].astype(BF16).astype(F32), I32)
    return hi | lax.shift_right_logical(lo, 16)


def _unpack_bf16_pairs(p):
    hi = lax.bitcast_convert_type(p & jnp.int32(-65536), F32)
    lo = lax.bitcast_convert_type(lax.shift_left(p, 16), F32)
    return hi, lo


def _mix_kernel(alpha, prompt_blocks, x_ref, yp_ref, bonp_ref, gp_ref, ap_ref, ys_ref, bons_ref, gs_ref,
                as_ref, e_ref, et_ref, gng_ref, gnb_ref, l0g_ref, l0b_ref, wo_ref, l1g_ref, l1b_ref,
                wr_ref, h_ref, sc_ref, pk_ref):
    is_prompt = pl.program_id(0) < prompt_blocks
    pick = lambda p_ref, s_ref: jnp.where(is_prompt, p_ref[...], s_ref[...])
    e, et = e_ref[...], et_ref[...]
    y = pick(yp_ref, ys_ref)
    inv = 1.0 / R_HEAD
    mu = _head_sums(y, e, et) * inv
    d = y - mu
    var = _head_sums(d * d, e, et) * inv
    yn = d * lax.rsqrt(var + GN_EPS) * gng_ref[...] + gnb_ref[...]
    r_out = (yn + pick(bonp_ref, bons_ref)) * pick(gp_ref, gs_ref)
    mix = (_dot(r_out.astype(BF16), wo_ref[:R_DIM, :])
           + _dot(pick(ap_ref, as_ref).astype(BF16), wo_ref[R_DIM:, :]))
    h0 = _layer_norm(x_ref[...], l0g_ref[...], l0b_ref[...])
    h1 = _layer_norm(alpha * h0 + mix, l1g_ref[...], l1b_ref[...])
    h_ref[...] = h1
    w_hi, w_lo = _split_bf16(wr_ref[...])
    h_hi, h_lo = _split_bf16(h1)
    logits = _dot(h_hi, w_hi) + _dot(h_lo, w_hi) + _dot(h_hi, w_lo)
    sc_ref[...] = _sigmoid(logits.T[:N_EXPERTS, :])
    pk_ref[...] = _pack_bf16_pairs(h1)


def mix_ln1_router(x, prompt_parts, step_parts, pw, alpha):
    n, d = x.shape
    tm = ROW_TILE
    pb = prompt_parts[0].shape[0] // tm
    row = lambda w: pl.BlockSpec((tm, w), lambda i: (i, 0))
    head = lambda w: pl.BlockSpec((tm, w), lambda i: (jnp.minimum(i, pb - 1), 0))
    tail = lambda w: pl.BlockSpec((tm, w), lambda i: (jnp.maximum(i - pb, 0), 0))
    full = lambda a: pl.BlockSpec(a.shape, lambda i: (0,) * a.ndim)
    params = (pw["e"], pw["et"], pw["gn_g"], pw["gn_b"], pw["ln0_g"], pw["ln0_b"], pw["w_out"],
              pw["ln1_g"], pw["ln1_b"], pw["w_router"])
    widths = (R_DIM, R_DIM, R_DIM, A_DIM)
    return pl.pallas_call(
        functools.partial(_mix_kernel, alpha, pb),
        grid=(n // tm,),
        in_specs=[row(d)] + [head(w) for w in widths] + [tail(w) for w in widths]
                 + [full(a) for a in params],
        out_specs=[row(d), pl.BlockSpec((N_EXPERTS, tm), lambda i: (0, i)), row(d // 2)],
        out_shape=[jax.ShapeDtypeStruct((n, d), F32),
                   jax.ShapeDtypeStruct((N_EXPERTS, n), F32),
                   jax.ShapeDtypeStruct((n, d // 2), I32)],
        compiler_params=_cparams(("parallel",)),
        name="mix_ln1_router",
    )(x, *prompt_parts, *step_parts, *params)


def _route_kernel(sc_ref, bias_ref, idx_ref, gate_ref, pos_ref, cnt_ref, cnt_scr):
    scores = sc_ref[...]
    biased = scores + bias_ref[...]
    tn = scores.shape[1]
    per = N_EXPERTS // N_EXPERT_GROUPS
    sub = lax.broadcasted_iota(I32, (per, tn), 0)
    grp_rows = []
    for g in range(N_EXPERT_GROUPS):
        xg = biased[g * per:(g + 1) * per, :]
        m1 = jnp.max(xg, axis=0, keepdims=True)
        first = jnp.min(jnp.where(xg == m1, sub, per), axis=0, keepdims=True)
        m2 = jnp.max(jnp.where(sub == first, -jnp.inf, xg), axis=0, keepdims=True)
        grp_rows.append(m1 + m2)
    grp = jnp.concatenate(grp_rows, axis=0)
    gi = lax.broadcasted_iota(I32, (N_EXPERT_GROUPS, tn), 0)
    gsel = jnp.zeros((N_EXPERT_GROUPS, tn), jnp.bool_)
    for _ in range(TOPK_GROUPS):
        m = jnp.max(grp, axis=0, keepdims=True)
        first = jnp.min(jnp.where(grp == m, gi, N_EXPERT_GROUPS), axis=0, keepdims=True)
        hit = gi == first
        gsel = jnp.logical_or(gsel, hit)
        grp = jnp.where(hit, -jnp.inf, grp)
    ei = lax.broadcasted_iota(I32, (N_EXPERTS, tn), 0)
    emask = jnp.concatenate(
        [jnp.broadcast_to(gsel[g:g + 1, :], (per, tn)) for g in range(N_EXPERT_GROUPS)], axis=0)
    cand = jnp.where(emask, biased, -jnp.inf)
    idxs, gates, hits = [], [], []
    for _ in range(TOP_K):
        m = jnp.max(cand, axis=0, keepdims=True)
        first = jnp.min(jnp.where(cand == m, ei, N_EXPERTS), axis=0, keepdims=True)
        hit = ei == first
        idxs.append(first)
        hits.append(hit)
        gates.append(jnp.sum(jnp.where(hit, scores, 0.0), axis=0, keepdims=True))
        cand = jnp.where(hit, -jnp.inf, cand)
    gate = jnp.concatenate(gates, axis=0)
    gate = gate / jnp.sum(gate, axis=0, keepdims=True) * ROUTED_SCALE
    idx_ref[...] = jnp.concatenate(idxs, axis=0)
    gate_ref[...] = gate
    chosen = hits[0]
    for hit in hits[1:]:
        chosen = jnp.logical_or(chosen, hit)
    onehot = jnp.where(chosen, 1.0, 0.0)
    ta = lax.broadcasted_iota(I32, (tn, tn), 0)
    tb = lax.broadcasted_iota(I32, (tn, tn), 1)
    prefix = _dot(onehot.astype(BF16), (ta < tb).astype(BF16))

    @pl.when(pl.program_id(0) == 0)
    def _():
        cnt_scr[...] = jnp.zeros_like(cnt_scr)

    rank = prefix + cnt_scr[:, 0:1]
    pos_ref[...] = jnp.concatenate(
        [jnp.sum(jnp.where(hit, rank, 0.0), axis=0, keepdims=True) for hit in hits], axis=0).astype(I32)
    cnt_scr[...] = cnt_scr[...] + jnp.sum(onehot, axis=1, keepdims=True)
    cnt_ref[...] = cnt_scr[...].astype(I32)


def route(scores_t, e_bias):
    n = scores_t.shape[1]
    tn = ROW_TILE
    tok = pl.BlockSpec((TOP_K, tn), lambda i: (0, i))
    return pl.pallas_call(
        _route_kernel,
        grid=(n // tn,),
        in_specs=[pl.BlockSpec((N_EXPERTS, tn), lambda i: (0, i)),
                  pl.BlockSpec((N_EXPERTS, 1), lambda i: (0, 0))],
        out_specs=[tok, tok, tok, pl.BlockSpec((N_EXPERTS, LANES), lambda i: (0, 0))],
        out_shape=[jax.ShapeDtypeStruct((TOP_K, n), I32), jax.ShapeDtypeStruct((TOP_K, n), F32),
                   jax.ShapeDtypeStruct((TOP_K, n), I32),
                   jax.ShapeDtypeStruct((N_EXPERTS, LANES), I32)],
        scratch_shapes=[pltpu.VMEM((N_EXPERTS, LANES), F32)],
        compiler_params=_cparams(("arbitrary",)),
        name="route",
    )(scores_t, e_bias)


def _dispatch_kernel(nb, dest_ref, segend_ref, cnt_ref, nu_ref, x_ref, inv0_ref, o_ref, invo_ref,
                     inv_ref, zbuf, sem, zsem, isem):
    tm = x_ref.shape[0]
    row0 = pl.program_id(0) * tm

    @pl.when(pl.program_id(0) == 0)
    def _():
        c = pltpu.make_async_copy(inv0_ref, inv_ref, isem)
        c.start()
        c.wait()

    def fill(start):
        return pltpu.make_async_copy(zbuf, o_ref.at[pl.ds(pl.multiple_of(start, EXPERT_TILE),
                                                          EXPERT_TILE)], zsem)

    @pl.when(pl.program_id(0) == 0)
    def _():
        zbuf[...] = jnp.zeros_like(zbuf)

        def each_expert(fn):
            def body(e, carry):
                @pl.when(cnt_ref[e] > 0)
                def _():
                    fn(fill(segend_ref[e] - EXPERT_TILE))
                return carry
            lax.fori_loop(0, N_EXPERTS, body, 0)

        def each_free_block(fn):
            def body(b, carry):
                fn(fill(b * EXPERT_TILE))
                return carry
            lax.fori_loop(nu_ref[0], nb, body, 0)

        each_expert(lambda c: c.start())
        each_free_block(lambda c: c.start())
        each_expert(lambda c: c.wait())
        each_free_block(lambda c: c.wait())

    def start(i, carry):
        for j in range(TOP_K):
            d = dest_ref[i * TOP_K + j]
            inv_ref[d] = (row0 + i) * TOP_K + j
            pltpu.make_async_copy(x_ref.at[pl.ds(i, 1)], o_ref.at[pl.ds(d, 1)], sem).start()
        return carry

    lax.fori_loop(0, tm, start, 0)
    for j in range(TOP_K):
        pltpu.make_async_copy(x_ref, o_ref.at[pl.ds(0, tm)], sem).wait()

    @pl.when(pl.program_id(0) == pl.num_programs(0) - 1)
    def _():
        c = pltpu.make_async_copy(inv_ref, invo_ref, isem)
        c.start()
        c.wait()


def moe_dispatch(dest_flat, seg_end, counts, n_used, xpk, inv_default, nb):
    n, w = xpk.shape
    tm = ROW_TILE
    rows = nb * EXPERT_TILE
    smem = lambda: pl.BlockSpec(memory_space=pltpu.SMEM)
    hbm = lambda: pl.BlockSpec(memory_space=pl.ANY)
    return pl.pallas_call(
        functools.partial(_dispatch_kernel, nb),
        grid=(n // tm,),
        in_specs=[pl.BlockSpec((tm * TOP_K,), lambda i: (i,), memory_space=pltpu.SMEM),
                  smem(), smem(), smem(),
                  pl.BlockSpec((tm, w), lambda i: (i, 0)), hbm()],
        out_specs=[hbm(), hbm()],
        out_shape=[jax.ShapeDtypeStruct((rows, w), I32), jax.ShapeDtypeStruct((rows,), I32)],
        scratch_shapes=[pltpu.SMEM((rows,), I32), pltpu.VMEM((EXPERT_TILE, w), I32),
                        pltpu.SemaphoreType.DMA(()), pltpu.SemaphoreType.DMA(()),
                        pltpu.SemaphoreType.DMA(())],
        compiler_params=_cparams(("arbitrary",)),
        name="moe_dispatch",
    )(dest_flat, seg_end, counts, n_used, xpk, inv_default)


def _experts_kernel(n_slots, nb, be_ref, nxt_ref, slot_ref, nu_ref, inv_ref, x_ref, wg_ref, wu_ref,
                    wd_ref, o_ref, wg_f, wu_f, wd_f, wg_s, wu_s, wd_s, obuf_a, obuf_b, sems, osems):
    i = pl.program_id(0)
    n_used = nu_ref[0]
    used = i < n_used
    prev = be_ref[jnp.maximum(i - 1, 0)]
    fresh = jnp.logical_and(used, jnp.logical_or(i == 0, be_ref[i] != prev))
    bm = x_ref.shape[0]

    obufs = (obuf_a, obuf_b)

    def scatter_rows(blk, par):
        for r in range(bm):
            pltpu.make_async_copy(obufs[par].at[pl.ds(r, 1)],
                                  o_ref.at[pl.ds(inv_ref[blk * bm + r], 1)], osems.at[par]).start()

    def wait_rows(par):
        pltpu.make_async_copy(obufs[par], o_ref.at[pl.ds(0, bm)], osems.at[par]).wait()

    def by_parity(cond, fn):
        for par in range(2):
            @pl.when(jnp.logical_and(cond, i % 2 == par))
            def _(par=par):
                fn(par)

    @pl.when(i == 0)
    def _():
        obuf_b[...] = jnp.zeros_like(obuf_b)
        spare = [pltpu.make_async_copy(obuf_b, o_ref.at[pl.ds(n_slots + e * bm, bm)], osems.at[1])
                 for e in range(N_EXPERTS)]
        for c in spare:
            c.start()
        for c in spare:
            c.wait()

    by_parity(jnp.logical_and(i >= 2, i - 2 < n_used), wait_rows)

    def weight_copies(e, slot):
        return (pltpu.make_async_copy(wg_ref.at[e], wg_f.at[slot], sems.at[slot, 0]),
                pltpu.make_async_copy(wu_ref.at[e], wu_f.at[slot], sems.at[slot, 1]),
                pltpu.make_async_copy(wd_ref.at[e], wd_f.at[slot], sems.at[slot, 2]))

    @pl.when(jnp.logical_and(used, i == 0))
    def _():
        for c in weight_copies(be_ref[0], 0):
            c.start()

    @pl.when(fresh)
    def _():
        slot = slot_ref[i]
        for c in weight_copies(be_ref[i], slot):
            c.wait()

        @pl.when(nxt_ref[i] >= 0)
        def _():
            for c in weight_copies(nxt_ref[i], 1 - slot):
                c.start()

        wg_s[...] = wg_f[slot].astype(BF16)
        wu_s[...] = wu_f[slot].astype(BF16)
        wd_s[...] = wd_f[slot].astype(BF16)

    def compute(par):
        hi, lo = _unpack_bf16_pairs(x_ref[...])
        hi, lo = hi.astype(BF16), lo.astype(BF16)
        half = hi.shape[1]
        gp = _dot(hi, wg_s[:half, :]) + _dot(lo, wg_s[half:, :])
        up = _dot(hi, wu_s[:half, :]) + _dot(lo, wu_s[half:, :])
        act = gp * _sigmoid(gp) * up
        obufs[par][...] = _pack_bf16_pairs(_dot(act.astype(BF16), wd_s[...]))

    @pl.when(jnp.logical_and(used, i == 0))
    def _():
        compute(0)

    def send_prev_and_compute(par):
        scatter_rows(i - 1, 1 - par)
        compute(par)

    by_parity(jnp.logical_and(used, i > 0), send_prev_and_compute)
    by_parity(jnp.logical_and(i == n_used, i > 0), lambda par: scatter_rows(i - 1, 1 - par))

    @pl.when(jnp.logical_and(i == nb - 1, n_used == nb - 1))
    def _():
        wait_rows((nb - 2) % 2)


def moe_experts(blk_e, nxt_e, slot, n_used, inv, xs, n_slots, w_gate, w_up, w_down):
    w = xs.shape[1]
    nb = xs.shape[0] // EXPERT_TILE
    _, d, de = w_gate.shape
    grid_spec = pltpu.PrefetchScalarGridSpec(
        num_scalar_prefetch=5,
        grid=(nb,),
        in_specs=[pl.BlockSpec((EXPERT_TILE, w), lambda i, be, nx, sl, nu, iv: (jnp.minimum(i, nu[0] - 1), 0)),
                  pl.BlockSpec(memory_space=pl.ANY), pl.BlockSpec(memory_space=pl.ANY),
                  pl.BlockSpec(memory_space=pl.ANY)],
        out_specs=pl.BlockSpec(memory_space=pl.ANY),
        scratch_shapes=[pltpu.VMEM((2, d, de), F32), pltpu.VMEM((2, d, de), F32),
                        pltpu.VMEM((2, de, d), F32),
                        pltpu.VMEM((d, de), BF16), pltpu.VMEM((d, de), BF16),
                        pltpu.VMEM((de, d), BF16), pltpu.VMEM((EXPERT_TILE, w), I32),
                        pltpu.VMEM((EXPERT_TILE, w), I32),
                        pltpu.SemaphoreType.DMA((2, 3)), pltpu.SemaphoreType.DMA((2,))],
    )
    return pl.pallas_call(
        functools.partial(_experts_kernel, n_slots, nb),
        grid_spec=grid_spec,
        out_shape=jax.ShapeDtypeStruct((n_slots + N_EXPERTS * EXPERT_TILE, w), I32),
        compiler_params=_cparams(("arbitrary",)),
        name="moe_experts",
    )(blk_e, nxt_e, slot, n_used, inv, xs, w_gate, w_up, w_down)


def _combine_kernel(alpha, prompt_blocks, h_ref, gate_ref, ys_ref, wsg_ref, wsu_ref, wsd_ref, l2g_ref,
                    l2b_ref, op_ref, os_ref):
    tm = h_ref.shape[0]
    h = h_ref[...]
    hb = h.astype(BF16)
    gp = _dot(hb, wsg_ref[...])
    up = _dot(hb, wsu_ref[...])
    shared = _dot((gp * _sigmoid(gp) * up).astype(BF16), wsd_ref[...])
    row = lax.broadcasted_iota(I32, (tm, tm * TOP_K), 0)
    col = lax.broadcasted_iota(I32, (tm, tm * TOP_K), 1)
    g_hi, g_lo = _split_bf16(jnp.where(col // TOP_K == row, gate_ref[0], 0.0))
    hi, lo = _unpack_bf16_pairs(ys_ref[...])
    hi, lo = hi.astype(BF16), lo.astype(BF16)
    routed = jnp.concatenate([_dot(g_hi, hi) + _dot(g_lo, hi), _dot(g_hi, lo) + _dot(g_lo, lo)], axis=1)
    out = _layer_norm(alpha * h + routed + shared, l2g_ref[...], l2b_ref[...])
    is_prompt = pl.program_id(0) < prompt_blocks

    @pl.when(is_prompt)
    def _():
        op_ref[...] = out

    @pl.when(jnp.logical_not(is_prompt))
    def _():
        os_ref[...] = out


def moe_combine(h1, gate_rows, ys, pw, alpha, n_prompt):
    n, d = h1.shape
    tm = gate_rows.shape[2] // TOP_K
    pb = n_prompt // tm
    full = lambda a: pl.BlockSpec(a.shape, lambda i: (0,) * a.ndim)
    params = (pw["ws_gate"], pw["ws_up"], pw["ws_down"], pw["ln2_g"], pw["ln2_b"])
    return pl.pallas_call(
        functools.partial(_combine_kernel, alpha, pb),
        grid=(n // tm,),
        in_specs=[pl.BlockSpec((tm, d), lambda i: (i, 0)),
                  pl.BlockSpec((1, 1, tm * TOP_K), lambda i: (i, 0, 0)),
                  pl.BlockSpec((tm * TOP_K, d // 2), lambda i: (i, 0))]
                 + [full(a) for a in params],
        out_specs=[pl.BlockSpec((tm, d), lambda i: (jnp.minimum(i, pb - 1), 0)),
                   pl.BlockSpec((tm, d), lambda i: (jnp.maximum(i - pb, 0), 0))],
        out_shape=[jax.ShapeDtypeStruct((n_prompt, d), F32),
                   jax.ShapeDtypeStruct((n - n_prompt, d), F32)],
        compiler_params=_cparams(("arbitrary",)),
        name="moe_combine",
    )(h1, gate_rows, ys, *params)


def _round_up(x, m):
    return (x + m - 1) // m * m


def _rope_tables(pos, head):
    half = head // 2
    inv = ROPE_THETA ** (-jnp.arange(half, dtype=F32) / half)
    ang = pos.astype(F32)[:, None] * inv[None, :]
    cos, sin = jnp.cos(ang), jnp.sin(ang)
    rep = LANES // head
    c = jnp.tile(jnp.concatenate([cos, cos], axis=1), (1, rep))
    s = jnp.tile(jnp.concatenate([-sin, sin], axis=1), (1, rep))
    return c, s


def _permute_cols(m, axis=-1):
    axis = axis % m.ndim
    cut = lambda lo, hi: lax.slice_in_dim(m, lo, hi, axis=axis)
    a0 = SHIFT_DIM
    i0 = a0 + A_DIM + 2 * A_KV_DIM + IDX_HEADS * IDX_DIM

    def pad(w):
        shape = list(m.shape)
        shape[axis] = w
        return jnp.zeros(shape, m.dtype)

    pieces = [
        cut(0, 3 * R_DIM),
        cut(a0, a0 + A_DIM),
        cut(a0 + A_DIM + 2 * A_KV_DIM, i0),
        cut(a0 + A_DIM, a0 + A_DIM + 2 * A_KV_DIM),
        cut(i0, i0 + IDX_DIM + IDX_HEADS),
        pad(LANES - IDX_DIM - IDX_HEADS),
        cut(3 * R_DIM, SHIFT_DIM),
        pad(LORA_W - (SHIFT_DIM - 3 * R_DIM)),
    ]
    return jnp.concatenate(pieces, axis=axis)


def kernel(x_prompt, x_sample, cache_k, cache_v, cache_idx_k, state_wkv, state_shift, page_table,
           meta, ln0_g, ln0_b, w_in, mu_shift, w0, w_b, a0, a_b, g_b, k_k, k_a, r_k, gn_g, gn_b,
           w_out, ln1_g, ln1_b, w_router, e_bias, w_gate, w_up, w_down, ws_gate, ws_up, ws_down,
           ln2_g, ln2_b):
    depth = w_in.shape[0]
    assert depth == 1, "single trunk layer"
    bsz, s_p, d = x_prompt.shape
    s_dec, s_s, _ = x_sample.shape
    assert s_s == 1, "one decode token per sequence"
    t_real = N_META + s_p
    tp = _round_up(t_real, LANES)
    assert (bsz * tp) % ROW_TILE == 0
    sp = _round_up(s_dec, ROW_TILE)
    n_prompt = bsz * tp
    n = n_prompt + sp
    n_pool, page = cache_k.shape[1], cache_k.shape[2]
    n_pages = page_table.shape[1]
    past = n_pages * page
    alpha = float((2 * depth) ** 0.25)
    row2 = lambda a: a.reshape(1, -1)

    meta_rows = jnp.broadcast_to(meta[None], (bsz, N_META, d))
    xp = jnp.concatenate([meta_rows, x_prompt, jnp.zeros((bsz, tp - t_real, d), F32)], axis=1)
    x_all = jnp.concatenate([xp.reshape(n_prompt, d), x_sample.reshape(s_dec, d),
                             jnp.zeros((sp - s_dec, d), F32)], axis=0)
    pos = jnp.concatenate([jnp.tile(jnp.arange(tp), bsz), jnp.full((sp,), past)])
    c128, s128 = _rope_tables(pos, A_HEAD)
    c64, s64 = _rope_tables(pos, IDX_DIM)

    w_in_t = _permute_cols(w_in[0].T.astype(BF16), axis=0)
    mu_k = _permute_cols(
        jnp.concatenate([mu_shift[0], jnp.zeros((w_in.shape[2] - SHIFT_DIM,), F32)])[None, :])
    head_of = jnp.arange(R_DIM) // R_HEAD
    e_mat = (head_of[:, None] == jnp.arange(R_HEADS)[None, :]).astype(F32)
    zpad = lambda a, rows_before, rows_total: jnp.concatenate(
        [jnp.zeros((rows_before, a.shape[1]), a.dtype), a,
         jnp.zeros((rows_total - rows_before - a.shape[0], a.shape[1]), a.dtype)], axis=0)
    pw = {
        "mu_x": mu_k[:, :3 * R_DIM], "mu_lo": mu_k[:, C_LORA:],
        "w0": row2(w0[0]), "a0": row2(a0[0]), "k_k": row2(k_k[0]), "k_a": row2(k_a[0]),
        "r_k": row2(r_k[0]), "gn_g": row2(gn_g[0]), "gn_b": row2(gn_b[0]),
        "w_b": zpad(w_b[0], 0, LANES).astype(BF16),
        "a_b": zpad(a_b[0], D_DECAY_LORA, LANES).astype(BF16),
        "g_b": zpad(g_b[0], 0, LORA_W - LANES).astype(BF16),
        "e": e_mat.astype(BF16), "et": e_mat.T.astype(BF16),
        "ln0_g": row2(ln0_g), "ln0_b": row2(ln0_b),
        "ln1_g": row2(ln1_g[0]), "ln1_b": row2(ln1_b[0]),
        "ln2_g": row2(ln2_g[0]), "ln2_b": row2(ln2_b[0]),
        "w_out": w_out[0].astype(BF16),
        "w_router": jnp.pad(w_router[0], ((0, 0), (0, LANES - N_EXPERTS))),
        "ws_gate": ws_gate[0].astype(BF16), "ws_up": ws_up[0].astype(BF16),
        "ws_down": ws_down[0].astype(BF16),
    }

    p = ln_proj(x_all, pw["ln0_g"], pw["ln0_b"], w_in_t)
    q_r, iq_r, k_r, ik_r, iw, kt_r, ikt_r = rope_all(p, c128, s128, c64, s64)

    pre_p = rwkv_pre(p, 0, n_prompt, None, pw, t_real, tp)
    shift_k = _permute_cols(jnp.concatenate(
        [state_shift[0], jnp.zeros((s_dec, w_in.shape[2] - SHIFT_DIM), F32)], axis=1))
    shift_k = jnp.concatenate([shift_k, jnp.zeros((sp - s_dec, P_COLS), F32)], axis=0)
    pre_s = rwkv_pre(p, n_prompt, sp, (shift_k[:, :3 * R_DIM], shift_k[:, C_LORA:]), pw, t_real, tp)
    r_p, ld_p, k_p, v_p, kk_p, b_p, g_p, bon_p = pre_p
    r_s, ld_s, k_s, v_s, kk_s, b_s, g_s, bon_s = pre_s
    y_p, wkv_p = rwkv_scan(r_p, ld_p, k_p, v_p, kk_p, b_p, bsz, tp)
    heads = lambda a: a[:s_dec].reshape(s_dec, R_HEADS, R_HEAD).transpose(1, 2, 0)
    y_hs, wkv_hs = rwkv_step(heads(r_s), heads(ld_s), heads(k_s), heads(v_s), heads(kk_s), heads(b_s),
                             state_wkv[0].transpose(1, 2, 3, 0))
    wkv_s = wkv_hs.transpose(3, 0, 1, 2)
    y_s = jnp.concatenate([y_hs.transpose(2, 0, 1).reshape(s_dec, R_DIM),
                           jnp.zeros((sp - s_dec, R_DIM), F32)], axis=0)

    n_sel_p = min(TOPK_KEYS, t_real // 4)
    a_p = dsa_prompt(q_r, iq_r, iw, ikt_r, kt_r, p, bsz, tp, n_sel_p)
    n_sel_s = min(TOPK_KEYS, (past + 1) // 4)
    pt_flat = page_table.reshape(-1).astype(I32)
    srow = slice(n_prompt, n_prompt + s_dec)
    sc_s = dsa_step_scores(pt_flat, iq_r[srow].reshape(s_dec, IDX_HEADS, IDX_DIM),
                           iw[srow].reshape(s_dec, IDX_HEADS, 1), ik_r[srow].reshape(s_dec, 1, IDX_DIM),
                           cache_idx_k[0].transpose(0, 2, 1).reshape(n_pool * IDX_DIM, page),
                           n_pages, page)
    sel_s = dsa_step_select(sc_s.reshape(s_dec, -1), n_sel_s, past).reshape(sc_s.shape)
    slot = jnp.arange(page)[:, None]
    expand = (jnp.arange(page * A_KV_HEADS)[None, :] // A_KV_HEADS == slot).astype(BF16)
    a_s = dsa_step_attn(pt_flat, q_r[srow].reshape(s_dec, A_HEADS, A_HEAD),
                        k_r[srow].reshape(s_dec, A_KV_HEADS, A_HEAD),
                        p[srow, C_VA:C_VA + A_KV_DIM].reshape(s_dec, A_KV_HEADS, A_HEAD),
                        sel_s, expand,
                        cache_k[0].reshape(n_pool * page * A_KV_HEADS, A_HEAD),
                        cache_v[0].reshape(n_pool * page * A_KV_HEADS, A_HEAD), n_pages, page)
    a_s = jnp.concatenate([a_s.reshape(s_dec, A_DIM), jnp.zeros((sp - s_dec, A_DIM), F32)], axis=0)

    h1, scores_t, xpk = mix_ln1_router(x_all, (y_p, bon_p, g_p, a_p), (y_s, bon_s, g_s, a_s), pw, alpha)
    eidx_t, gate_t, pos_t, counts = route(scores_t, e_bias[0].reshape(N_EXPERTS, 1))

    n_slots = n * TOP_K
    nb = (n_slots + N_EXPERTS * (EXPERT_TILE - 1)) // EXPERT_TILE + 1
    counts = counts[:, 0]
    padded = (counts + EXPERT_TILE - 1) // EXPERT_TILE * EXPERT_TILE
    seg_end = jnp.cumsum(padded).astype(I32)
    seg_start = seg_end - padded
    experts = jnp.arange(N_EXPERTS)
    start_of = jnp.sum(jnp.where(eidx_t[:, :, None] == experts, seg_start, 0), axis=-1)
    dest = (start_of + pos_t).T.astype(I32).reshape(-1)
    blk_row = jnp.arange(nb) * EXPERT_TILE
    blk_e = jnp.minimum(jnp.sum(seg_end[None, :] <= blk_row[:, None], axis=1),
                        N_EXPERTS - 1).astype(I32)
    n_used = (seg_end[-1] // EXPERT_TILE).astype(I32)
    run_start = jnp.concatenate([jnp.ones((1,), I32), (blk_e[1:] != blk_e[:-1]).astype(I32)])
    slot = ((jnp.cumsum(run_start) - 1) % 2).astype(I32)
    run_end = seg_end[blk_e] // EXPERT_TILE
    nxt_e = jnp.where(run_end < n_used, blk_e[jnp.minimum(run_end, nb - 1)], -1).astype(I32)
    n_used = n_used.reshape(1)
    spare = (n_slots + blk_e[:, None] * EXPERT_TILE + jnp.arange(EXPERT_TILE)[None, :]).astype(I32)
    xs, inv = moe_dispatch(dest, seg_end, counts.astype(I32), n_used, xpk, spare.reshape(-1), nb)
    ys = moe_experts(blk_e, nxt_e, slot, n_used, inv, xs, n_slots, w_gate[0], w_up[0], w_down[0])
    tc = Q_TILE
    h2_p, h2_s = moe_combine(h1, gate_t.T.reshape(n // tc, 1, tc * TOP_K), ys, pw, alpha, n_prompt)

    def prompt_rows(a):
        return a[:n_prompt].reshape(bsz, tp, -1)[:, :t_real]

    y_prompt = h2_p.reshape(bsz, tp, d)[:, N_META:t_real]
    y_sample = h2_s[:s_dec].reshape(s_dec, 1, d)
    k_prompt = prompt_rows(k_r).reshape(1, bsz, t_real, A_KV_HEADS, A_HEAD)
    v_prompt = prompt_rows(p[:, C_VA:C_VA + A_KV_DIM]).reshape(1, bsz, t_real, A_KV_HEADS, A_HEAD)
    ik_prompt = prompt_rows(ik_r)[None]
    last = jnp.arange(bsz) * tp + t_real - 1
    unperm = lambda rows: jnp.concatenate([rows[:, :3 * R_DIM],
                                           rows[:, C_LORA:C_LORA + SHIFT_DIM - 3 * R_DIM]], axis=1)
    shift_prompt = unperm(p[last])[None]
    k_sample = k_r[srow].reshape(1, s_dec, 1, A_KV_HEADS, A_HEAD)
    v_sample = p[srow, C_VA:C_VA + A_KV_DIM].reshape(1, s_dec, 1, A_KV_HEADS, A_HEAD)
    ik_sample = ik_r[srow].reshape(1, s_dec, 1, IDX_DIM)
    shift_sample = unperm(p[srow])[None]
    return (y_prompt, y_sample, k_prompt, v_prompt, ik_prompt, wkv_p[None], shift_prompt,
            k_sample, v_sample, ik_sample, wkv_s[None], shift_sample)
```

```python
import functools

import numpy as np
import jax
import jax.numpy as jnp
from jax import lax
from jax.experimental import pallas as pl
from jax.experimental.pallas import tpu as pltpu

F32 = jnp.float32
BF16 = jnp.bfloat16
I32 = jnp.int32
HIGHEST = lax.Precision.HIGHEST

N_META = 16
R_HEADS, R_HEAD = 16, 64
R_DIM = R_HEADS * R_HEAD
D_DECAY_LORA, D_AAA_LORA, D_GATE_LORA = 64, 64, 160
SHIFT_DIM = 3 * R_DIM + D_DECAY_LORA + D_AAA_LORA + D_GATE_LORA
GN_EPS = 64e-5
A_HEADS, A_KV_HEADS, A_HEAD = 8, 2, 128
A_DIM = A_HEADS * A_HEAD
A_KV_DIM = A_KV_HEADS * A_HEAD
IDX_HEADS, IDX_DIM = 16, 64
TOPK_KEYS = 256
ROPE_THETA = 10000.0
N_EXPERTS, N_EXPERT_GROUPS, TOPK_GROUPS, TOP_K = 64, 8, 4, 8
ROUTED_SCALE = 2.5
LN_EPS = 1e-5

LANES = 128
SUBLANES = 8
ROW_TILE = 256
Q_TILE = 128
CHUNK = 64
EXPERT_TILE = 256
VMEM_LIMIT = 56 * 1024 * 1024
NEG_BIG = -1e30
INT_MIN = -2 ** 31

C_R, C_K, C_V = 0, R_DIM, 2 * R_DIM
C_Q = 3 * R_DIM
C_IQ = C_Q + A_DIM
C_KA = C_IQ + IDX_HEADS * IDX_DIM
C_VA = C_KA + A_KV_DIM
C_IK = C_VA + A_KV_DIM
C_LORA = C_IK + LANES
LORA_W = 384
P_COLS = C_LORA + LORA_W


def _cparams(sem):
    return pltpu.CompilerParams(dimension_semantics=sem, vmem_limit_bytes=VMEM_LIMIT)


def _dot(a, b, precision=None):
    return jnp.dot(a, b, preferred_element_type=F32, precision=precision)


def _dot_nt(a, b, precision=None):
    return lax.dot_general(a, b, (((1,), (1,)), ((), ())), preferred_element_type=F32,
                           precision=precision)


def _dot_tn(a, b, precision=None):
    return lax.dot_general(a, b, (((0,), (0,)), ((), ())), preferred_element_type=F32,
                           precision=precision)


def _split_bf16(x):
    hi = x.astype(BF16)
    return hi, (x - hi.astype(F32)).astype(BF16)


def _dot_f32_by_bf16(a, b):
    hi, lo = _split_bf16(a)
    return _dot(hi, b) + _dot(lo, b)


def _head_sums(x, e, et):
    return _dot_f32_by_bf16(_dot_f32_by_bf16(x, e), et)


def _layer_norm(x, g, b):
    mu = jnp.mean(x, axis=-1, keepdims=True)
    xc = x - mu
    var = jnp.mean(xc * xc, axis=-1, keepdims=True)
    return xc * lax.rsqrt(var + LN_EPS) * g + b


def _sigmoid(z):
    return 1.0 / (1.0 + jnp.exp(-z))


def _ln_proj_kernel(x0_ref, xn_ref, g_ref, b_ref, w_ref, o_ref, h_even, h_odd):
    i = pl.program_id(1)
    norm = lambda ref: _layer_norm(ref[...], g_ref[...], b_ref[...]).astype(BF16)

    @pl.when(i == 0)
    def _():
        h_even[...] = norm(x0_ref)

    for par, (cur, nxt) in enumerate(((h_even, h_odd), (h_odd, h_even))):
        @pl.when(i % 2 == par)
        def _(cur=cur, nxt=nxt):
            o_ref[...] = _dot_nt(cur[...], w_ref[...])
            nxt[...] = norm(xn_ref)


PROJ_ROWS = 256
PROJ_COLS = 2048


def ln_proj(x, g, b, w_t):
    n, d = x.shape
    cols = w_t.shape[0]
    tm = max(t for t in range(LANES, PROJ_ROWS + 1, LANES) if n % t == 0)
    tn = PROJ_COLS
    ni = n // tm
    return pl.pallas_call(
        _ln_proj_kernel,
        grid=(cols // tn, ni),
        in_specs=[
            pl.BlockSpec((tm, d), lambda j, i: (0, 0)),
            pl.BlockSpec((tm, d), lambda j, i: ((i + 1) % ni, 0)),
            pl.BlockSpec((1, d), lambda j, i: (0, 0)),
            pl.BlockSpec((1, d), lambda j, i: (0, 0)),
            pl.BlockSpec((tn, d), lambda j, i: (j, 0)),
        ],
        out_specs=pl.BlockSpec((tm, tn), lambda j, i: (i, j)),
        out_shape=jax.ShapeDtypeStruct((n, cols), F32),
        scratch_shapes=[pltpu.VMEM((tm, d), BF16), pltpu.VMEM((tm, d), BF16)],
        compiler_params=_cparams(("arbitrary", "arbitrary")),
        name="ln_proj",
    )(x, x, g, b, w_t)


def _rot_half(x, head):
    w = x.shape[-1]
    half = head // 2
    lane = lax.broadcasted_iota(I32, x.shape, 1)
    left = pltpu.roll(x, w - half, axis=1)
    right = pltpu.roll(x, half, axis=1)
    return jnp.where((lane % head) < half, left, right)


def _rope_kernel(q_ref, iq_ref, ka_ref, ikw_ref, c128_ref, s128_ref, c64_ref, s64_ref,
                 qo_ref, iqo_ref, ko_ref, iko_ref, iwo_ref, kt_ref, ikt_ref):
    c128, s128 = c128_ref[...], s128_ref[...]
    c64, s64 = c64_ref[...], s64_ref[...]

    def rope(x, head, c, s):
        rep = x.shape[-1] // LANES
        if rep > 1:
            c = jnp.concatenate([c] * rep, axis=1)
            s = jnp.concatenate([s] * rep, axis=1)
        return x * c + _rot_half(x, head) * s

    q = rope(q_ref[...], A_HEAD, c128, s128)
    qo_ref[...] = (q * (A_HEAD ** -0.5)).astype(BF16)
    iqo_ref[...] = rope(iq_ref[...], IDX_DIM, c64, s64).astype(BF16)
    k = rope(ka_ref[...], A_HEAD, c128, s128)
    ko_ref[...] = k
    ikw = ikw_ref[...]
    ik = rope(ikw, IDX_DIM, c64, s64)
    iko_ref[...] = ik[:, :IDX_DIM]
    iwo_ref[...] = ikw[:, IDX_DIM:IDX_DIM + IDX_HEADS]
    kt_ref[...] = k.T.astype(BF16)
    ikt_ref[...] = ik.T[:IDX_DIM, :].astype(BF16)


def rope_all(p, c128, s128, c64, s64):
    n = p.shape[0]
    tm = ROW_TILE
    row = lambda w, blk: pl.BlockSpec((tm, w), lambda i: (i, blk))
    return pl.pallas_call(
        _rope_kernel,
        grid=(n // tm,),
        in_specs=[row(A_DIM, C_Q // A_DIM), row(A_DIM, C_IQ // A_DIM),
                  row(A_KV_DIM, C_KA // A_KV_DIM), row(LANES, C_IK // LANES),
                  row(LANES, 0), row(LANES, 0), row(LANES, 0), row(LANES, 0)],
        out_specs=[row(A_DIM, 0), row(A_DIM, 0), row(A_KV_DIM, 0),
                   row(IDX_DIM, 0), row(IDX_HEADS, 0),
                   pl.BlockSpec((A_KV_DIM, tm), lambda i: (0, i)),
                   pl.BlockSpec((IDX_DIM, tm), lambda i: (0, i))],
        out_shape=[jax.ShapeDtypeStruct((n, A_DIM), BF16),
                   jax.ShapeDtypeStruct((n, IDX_HEADS * IDX_DIM), BF16),
                   jax.ShapeDtypeStruct((n, A_KV_DIM), F32),
                   jax.ShapeDtypeStruct((n, IDX_DIM), F32),
                   jax.ShapeDtypeStruct((n, IDX_HEADS), F32),
                   jax.ShapeDtypeStruct((A_KV_DIM, n), BF16),
                   jax.ShapeDtypeStruct((IDX_DIM, n), BF16)],
        compiler_params=_cparams(("parallel",)),
        name="rope",
    )(p, p, p, p, c128, s128, c64, s64)


def _rwkv_pre_kernel(t_real, tp, from_rows, *refs):
    (x_ref, lo_ref, px_ref, plo_ref, mu_ref, mulo_ref, w0_ref, wb_ref, a0_ref, ab_ref,
     gb_ref, kk_ref, ka_ref, rk_ref, e_ref, et_ref,
     r_o, ld_o, k_o, v_o, kk_o, b_o, g_o, bon_o) = refs
    x = x_ref[...]
    lo = lo_ref[...]
    tm = x.shape[0]
    if from_rows:
        i = pl.program_id(0)
        row = lax.broadcasted_iota(I32, (tm, 1), 0)
        t = (i * tm + row) % tp
        first = row == 0
        sx = jnp.where(first, px_ref[SUBLANES - 1:SUBLANES, :], pltpu.roll(x, 1, axis=0))
        slo = jnp.where(first, plo_ref[SUBLANES - 1:SUBLANES, :], pltpu.roll(lo, 1, axis=0))
        sx = jnp.where(t == 0, 0.0, sx)
        slo = jnp.where(t == 0, 0.0, slo)
        live = t < t_real
    else:
        sx = px_ref[...]
        slo = plo_ref[...]
        live = None
    xx = x + (sx - x) * mu_ref[...]
    xlo = lo + (slo - lo) * mulo_ref[...]
    r = xx[:, C_R:C_R + R_DIM]
    k = xx[:, C_K:C_K + R_DIM]
    v = xx[:, C_V:C_V + R_DIM]
    wa = xlo[:, :LANES]
    xg = xlo[:, LANES:]
    z = w0_ref[...] + _dot(jnp.tanh(wa).astype(BF16), wb_ref[...])
    nz = -z
    softplus = jnp.maximum(nz, 0.0) + jnp.log(1.0 + jnp.exp(-jnp.abs(nz)))
    logd = -jnp.exp(-softplus - 0.5)
    a = _sigmoid(a0_ref[...] + _dot(wa.astype(BF16), ab_ref[...]))
    g = _dot(_sigmoid(xg).astype(BF16), gb_ref[...])
    e, et = e_ref[...], et_ref[...]
    kkr = k * kk_ref[...]
    ss = _head_sums(kkr * kkr, e, et)
    kk = kkr / jnp.maximum(jnp.sqrt(ss), 1e-12)
    k2 = k * (1.0 + (a - 1.0) * ka_ref[...])
    bonus = _head_sums(r * k2 * rk_ref[...], e, et) * v
    b = kk * a
    if live is not None:
        zero = lambda y: jnp.where(live, y, 0.0)
        logd, k2s, vs, kk, b = zero(logd), zero(k2), zero(v), zero(kk), zero(b)
    else:
        k2s, vs = k2, v
    r_o[...] = r
    ld_o[...] = logd
    k_o[...] = k2s
    v_o[...] = vs
    kk_o[...] = kk
    b_o[...] = b
    g_o[...] = g
    bon_o[...] = bonus


def rwkv_pre(p, row0, nrows, prev, pw, t_real, tp):
    tm = min(ROW_TILE, nrows)
    blk0 = row0 // tm
    from_rows = prev is None
    xw = 3 * R_DIM
    cur_x = pl.BlockSpec((tm, xw), lambda i: (blk0 + i, 0))
    cur_lo = pl.BlockSpec((tm, LORA_W), lambda i: (blk0 + i, C_LORA // LORA_W))
    if from_rows:
        r8 = tm // SUBLANES
        prev_x = pl.BlockSpec((SUBLANES, xw), lambda i: (jnp.maximum((blk0 + i) * r8 - 1, 0), 0))
        prev_lo = pl.BlockSpec((SUBLANES, LORA_W),
                               lambda i: (jnp.maximum((blk0 + i) * r8 - 1, 0), C_LORA // LORA_W))
        prev_args = (p, p)
    else:
        prev_x = pl.BlockSpec((tm, xw), lambda i: (i, 0))
        prev_lo = pl.BlockSpec((tm, LORA_W), lambda i: (i, 0))
        prev_args = prev
    full = lambda a: pl.BlockSpec(a.shape, lambda i: (0,) * a.ndim)
    params = (pw["mu_x"], pw["mu_lo"], pw["w0"], pw["w_b"], pw["a0"], pw["a_b"], pw["g_b"],
              pw["k_k"], pw["k_a"], pw["r_k"], pw["e"], pw["et"])
    out = pl.BlockSpec((tm, R_DIM), lambda i: (i, 0))
    return pl.pallas_call(
        functools.partial(_rwkv_pre_kernel, t_real, tp, from_rows),
        grid=(nrows // tm,),
        in_specs=[cur_x, cur_lo, prev_x, prev_lo] + [full(a) for a in params],
        out_specs=[out] * 8,
        out_shape=[jax.ShapeDtypeStruct((nrows, R_DIM), F32)] * 8,
        compiler_params=_cparams(("parallel",)),
        name="rwkv_pre_rows" if from_rows else "rwkv_pre_step",
    )(p, p, *prev_args, *params)


def _rwkv_scan_kernel(r_ref, ld_ref, k_ref, v_ref, kk_ref, b_ref, y_ref, s_ref, ss_scr):
    c = pl.program_id(1)

    @pl.when(c == 0)
    def _():
        ss_scr[...] = jnp.zeros_like(ss_scr)

    n = CHUNK
    n2 = 2 * n
    pairs = R_HEADS // 2
    ld_all = ld_ref[...]
    ri = lax.broadcasted_iota(I32, (n, n), 0)
    ci = lax.broadcasted_iota(I32, (n, n), 1)
    cum_all = _dot((ci <= ri).astype(F32), ld_all, HIGHEST)
    head0 = lax.broadcasted_iota(I32, (n, LANES), 1) < R_HEAD
    r4 = lax.broadcasted_iota(I32, (2 * n2, 2 * n2), 0)
    c4 = lax.broadcasted_iota(I32, (2 * n2, 2 * n2), 1)
    tri = (c4 % n) < (r4 % n) + jnp.where(r4 < n2, 0, 1)
    re = lax.broadcasted_iota(I32, (n2, n2), 0)
    ce = lax.broadcasted_iota(I32, (n2, n2), 1)
    eye = (re == ce).astype(F32)

    def stack(x):
        return jnp.concatenate([jnp.where(head0, x, 0.0), jnp.where(head0, 0.0, x)], axis=0)

    ar, bk, v2, ss, e_last = [], [], [], [], []
    for p in range(pairs):
        sl = slice(p * LANES, (p + 1) * LANES)
        cum, ld = cum_all[:, sl], ld_all[:, sl]
        e_pos = jnp.exp(cum)
        e_neg = jnp.exp(-cum)
        at = -kk_ref[:, sl] * jnp.exp(cum - ld)
        ar.append(jnp.concatenate([stack(at), stack(r_ref[:, sl] * e_pos)], axis=0).astype(BF16))
        bk.append(jnp.concatenate([stack(b_ref[:, sl] * e_neg), stack(k_ref[:, sl] * e_neg)],
                                  axis=0).astype(BF16))
        v2.append(stack(v_ref[:, sl]).astype(BF16))
        ss.append(ss_scr[p])
        e_last.append(e_pos[n - 1:n, :])
    xy0 = [_dot_nt(ar[p], ss[p].astype(BF16)) for p in range(pairs)]
    sc = [jnp.where(tri, _dot_nt(ar[p], bk[p]), 0.0) for p in range(pairs)]
    lp = [s[:n2, :n2] for s in sc]
    t = [eye + l for l in lp]
    m = 1
    while 2 * m < n:
        lpb = [l.astype(BF16) for l in lp]
        lp = [_dot(l, l) for l in lpb]
        t = [t[p] + _dot(t[p].astype(BF16), lp[p].astype(BF16)) for p in range(pairs)]
        m *= 2
    w = [xy0[p][:n2] + _dot(sc[p][:n2, n2:].astype(BF16), v2[p]) for p in range(pairs)]
    u = [_dot(t[p].astype(BF16), w[p].astype(BF16)) for p in range(pairs)]
    uv = [jnp.concatenate([u[p].astype(BF16), v2[p]], axis=0) for p in range(pairs)]
    y = [xy0[p][n2:] + _dot(sc[p][n2:].astype(BF16), uv[p]) for p in range(pairs)]
    upd = [_dot_tn(uv[p], bk[p]) for p in range(pairs)]
    for p in range(pairs):
        y_ref[:, p * LANES:(p + 1) * LANES] = y[p][:n] + y[p][n:]
        ss_scr[p] = (ss[p] + upd[p]) * e_last[p]

    @pl.when(c == pl.num_programs(1) - 1)
    def _():
        for p in range(R_HEADS // 2):
            ss = ss_scr[p]
            s_ref[0, 2 * p] = ss[:R_HEAD, :R_HEAD]
            s_ref[0, 2 * p + 1] = ss[R_HEAD:, R_HEAD:]


def rwkv_scan(r, ld, k, v, kk, b, batch, tp):
    nchunk = tp // CHUNK
    blk = pl.BlockSpec((CHUNK, R_DIM), lambda bi, c: (bi * nchunk + c, 0))
    return pl.pallas_call(
        _rwkv_scan_kernel,
        grid=(batch, nchunk),
        in_specs=[blk] * 6,
        out_specs=[blk, pl.BlockSpec((1, R_HEADS, R_HEAD, R_HEAD), lambda bi, c: (bi, 0, 0, 0))],
        out_shape=[jax.ShapeDtypeStruct((batch * tp, R_DIM), F32),
                   jax.ShapeDtypeStruct((batch, R_HEADS, R_HEAD, R_HEAD), F32)],
        scratch_shapes=[pltpu.VMEM((R_HEADS // 2, LANES, LANES), F32)],
        compiler_params=_cparams(("parallel", "arbitrary")),
        name="rwkv_scan",
    )(r, ld, k, v, kk, b)


STEP_ROWS = 16


def _rwkv_step_kernel(r_ref, ld_ref, k_ref, v_ref, kk_ref, b_ref, s_ref, y_ref, so_ref):
    r, k = r_ref[0], k_ref[0]
    dec = jnp.exp(ld_ref[0])
    na = -kk_ref[0]
    b = b_ref[0]
    v = v_ref[0]
    ys = []
    for v0 in range(0, R_HEAD, STEP_ROWS):
        rows = range(v0, v0 + STEP_ROWS)
        s = [s_ref[0, vi] for vi in rows]
        sa = [jnp.sum(x * na, axis=0, keepdims=True) for x in s]
        s_new = [x * dec + a * b + v[vi:vi + 1, :] * k for x, a, vi in zip(s, sa, rows)]
        ys += [jnp.sum(x * r, axis=0, keepdims=True) for x in s_new]
        for x, vi in zip(s_new, rows):
            so_ref[0, vi] = x
    y_ref[0] = jnp.concatenate(ys, axis=0)


def rwkv_step(r, ld, k, v, kk, b, state):
    s = state.shape[-1]
    vec = pl.BlockSpec((1, R_HEAD, s), lambda h: (h, 0, 0))
    st = pl.BlockSpec((1, R_HEAD, R_HEAD, s), lambda h: (h, 0, 0, 0))
    return pl.pallas_call(
        _rwkv_step_kernel,
        grid=(R_HEADS,),
        in_specs=[vec] * 6 + [st],
        out_specs=[vec, st],
        out_shape=[jax.ShapeDtypeStruct((R_HEADS, R_HEAD, s), F32),
                   jax.ShapeDtypeStruct(state.shape, F32)],
        compiler_params=_cparams(("parallel",)),
        name="rwkv_step",
    )(r, ld, k, v, kk, b, state)


def _select_topk(score, allowed, n_sel, store):
    bits = lax.bitcast_convert_type(score, I32)
    key = jnp.where(bits < 0, bits ^ jnp.int32(0x7FFFFFFF), bits)
    key = jnp.where(allowed, key, jnp.int32(INT_MIN))
    m, w = score.shape
    one, zero = jnp.ones((), BF16), jnp.zeros((), BF16)

    def byte(shift):
        if shift == 24:
            d = lax.shift_right_arithmetic(key, 24) + 128
        else:
            d = lax.shift_right_logical(key, shift) & 255
        return d.astype(F32).astype(BF16)

    def count(flags):
        acc = flags[:, :LANES]
        for c in range(LANES, w, LANES):
            acc = acc + flags[:, c:c + LANES]
        return jnp.sum(acc.astype(F32), axis=1, keepdims=True)

    need = jnp.full((m, 1), float(n_sel), F32)
    x = byte(24)
    tau = jnp.zeros((m, 1), I32)
    for shift in (24, 16, 8, 0):
        def body(it, t, x=x, need=need):
            step = lax.shift_left(jnp.int32(1), jnp.int32(6) - 2 * it).astype(F32)
            cnts = [count(jnp.where(x >= (t + mult * step).astype(BF16), one, zero))
                    for mult in (1.0, 2.0, 3.0)]
            hits = sum(jnp.where(cnt >= need, 1.0, 0.0) for cnt in cnts)
            return t + hits * step

        t = lax.fori_loop(0, 4, body, jnp.zeros((m, 1), F32))
        tb = t.astype(BF16)
        need = need - count(jnp.where(x > tb, one, zero))
        digit = t.astype(I32) - (128 if shift == 24 else 0)
        tau = tau | lax.shift_left(digit, shift)
        if shift:
            x = jnp.where(x == tb, byte(shift - 8), -one)
    store(jnp.logical_and(key >= tau, allowed))
    ties = count(jnp.where(x == tb, one, zero))
    crowded = jnp.logical_and(ties > need, tau > INT_MIN)

    @pl.when(jnp.max(jnp.where(crowded, 1.0, 0.0)) > 0.0)
    def _():
        above = jnp.logical_and(key > tau, allowed)
        equal = jnp.logical_and(key == tau, allowed)
        ra = lax.broadcasted_iota(I32, (KEY_TILE, KEY_TILE), 0)
        rb = lax.broadcasted_iota(I32, (KEY_TILE, KEY_TILE), 1)
        upto = (ra <= rb).astype(BF16)
        seen = jnp.zeros((m, 1), F32)
        keep = []
        for c0 in range(0, w, KEY_TILE):
            c1 = min(c0 + KEY_TILE, w)
            eq = equal[:, c0:c1]
            flags = jnp.where(eq, 1.0, 0.0)
            rank = seen + _dot(flags.astype(BF16), upto[:c1 - c0, :c1 - c0])
            keep.append(jnp.logical_and(eq, rank <= need))
            seen = seen + jnp.sum(flags, axis=1, keepdims=True)
        store(jnp.logical_or(above, jnp.concatenate(keep, axis=1)))


KEY_TILE = 256
Q_TILES_PER_EXTENT = 4


def _dsa_prompt_block(n_sel, tk, i, q_ref, iq_ref, iw_ref, ikt_ref, kt_ref, v_ref, o_ref, sc_ref):
    tq = q_ref.shape[0]
    iw = iw_ref[...] * ((IDX_HEADS * IDX_DIM) ** -0.5)
    iq = iq_ref[...]
    iq_h = [iq[:, h * IDX_DIM:(h + 1) * IDX_DIM] for h in range(IDX_HEADS)]
    iw_h = [iw[:, h:h + 1] for h in range(IDX_HEADS)]
    for c0 in range(0, tk, KEY_TILE):
        c1 = min(c0 + KEY_TILE, tk)
        ikb = ikt_ref[:, c0:c1]
        acc = jnp.maximum(_dot(iq_h[0], ikb), 0.0) * iw_h[0]
        for h in range(1, IDX_HEADS):
            acc = acc + jnp.maximum(_dot(iq_h[h], ikb), 0.0) * iw_h[h]
        sc_ref[:, c0:c1] = acc
    qpos = i * tq + lax.broadcasted_iota(I32, (tq, 1), 0)
    kpos = lax.broadcasted_iota(I32, (1, tk), 1)

    def store_bias(mask):
        sc_ref[:, :tk] = jnp.where(mask, 0.0, NEG_BIG)

    _select_topk(sc_ref[:, :tk], kpos <= qpos, n_sel, store_bias)
    bias = sc_ref[:, :tk]
    q = q_ref[...]
    rep = A_HEADS // A_KV_HEADS
    for g in range(A_KV_HEADS):
        kg = kt_ref[g * A_HEAD:(g + 1) * A_HEAD, :tk]
        vg = v_ref[:tk, g * A_HEAD:(g + 1) * A_HEAD].astype(BF16)
        for rr in range(rep):
            h = g * rep + rr
            s = _dot(q[:, h * A_HEAD:(h + 1) * A_HEAD], kg) + bias
            m = jnp.max(s, axis=1, keepdims=True)
            p = jnp.exp(s - m)
            l = jnp.sum(p, axis=1, keepdims=True)
            o_ref[:, h * A_HEAD:(h + 1) * A_HEAD] = _dot(p.astype(BF16), vg) / l


def _dsa_prompt_kernel(n_sel, *refs):
    i = pl.program_id(1)
    tq = refs[0].shape[0]
    tp = refs[5].shape[0]
    nq = tp // tq
    for lo in range(0, nq, Q_TILES_PER_EXTENT):
        hi = min(lo + Q_TILES_PER_EXTENT, nq)

        @pl.when(jnp.logical_and(i >= lo, i < hi))
        def _(hi=hi):
            _dsa_prompt_block(n_sel, hi * tq, i, *refs)


def dsa_prompt(q, iq, iw, ikt, kt, p, batch, tp, n_sel):
    nq = tp // Q_TILE
    qrow = lambda w: pl.BlockSpec((Q_TILE, w), lambda b, i: (b * nq + i, 0))
    keys = lambda w, blk: pl.BlockSpec((tp, w), lambda b, i: (b, blk))
    keys_t = lambda w: pl.BlockSpec((w, tp), lambda b, i: (0, b))
    return pl.pallas_call(
        functools.partial(_dsa_prompt_kernel, n_sel),
        grid=(batch, nq),
        in_specs=[qrow(A_DIM), qrow(IDX_HEADS * IDX_DIM), qrow(IDX_HEADS),
                  keys_t(IDX_DIM), keys_t(A_KV_DIM), keys(A_KV_DIM, C_VA // A_KV_DIM)],
        out_specs=qrow(A_DIM),
        out_shape=jax.ShapeDtypeStruct((batch * tp, A_DIM), F32),
        scratch_shapes=[pltpu.VMEM((Q_TILE, tp), F32)],
        compiler_params=_cparams(("parallel", "parallel")),
        name="dsa_prompt",
    )(q, iq, iw, ikt, kt, p)


SCORE_SEQS = 4
ATTN_SEQS = 2


def _dsa_step_score_kernel(n_pages, page, pt_ref, iq_ref, iw_ref, ikn_ref, *refs):
    sb = iq_ref.shape[0]
    pages = refs[:sb * n_pages]
    o_ref = refs[sb * n_pages]
    lane = lax.broadcasted_iota(I32, (1, LANES), 1)
    for q in range(sb):
        iq = iq_ref[q]
        iw = iw_ref[q] * ((IDX_HEADS * IDX_DIM) ** -0.5)
        for j in range(n_pages):
            d = _dot(iq, pages[q * n_pages + j][...].astype(BF16))
            o_ref[q, :, j * page:(j + 1) * page] = jnp.sum(jnp.maximum(d, 0.0) * iw, axis=0,
                                                           keepdims=True)
        dn = jnp.sum(iq.astype(F32) * ikn_ref[q], axis=1, keepdims=True)
        sn = jnp.sum(jnp.maximum(dn, 0.0) * iw, axis=0, keepdims=True)
        o_ref[q, :, n_pages * page:] = jnp.where(lane == 0, sn, 0.0)


def _seqs_per_step(s, want):
    return want if s % want == 0 else 1


def dsa_step_scores(pt_flat, iq, iw, ik_new, cik2d, n_pages, page):
    s = iq.shape[0]
    sb = _seqs_per_step(s, SCORE_SEQS)
    kw = n_pages * page + LANES
    page_spec = lambda q, j: pl.BlockSpec(
        (IDX_DIM, page), lambda i, pt: (pt[(i * sb + q) * n_pages + j], 0))
    grid_spec = pltpu.PrefetchScalarGridSpec(
        num_scalar_prefetch=1,
        grid=(s // sb,),
        in_specs=[pl.BlockSpec((sb, IDX_HEADS, IDX_DIM), lambda i, pt: (i, 0, 0)),
                  pl.BlockSpec((sb, IDX_HEADS, 1), lambda i, pt: (i, 0, 0)),
                  pl.BlockSpec((sb, 1, IDX_DIM), lambda i, pt: (i, 0, 0))]
                 + [page_spec(q, j) for q in range(sb) for j in range(n_pages)],
        out_specs=pl.BlockSpec((sb, 1, kw), lambda i, pt: (i, 0, 0)),
    )
    return pl.pallas_call(
        functools.partial(_dsa_step_score_kernel, n_pages, page),
        grid_spec=grid_spec,
        out_shape=jax.ShapeDtypeStruct((s, 1, kw), F32),
        compiler_params=_cparams(("arbitrary",)),
        name="dsa_step_scores",
    )(pt_flat, iq, iw, ik_new, *([cik2d] * (sb * n_pages)))


def _dsa_step_select_kernel(n_sel, past, sc_ref, o_ref):
    sc = sc_ref[...]
    kpos = lax.broadcasted_iota(I32, sc.shape, 1)

    def store(mask):
        o_ref[...] = jnp.where(mask, 1.0, 0.0)

    _select_topk(sc, kpos <= past, n_sel, store)


def dsa_step_select(sc, n_sel, past):
    return pl.pallas_call(
        functools.partial(_dsa_step_select_kernel, n_sel, past),
        out_shape=jax.ShapeDtypeStruct(sc.shape, F32),
        compiler_params=pltpu.CompilerParams(vmem_limit_bytes=VMEM_LIMIT),
        name="dsa_step_select",
    )(sc)


def _dsa_step_attn_kernel(n_pages, page, pt_ref, q_ref, kn_ref, vn_ref, sel_ref, ex_ref, *refs):
    sb = q_ref.shape[0]
    kps = refs[:sb * n_pages]
    vps = refs[sb * n_pages:2 * sb * n_pages]
    o_ref = refs[2 * sb * n_pages]
    rep = A_HEADS // A_KV_HEADS
    w2 = page * A_KV_HEADS
    hrow = lax.broadcasted_iota(I32, (A_HEADS, w2), 0)
    col = lax.broadcasted_iota(I32, (A_HEADS, w2), 1)
    own = (col % A_KV_HEADS) == (hrow // rep)
    h8 = lax.broadcasted_iota(I32, (A_HEADS, A_HEAD), 0)
    ex = ex_ref[...]
    for u in range(sb):
        kp = kps[u * n_pages:(u + 1) * n_pages]
        vp = vps[u * n_pages:(u + 1) * n_pages]
        q = q_ref[u]
        logits = []
        for j in range(n_pages):
            s = _dot_nt(q, kp[j][...].astype(BF16))
            selj = _dot(sel_ref[u, :, j * page:(j + 1) * page].astype(BF16), ex)
            logits.append(jnp.where(jnp.logical_and(selj > 0.5, own), s, NEG_BIG))
        kn = jnp.where(h8 < rep, kn_ref[u, 0:1, :], kn_ref[u, 1:2, :])
        vn = jnp.where(h8 < rep, vn_ref[u, 0:1, :], vn_ref[u, 1:2, :])
        sn = jnp.sum(q.astype(F32) * kn, axis=1, keepdims=True)
        seln = sel_ref[u, :, n_pages * page:n_pages * page + 1]
        sn = jnp.where(seln > 0.5, sn, NEG_BIG)
        m = sn
        for s in logits:
            m = jnp.maximum(m, jnp.max(s, axis=1, keepdims=True))
        pn = jnp.exp(sn - m)
        l = pn
        acc = pn * vn
        for j in range(n_pages):
            p = jnp.exp(logits[j] - m)
            l = l + jnp.sum(p, axis=1, keepdims=True)
            acc = acc + _dot(p.astype(BF16), vp[j][...].astype(BF16))
        o_ref[u] = acc / l


def dsa_step_attn(pt_flat, q, k_new, v_new, sel, expand, ck2d, cv2d, n_pages, page):
    s = q.shape[0]
    sb = _seqs_per_step(s, ATTN_SEQS)
    kw = sel.shape[-1]
    w2 = page * A_KV_HEADS
    page_spec = lambda u, j: pl.BlockSpec(
        (w2, A_HEAD), lambda i, pt: (pt[(i * sb + u) * n_pages + j], 0))
    pages = [page_spec(u, j) for u in range(sb) for j in range(n_pages)]
    grid_spec = pltpu.PrefetchScalarGridSpec(
        num_scalar_prefetch=1,
        grid=(s // sb,),
        in_specs=[pl.BlockSpec((sb, A_HEADS, A_HEAD), lambda i, pt: (i, 0, 0)),
                  pl.BlockSpec((sb, A_KV_HEADS, A_HEAD), lambda i, pt: (i, 0, 0)),
                  pl.BlockSpec((sb, A_KV_HEADS, A_HEAD), lambda i, pt: (i, 0, 0)),
                  pl.BlockSpec((sb, 1, kw), lambda i, pt: (i, 0, 0)),
                  pl.BlockSpec((page, w2), lambda i, pt: (0, 0))] + pages * 2,
        out_specs=pl.BlockSpec((sb, A_HEADS, A_HEAD), lambda i, pt: (i, 0, 0)),
    )
    return pl.pallas_call(
        functools.partial(_dsa_step_attn_kernel, n_pages, page),
        grid_spec=grid_spec,
        out_shape=jax.ShapeDtypeStruct((s, A_HEADS, A_HEAD), F32),
        compiler_params=_cparams(("arbitrary",)),
        name="dsa_step_attn",
    )(pt_flat, q, k_new, v_new, sel, expand, *([ck2d] * (sb * n_pages)), *([cv2d] * (sb * n_pages)))


def _pack_bf16_pairs(x):
    w = x.shape[1] // 2
    hi = lax.bitcast_convert_type(x[:, :w].astype(BF16).astype(F32), I32)
    lo = lax.bitcast_convert_type(x[:, w:].astype(BF16).astype(F32), I32)
    return hi | lax.shift_right_logical(lo, 16)


def _unpack_bf16_pairs(p):
    hi = lax.bitcast_convert_type(p & jnp.int32(-65536), F32)
    lo = lax.bitcast_convert_type(lax.shift_left(p, 16), F32)
    return hi, lo


def _mix_kernel(alpha, prompt_blocks, x_ref, yp_ref, bonp_ref, gp_ref, ap_ref, ys_ref, bons_ref, gs_ref,
                as_ref, e_ref, et_ref, gng_ref, gnb_ref, l0g_ref, l0b_ref, wo_ref, l1g_ref, l1b_ref,
                wr_ref, h_ref, sc_ref, pk_ref):
    is_prompt = pl.program_id(0) < prompt_blocks
    pick = lambda p_ref, s_ref: jnp.where(is_prompt, p_ref[...], s_ref[...])
    e, et = e_ref[...], et_ref[...]
    y = pick(yp_ref, ys_ref)
    inv = 1.0 / R_HEAD
    mu = _head_sums(y, e, et) * inv
    d = y - mu
    var = _head_sums(d * d, e, et) * inv
    yn = d * lax.rsqrt(var + GN_EPS) * gng_ref[...] + gnb_ref[...]
    r_out = (yn + pick(bonp_ref, bons_ref)) * pick(gp_ref, gs_ref)
    mix = (_dot(r_out.astype(BF16), wo_ref[:R_DIM, :])
           + _dot(pick(ap_ref, as_ref).astype(BF16), wo_ref[R_DIM:, :]))
    h0 = _layer_norm(x_ref[...], l0g_ref[...], l0b_ref[...])
    h1 = _layer_norm(alpha * h0 + mix, l1g_ref[...], l1b_ref[...])
    h_ref[...] = h1
    w_hi, w_lo = _split_bf16(wr_ref[...])
    h_hi, h_lo = _split_bf16(h1)
    logits = _dot(h_hi, w_hi) + _dot(h_lo, w_hi) + _dot(h_hi, w_lo)
    sc_ref[...] = _sigmoid(logits.T[:N_EXPERTS, :])
    pk_ref[...] = _pack_bf16_pairs(h1)


def mix_ln1_router(x, prompt_parts, step_parts, pw, alpha):
    n, d = x.shape
    tm = ROW_TILE
    pb = prompt_parts[0].shape[0] // tm
    row = lambda w: pl.BlockSpec((tm, w), lambda i: (i, 0))
    head = lambda w: pl.BlockSpec((tm, w), lambda i: (jnp.minimum(i, pb - 1), 0))
    tail = lambda w: pl.BlockSpec((tm, w), lambda i: (jnp.maximum(i - pb, 0), 0))
    full = lambda a: pl.BlockSpec(a.shape, lambda i: (0,) * a.ndim)
    params = (pw["e"], pw["et"], pw["gn_g"], pw["gn_b"], pw["ln0_g"], pw["ln0_b"], pw["w_out"],
              pw["ln1_g"], pw["ln1_b"], pw["w_router"])
    widths = (R_DIM, R_DIM, R_DIM, A_DIM)
    return pl.pallas_call(
        functools.partial(_mix_kernel, alpha, pb),
        grid=(n // tm,),
        in_specs=[row(d)] + [head(w) for w in widths] + [tail(w) for w in widths]
                 + [full(a) for a in params],
        out_specs=[row(d), pl.BlockSpec((N_EXPERTS, tm), lambda i: (0, i)), row(d // 2)],
        out_shape=[jax.ShapeDtypeStruct((n, d), F32),
                   jax.ShapeDtypeStruct((N_EXPERTS, n), F32),
                   jax.ShapeDtypeStruct((n, d // 2), I32)],
        compiler_params=_cparams(("parallel",)),
        name="mix_ln1_router",
    )(x, *prompt_parts, *step_parts, *params)


def _route_kernel(sc_ref, bias_ref, idx_ref, gate_ref, pos_ref, cnt_ref, cnt_scr):
    scores = sc_ref[...]
    biased = scores + bias_ref[...]
    tn = scores.shape[1]
    per = N_EXPERTS // N_EXPERT_GROUPS
    sub = lax.broadcasted_iota(I32, (per, tn), 0)
    grp_rows = []
    for g in range(N_EXPERT_GROUPS):
        xg = biased[g * per:(g + 1) * per, :]
        m1 = jnp.max(xg, axis=0, keepdims=True)
        first = jnp.min(jnp.where(xg == m1, sub, per), axis=0, keepdims=True)
        m2 = jnp.max(jnp.where(sub == first, -jnp.inf, xg), axis=0, keepdims=True)
        grp_rows.append(m1 + m2)
    grp = jnp.concatenate(grp_rows, axis=0)
    gi = lax.broadcasted_iota(I32, (N_EXPERT_GROUPS, tn), 0)
    gsel = jnp.zeros((N_EXPERT_GROUPS, tn), jnp.bool_)
    for _ in range(TOPK_GROUPS):
        m = jnp.max(grp, axis=0, keepdims=True)
        first = jnp.min(jnp.where(grp == m, gi, N_EXPERT_GROUPS), axis=0, keepdims=True)
        hit = gi == first
        gsel = jnp.logical_or(gsel, hit)
        grp = jnp.where(hit, -jnp.inf, grp)
    ei = lax.broadcasted_iota(I32, (N_EXPERTS, tn), 0)
    emask = jnp.concatenate(
        [jnp.broadcast_to(gsel[g:g + 1, :], (per, tn)) for g in range(N_EXPERT_GROUPS)], axis=0)
    cand = jnp.where(emask, biased, -jnp.inf)
    idxs, gates, hits = [], [], []
    for _ in range(TOP_K):
        m = jnp.max(cand, axis=0, keepdims=True)
        first = jnp.min(jnp.where(cand == m, ei, N_EXPERTS), axis=0, keepdims=True)
        hit = ei == first
        idxs.append(first)
        hits.append(hit)
        gates.append(jnp.sum(jnp.where(hit, scores, 0.0), axis=0, keepdims=True))
        cand = jnp.where(hit, -jnp.inf, cand)
    gate = jnp.concatenate(gates, axis=0)
    gate = gate / jnp.sum(gate, axis=0, keepdims=True) * ROUTED_SCALE
    idx_ref[...] = jnp.concatenate(idxs, axis=0)
    gate_ref[...] = gate
    chosen = hits[0]
    for hit in hits[1:]:
        chosen = jnp.logical_or(chosen, hit)
    onehot = jnp.where(chosen, 1.0, 0.0)
    ta = lax.broadcasted_iota(I32, (tn, tn), 0)
    tb = lax.broadcasted_iota(I32, (tn, tn), 1)
    prefix = _dot(onehot.astype(BF16), (ta < tb).astype(BF16))

    @pl.when(pl.program_id(0) == 0)
    def _():
        cnt_scr[...] = jnp.zeros_like(cnt_scr)

    rank = prefix + cnt_scr[:, 0:1]
    pos_ref[...] = jnp.concatenate(
        [jnp.sum(jnp.where(hit, rank, 0.0), axis=0, keepdims=True) for hit in hits], axis=0).astype(I32)
    cnt_scr[...] = cnt_scr[...] + jnp.sum(onehot, axis=1, keepdims=True)
    cnt_ref[...] = cnt_scr[...].astype(I32)


def route(scores_t, e_bias):
    n = scores_t.shape[1]
    tn = ROW_TILE
    tok = pl.BlockSpec((TOP_K, tn), lambda i: (0, i))
    return pl.pallas_call(
        _route_kernel,
        grid=(n // tn,),
        in_specs=[pl.BlockSpec((N_EXPERTS, tn), lambda i: (0, i)),
                  pl.BlockSpec((N_EXPERTS, 1), lambda i: (0, 0))],
        out_specs=[tok, tok, tok, pl.BlockSpec((N_EXPERTS, LANES), lambda i: (0, 0))],
        out_shape=[jax.ShapeDtypeStruct((TOP_K, n), I32), jax.ShapeDtypeStruct((TOP_K, n), F32),
                   jax.ShapeDtypeStruct((TOP_K, n), I32),
                   jax.ShapeDtypeStruct((N_EXPERTS, LANES), I32)],
        scratch_shapes=[pltpu.VMEM((N_EXPERTS, LANES), F32)],
        compiler_params=_cparams(("arbitrary",)),
        name="route",
    )(scores_t, e_bias)


def _dispatch_kernel(nb, dest_ref, segend_ref, cnt_ref, nu_ref, x_ref, inv0_ref, o_ref, invo_ref,
                     inv_ref, zbuf, sem, zsem, isem):
    tm = x_ref.shape[0]
    row0 = pl.program_id(0) * tm

    @pl.when(pl.program_id(0) == 0)
    def _():
        c = pltpu.make_async_copy(inv0_ref, inv_ref, isem)
        c.start()
        c.wait()

    def fill(start):
        return pltpu.make_async_copy(zbuf, o_ref.at[pl.ds(pl.multiple_of(start, EXPERT_TILE),
                                                          EXPERT_TILE)], zsem)

    @pl.when(pl.program_id(0) == 0)
    def _():
        zbuf[...] = jnp.zeros_like(zbuf)

        def each_expert(fn):
            def body(e, carry):
                @pl.when(cnt_ref[e] > 0)
                def _():
                    fn(fill(segend_ref[e] - EXPERT_TILE))
                return carry
            lax.fori_loop(0, N_EXPERTS, body, 0)

        def each_free_block(fn):
            def body(b, carry):
                fn(fill(b * EXPERT_TILE))
                return carry
            lax.fori_loop(nu_ref[0], nb, body, 0)

        each_expert(lambda c: c.start())
        each_free_block(lambda c: c.start())
        each_expert(lambda c: c.wait())
        each_free_block(lambda c: c.wait())

    def start(i, carry):
        for j in range(TOP_K):
            d = dest_ref[i * TOP_K + j]
            inv_ref[d] = (row0 + i) * TOP_K + j
            pltpu.make_async_copy(x_ref.at[pl.ds(i, 1)], o_ref.at[pl.ds(d, 1)],
                                  sem).start(priority=j % 2)
        return carry

    lax.fori_loop(0, tm, start, 0)
    for j in range(TOP_K):
        pltpu.make_async_copy(x_ref, o_ref.at[pl.ds(0, tm)], sem).wait()

    @pl.when(pl.program_id(0) == pl.num_programs(0) - 1)
    def _():
        c = pltpu.make_async_copy(inv_ref, invo_ref, isem)
        c.start()
        c.wait()


def moe_dispatch(dest_flat, seg_end, counts, n_used, xpk, inv_default, nb):
    n, w = xpk.shape
    tm = ROW_TILE
    rows = nb * EXPERT_TILE
    smem = lambda: pl.BlockSpec(memory_space=pltpu.SMEM)
    hbm = lambda: pl.BlockSpec(memory_space=pl.ANY)
    return pl.pallas_call(
        functools.partial(_dispatch_kernel, nb),
        grid=(n // tm,),
        in_specs=[pl.BlockSpec((tm * TOP_K,), lambda i: (i,), memory_space=pltpu.SMEM),
                  smem(), smem(), smem(),
                  pl.BlockSpec((tm, w), lambda i: (i, 0)), hbm()],
        out_specs=[hbm(), hbm()],
        out_shape=[jax.ShapeDtypeStruct((rows, w), I32), jax.ShapeDtypeStruct((rows,), I32)],
        scratch_shapes=[pltpu.SMEM((rows,), I32), pltpu.VMEM((EXPERT_TILE, w), I32),
                        pltpu.SemaphoreType.DMA(()), pltpu.SemaphoreType.DMA(()),
                        pltpu.SemaphoreType.DMA(())],
        compiler_params=_cparams(("arbitrary",)),
        name="moe_dispatch",
    )(dest_flat, seg_end, counts, n_used, xpk, inv_default)


def _experts_kernel(n_slots, nb, be_ref, nxt_ref, slot_ref, nu_ref, inv_ref, x_ref, wg_ref, wu_ref,
                    wd_ref, o_ref, wg_f, wu_f, wd_f, wg_s, wu_s, wd_s, obuf_a, obuf_b, sems, osems):
    i = pl.program_id(0)
    n_used = nu_ref[0]
    used = i < n_used
    prev = be_ref[jnp.maximum(i - 1, 0)]
    fresh = jnp.logical_and(used, jnp.logical_or(i == 0, be_ref[i] != prev))
    bm = x_ref.shape[0]

    obufs = (obuf_a, obuf_b)

    def scatter_rows(blk, par):
        for r in range(bm):
            pltpu.make_async_copy(obufs[par].at[pl.ds(r, 1)],
                                  o_ref.at[pl.ds(inv_ref[blk * bm + r], 1)], osems.at[par]).start()

    def wait_rows(par):
        pltpu.make_async_copy(obufs[par], o_ref.at[pl.ds(0, bm)], osems.at[par]).wait()

    def by_parity(cond, fn):
        for par in range(2):
            @pl.when(jnp.logical_and(cond, i % 2 == par))
            def _(par=par):
                fn(par)

    @pl.when(i == 0)
    def _():
        obuf_b[...] = jnp.zeros_like(obuf_b)
        spare = [pltpu.make_async_copy(obuf_b, o_ref.at[pl.ds(n_slots + e * bm, bm)], osems.at[1])
                 for e in range(N_EXPERTS)]
        for c in spare:
            c.start()
        for c in spare:
            c.wait()

    by_parity(jnp.logical_and(i >= 2, i - 2 < n_used), wait_rows)

    def weight_copies(e, slot):
        return (pltpu.make_async_copy(wg_ref.at[e], wg_f.at[slot], sems.at[slot, 0]),
                pltpu.make_async_copy(wu_ref.at[e], wu_f.at[slot], sems.at[slot, 1]),
                pltpu.make_async_copy(wd_ref.at[e], wd_f.at[slot], sems.at[slot, 2]))

    @pl.when(jnp.logical_and(used, i == 0))
    def _():
        for c in weight_copies(be_ref[0], 0):
            c.start()

    @pl.when(fresh)
    def _():
        slot = slot_ref[i]
        for c in weight_copies(be_ref[i], slot):
            c.wait()

        @pl.when(nxt_ref[i] >= 0)
        def _():
            for c in weight_copies(nxt_ref[i], 1 - slot):
                c.start()

        wg_s[...] = wg_f[slot].astype(BF16)
        wu_s[...] = wu_f[slot].astype(BF16)
        wd_s[...] = wd_f[slot].astype(BF16)

    def compute(par):
        hi, lo = _unpack_bf16_pairs(x_ref[...])
        hi, lo = hi.astype(BF16), lo.astype(BF16)
        half = hi.shape[1]
        gp = _dot(hi, wg_s[:half, :]) + _dot(lo, wg_s[half:, :])
        up = _dot(hi, wu_s[:half, :]) + _dot(lo, wu_s[half:, :])
        act = gp * _sigmoid(gp) * up
        obufs[par][...] = _pack_bf16_pairs(_dot(act.astype(BF16), wd_s[...]))

    @pl.when(jnp.logical_and(used, i == 0))
    def _():
        compute(0)

    def send_prev_and_compute(par):
        scatter_rows(i - 1, 1 - par)
        compute(par)

    by_parity(jnp.logical_and(used, i > 0), send_prev_and_compute)
    by_parity(jnp.logical_and(i == n_used, i > 0), lambda par: scatter_rows(i - 1, 1 - par))

    @pl.when(jnp.logical_and(i == nb - 1, n_used == nb - 1))
    def _():
        wait_rows((nb - 2) % 2)


def moe_experts(blk_e, nxt_e, slot, n_used, inv, xs, n_slots, w_gate, w_up, w_down):
    w = xs.shape[1]
    nb = xs.shape[0] // EXPERT_TILE
    _, d, de = w_gate.shape
    grid_spec = pltpu.PrefetchScalarGridSpec(
        num_scalar_prefetch=5,
        grid=(nb,),
        in_specs=[pl.BlockSpec((EXPERT_TILE, w), lambda i, be, nx, sl, nu, iv: (jnp.minimum(i, nu[0] - 1), 0)),
                  pl.BlockSpec(memory_space=pl.ANY), pl.BlockSpec(memory_space=pl.ANY),
                  pl.BlockSpec(memory_space=pl.ANY)],
        out_specs=pl.BlockSpec(memory_space=pl.ANY),
        scratch_shapes=[pltpu.VMEM((2, d, de), F32), pltpu.VMEM((2, d, de), F32),
                        pltpu.VMEM((2, de, d), F32),
                        pltpu.VMEM((d, de), BF16), pltpu.VMEM((d, de), BF16),
                        pltpu.VMEM((de, d), BF16), pltpu.VMEM((EXPERT_TILE, w), I32),
                        pltpu.VMEM((EXPERT_TILE, w), I32),
                        pltpu.SemaphoreType.DMA((2, 3)), pltpu.SemaphoreType.DMA((2,))],
    )
    return pl.pallas_call(
        functools.partial(_experts_kernel, n_slots, nb),
        grid_spec=grid_spec,
        out_shape=jax.ShapeDtypeStruct((n_slots + N_EXPERTS * EXPERT_TILE, w), I32),
        compiler_params=_cparams(("arbitrary",)),
        name="moe_experts",
    )(blk_e, nxt_e, slot, n_used, inv, xs, w_gate, w_up, w_down)


def _combine_kernel(alpha, prompt_blocks, h_ref, gate_ref, ys_ref, wsg_ref, wsu_ref, wsd_ref, l2g_ref,
                    l2b_ref, op_ref, os_ref):
    tm = h_ref.shape[0]
    h = h_ref[...]
    hb = h.astype(BF16)
    gp = _dot(hb, wsg_ref[...])
    up = _dot(hb, wsu_ref[...])
    shared = _dot((gp * _sigmoid(gp) * up).astype(BF16), wsd_ref[...])
    row = lax.broadcasted_iota(I32, (tm, tm * TOP_K), 0)
    col = lax.broadcasted_iota(I32, (tm, tm * TOP_K), 1)
    g_hi, g_lo = _split_bf16(jnp.where(col // TOP_K == row, gate_ref[0], 0.0))
    hi, lo = _unpack_bf16_pairs(ys_ref[...])
    hi, lo = hi.astype(BF16), lo.astype(BF16)
    routed = jnp.concatenate([_dot(g_hi, hi) + _dot(g_lo, hi), _dot(g_hi, lo) + _dot(g_lo, lo)], axis=1)
    out = _layer_norm(alpha * h + routed + shared, l2g_ref[...], l2b_ref[...])
    is_prompt = pl.program_id(0) < prompt_blocks

    @pl.when(is_prompt)
    def _():
        op_ref[...] = out

    @pl.when(jnp.logical_not(is_prompt))
    def _():
        os_ref[...] = out


def moe_combine(h1, gate_rows, ys, pw, alpha, n_prompt):
    n, d = h1.shape
    tm = gate_rows.shape[2] // TOP_K
    pb = n_prompt // tm
    full = lambda a: pl.BlockSpec(a.shape, lambda i: (0,) * a.ndim)
    params = (pw["ws_gate"], pw["ws_up"], pw["ws_down"], pw["ln2_g"], pw["ln2_b"])
    return pl.pallas_call(
        functools.partial(_combine_kernel, alpha, pb),
        grid=(n // tm,),
        in_specs=[pl.BlockSpec((tm, d), lambda i: (i, 0)),
                  pl.BlockSpec((1, 1, tm * TOP_K), lambda i: (i, 0, 0)),
                  pl.BlockSpec((tm * TOP_K, d // 2), lambda i: (i, 0))]
                 + [full(a) for a in params],
        out_specs=[pl.BlockSpec((tm, d), lambda i: (jnp.minimum(i, pb - 1), 0)),
                   pl.BlockSpec((tm, d), lambda i: (jnp.maximum(i - pb, 0), 0))],
        out_shape=[jax.ShapeDtypeStruct((n_prompt, d), F32),
                   jax.ShapeDtypeStruct((n - n_prompt, d), F32)],
        compiler_params=_cparams(("arbitrary",)),
        name="moe_combine",
    )(h1, gate_rows, ys, *params)


def _round_up(x, m):
    return (x + m - 1) // m * m


def _rope_tables(pos, head):
    half = head // 2
    inv = ROPE_THETA ** (-jnp.arange(half, dtype=F32) / half)
    ang = pos.astype(F32)[:, None] * inv[None, :]
    cos, sin = jnp.cos(ang), jnp.sin(ang)
    rep = LANES // head
    c = jnp.tile(jnp.concatenate([cos, cos], axis=1), (1, rep))
    s = jnp.tile(jnp.concatenate([-sin, sin], axis=1), (1, rep))
    return c, s


def _permute_cols(m, axis=-1):
    axis = axis % m.ndim
    cut = lambda lo, hi: lax.slice_in_dim(m, lo, hi, axis=axis)
    a0 = SHIFT_DIM
    i0 = a0 + A_DIM + 2 * A_KV_DIM + IDX_HEADS * IDX_DIM

    def pad(w):
        shape = list(m.shape)
        shape[axis] = w
        return jnp.zeros(shape, m.dtype)

    pieces = [
        cut(0, 3 * R_DIM),
        cut(a0, a0 + A_DIM),
        cut(a0 + A_DIM + 2 * A_KV_DIM, i0),
        cut(a0 + A_DIM, a0 + A_DIM + 2 * A_KV_DIM),
        cut(i0, i0 + IDX_DIM + IDX_HEADS),
        pad(LANES - IDX_DIM - IDX_HEADS),
        cut(3 * R_DIM, SHIFT_DIM),
        pad(LORA_W - (SHIFT_DIM - 3 * R_DIM)),
    ]
    return jnp.concatenate(pieces, axis=axis)


def kernel(x_prompt, x_sample, cache_k, cache_v, cache_idx_k, state_wkv, state_shift, page_table,
           meta, ln0_g, ln0_b, w_in, mu_shift, w0, w_b, a0, a_b, g_b, k_k, k_a, r_k, gn_g, gn_b,
           w_out, ln1_g, ln1_b, w_router, e_bias, w_gate, w_up, w_down, ws_gate, ws_up, ws_down,
           ln2_g, ln2_b):
    depth = w_in.shape[0]
    assert depth == 1, "single trunk layer"
    bsz, s_p, d = x_prompt.shape
    s_dec, s_s, _ = x_sample.shape
    assert s_s == 1, "one decode token per sequence"
    t_real = N_META + s_p
    tp = _round_up(t_real, LANES)
    assert (bsz * tp) % ROW_TILE == 0
    sp = _round_up(s_dec, ROW_TILE)
    n_prompt = bsz * tp
    n = n_prompt + sp
    n_pool, page = cache_k.shape[1], cache_k.shape[2]
    n_pages = page_table.shape[1]
    past = n_pages * page
    alpha = float((2 * depth) ** 0.25)
    row2 = lambda a: a.reshape(1, -1)

    meta_rows = jnp.broadcast_to(meta[None], (bsz, N_META, d))
    xp = jnp.concatenate([meta_rows, x_prompt, jnp.zeros((bsz, tp - t_real, d), F32)], axis=1)
    x_all = jnp.concatenate([xp.reshape(n_prompt, d), x_sample.reshape(s_dec, d),
                             jnp.zeros((sp - s_dec, d), F32)], axis=0)
    pos = jnp.concatenate([jnp.tile(jnp.arange(tp), bsz), jnp.full((sp,), past)])
    c128, s128 = _rope_tables(pos, A_HEAD)
    c64, s64 = _rope_tables(pos, IDX_DIM)

    w_in_t = _permute_cols(w_in[0].T.astype(BF16), axis=0)
    mu_k = _permute_cols(
        jnp.concatenate([mu_shift[0], jnp.zeros((w_in.shape[2] - SHIFT_DIM,), F32)])[None, :])
    head_of = jnp.arange(R_DIM) // R_HEAD
    e_mat = (head_of[:, None] == jnp.arange(R_HEADS)[None, :]).astype(F32)
    zpad = lambda a, rows_before, rows_total: jnp.concatenate(
        [jnp.zeros((rows_before, a.shape[1]), a.dtype), a,
         jnp.zeros((rows_total - rows_before - a.shape[0], a.shape[1]), a.dtype)], axis=0)
    pw = {
        "mu_x": mu_k[:, :3 * R_DIM], "mu_lo": mu_k[:, C_LORA:],
        "w0": row2(w0[0]), "a0": row2(a0[0]), "k_k": row2(k_k[0]), "k_a": row2(k_a[0]),
        "r_k": row2(r_k[0]), "gn_g": row2(gn_g[0]), "gn_b": row2(gn_b[0]),
        "w_b": zpad(w_b[0], 0, LANES).astype(BF16),
        "a_b": zpad(a_b[0], D_DECAY_LORA, LANES).astype(BF16),
        "g_b": zpad(g_b[0], 0, LORA_W - LANES).astype(BF16),
        "e": e_mat.astype(BF16), "et": e_mat.T.astype(BF16),
        "ln0_g": row2(ln0_g), "ln0_b": row2(ln0_b),
        "ln1_g": row2(ln1_g[0]), "ln1_b": row2(ln1_b[0]),
        "ln2_g": row2(ln2_g[0]), "ln2_b": row2(ln2_b[0]),
        "w_out": w_out[0].astype(BF16),
        "w_router": jnp.pad(w_router[0], ((0, 0), (0, LANES - N_EXPERTS))),
        "ws_gate": ws_gate[0].astype(BF16), "ws_up": ws_up[0].astype(BF16),
        "ws_down": ws_down[0].astype(BF16),
    }

    p = ln_proj(x_all, pw["ln0_g"], pw["ln0_b"], w_in_t)
    q_r, iq_r, k_r, ik_r, iw, kt_r, ikt_r = rope_all(p, c128, s128, c64, s64)

    pre_p = rwkv_pre(p, 0, n_prompt, None, pw, t_real, tp)
    shift_k = _permute_cols(jnp.concatenate(
        [state_shift[0], jnp.zeros((s_dec, w_in.shape[2] - SHIFT_DIM), F32)], axis=1))
    shift_k = jnp.concatenate([shift_k, jnp.zeros((sp - s_dec, P_COLS), F32)], axis=0)
    pre_s = rwkv_pre(p, n_prompt, sp, (shift_k[:, :3 * R_DIM], shift_k[:, C_LORA:]), pw, t_real, tp)
    r_p, ld_p, k_p, v_p, kk_p, b_p, g_p, bon_p = pre_p
    r_s, ld_s, k_s, v_s, kk_s, b_s, g_s, bon_s = pre_s
    y_p, wkv_p = rwkv_scan(r_p, ld_p, k_p, v_p, kk_p, b_p, bsz, tp)
    heads = lambda a: a[:s_dec].reshape(s_dec, R_HEADS, R_HEAD).transpose(1, 2, 0)
    y_hs, wkv_hs = rwkv_step(heads(r_s), heads(ld_s), heads(k_s), heads(v_s), heads(kk_s), heads(b_s),
                             state_wkv[0].transpose(1, 2, 3, 0))
    wkv_s = wkv_hs.transpose(3, 0, 1, 2)
    y_s = jnp.concatenate([y_hs.transpose(2, 0, 1).reshape(s_dec, R_DIM),
                           jnp.zeros((sp - s_dec, R_DIM), F32)], axis=0)

    n_sel_p = min(TOPK_KEYS, t_real // 4)
    a_p = dsa_prompt(q_r, iq_r, iw, ikt_r, kt_r, p, bsz, tp, n_sel_p)
    n_sel_s = min(TOPK_KEYS, (past + 1) // 4)
    pt_flat = page_table.reshape(-1).astype(I32)
    srow = slice(n_prompt, n_prompt + s_dec)
    sc_s = dsa_step_scores(pt_flat, iq_r[srow].reshape(s_dec, IDX_HEADS, IDX_DIM),
                           iw[srow].reshape(s_dec, IDX_HEADS, 1), ik_r[srow].reshape(s_dec, 1, IDX_DIM),
                           cache_idx_k[0].transpose(0, 2, 1).reshape(n_pool * IDX_DIM, page),
                           n_pages, page)
    sel_s = dsa_step_select(sc_s.reshape(s_dec, -1), n_sel_s, past).reshape(sc_s.shape)
    slot = jnp.arange(page)[:, None]
    expand = (jnp.arange(page * A_KV_HEADS)[None, :] // A_KV_HEADS == slot).astype(BF16)
    a_s = dsa_step_attn(pt_flat, q_r[srow].reshape(s_dec, A_HEADS, A_HEAD),
                        k_r[srow].reshape(s_dec, A_KV_HEADS, A_HEAD),
                        p[srow, C_VA:C_VA + A_KV_DIM].reshape(s_dec, A_KV_HEADS, A_HEAD),
                        sel_s, expand,
                        cache_k[0].reshape(n_pool * page * A_KV_HEADS, A_HEAD),
                        cache_v[0].reshape(n_pool * page * A_KV_HEADS, A_HEAD), n_pages, page)
    a_s = jnp.concatenate([a_s.reshape(s_dec, A_DIM), jnp.zeros((sp - s_dec, A_DIM), F32)], axis=0)

    h1, scores_t, xpk = mix_ln1_router(x_all, (y_p, bon_p, g_p, a_p), (y_s, bon_s, g_s, a_s), pw, alpha)
    eidx_t, gate_t, pos_t, counts = route(scores_t, e_bias[0].reshape(N_EXPERTS, 1))

    n_slots = n * TOP_K
    nb = (n_slots + N_EXPERTS * (EXPERT_TILE - 1)) // EXPERT_TILE + 1
    counts = counts[:, 0]
    padded = (counts + EXPERT_TILE - 1) // EXPERT_TILE * EXPERT_TILE
    seg_end = jnp.cumsum(padded).astype(I32)
    seg_start = seg_end - padded
    experts = jnp.arange(N_EXPERTS)
    start_of = jnp.sum(jnp.where(eidx_t[:, :, None] == experts, seg_start, 0), axis=-1)
    dest = (start_of + pos_t).T.astype(I32).reshape(-1)
    blk_row = jnp.arange(nb) * EXPERT_TILE
    blk_e = jnp.minimum(jnp.sum(seg_end[None, :] <= blk_row[:, None], axis=1),
                        N_EXPERTS - 1).astype(I32)
    n_used = (seg_end[-1] // EXPERT_TILE).astype(I32)
    run_start = jnp.concatenate([jnp.ones((1,), I32), (blk_e[1:] != blk_e[:-1]).astype(I32)])
    slot = ((jnp.cumsum(run_start) - 1) % 2).astype(I32)
    run_end = seg_end[blk_e] // EXPERT_TILE
    nxt_e = jnp.where(run_end < n_used, blk_e[jnp.minimum(run_end, nb - 1)], -1).astype(I32)
    n_used = n_used.reshape(1)
    spare = (n_slots + blk_e[:, None] * EXPERT_TILE + jnp.arange(EXPERT_TILE)[None, :]).astype(I32)
    xs, inv = moe_dispatch(dest, seg_end, counts.astype(I32), n_used, xpk, spare.reshape(-1), nb)
    ys = moe_experts(blk_e, nxt_e, slot, n_used, inv, xs, n_slots, w_gate[0], w_up[0], w_down[0])
    tc = Q_TILE
    h2_p, h2_s = moe_combine(h1, gate_t.T.reshape(n // tc, 1, tc * TOP_K), ys, pw, alpha, n_prompt)

    def prompt_rows(a):
        return a[:n_prompt].reshape(bsz, tp, -1)[:, :t_real]

    y_prompt = h2_p.reshape(bsz, tp, d)[:, N_META:t_real]
    y_sample = h2_s[:s_dec].reshape(s_dec, 1, d)
    k_prompt = prompt_rows(k_r).reshape(1, bsz, t_real, A_KV_HEADS, A_HEAD)
    v_prompt = prompt_rows(p[:, C_VA:C_VA + A_KV_DIM]).reshape(1, bsz, t_real, A_KV_HEADS, A_HEAD)
    ik_prompt = prompt_rows(ik_r)[None]
    last = jnp.arange(bsz) * tp + t_real - 1
    unperm = lambda rows: jnp.concatenate([rows[:, :3 * R_DIM],
                                           rows[:, C_LORA:C_LORA + SHIFT_DIM - 3 * R_DIM]], axis=1)
    shift_prompt = unperm(p[last])[None]
    k_sample = k_r[srow].reshape(1, s_dec, 1, A_KV_HEADS, A_HEAD)
    v_sample = p[srow, C_VA:C_VA + A_KV_DIM].reshape(1, s_dec, 1, A_KV_HEADS, A_HEAD)
    ik_sample = ik_r[srow].reshape(1, s_dec, 1, IDX_DIM)
    shift_sample = unperm(p[srow])[None]
    return (y_prompt, y_sample, k_prompt, v_prompt, ik_prompt, wkv_p[None], shift_prompt,
            k_sample, v_sample, ik_sample, wkv_s[None], shift_sample)
```
